```python
import math
import jax, jax.numpy as jnp
from jax import lax
import numpy as np

D_MODEL = 1024
BATCH = 16
SEQ = 2048
DEPTH = 4

CHUNK = 64
N_MIXERS = 2
N_HEADS_A = 16
HEAD_DIM_A = D_MODEL // N_HEADS_A
LEFT_CHUNKS = 8
BAND = (LEFT_CHUNKS + 1) * CHUNK
REL_CLIP = 128
N_HEADS_B = 16
HEAD_DIM_B = D_MODEL // N_HEADS_B
SB_BLOCK = 128
D_FF = 2816
ALPHA = (2.0 * DEPTH) ** 0.25
BETA = (8.0 * DEPTH) ** -0.25
LN_EPS = 1e-5
N_A_LAYERS = (DEPTH + N_MIXERS - 1) // N_MIXERS
N_B_LAYERS = DEPTH // N_MIXERS

kernel_name = "hybrid_chunked_relbias_stickbreaking_macaron_deepnorm"


def layer_norm(x, g, b):
    xf = x.astype(jnp.float32)
    mu = jnp.mean(xf, axis=-1, keepdims=True)
    var = jnp.mean(jnp.square(xf - mu), axis=-1, keepdims=True)
    y = (xf - mu) * lax.rsqrt(var + LN_EPS)
    return (y * g.astype(jnp.float32) + b.astype(jnp.float32)).astype(x.dtype)


def swiglu(x, w_gate, w_up, w_down):
    return (jax.nn.silu(x @ w_gate) * (x @ w_up)) @ w_down


def rel_bias_band(rel_table):
    i = np.arange(CHUNK)[:, None]
    j = np.arange(BAND)[None, :]
    rel = (np.clip(i - j + LEFT_CHUNKS * CHUNK, -REL_CLIP, REL_CLIP) + REL_CLIP).astype(np.int32)
    return jnp.transpose(rel_table[rel], (2, 0, 1))


def chunked_rel_attention(x, w_qkv, w_o, rel_table):
    B, S, _ = x.shape
    n_chunks = S // CHUNK
    qkv = (x @ w_qkv).reshape(B, S, 3, N_HEADS_A, HEAD_DIM_A)
    q, k, v = qkv[:, :, 0], qkv[:, :, 1], qkv[:, :, 2]
    pad = LEFT_CHUNKS * CHUNK
    k_pad = jnp.pad(k, ((0, 0), (pad, 0), (0, 0), (0, 0)))
    v_pad = jnp.pad(v, ((0, 0), (pad, 0), (0, 0), (0, 0)))
    bias = rel_bias_band(rel_table).astype(jnp.float32)
    key_offset = jnp.arange(BAND) - pad
    scale = HEAD_DIM_A ** -0.5

    def one_chunk(c):
        start = c * CHUNK
        q_c = lax.dynamic_slice_in_dim(q, start, CHUNK, axis=1)
        k_c = lax.dynamic_slice_in_dim(k_pad, start, BAND, axis=1)
        v_c = lax.dynamic_slice_in_dim(v_pad, start, BAND, axis=1)
        s = jnp.einsum('bqhd,bkhd->bhqk', q_c, k_c,
                       preferred_element_type=jnp.float32) * scale + bias
        valid = (start + key_offset) >= 0
        s = jnp.where(valid[None, None, None, :], s, -jnp.inf)
        p = jax.nn.softmax(s, axis=-1).astype(v_c.dtype)
        return jnp.einsum('bhqk,bkhd->bqhd', p, v_c)

    out = lax.map(one_chunk, jnp.arange(n_chunks))
    out = jnp.transpose(out, (1, 0, 2, 3, 4)).reshape(B, S, D_MODEL)
    return out @ w_o


def stick_breaking_attention(x, w_qkv, w_o):
    B, S, _ = x.shape
    qkv = (x @ w_qkv).reshape(B, S, 3, N_HEADS_B, HEAD_DIM_B)
    q, k, v = qkv[:, :, 0], qkv[:, :, 1], qkv[:, :, 2]
    scale = HEAD_DIM_B ** -0.5
    outs = []
    for blk in range(S // SB_BLOCK):
        q0 = blk * SB_BLOCK
        kv_len = q0 + SB_BLOCK
        q_b = q[:, q0:kv_len]
        k_b = k[:, :kv_len]
        v_b = v[:, :kv_len]
        z = jnp.einsum('bqhd,bkhd->bhqk', q_b, k_b,
                       preferred_element_type=jnp.float32) * scale
        t_pos = q0 + jnp.arange(SB_BLOCK)[:, None]
        s_pos = jnp.arange(kv_len)[None, :]
        causal = s_pos < t_pos
        log_beta = jnp.where(causal, jax.nn.log_sigmoid(z), -jnp.inf)
        log_1m_beta = jnp.where(causal, jax.nn.log_sigmoid(-z), 0.0)
        stick = lax.cumsum(log_1m_beta, axis=log_1m_beta.ndim - 1, reverse=True) - log_1m_beta
        a = jnp.exp(log_beta + stick).astype(v_b.dtype)
        outs.append(jnp.einsum('bhqk,bkhd->bqhd', a, v_b))
    out = jnp.concatenate(outs, axis=1).reshape(B, S, D_MODEL)
    return out @ w_o


def _fwd_setup_inputs(seed: int = 0) -> dict:
    key = jax.random.key(seed)
    ks = jax.random.split(key, 16)
    d_sc = D_MODEL ** -0.5
    f_sc = D_FF ** -0.5
    x = jax.random.normal(ks[0], (BATCH, SEQ, D_MODEL), jnp.float32)

    def qkv_weights(k, n):
        kq, kk, kv = jax.random.split(k, 3)
        wq = jax.random.normal(kq, (n, D_MODEL, D_MODEL), jnp.float32) * d_sc
        wk = jax.random.normal(kk, (n, D_MODEL, D_MODEL), jnp.float32) * d_sc
        wv = jax.random.normal(kv, (n, D_MODEL, D_MODEL), jnp.float32) * (d_sc * BETA)
        return jnp.concatenate([wq, wk, wv], axis=-1)

    w_qkv_a = qkv_weights(ks[1], N_A_LAYERS)
    w_o_a = jax.random.normal(ks[2], (N_A_LAYERS, D_MODEL, D_MODEL), jnp.float32) * (d_sc * BETA)
    rel_bias = jax.random.normal(ks[3], (2 * REL_CLIP + 1, N_HEADS_A), jnp.float32) * 0.5
    w_qkv_b = qkv_weights(ks[4], N_B_LAYERS)
    w_o_b = jax.random.normal(ks[5], (N_B_LAYERS, D_MODEL, D_MODEL), jnp.float32) * (d_sc * BETA)
    ffn_w_gate = jax.random.normal(ks[6], (DEPTH, 2, D_MODEL, D_FF), jnp.float32) * d_sc
    ffn_w_up = jax.random.normal(ks[7], (DEPTH, 2, D_MODEL, D_FF), jnp.float32) * d_sc
    ffn_w_down = jax.random.normal(ks[8], (DEPTH, 2, D_FF, D_MODEL), jnp.float32) * (f_sc * BETA)
    ln_g = 1.0 + 0.02 * jax.random.normal(ks[9], (DEPTH, 3, D_MODEL), jnp.float32)
    ln_b = 0.02 * jax.random.normal(ks[10], (DEPTH, 3, D_MODEL), jnp.float32)
    return {"x": x, "w_qkv_a": w_qkv_a, "w_o_a": w_o_a, "rel_bias": rel_bias,
            "w_qkv_b": w_qkv_b, "w_o_b": w_o_b, "ffn_w_gate": ffn_w_gate,
            "ffn_w_up": ffn_w_up, "ffn_w_down": ffn_w_down, "ln_g": ln_g, "ln_b": ln_b}


def _fwd_reference(x, w_qkv_a, w_o_a, rel_bias, w_qkv_b, w_o_b, ffn_w_gate, ffn_w_up,
              ffn_w_down, ln_g, ln_b):
    for i in range(DEPTH):
        h = swiglu(x, ffn_w_gate[i, 0], ffn_w_up[i, 0], ffn_w_down[i, 0])
        x = layer_norm(ALPHA * x + 0.5 * h, ln_g[i, 0], ln_b[i, 0])
        j = i // N_MIXERS
        if i % N_MIXERS == 0:
            y = chunked_rel_attention(x, w_qkv_a[j], w_o_a[j], rel_bias)
        else:
            y = stick_breaking_attention(x, w_qkv_b[j], w_o_b[j])
        x = layer_norm(ALPHA * x + y, ln_g[i, 1], ln_b[i, 1])
        h = swiglu(x, ffn_w_gate[i, 1], ffn_w_up[i, 1], ffn_w_down[i, 1])
        x = layer_norm(ALPHA * x + 0.5 * h, ln_g[i, 2], ln_b[i, 2])
    return x


import jax as _jax
import jax.numpy as _jnp

TWIN_FORMAT = 'train_step'
FWD_PARAMS = ['x', 'w_qkv_a', 'w_o_a', 'rel_bias', 'w_qkv_b', 'w_o_b', 'ffn_w_gate', 'ffn_w_up', 'ffn_w_down', 'ln_g', 'ln_b']
TWIN_WEIGHTS = ['w_qkv_a', 'w_o_a', 'rel_bias', 'w_qkv_b', 'w_o_b', 'ffn_w_gate', 'ffn_w_up', 'ffn_w_down', 'ln_g', 'ln_b']
TWIN_DIFF_INPUT = 'x'
TWIN_INPUTS = ['x', 'w_qkv_a', 'w_o_a', 'rel_bias', 'w_qkv_b', 'w_o_b', 'ffn_w_gate', 'ffn_w_up', 'ffn_w_down', 'ln_g', 'ln_b', 'loss_target', 'm_w_qkv_a', 'm_w_o_a', 'm_rel_bias', 'm_w_qkv_b', 'm_w_o_b', 'm_ffn_w_gate', 'm_ffn_w_up', 'm_ffn_w_down', 'm_ln_g', 'm_ln_b', 'v_w_qkv_a', 'v_w_o_a', 'v_rel_bias', 'v_w_qkv_b', 'v_w_o_b', 'v_ffn_w_gate', 'v_ffn_w_up', 'v_ffn_w_down', 'v_ln_g', 'v_ln_b']
TWIN_OUTPUTS = ['loss', 'grad_x', 'grad_w_qkv_a', 'grad_w_o_a', 'grad_rel_bias', 'grad_w_qkv_b', 'grad_w_o_b', 'grad_ffn_w_gate', 'grad_ffn_w_up', 'grad_ffn_w_down', 'grad_ln_g', 'grad_ln_b', 'delta_w_qkv_a', 'delta_w_o_a', 'delta_rel_bias', 'delta_w_qkv_b', 'delta_w_o_b', 'delta_ffn_w_gate', 'delta_ffn_w_up', 'delta_ffn_w_down', 'delta_ln_g', 'delta_ln_b', 'new_m_w_qkv_a', 'new_m_w_o_a', 'new_m_rel_bias', 'new_m_w_qkv_b', 'new_m_w_o_b', 'new_m_ffn_w_gate', 'new_m_ffn_w_up', 'new_m_ffn_w_down', 'new_m_ln_g', 'new_m_ln_b', 'new_v_w_qkv_a', 'new_v_w_o_a', 'new_v_rel_bias', 'new_v_w_qkv_b', 'new_v_w_o_b', 'new_v_ffn_w_gate', 'new_v_ffn_w_up', 'new_v_ffn_w_down', 'new_v_ln_g', 'new_v_ln_b']
TWIN_LEAF_KINDS = {'loss': 'loss', 'grad_x': 'grad_x', 'grad_w_qkv_a': 'grad_w', 'grad_w_o_a': 'grad_w', 'grad_rel_bias': 'grad_w', 'grad_w_qkv_b': 'grad_w', 'grad_w_o_b': 'grad_w', 'grad_ffn_w_gate': 'grad_w', 'grad_ffn_w_up': 'grad_w', 'grad_ffn_w_down': 'grad_w', 'grad_ln_g': 'grad_w', 'grad_ln_b': 'grad_w', 'delta_w_qkv_a': 'delta_w', 'delta_w_o_a': 'delta_w', 'delta_rel_bias': 'delta_w', 'delta_w_qkv_b': 'delta_w', 'delta_w_o_b': 'delta_w', 'delta_ffn_w_gate': 'delta_w', 'delta_ffn_w_up': 'delta_w', 'delta_ffn_w_down': 'delta_w', 'delta_ln_g': 'delta_w', 'delta_ln_b': 'delta_w', 'new_m_w_qkv_a': 'new_m', 'new_m_w_o_a': 'new_m', 'new_m_rel_bias': 'new_m', 'new_m_w_qkv_b': 'new_m', 'new_m_w_o_b': 'new_m', 'new_m_ffn_w_gate': 'new_m', 'new_m_ffn_w_up': 'new_m', 'new_m_ffn_w_down': 'new_m', 'new_m_ln_g': 'new_m', 'new_m_ln_b': 'new_m', 'new_v_w_qkv_a': 'new_v', 'new_v_w_o_a': 'new_v', 'new_v_rel_bias': 'new_v', 'new_v_w_qkv_b': 'new_v', 'new_v_w_o_b': 'new_v', 'new_v_ffn_w_gate': 'new_v', 'new_v_ffn_w_up': 'new_v', 'new_v_ffn_w_down': 'new_v', 'new_v_ln_g': 'new_v', 'new_v_ln_b': 'new_v'}


def _forward(args):
    return _fwd_reference(*[args[k] for k in FWD_PARAMS])


def _output_shape():
    out = _jax.eval_shape(lambda: _forward(_fwd_setup_inputs(0)))
    return out.shape, out.dtype

N_MICROBATCH = 1
ADAM_LR = 0.001
ADAM_B1 = 0.9
ADAM_B2 = 0.999
ADAM_EPS = 1e-08
ADAM_WD = 0.01
ADAM_STEP = 10
PER_EXAMPLE_BATCH_AXIS = {'x': 0, 'loss_target': 0}
SHARED_INPUTS = []
_WEIGHT_DTYPES = {'w_qkv_a': _jnp.float32, 'w_o_a': _jnp.float32, 'rel_bias': _jnp.float32, 'w_qkv_b': _jnp.float32, 'w_o_b': _jnp.float32, 'ffn_w_gate': _jnp.float32, 'ffn_w_up': _jnp.float32, 'ffn_w_down': _jnp.float32, 'ln_g': _jnp.float32, 'ln_b': _jnp.float32}
MOMENT_SCALE = {'w_qkv_a': 5.064904e-03, 'w_o_a': 7.742445e-03, 'rel_bias': 1.960704e-03, 'w_qkv_b': 1.744031e-02, 'w_o_b': 2.917245e-02, 'ffn_w_gate': 8.284503e-03, 'ffn_w_up': 8.025587e-03, 'ffn_w_down': 3.170916e-02, 'ln_g': 9.337687e+00, 'ln_b': 6.587186e-01}


def _to_microbatches(a, axis):
    t = _jnp.moveaxis(a, axis, 0)
    t = t.reshape((N_MICROBATCH, t.shape[0] // N_MICROBATCH) + t.shape[1:])
    return _jnp.moveaxis(t, 1, axis + 1)


def setup_inputs(seed: int = 0) -> dict:
    inp = _fwd_setup_inputs(seed)
    key = _jax.random.fold_in(_jax.random.key(seed), 7919)
    shape, _ = _output_shape()
    out = dict(inp)
    out["loss_target"] = _jax.random.normal(_jax.random.fold_in(key, 0), shape, _jnp.float32)
    for i, name in enumerate(TWIN_WEIGHTS):
        w = inp[name].astype(_jnp.float32)
        if MOMENT_SCALE is None:
            s = _jnp.sqrt(_jnp.mean(_jnp.square(w)) + 1e-30)
        else:
            s = MOMENT_SCALE[name]
        km, kv = _jax.random.split(_jax.random.fold_in(key, i + 1))
        out[name] = w
        out["m_" + name] = s * _jax.random.normal(km, w.shape, _jnp.float32)
        out["v_" + name] = (s * s) * _jax.random.uniform(kv, w.shape, _jnp.float32, 0.5, 1.5)
    if N_MICROBATCH > 1:
        for name, axis in PER_EXAMPLE_BATCH_AXIS.items():
            out[name] = _to_microbatches(out[name], axis)
    return {'x': out['x'], 'w_qkv_a': out['w_qkv_a'], 'w_o_a': out['w_o_a'], 'rel_bias': out['rel_bias'], 'w_qkv_b': out['w_qkv_b'], 'w_o_b': out['w_o_b'], 'ffn_w_gate': out['ffn_w_gate'], 'ffn_w_up': out['ffn_w_up'], 'ffn_w_down': out['ffn_w_down'], 'ln_g': out['ln_g'], 'ln_b': out['ln_b'], 'loss_target': out['loss_target'], 'm_w_qkv_a': out['m_w_qkv_a'], 'm_w_o_a': out['m_w_o_a'], 'm_rel_bias': out['m_rel_bias'], 'm_w_qkv_b': out['m_w_qkv_b'], 'm_w_o_b': out['m_w_o_b'], 'm_ffn_w_gate': out['m_ffn_w_gate'], 'm_ffn_w_up': out['m_ffn_w_up'], 'm_ffn_w_down': out['m_ffn_w_down'], 'm_ln_g': out['m_ln_g'], 'm_ln_b': out['m_ln_b'], 'v_w_qkv_a': out['v_w_qkv_a'], 'v_w_o_a': out['v_w_o_a'], 'v_rel_bias': out['v_rel_bias'], 'v_w_qkv_b': out['v_w_qkv_b'], 'v_w_o_b': out['v_w_o_b'], 'v_ffn_w_gate': out['v_ffn_w_gate'], 'v_ffn_w_up': out['v_ffn_w_up'], 'v_ffn_w_down': out['v_ffn_w_down'], 'v_ln_g': out['v_ln_g'], 'v_ln_b': out['v_ln_b']}


def _loss(weights, diff, rest, loss_target):
    with _jax.named_scope("forward"):
        args = {**rest, TWIN_DIFF_INPUT: diff, **{k: w.astype(_WEIGHT_DTYPES[k]) for k, w in weights.items()}}
        y = _forward(args)
    with _jax.named_scope("loss_head"):
        err = _jnp.square(y.astype(_jnp.float32) - loss_target)
        return 0.5 * _jnp.sum(_jnp.mean(err, axis=-1)) if err.ndim else 0.5 * err


def _adamw(w, g, m, v):
    m = ADAM_B1 * m + (1.0 - ADAM_B1) * g
    v = ADAM_B2 * v + (1.0 - ADAM_B2) * _jnp.square(g)
    m_hat = m / (1.0 - ADAM_B1 ** ADAM_STEP)
    v_hat = v / (1.0 - ADAM_B2 ** ADAM_STEP)
    delta = -ADAM_LR * (m_hat / (_jnp.sqrt(v_hat) + ADAM_EPS) + ADAM_WD * w)
    return delta, m, v


def reference(x, w_qkv_a, w_o_a, rel_bias, w_qkv_b, w_o_b, ffn_w_gate, ffn_w_up, ffn_w_down, ln_g, ln_b, loss_target, m_w_qkv_a, m_w_o_a, m_rel_bias, m_w_qkv_b, m_w_o_b, m_ffn_w_gate, m_ffn_w_up, m_ffn_w_down, m_ln_g, m_ln_b, v_w_qkv_a, v_w_o_a, v_rel_bias, v_w_qkv_b, v_w_o_b, v_ffn_w_gate, v_ffn_w_up, v_ffn_w_down, v_ln_g, v_ln_b):
    given = dict(x=x, w_qkv_a=w_qkv_a, w_o_a=w_o_a, rel_bias=rel_bias, w_qkv_b=w_qkv_b, w_o_b=w_o_b, ffn_w_gate=ffn_w_gate, ffn_w_up=ffn_w_up, ffn_w_down=ffn_w_down, ln_g=ln_g, ln_b=ln_b, loss_target=loss_target, m_w_qkv_a=m_w_qkv_a, m_w_o_a=m_w_o_a, m_rel_bias=m_rel_bias, m_w_qkv_b=m_w_qkv_b, m_w_o_b=m_w_o_b, m_ffn_w_gate=m_ffn_w_gate, m_ffn_w_up=m_ffn_w_up, m_ffn_w_down=m_ffn_w_down, m_ln_g=m_ln_g, m_ln_b=m_ln_b, v_w_qkv_a=v_w_qkv_a, v_w_o_a=v_w_o_a, v_rel_bias=v_rel_bias, v_w_qkv_b=v_w_qkv_b, v_w_o_b=v_w_o_b, v_ffn_w_gate=v_ffn_w_gate, v_ffn_w_up=v_ffn_w_up, v_ffn_w_down=v_ffn_w_down, v_ln_g=v_ln_g, v_ln_b=v_ln_b)
    weights = {n: given[n] for n in TWIN_WEIGHTS}
    shared = {n: given[n] for n in SHARED_INPUTS}
    per_example = {n: given[n] for n in ['x']}
    grad_fn = _jax.value_and_grad(_loss, argnums=(0, 1))

    def one_microbatch(ex, loss_target):
        ex = dict(ex)
        diff = ex.pop(TWIN_DIFF_INPUT)
        return grad_fn(weights, diff, {**shared, **ex}, loss_target)

    if N_MICROBATCH == 1:
        loss, (grad_w, grad_x) = one_microbatch(per_example, given["loss_target"])
    else:
        def body(carry, xs):
            loss_sum, grad_sum = carry
            l_k, (gw_k, gx_k) = one_microbatch(xs[0], xs[1])
            with _jax.named_scope("update"):
                return (loss_sum + l_k, _jax.tree.map(_jnp.add, grad_sum, gw_k)), gx_k

        init = (_jnp.zeros((), _jnp.float32), _jax.tree.map(_jnp.zeros_like, weights))
        (loss, grad_w), grad_x = _jax.lax.scan(body, init, (per_example, given["loss_target"]))
    with _jax.named_scope("update"):
        delta_w, new_m, new_v = {}, {}, {}
        for n in TWIN_WEIGHTS:
            delta_w[n], new_m[n], new_v[n] = _adamw(weights[n], grad_w[n], given["m_" + n], given["v_" + n])
    return (loss, grad_x, *[grad_w[n] for n in TWIN_WEIGHTS], *[delta_w[n] for n in TWIN_WEIGHTS],
            *[new_m[n] for n in TWIN_WEIGHTS], *[new_v[n] for n in TWIN_WEIGHTS])
```

```python
import functools

import jax
import jax.numpy as jnp
from jax import lax
from jax.experimental import pallas as pl
from jax.experimental.pallas import tpu as pltpu

BF16 = jnp.bfloat16
F32 = jnp.float32
MESH_ID = pl.DeviceIdType.MESH
ANY = pl.BlockSpec(memory_space=pl.ANY)

N_DEV = 8
LANE = 128
SUBLANE = 8
VMEM_LIMIT = 56 * 1024 * 1024

HEAD_DIM = 64
CHUNK = 64
LEFT_CHUNKS = 8
REL_CLIP = 128
N_REL = 2 * REL_CLIP + 1
REL_PAD = 384
LN_EPS = 1e-5
ADAM_LR, ADAM_B1, ADAM_B2, ADAM_EPS, ADAM_WD, ADAM_STEP = 0.001, 0.9, 0.999, 1e-08, 0.01, 10
NEG = -1e30

A_TQ = 128
A_NWB = LEFT_CHUNKS * CHUNK // A_TQ + 1
A_W = A_NWB * A_TQ
B_T = 256

NT_DIMS = (((1,), (1,)), ((), ()))
TN_DIMS = (((0,), (0,)), ((), ()))


def _tile(n, pref, unit=LANE):
    if n <= pref:
        return n
    t = pref - pref % unit
    while t > unit and n % t:
        t -= unit
    assert n % t == 0, (n, pref)
    return t


def _params(*sem):
    return pltpu.CompilerParams(dimension_semantics=sem, vmem_limit_bytes=VMEM_LIMIT)


def _split3(v):
    h = v.astype(BF16)
    r = v - h.astype(F32)
    m = r.astype(BF16)
    lo = (r - m.astype(F32)).astype(BF16)
    return h, m, lo


def _dot3(v, w):
    h, m, lo = _split3(v)
    return (jnp.dot(h, w, preferred_element_type=F32) + jnp.dot(m, w, preferred_element_type=F32)
            + jnp.dot(lo, w, preferred_element_type=F32))


def mm_nn(a, w, name):
    T, K = a.shape
    N = w.shape[1]
    tm, tn = _tile(T, 1024), _tile(N, 768)

    def body(a_ref, w_ref, o_ref):
        o_ref[...] = jnp.dot(a_ref[...], w_ref[...], preferred_element_type=F32).astype(o_ref.dtype)

    return pl.pallas_call(
        body, grid=(T // tm, N // tn),
        in_specs=[pl.BlockSpec((tm, K), lambda i, j: (i, 0)), pl.BlockSpec((K, tn), lambda i, j: (0, j))],
        out_specs=pl.BlockSpec((tm, tn), lambda i, j: (i, j)),
        out_shape=jax.ShapeDtypeStruct((T, N), BF16),
        compiler_params=_params("parallel", "parallel"), name=name)(a, w)


def ffn_up(xb, wg, wu, name):
    T, K = xb.shape
    N = wg.shape[1]
    tm, tn = _tile(T, 512), _tile(N, 768)

    def body(x_ref, wg_ref, wu_ref, h_ref, u_ref, a_ref):
        x = x_ref[...]
        h = jnp.dot(x, wg_ref[...], preferred_element_type=F32)
        u = jnp.dot(x, wu_ref[...], preferred_element_type=F32)
        h_ref[...] = h.astype(BF16)
        u_ref[...] = u.astype(BF16)
        a_ref[...] = (h * jax.nn.sigmoid(h) * u).astype(BF16)

    wspec = pl.BlockSpec((K, tn), lambda i, j: (0, j))
    ospec = pl.BlockSpec((tm, tn), lambda i, j: (i, j))
    return pl.pallas_call(
        body, grid=(T // tm, N // tn),
        in_specs=[pl.BlockSpec((tm, K), lambda i, j: (i, 0)), wspec, wspec],
        out_specs=[ospec, ospec, ospec],
        out_shape=[jax.ShapeDtypeStruct((T, N), BF16)] * 3,
        compiler_params=_params("parallel", "parallel"), name=name)(xb, wg, wu)


def mm_res_ln(a, w, x, g, b, alpha, scale, name):
    T, K = a.shape
    D = w.shape[1]
    tm = _tile(T, 256)

    def body(a_ref, w_ref, x_ref, g_ref, b_ref, z_ref, o_ref, ob_ref):
        y = jnp.dot(a_ref[...], w_ref[...], preferred_element_type=F32)
        z = alpha * x_ref[...] + scale * y
        mu = jnp.mean(z, axis=1, keepdims=True)
        zc = z - mu
        var = jnp.mean(zc * zc, axis=1, keepdims=True)
        o = zc * lax.rsqrt(var + LN_EPS) * g_ref[...] + b_ref[...]
        z_ref[...] = z
        o_ref[...] = o
        ob_ref[...] = o.astype(BF16)

    row = pl.BlockSpec((tm, D), lambda i: (i, 0))
    vec = pl.BlockSpec((1, D), lambda i: (0, 0))
    return pl.pallas_call(
        body, grid=(T // tm,),
        in_specs=[pl.BlockSpec((tm, K), lambda i: (i, 0)), pl.BlockSpec((K, D), lambda i: (0, 0)), row, vec, vec],
        out_specs=[row, row, row],
        out_shape=[jax.ShapeDtypeStruct((T, D), F32), jax.ShapeDtypeStruct((T, D), F32),
                   jax.ShapeDtypeStruct((T, D), BF16)],
        compiler_params=_params("parallel"), name=name)(a, w, x, g, b)


def ln_bwd(z, g, do, name):
    T, D = z.shape
    tm = _tile(T, 512)

    def body(z_ref, g_ref, do_ref, dz_ref, dzb_ref, dg_ref, db_ref):
        @pl.when(pl.program_id(0) == 0)
        def _():
            dg_ref[...] = jnp.zeros_like(dg_ref)
            db_ref[...] = jnp.zeros_like(db_ref)

        zv = z_ref[...]
        dov = do_ref[...]
        mu = jnp.mean(zv, axis=1, keepdims=True)
        zc = zv - mu
        var = jnp.mean(zc * zc, axis=1, keepdims=True)
        rstd = lax.rsqrt(var + LN_EPS)
        xhat = zc * rstd
        dxhat = dov * g_ref[...]
        m1 = jnp.mean(dxhat, axis=1, keepdims=True)
        m2 = jnp.mean(dxhat * xhat, axis=1, keepdims=True)
        dz = rstd * (dxhat - m1 - xhat * m2)
        dz_ref[...] = dz
        dzb_ref[...] = dz.astype(BF16)
        dg_ref[...] += jnp.sum(dov * xhat, axis=0, keepdims=True)
        db_ref[...] += jnp.sum(dov, axis=0, keepdims=True)

    row = pl.BlockSpec((tm, D), lambda i: (i, 0))
    vec = pl.BlockSpec((1, D), lambda i: (0, 0))
    return pl.pallas_call(
        body, grid=(T // tm,), in_specs=[row, vec, row], out_specs=[row, row, vec, vec],
        out_shape=[jax.ShapeDtypeStruct((T, D), F32), jax.ShapeDtypeStruct((T, D), BF16),
                   jax.ShapeDtypeStruct((1, D), F32), jax.ShapeDtypeStruct((1, D), F32)],
        compiler_params=_params("arbitrary"), name=name)(z, g, do)


def mm_nt(a, w, name):
    T, K = a.shape
    N = w.shape[0]
    tm, tn = _tile(T, 1024), _tile(N, 512)

    def body(a_ref, w_ref, o_ref):
        o_ref[...] = lax.dot_general(a_ref[...], w_ref[...], NT_DIMS, preferred_element_type=F32).astype(o_ref.dtype)

    return pl.pallas_call(
        body, grid=(T // tm, N // tn),
        in_specs=[pl.BlockSpec((tm, K), lambda i, j: (i, 0)), pl.BlockSpec((tn, K), lambda i, j: (j, 0))],
        out_specs=pl.BlockSpec((tm, tn), lambda i, j: (i, j)),
        out_shape=jax.ShapeDtypeStruct((T, N), BF16),
        compiler_params=_params("parallel", "parallel"), name=name)(a, w)


def ffn_bwd_mid(dzb, wd, h, u, scale, name):
    T, K = dzb.shape
    N = wd.shape[0]
    tm, tn = _tile(T, 512), _tile(N, 768)

    def body(dz_ref, w_ref, h_ref, u_ref, dh_ref, du_ref):
        da = scale * lax.dot_general(dz_ref[...], w_ref[...], NT_DIMS, preferred_element_type=F32)
        hv = h_ref[...].astype(F32)
        uv = u_ref[...].astype(F32)
        s = jax.nn.sigmoid(hv)
        silu = hv * s
        dh_ref[...] = (da * uv * (s + silu * (1.0 - s))).astype(BF16)
        du_ref[...] = (da * silu).astype(BF16)

    tile = pl.BlockSpec((tm, tn), lambda i, j: (i, j))
    return pl.pallas_call(
        body, grid=(T // tm, N // tn),
        in_specs=[pl.BlockSpec((tm, K), lambda i, j: (i, 0)), pl.BlockSpec((tn, K), lambda i, j: (j, 0)), tile, tile],
        out_specs=[tile, tile],
        out_shape=[jax.ShapeDtypeStruct((T, N), BF16)] * 2,
        compiler_params=_params("parallel", "parallel"), name=name)(dzb, wd, h, u)


def mm_nt_res(pairs, dz, alpha, name):
    T, N = pairs[0][0].shape
    D = pairs[0][1].shape[0]
    n = len(pairs)
    tm, tk = _tile(T, 512), _tile(N, 768)
    nk = N // tk

    def body(*refs):
        a_refs, w_refs = refs[:n], refs[n:2 * n]
        dz_ref, o_ref, acc_ref = refs[2 * n:]
        k = pl.program_id(1)

        @pl.when(k == 0)
        def _():
            acc_ref[...] = jnp.zeros_like(acc_ref)

        part = lax.dot_general(a_refs[0][...], w_refs[0][...], NT_DIMS, preferred_element_type=F32)
        for p in range(1, n):
            part += lax.dot_general(a_refs[p][...], w_refs[p][...], NT_DIMS, preferred_element_type=F32)
        acc_ref[...] += part

        @pl.when(k == nk - 1)
        def _():
            o_ref[...] = acc_ref[...] + alpha * dz_ref[...]

    row = pl.BlockSpec((tm, D), lambda i, k: (i, 0))
    return pl.pallas_call(
        body, grid=(T // tm, nk),
        in_specs=[pl.BlockSpec((tm, tk), lambda i, k: (i, k))] * n + [pl.BlockSpec((D, tk), lambda i, k: (0, k))] * n + [row],
        out_specs=row,
        out_shape=jax.ShapeDtypeStruct((T, D), F32),
        scratch_shapes=[pltpu.VMEM((tm, D), F32)],
        compiler_params=_params("parallel", "arbitrary"), name=name)(*[p[0] for p in pairs], *[p[1] for p in pairs], dz)


def mm_tn(a, b, scale, shard_cols, name):
    T, M = a.shape
    N = b.shape[1]
    tm, tk = _tile(M, 512), _tile(T, 512)
    nk = T // tk
    if shard_cols:
        ns = N // N_DEV
        per = 2 if (2 * ns) % 256 == 0 else 1
        tn = per * ns
        out_shape = jax.ShapeDtypeStruct((N_DEV, M, ns), F32)
        out_spec = pl.BlockSpec((per, tm, ns), lambda i, j, k: (j, i, 0))
    else:
        tn = _tile(N, 1024)
        out_shape = jax.ShapeDtypeStruct((M, N), F32)
        out_spec = pl.BlockSpec((tm, tn), lambda i, j, k: (i, j))

    def body(a_ref, b_ref, o_ref, acc_ref):
        k = pl.program_id(2)

        @pl.when(k == 0)
        def _():
            acc_ref[...] = jnp.zeros_like(acc_ref)

        acc_ref[...] += lax.dot_general(a_ref[...], b_ref[...], TN_DIMS, preferred_element_type=F32)

        @pl.when(k == nk - 1)
        def _():
            if shard_cols:
                for s in range(per):
                    o_ref[s] = scale * acc_ref[:, s * ns:(s + 1) * ns]
            else:
                o_ref[...] = scale * acc_ref[...]

    return pl.pallas_call(
        body, grid=(M // tm, N // tn, nk),
        in_specs=[pl.BlockSpec((tk, tm), lambda i, j, k: (k, i)), pl.BlockSpec((tk, tn), lambda i, j, k: (k, j))],
        out_specs=out_spec, out_shape=out_shape,
        scratch_shapes=[pltpu.VMEM((tm, tn), F32)],
        compiler_params=_params("parallel", "parallel", "arbitrary"), name=name)(a, b)


def loss_head(y, target, name):
    T, D = y.shape
    tm = _tile(T, 512)

    def body(y_ref, t_ref, l_ref, dy_ref):
        @pl.when(pl.program_id(0) == 0)
        def _():
            l_ref[...] = jnp.zeros_like(l_ref)

        e = y_ref[...] - t_ref[...]
        dy_ref[...] = e * (1.0 / D)
        rows = jnp.sum(e * e, axis=1, keepdims=True) * (0.5 / D)
        l_ref[...] += jnp.sum(rows, axis=0, keepdims=True)

    row = pl.BlockSpec((tm, D), lambda i: (i, 0))
    return pl.pallas_call(
        body, grid=(T // tm,), in_specs=[row, row],
        out_specs=[pl.BlockSpec((1, 1), lambda i: (0, 0)), row],
        out_shape=[jax.ShapeDtypeStruct((1, 1), F32), jax.ShapeDtypeStruct((T, D), F32)],
        compiler_params=_params("arbitrary"), name=name)(y, target)


def _rel_index(i, j):
    return jnp.clip(i - j + LEFT_CHUNKS * CHUNK, -REL_CLIP, REL_CLIP) + REL_CLIP


def bias_band(table_t, name):
    H = table_t.shape[0]
    rows = SUBLANE

    def body(t_ref, o_ref):
        i0 = pl.program_id(0) * rows
        parts = _split3(t_ref[...])
        r = lax.broadcasted_iota(jnp.int32, (REL_PAD, A_W), 0)
        j = lax.broadcasted_iota(jnp.int32, (REL_PAD, A_W), 1)
        for ii in range(rows):
            onehot = jnp.where(r == _rel_index(i0 + ii, j), 1.0, 0.0).astype(BF16)
            o_ref[ii] = sum(jnp.dot(p, onehot, preferred_element_type=F32) for p in parts)

    return pl.pallas_call(
        body, grid=(A_TQ // rows,),
        in_specs=[pl.BlockSpec((H, REL_PAD), lambda i: (0, 0))],
        out_specs=pl.BlockSpec((rows, H, A_W), lambda i: (i, 0, 0)),
        out_shape=jax.ShapeDtypeStruct((A_TQ, H, A_W), F32),
        compiler_params=_params("parallel"), name=name)(table_t)


def bias_band_bwd(dbands, name):
    H = dbands[0].shape[1]
    rows = SUBLANE
    n = len(dbands)

    def body(*refs):
        g_refs, o_ref = refs[:n], refs[n]

        @pl.when(pl.program_id(0) == 0)
        def _():
            o_ref[...] = jnp.zeros_like(o_ref)

        i0 = pl.program_id(0) * rows
        j = lax.broadcasted_iota(jnp.int32, (A_W, REL_PAD), 0)
        r = lax.broadcasted_iota(jnp.int32, (A_W, REL_PAD), 1)
        acc = jnp.zeros((H, REL_PAD), F32)
        for ii in range(rows):
            onehot = jnp.where(r == _rel_index(i0 + ii, j), 1.0, 0.0).astype(BF16)
            g = g_refs[0][ii]
            for q in range(1, n):
                g = g + g_refs[q][ii]
            acc += _dot3(g, onehot)
        o_ref[...] += acc

    spec = pl.BlockSpec((rows, H, A_W), lambda i: (i, 0, 0))
    return pl.pallas_call(
        body, grid=(A_TQ // rows,), in_specs=[spec] * n,
        out_specs=pl.BlockSpec((H, REL_PAD), lambda i: (0, 0)),
        out_shape=jax.ShapeDtypeStruct((H, REL_PAD), F32),
        compiler_params=_params("arbitrary"), name=name)(*dbands)


def _a_window(ref, qi):
    parts = []
    for d in range(A_NWB):
        kb = jnp.maximum(qi - (A_NWB - 1) + d, 0)
        parts.append(ref[pl.ds(pl.multiple_of(kb * A_TQ, A_TQ), A_TQ), :])
    return jnp.concatenate(parts, axis=0)


def _a_valid(qi):
    i = lax.broadcasted_iota(jnp.int32, (A_TQ, A_W), 0)
    j = lax.broadcasted_iota(jnp.int32, (A_TQ, A_W), 1)
    ic, jc = i // CHUNK, j // CHUNK
    return (jc >= ic) & (jc <= ic + LEFT_CHUNKS) & (j >= LEFT_CHUNKS * CHUNK - qi * A_TQ)


def _head_masks():
    lane = lax.broadcasted_iota(jnp.int32, (1, LANE), 1)
    return [lane < HEAD_DIM, lane >= HEAD_DIM]


def _a_probs(qm, kw, bias, valid):
    s = lax.dot_general(qm, kw, NT_DIMS, preferred_element_type=F32) * (HEAD_DIM ** -0.5) + bias
    s = jnp.where(valid, s, NEG)
    p = jnp.exp(s - jnp.max(s, axis=1, keepdims=True))
    return p / jnp.sum(p, axis=1, keepdims=True)


def _attn_specs(S, D, tq):
    hp_n = D // LANE
    nq = S // tq
    q = pl.BlockSpec((tq, LANE), lambda hp, b, qi: (b * nq + qi, hp))
    k = pl.BlockSpec((S, LANE), lambda hp, b, qi: (b, hp_n + hp))
    v = pl.BlockSpec((S, LANE), lambda hp, b, qi: (b, 2 * hp_n + hp))
    tile = pl.BlockSpec((tq, LANE), lambda hp, b, qi: (b * nq + qi, hp))
    seq = pl.BlockSpec((S, LANE), lambda hp, b, qi: (b, hp))
    return q, k, v, tile, seq


def attn_a_fwd(qkv, band, S, name):
    T, D3 = qkv.shape
    D = D3 // 3
    nb, nq = T // S, S // A_TQ

    def body(q_ref, k_ref, v_ref, b_ref, o_ref):
        qi = pl.program_id(2)
        q2 = q_ref[...]
        kw, vw = _a_window(k_ref, qi), _a_window(v_ref, qi)
        valid = _a_valid(qi)
        o = jnp.zeros((A_TQ, LANE), F32)
        for hh, hm in enumerate(_head_masks()):
            p = _a_probs(jnp.where(hm, q2, jnp.zeros_like(q2)), kw, b_ref[hh], valid)
            o = jnp.where(hm, jnp.dot(p.astype(BF16), vw, preferred_element_type=F32), o)
        o_ref[...] = o.astype(BF16)

    q, k, v, tile, _ = _attn_specs(S, D, A_TQ)
    return pl.pallas_call(
        body, grid=(D // LANE, nb, nq),
        in_specs=[q, k, v, pl.BlockSpec((2, A_TQ, A_W), lambda hp, b, qi: (hp, 0, 0))],
        out_specs=tile, out_shape=jax.ShapeDtypeStruct((T, D), BF16),
        compiler_params=_params("parallel", "parallel", "parallel"), name=name)(qkv, qkv, qkv, band)


def attn_a_bwd(qkv, band, do, S, name):
    T, D3 = qkv.shape
    D = D3 // 3
    nb, nq = T // S, S // A_TQ
    scale = HEAD_DIM ** -0.5

    def body(q_ref, k_ref, v_ref, b_ref, do_ref, dq_ref, dk_ref, dv_ref, db_ref, dk_acc, dv_acc):
        b, qi = pl.program_id(1), pl.program_id(2)

        @pl.when((b == 0) & (qi == 0))
        def _():
            db_ref[...] = jnp.zeros_like(db_ref)

        @pl.when(qi == 0)
        def _():
            dk_acc[...] = jnp.zeros_like(dk_acc)
            dv_acc[...] = jnp.zeros_like(dv_acc)

        q2, do2 = q_ref[...], do_ref[...]
        kw, vw = _a_window(k_ref, qi), _a_window(v_ref, qi)
        valid = _a_valid(qi)
        dq = jnp.zeros((A_TQ, LANE), F32)
        dkw = jnp.zeros((A_W, LANE), F32)
        dvw = jnp.zeros((A_W, LANE), F32)
        for hh, hm in enumerate(_head_masks()):
            qm = jnp.where(hm, q2, jnp.zeros_like(q2))
            dom = jnp.where(hm, do2, jnp.zeros_like(do2))
            p = _a_probs(qm, kw, b_ref[hh], valid)
            dp = lax.dot_general(dom, vw, NT_DIMS, preferred_element_type=F32)
            ds = p * (dp - jnp.sum(p * dp, axis=1, keepdims=True))
            db_ref[hh] += ds
            dsb = ds.astype(BF16)
            dq = jnp.where(hm, jnp.dot(dsb, kw, preferred_element_type=F32) * scale, dq)
            dkw += lax.dot_general(dsb, qm, TN_DIMS, preferred_element_type=F32) * scale
            dvw += lax.dot_general(p.astype(BF16), dom, TN_DIMS, preferred_element_type=F32)
        dq_ref[...] = dq.astype(BF16)
        for d in range(A_NWB):
            kb = jnp.maximum(qi - (A_NWB - 1) + d, 0)
            rows = pl.ds(pl.multiple_of(kb * A_TQ, A_TQ), A_TQ)
            dk_acc[rows, :] += dkw[d * A_TQ:(d + 1) * A_TQ]
            dv_acc[rows, :] += dvw[d * A_TQ:(d + 1) * A_TQ]

        @pl.when(qi == nq - 1)
        def _():
            dk_ref[...] = dk_acc[...].astype(BF16)
            dv_ref[...] = dv_acc[...].astype(BF16)

    q, k, v, tile, seq = _attn_specs(S, D, A_TQ)
    bspec = pl.BlockSpec((2, A_TQ, A_W), lambda hp, b, qi: (hp, 0, 0))
    act = jax.ShapeDtypeStruct((T, D), BF16)
    return pl.pallas_call(
        body, grid=(D // LANE, nb, nq),
        in_specs=[q, k, v, bspec, tile], out_specs=[tile, seq, seq, bspec],
        out_shape=[act, act, act, jax.ShapeDtypeStruct(band.shape, F32)],
        scratch_shapes=[pltpu.VMEM((S, LANE), F32), pltpu.VMEM((S, LANE), F32)],
        compiler_params=_params("arbitrary", "arbitrary", "arbitrary"), name=name)(qkv, qkv, qkv, band, do)


def _b_logits(qm, kt, qi, kb):
    z = lax.dot_general(qm, kt, NT_DIMS, preferred_element_type=F32) * (HEAD_DIM ** -0.5)
    row = lax.broadcasted_iota(jnp.int32, (B_T, B_T), 0)
    col = lax.broadcasted_iota(jnp.int32, (B_T, B_T), 1)
    causal = (kb * B_T + col) < (qi * B_T + row)
    soft = jnp.log(1.0 + jnp.exp(-jnp.abs(z)))
    lb = jnp.minimum(z, 0.0) - soft
    l1m = jnp.where(causal, jnp.minimum(-z, 0.0) - soft, 0.0)
    return z, lb, l1m, causal


def _tri(strict_lower):
    r = lax.broadcasted_iota(jnp.int32, (B_T, B_T), 0)
    c = lax.broadcasted_iota(jnp.int32, (B_T, B_T), 1)
    return jnp.where((r > c) if strict_lower else (r < c), 1.0, 0.0).astype(BF16)


def attn_b_fwd(qkv, S, name):
    T, D3 = qkv.shape
    D = D3 // 3
    nb, nq = T // S, S // B_T

    def body(q_ref, k_ref, v_ref, o_ref):
        qi = pl.program_id(2)
        q2 = q_ref[...]
        after = _tri(True)
        masks = _head_masks()
        qms = [jnp.where(hm, q2, jnp.zeros_like(q2)) for hm in masks]

        def step(it, carry):
            kb = qi - it
            rows = pl.ds(pl.multiple_of(kb * B_T, B_T), B_T)
            kt, vt = k_ref[rows, :], v_ref[rows, :]
            out = []
            for hh in range(2):
                right, acc = carry[2 * hh], carry[2 * hh + 1]
                _, lb, l1m, causal = _b_logits(qms[hh], kt, qi, kb)
                stick = right + _dot3(l1m, after)
                a = jnp.where(causal, jnp.exp(lb + stick), 0.0)
                acc = acc + jnp.dot(a.astype(BF16), vt, preferred_element_type=F32)
                out += [right + jnp.sum(l1m, axis=1, keepdims=True), acc]
            return tuple(out)

        init = (jnp.zeros((B_T, 1), F32), jnp.zeros((B_T, LANE), F32)) * 2
        res = lax.fori_loop(0, qi + 1, step, init)
        o_ref[...] = jnp.where(masks[0], res[1], res[3]).astype(BF16)

    q, k, v, tile, _ = _attn_specs(S, D, B_T)
    return pl.pallas_call(
        body, grid=(D // LANE, nb, nq), in_specs=[q, k, v], out_specs=tile,
        out_shape=jax.ShapeDtypeStruct((T, D), BF16),
        compiler_params=_params("parallel", "parallel", "parallel"), name=name)(qkv, qkv, qkv)


def attn_b_bwd(qkv, do, S, name):
    T, D3 = qkv.shape
    D = D3 // 3
    nb, nq = T // S, S // B_T
    scale = HEAD_DIM ** -0.5

    def body(q_ref, k_ref, v_ref, do_ref, dq_ref, dk_ref, dv_ref, dk_acc, dv_acc, z_s, g_s):
        qi = pl.program_id(2)

        @pl.when(qi == 0)
        def _():
            dk_acc[...] = jnp.zeros_like(dk_acc)
            dv_acc[...] = jnp.zeros_like(dv_acc)

        q2, do2 = q_ref[...], do_ref[...]
        after, before = _tri(True), _tri(False)
        dq = jnp.zeros((B_T, LANE), F32)
        for hm in _head_masks():
            qm = jnp.where(hm, q2, jnp.zeros_like(q2))
            dom = jnp.where(hm, do2, jnp.zeros_like(do2))

            def sweep_left(it, right):
                kb = qi - it
                rows = pl.ds(pl.multiple_of(kb * B_T, B_T), B_T)
                kt, vt = k_ref[rows, :], v_ref[rows, :]
                z, lb, l1m, causal = _b_logits(qm, kt, qi, kb)
                a = jnp.where(causal, jnp.exp(lb + right + _dot3(l1m, after)), 0.0)
                da = lax.dot_general(dom, vt, NT_DIMS, preferred_element_type=F32)
                z_s[kb] = z
                g_s[kb] = da * a
                dv_acc[rows, :] += lax.dot_general(a.astype(BF16), dom, TN_DIMS, preferred_element_type=F32)
                return right + jnp.sum(l1m, axis=1, keepdims=True)

            lax.fori_loop(0, qi + 1, sweep_left, jnp.zeros((B_T, 1), F32))

            def sweep_right(kb, carry):
                left, dq_h = carry
                rows = pl.ds(pl.multiple_of(kb * B_T, B_T), B_T)
                kt = k_ref[rows, :]
                z, g = z_s[kb], g_s[kb]
                row = lax.broadcasted_iota(jnp.int32, (B_T, B_T), 0)
                col = lax.broadcasted_iota(jnp.int32, (B_T, B_T), 1)
                causal = (kb * B_T + col) < (qi * B_T + row)
                beta = jax.nn.sigmoid(z)
                dz = jnp.where(causal, g * (1.0 - beta) - beta * (left + _dot3(g, before)), 0.0)
                dzb = dz.astype(BF16)
                dk_acc[rows, :] += lax.dot_general(dzb, qm, TN_DIMS, preferred_element_type=F32) * scale
                dq_h = dq_h + jnp.dot(dzb, kt, preferred_element_type=F32) * scale
                return left + jnp.sum(g, axis=1, keepdims=True), dq_h

            _, dq_h = lax.fori_loop(0, qi + 1, sweep_right, (jnp.zeros((B_T, 1), F32), jnp.zeros((B_T, LANE), F32)))
            dq = jnp.where(hm, dq_h, dq)
        dq_ref[...] = dq.astype(BF16)

        @pl.when(qi == nq - 1)
        def _():
            dk_ref[...] = dk_acc[...].astype(BF16)
            dv_ref[...] = dv_acc[...].astype(BF16)

    q, k, v, tile, seq = _attn_specs(S, D, B_T)
    act = jax.ShapeDtypeStruct((T, D), BF16)
    return pl.pallas_call(
        body, grid=(D // LANE, nb, nq),
        in_specs=[q, k, v, tile], out_specs=[tile, seq, seq], out_shape=[act, act, act],
        scratch_shapes=[pltpu.VMEM((S, LANE), F32), pltpu.VMEM((S, LANE), F32),
                        pltpu.VMEM((nq, B_T, B_T), F32), pltpu.VMEM((nq, B_T, B_T), F32)],
        compiler_params=_params("arbitrary", "arbitrary", "arbitrary"), name=name)(qkv, qkv, qkv, do)


def _place():
    return lax.axis_index("x"), lax.axis_index("y"), lax.axis_index("c")


def _other_chips(x, y):
    return [(1 - x, y), (x, 1 - y), (1 - x, 1 - y)]


def gather_weights(shards, by_cols, name):
    n = len(shards)
    out_shape = [jax.ShapeDtypeStruct((s.shape[0], N_DEV * s.shape[1]) if col else (N_DEV * s.shape[0], s.shape[1]), s.dtype)
                 for s, col in zip(shards, by_cols)]

    def body(*refs):
        ins, outs = refs[:n], refs[n:2 * n]
        send_sems, recv_sems, local_sems = refs[2 * n:]
        x, y, c = _place()
        me, sibling = (x, y, c), (x, y, 1 - c)
        chips = _other_chips(x, y)

        def block(t, px, py, pc):
            j = 4 * px + 2 * py + pc
            r, w = shards[t].shape
            if by_cols[t]:
                return outs[t].at[:, pl.ds(pl.multiple_of(j * w, LANE), w)]
            return outs[t].at[pl.ds(pl.multiple_of(j * r, SUBLANE), r), :]

        def copy(t, k, owner, to, src=None):
            return pltpu.make_async_remote_copy(
                src_ref=block(t, *owner) if src is None else src, dst_ref=block(t, *owner),
                send_sem=send_sems.at[7 * t + k], recv_sem=recv_sems.at[7 * t + k],
                device_id=to, device_id_type=MESH_ID)

        mine = [pltpu.make_async_copy(ins[t], block(t, *me), local_sems.at[t]) for t in range(n)]
        for cp in mine:
            cp.start()
        first = []
        for t in range(n):
            first.append(copy(t, 0, me, sibling, src=ins[t]))
            first += [copy(t, 1 + j, me, (*chip, c), src=ins[t]) for j, chip in enumerate(chips)]
        for cp in first:
            cp.start()
        passed = []
        for j, chip in enumerate(chips):
            for t in range(n):
                copy(t, 1 + j, (*chip, c), me).wait_recv()
                passed.append(copy(t, 4 + j, (*chip, c), sibling))
                passed[-1].start()
        for t in range(n):
            copy(t, 0, sibling, me).wait_recv()
            for j, chip in enumerate(chips):
                copy(t, 4 + j, (*chip, 1 - c), me).wait_recv()
        for cp in first + passed:
            cp.wait_send()
        for cp in mine:
            cp.wait()

    return pl.pallas_call(
        body, in_specs=[ANY] * n, out_specs=[ANY] * n, out_shape=out_shape,
        scratch_shapes=[pltpu.SemaphoreType.DMA((7 * n,)), pltpu.SemaphoreType.DMA((7 * n,)),
                        pltpu.SemaphoreType.DMA((n,))],
        name=name)(*shards)


def rs_sibling(grads, name):
    n = len(grads)

    def body(*refs):
        ins, outs = refs[:n], refs[n:2 * n]
        send_sems, recv_sems = refs[2 * n:]
        x, y, c = _place()
        copies = [pltpu.make_async_remote_copy(
            src_ref=ins[t].at[:, 1 - c], dst_ref=outs[t], send_sem=send_sems.at[t], recv_sem=recv_sems.at[t],
            device_id=(x, y, 1 - c), device_id_type=MESH_ID) for t in range(n)]
        for cp in copies:
            cp.start()
        for cp in copies:
            cp.wait()

    return pl.pallas_call(
        body, in_specs=[ANY] * n, out_specs=[ANY] * n,
        out_shape=[jax.ShapeDtypeStruct((4,) + g.shape[2:], F32) for g in grads],
        scratch_shapes=[pltpu.SemaphoreType.DMA((n,)), pltpu.SemaphoreType.DMA((n,))],
        name=name)(*grads)


def rs_add(g4, recv, core, name):
    _, _, R, C = g4.shape
    tr = _tile(R, 512, SUBLANE)

    def body(c_ref, g_ref, r_ref, o_ref):
        o_ref[...] = g_ref[...] + r_ref[...]

    return pl.pallas_call(
        body,
        grid_spec=pltpu.PrefetchScalarGridSpec(
            num_scalar_prefetch=1, grid=(4, R // tr),
            in_specs=[pl.BlockSpec((None, None, tr, C), lambda i, r, c_ref: (i, c_ref[0], r, 0)),
                      pl.BlockSpec((None, tr, C), lambda i, r, c_ref: (i, r, 0))],
            out_specs=pl.BlockSpec((None, tr, C), lambda i, r, c_ref: (i, r, 0))),
        out_shape=jax.ShapeDtypeStruct((4, R, C), F32),
        compiler_params=_params("parallel", "parallel"), name=name)(core, g4, recv)


def rs_chips(sums, name):
    n = len(sums)

    def body(*refs):
        ins, outs = refs[:n], refs[n:2 * n]
        send_sems, recv_sems = refs[2 * n:]
        x, y, c = _place()
        copies = []
        for t in range(n):
            for k, (px, py) in enumerate(_other_chips(x, y)):
                copies.append(pltpu.make_async_remote_copy(
                    src_ref=ins[t].at[2 * px + py], dst_ref=outs[t].at[k],
                    send_sem=send_sems.at[3 * t + k], recv_sem=recv_sems.at[3 * t + k],
                    device_id=(px, py, c), device_id_type=MESH_ID))
        for cp in copies:
            cp.start()
        for cp in copies:
            cp.wait()

    return pl.pallas_call(
        body, in_specs=[ANY] * n, out_specs=[ANY] * n,
        out_shape=[jax.ShapeDtypeStruct((3,) + s.shape[1:], F32) for s in sums],
        scratch_shapes=[pltpu.SemaphoreType.DMA((3 * n,)), pltpu.SemaphoreType.DMA((3 * n,))],
        name=name)(*sums)


def gather_small(v, name):
    R, C = v.shape

    def body(v_ref, o_ref, send_sems, recv_sems):
        x, y, c = _place()
        o_ref[4 * x + 2 * y + c] = v_ref[...]
        peers = []
        for r in range(1, N_DEV):
            px = 1 - x if r & 4 else x
            py = 1 - y if r & 2 else y
            pc = 1 - c if r & 1 else c
            peers.append((px, py, pc))

        def copy(r, owner, to):
            slot = o_ref.at[4 * owner[0] + 2 * owner[1] + owner[2]]
            return pltpu.make_async_remote_copy(
                src_ref=slot, dst_ref=slot, send_sem=send_sems.at[r], recv_sem=recv_sems.at[r],
                device_id=to, device_id_type=MESH_ID)

        sends = [copy(r, (x, y, c), peer) for r, peer in enumerate(peers)]
        for cp in sends:
            cp.start()
        for r, peer in enumerate(peers):
            copy(r, peer, (x, y, c)).wait_recv()
        for cp in sends:
            cp.wait_send()

    vm = pl.BlockSpec(memory_space=pltpu.VMEM)
    return pl.pallas_call(
        body, in_specs=[vm], out_specs=vm, out_shape=jax.ShapeDtypeStruct((N_DEV, R, C), F32),
        scratch_shapes=[pltpu.SemaphoreType.DMA((N_DEV - 1,)), pltpu.SemaphoreType.DMA((N_DEV - 1,))],
        name=name)(v)


def sum_devices(g, name):
    _, R, C = g.shape

    def body(g_ref, o_ref):
        acc = g_ref[0]
        for j in range(1, N_DEV):
            acc = acc + g_ref[j]
        o_ref[...] = acc

    vm = pl.BlockSpec(memory_space=pltpu.VMEM)
    return pl.pallas_call(body, in_specs=[vm], out_specs=vm, out_shape=jax.ShapeDtypeStruct((R, C), F32), name=name)(g)


def adamw(w, m, v, index, parts, name):
    _, R, C = w.shape
    tr = _tile(R, 512, SUBLANE)
    n = len(parts)

    def body(*refs):
        w_ref, m_ref, v_ref = refs[:3]
        g_refs = refs[3:3 + n]
        g_out, d_out, m_out, v_out = refs[3 + n:]
        g = g_refs[0][...]
        for q in range(1, n):
            g = g + g_refs[q][...]
        mn = ADAM_B1 * m_ref[...] + (1.0 - ADAM_B1) * g
        vn = ADAM_B2 * v_ref[...] + (1.0 - ADAM_B2) * (g * g)
        m_hat = mn / (1.0 - ADAM_B1 ** ADAM_STEP)
        v_hat = vn / (1.0 - ADAM_B2 ** ADAM_STEP)
        g_out[...] = g
        d_out[...] = -ADAM_LR * (m_hat / (jnp.sqrt(v_hat) + ADAM_EPS) + ADAM_WD * w_ref[...])
        m_out[...] = mn
        v_out[...] = vn

    slab = pl.BlockSpec((None, tr, C), lambda r: (index, r, 0))
    flat = pl.BlockSpec((tr, C), lambda r: (r, 0))
    return pl.pallas_call(
        body, grid=(R // tr,), in_specs=[slab] * 3 + [flat] * n, out_specs=[flat] * 4,
        out_shape=[jax.ShapeDtypeStruct((R, C), F32)] * 4,
        compiler_params=_params("parallel"), name=name)(w, m, v, *parts)


def _pad_to(a, axis, size):
    pad = [(0, 0)] * a.ndim
    pad[axis] = (0, size - a.shape[axis])
    return jnp.pad(a, pad)


def kernel(x, w_qkv_a, w_o_a, rel_bias, w_qkv_b, w_o_b, ffn_w_gate, ffn_w_up, ffn_w_down, ln_g, ln_b, loss_target, m_w_qkv_a, m_w_o_a, m_rel_bias, m_w_qkv_b, m_w_o_b, m_ffn_w_gate, m_ffn_w_up, m_ffn_w_down, m_ln_g, m_ln_b, v_w_qkv_a, v_w_o_a, v_rel_bias, v_w_qkv_b, v_w_o_b, v_ffn_w_gate, v_ffn_w_up, v_ffn_w_down, v_ln_g, v_ln_b):
    nb, S, D = x.shape
    T = nb * S
    depth = ffn_w_gate.shape[0]
    H = D // HEAD_DIM
    fs = ffn_w_gate.shape[-1]
    fp = -(-fs // LANE) * LANE
    alpha = (2.0 * depth) ** 0.25
    cx, cy, cc = _place()
    me = 4 * cx + 2 * cy + cc
    core = jnp.reshape(cc, (1,)).astype(jnp.int32)

    ln_local = jnp.concatenate([ln_g.reshape(depth * 3, -1), ln_b.reshape(depth * 3, -1)], axis=0)
    ln_all = gather_small(ln_local, "gather_ln")
    ln_full = jnp.transpose(ln_all, (1, 0, 2)).reshape(2 * depth * 3, D)
    ln_gain = lambda i, s: ln_full[3 * i + s][None, :]
    ln_bias = lambda i, s: ln_full[3 * depth + 3 * i + s][None, :]

    table_t = _pad_to(rel_bias.T, 1, REL_PAD)
    band = jnp.transpose(bias_band(table_t, "bias_band"), (1, 0, 2))

    xf = x.reshape(T, D)
    act, act_b = xf, xf.astype(BF16)
    saved = []
    weights = []
    for i in range(depth):
        layer = {}
        for s in (0, 1, 2):
            tag = f"L{i}S{s}"
            if s == 1:
                wq, wo = (w_qkv_a, w_o_a) if i % 2 == 0 else (w_qkv_b, w_o_b)
                j = i // 2
                wqkv_f, wo_f = gather_weights([wq[j].astype(BF16), wo[j].astype(BF16)], [True, False], "gather_mix_" + tag)
                qkv = mm_nn(act_b, wqkv_f, "qkv_" + tag)
                if i % 2 == 0:
                    att = attn_a_fwd(qkv, band, S, "attn_a_fwd_" + tag)
                else:
                    att = attn_b_fwd(qkv, S, "attn_b_fwd_" + tag)
                z, o, ob = mm_res_ln(att, wo_f, act, ln_gain(i, s), ln_bias(i, s), alpha, 1.0, "out_ln_" + tag)
                layer[s] = dict(x_b=act_b, qkv=qkv, att=att, z=z, wqkv=wqkv_f, wo=wo_f)
            else:
                f = 0 if s == 0 else 1
                wg_f, wu_f, wd_f = gather_weights(
                    [_pad_to(ffn_w_gate[i, f].astype(BF16), 1, fp), _pad_to(ffn_w_up[i, f].astype(BF16), 1, fp),
                     _pad_to(ffn_w_down[i, f].astype(BF16), 0, fp)], [True, True, False], "gather_ffn_" + tag)
                h, u, a = ffn_up(act_b, wg_f, wu_f, "ffn_up_" + tag)
                z, o, ob = mm_res_ln(a, wd_f, act, ln_gain(i, s), ln_bias(i, s), alpha, 0.5, "down_ln_" + tag)
                layer[s] = dict(x_b=act_b, h=h, u=u, a=a, z=z, wg=wg_f, wu=wu_f, wd=wd_f)
            act, act_b = o, ob
        saved.append(layer)

    loss_local, d_act = loss_head(act, loss_target.reshape(T, D), "loss_head")
    loss = lax.psum(loss_local[0, 0], ("x", "y", "c"))

    def reduce_scatter(grads, tag):
        g4 = [g.reshape((4, 2) + g.shape[1:]) for g in grads]
        recv = rs_sibling(g4, "rs_sibling_" + tag)
        sums = [rs_add(g, r, core, f"rs_add{t}_" + tag) for t, (g, r) in enumerate(zip(g4, recv))]
        others = rs_chips(sums, "rs_chips_" + tag)
        own = 2 * cx + cy
        return [[lax.dynamic_index_in_dim(s_, own, 0, keepdims=False), o_[0], o_[1], o_[2]] for s_, o_ in zip(sums, others)]

    results = {}

    def update(name, w, m, v, index, parts, crop=None):
        L = w.shape[0] if w.ndim == 3 else w.shape[0] * w.shape[1]
        flat = lambda t: t.reshape((L,) + t.shape[-2:])
        if crop is not None:
            parts = [lax.slice_in_dim(p, 0, crop[1], axis=crop[0]) for p in parts]
        results.setdefault(name, {})[index] = adamw(flat(w), flat(m), flat(v), index, parts, f"adamw_{name}_{index}")

    dbands = []
    dln_g = [None] * (3 * depth)
    dln_b = [None] * (3 * depth)
    for i in reversed(range(depth)):
        for s in (2, 1, 0):
            tag = f"L{i}S{s}"
            sv = saved[i][s]
            dz, dzb, dg, db = ln_bwd(sv["z"], ln_gain(i, s), d_act, "ln_bwd_" + tag)
            dln_g[3 * i + s], dln_b[3 * i + s] = dg, db
            if s == 1:
                j = i // 2
                d_att = mm_nt(dzb, sv["wo"], "att_bwd_" + tag)
                g_wo = mm_tn(sv["att"], dzb, 1.0, False, "dwo_" + tag)
                if i % 2 == 0:
                    dq, dk, dv, dband = attn_a_bwd(sv["qkv"], band, d_att, S, "attn_a_bwd_" + tag)
                    dbands.append(jnp.transpose(dband, (1, 0, 2)))
                else:
                    dq, dk, dv = attn_b_bwd(sv["qkv"], d_att, S, "attn_b_bwd_" + tag)
                dqkv = jnp.concatenate([dq, dk, dv], axis=1)
                g_wqkv = mm_tn(sv["x_b"], dqkv, 1.0, True, "dwqkv_" + tag)
                d_act = mm_nt_res([(dqkv, sv["wqkv"])], dz, alpha, "dx_mix_" + tag)
                p_qkv, p_o = reduce_scatter([g_wqkv, g_wo.reshape(N_DEV, D // N_DEV, D)], tag)
                names = ("w_qkv_a", "w_o_a") if i % 2 == 0 else ("w_qkv_b", "w_o_b")
                ws = (w_qkv_a, w_o_a, m_w_qkv_a, m_w_o_a, v_w_qkv_a, v_w_o_a) if i % 2 == 0 else \
                     (w_qkv_b, w_o_b, m_w_qkv_b, m_w_o_b, v_w_qkv_b, v_w_o_b)
                update(names[0], ws[0], ws[2], ws[4], j, p_qkv)
                update(names[1], ws[1], ws[3], ws[5], j, p_o)
            else:
                f = 0 if s == 0 else 1
                dh, du = ffn_bwd_mid(dzb, sv["wd"], sv["h"], sv["u"], 0.5, "ffn_mid_" + tag)
                g_wd = mm_tn(sv["a"], dzb, 0.5, False, "dwd_" + tag)
                g_wg = mm_tn(sv["x_b"], dh, 1.0, True, "dwg_" + tag)
                g_wu = mm_tn(sv["x_b"], du, 1.0, True, "dwu_" + tag)
                d_act = mm_nt_res([(dh, sv["wg"]), (du, sv["wu"])], dz, alpha, "dx_ffn_" + tag)
                p_g, p_u, p_d = reduce_scatter([g_wg, g_wu, g_wd.reshape(N_DEV, fp, D)], tag)
                idx = 2 * i + f
                update("ffn_w_gate", ffn_w_gate, m_ffn_w_gate, v_ffn_w_gate, idx, p_g, crop=(1, fs))
                update("ffn_w_up", ffn_w_up, m_ffn_w_up, v_ffn_w_up, idx, p_u, crop=(1, fs))
                update("ffn_w_down", ffn_w_down, m_ffn_w_down, v_ffn_w_down, idx, p_d, crop=(0, fs))
    grad_x = d_act.reshape(nb, S, D)

    dtable_t = bias_band_bwd(dbands, "bias_band_bwd")
    small = jnp.concatenate(dln_g + dln_b + [_pad_to(dtable_t, 1, D)], axis=0)
    small = _pad_to(small, 0, -(-small.shape[0] // SUBLANE) * SUBLANE)
    total = sum_devices(gather_small(small, "gather_small_grads"), "sum_small_grads")
    n_ln = 3 * depth
    g_ln_g = lax.dynamic_slice_in_dim(total[:n_ln], me * (D // N_DEV), D // N_DEV, axis=1)
    g_ln_b = lax.dynamic_slice_in_dim(total[n_ln:2 * n_ln], me * (D // N_DEV), D // N_DEV, axis=1)
    g_rel = total[2 * n_ln:2 * n_ln + H, :N_REL].T
    as3 = lambda t: t.reshape((1, -1, t.shape[-1]))
    r_ln_g = adamw(as3(ln_g), as3(m_ln_g), as3(v_ln_g), 0, [g_ln_g], "adamw_ln_g")
    r_ln_b = adamw(as3(ln_b), as3(m_ln_b), as3(v_ln_b), 0, [g_ln_b], "adamw_ln_b")
    r_rel = adamw(as3(rel_bias), as3(m_rel_bias), as3(v_rel_bias), 0, [g_rel], "adamw_rel_bias")

    def stacked(name, like, q):
        res = results[name]
        return jnp.stack([res[k][q] for k in range(len(res))]).reshape(like.shape)

    order = [("w_qkv_a", w_qkv_a), ("w_o_a", w_o_a), ("rel_bias", rel_bias), ("w_qkv_b", w_qkv_b), ("w_o_b", w_o_b),
             ("ffn_w_gate", ffn_w_gate), ("ffn_w_up", ffn_w_up), ("ffn_w_down", ffn_w_down), ("ln_g", ln_g), ("ln_b", ln_b)]
    single = {"rel_bias": r_rel, "ln_g": r_ln_g, "ln_b": r_ln_b}
    outs = [loss, grad_x]
    for q in range(4):
        for name, like in order:
            outs.append(single[name][q].reshape(like.shape) if name in single else stacked(name, like, q))
    return tuple(outs)
```

```python
import functools

import jax
import jax.numpy as jnp
from jax import lax
from jax.experimental import pallas as pl
from jax.experimental.pallas import tpu as pltpu

BF16 = jnp.bfloat16
F32 = jnp.float32
MESH_ID = pl.DeviceIdType.MESH
ANY = pl.BlockSpec(memory_space=pl.ANY)

N_DEV = 8
LANE = 128
SUBLANE = 8
VMEM_LIMIT = 56 * 1024 * 1024

HEAD_DIM = 64
CHUNK = 64
LEFT_CHUNKS = 8
REL_CLIP = 128
N_REL = 2 * REL_CLIP + 1
REL_PAD = 384
LN_EPS = 1e-5
ADAM_LR, ADAM_B1, ADAM_B2, ADAM_EPS, ADAM_WD, ADAM_STEP = 0.001, 0.9, 0.999, 1e-08, 0.01, 10
NEG = -1e30

A_TQ = 128
A_NWB = LEFT_CHUNKS * CHUNK // A_TQ + 1
A_W = A_NWB * A_TQ
B_T = 256

NT_DIMS = (((1,), (1,)), ((), ()))
TN_DIMS = (((0,), (0,)), ((), ()))


def _tile(n, pref, unit=LANE):
    if n <= pref:
        return n
    t = pref - pref % unit
    while t > unit and n % t:
        t -= unit
    assert n % t == 0, (n, pref)
    return t


def _params(*sem):
    return pltpu.CompilerParams(dimension_semantics=sem, vmem_limit_bytes=VMEM_LIMIT)


def _split3(v):
    h = v.astype(BF16)
    r = v - h.astype(F32)
    m = r.astype(BF16)
    lo = (r - m.astype(F32)).astype(BF16)
    return h, m, lo


def _dot3(v, w):
    h, m, lo = _split3(v)
    return (jnp.dot(h, w, preferred_element_type=F32) + jnp.dot(m, w, preferred_element_type=F32)
            + jnp.dot(lo, w, preferred_element_type=F32))


def mm_nn(a, w, name):
    T, K = a.shape
    N = w.shape[1]
    tm, tn = _tile(T, 1024), _tile(N, 768)

    def body(a_ref, w_ref, o_ref):
        o_ref[...] = jnp.dot(a_ref[...], w_ref[...], preferred_element_type=F32).astype(o_ref.dtype)

    return pl.pallas_call(
        body, grid=(T // tm, N // tn),
        in_specs=[pl.BlockSpec((tm, K), lambda i, j: (i, 0)), pl.BlockSpec((K, tn), lambda i, j: (0, j))],
        out_specs=pl.BlockSpec((tm, tn), lambda i, j: (i, j)),
        out_shape=jax.ShapeDtypeStruct((T, N), BF16),
        compiler_params=_params("parallel", "parallel"), name=name)(a, w)


def ffn_up(xb, wg, wu, name):
    T, K = xb.shape
    N = wg.shape[1]
    tm, tn = _tile(T, 512), _tile(N, 768)

    def body(x_ref, wg_ref, wu_ref, h_ref, u_ref, a_ref):
        x = x_ref[...]
        h = jnp.dot(x, wg_ref[...], preferred_element_type=F32)
        u = jnp.dot(x, wu_ref[...], preferred_element_type=F32)
        h_ref[...] = h.astype(BF16)
        u_ref[...] = u.astype(BF16)
        a_ref[...] = (h * jax.nn.sigmoid(h) * u).astype(BF16)

    wspec = pl.BlockSpec((K, tn), lambda i, j: (0, j))
    ospec = pl.BlockSpec((tm, tn), lambda i, j: (i, j))
    return pl.pallas_call(
        body, grid=(T // tm, N // tn),
        in_specs=[pl.BlockSpec((tm, K), lambda i, j: (i, 0)), wspec, wspec],
        out_specs=[ospec, ospec, ospec],
        out_shape=[jax.ShapeDtypeStruct((T, N), BF16)] * 3,
        compiler_params=_params("parallel", "parallel"), name=name)(xb, wg, wu)


def mm_res_ln(a, w, x, g, b, alpha, scale, name):
    T, K = a.shape
    D = w.shape[1]
    tm = _tile(T, 256)

    def body(a_ref, w_ref, x_ref, g_ref, b_ref, z_ref, o_ref, ob_ref):
        y = jnp.dot(a_ref[...], w_ref[...], preferred_element_type=F32)
        z = alpha * x_ref[...] + scale * y
        mu = jnp.mean(z, axis=1, keepdims=True)
        zc = z - mu
        var = jnp.mean(zc * zc, axis=1, keepdims=True)
        o = zc * lax.rsqrt(var + LN_EPS) * g_ref[...] + b_ref[...]
        z_ref[...] = z
        o_ref[...] = o
        ob_ref[...] = o.astype(BF16)

    row = pl.BlockSpec((tm, D), lambda i: (i, 0))
    vec = pl.BlockSpec((1, D), lambda i: (0, 0))
    return pl.pallas_call(
        body, grid=(T // tm,),
        in_specs=[pl.BlockSpec((tm, K), lambda i: (i, 0)), pl.BlockSpec((K, D), lambda i: (0, 0)), row, vec, vec],
        out_specs=[row, row, row],
        out_shape=[jax.ShapeDtypeStruct((T, D), F32), jax.ShapeDtypeStruct((T, D), F32),
                   jax.ShapeDtypeStruct((T, D), BF16)],
        compiler_params=_params("parallel"), name=name)(a, w, x, g, b)


def ln_bwd(z, g, do, name):
    T, D = z.shape
    tm = _tile(T, 512)

    def body(z_ref, g_ref, do_ref, dz_ref, dzb_ref, dg_ref, db_ref):
        @pl.when(pl.program_id(0) == 0)
        def _():
            dg_ref[...] = jnp.zeros_like(dg_ref)
            db_ref[...] = jnp.zeros_like(db_ref)

        zv = z_ref[...]
        dov = do_ref[...]
        mu = jnp.mean(zv, axis=1, keepdims=True)
        zc = zv - mu
        var = jnp.mean(zc * zc, axis=1, keepdims=True)
        rstd = lax.rsqrt(var + LN_EPS)
        xhat = zc * rstd
        dxhat = dov * g_ref[...]
        m1 = jnp.mean(dxhat, axis=1, keepdims=True)
        m2 = jnp.mean(dxhat * xhat, axis=1, keepdims=True)
        dz = rstd * (dxhat - m1 - xhat * m2)
        dz_ref[...] = dz
        dzb_ref[...] = dz.astype(BF16)
        dg_ref[...] += jnp.sum(dov * xhat, axis=0, keepdims=True)
        db_ref[...] += jnp.sum(dov, axis=0, keepdims=True)

    row = pl.BlockSpec((tm, D), lambda i: (i, 0))
    vec = pl.BlockSpec((1, D), lambda i: (0, 0))
    return pl.pallas_call(
        body, grid=(T // tm,), in_specs=[row, vec, row], out_specs=[row, row, vec, vec],
        out_shape=[jax.ShapeDtypeStruct((T, D), F32), jax.ShapeDtypeStruct((T, D), BF16),
                   jax.ShapeDtypeStruct((1, D), F32), jax.ShapeDtypeStruct((1, D), F32)],
        compiler_params=_params("arbitrary"), name=name)(z, g, do)


def mm_nt(a, w, name):
    T, K = a.shape
    N = w.shape[0]
    tm, tn = _tile(T, 1024), _tile(N, 512)

    def body(a_ref, w_ref, o_ref):
        o_ref[...] = lax.dot_general(a_ref[...], w_ref[...], NT_DIMS, preferred_element_type=F32).astype(o_ref.dtype)

    return pl.pallas_call(
        body, grid=(T // tm, N // tn),
        in_specs=[pl.BlockSpec((tm, K), lambda i, j: (i, 0)), pl.BlockSpec((tn, K), lambda i, j: (j, 0))],
        out_specs=pl.BlockSpec((tm, tn), lambda i, j: (i, j)),
        out_shape=jax.ShapeDtypeStruct((T, N), BF16),
        compiler_params=_params("parallel", "parallel"), name=name)(a, w)


def ffn_bwd_mid(dzb, wd, h, u, scale, name):
    T, K = dzb.shape
    N = wd.shape[0]
    tm, tn = _tile(T, 512), _tile(N, 768)

    def body(dz_ref, w_ref, h_ref, u_ref, dh_ref, du_ref):
        da = scale * lax.dot_general(dz_ref[...], w_ref[...], NT_DIMS, preferred_element_type=F32)
        hv = h_ref[...].astype(F32)
        uv = u_ref[...].astype(F32)
        s = jax.nn.sigmoid(hv)
        silu = hv * s
        dh_ref[...] = (da * uv * (s + silu * (1.0 - s))).astype(BF16)
        du_ref[...] = (da * silu).astype(BF16)

    tile = pl.BlockSpec((tm, tn), lambda i, j: (i, j))
    return pl.pallas_call(
        body, grid=(T // tm, N // tn),
        in_specs=[pl.BlockSpec((tm, K), lambda i, j: (i, 0)), pl.BlockSpec((tn, K), lambda i, j: (j, 0)), tile, tile],
        out_specs=[tile, tile],
        out_shape=[jax.ShapeDtypeStruct((T, N), BF16)] * 2,
        compiler_params=_params("parallel", "parallel"), name=name)(dzb, wd, h, u)


def mm_nt_res(pairs, dz, alpha, name):
    T, N = pairs[0][0].shape
    D = pairs[0][1].shape[0]
    n = len(pairs)
    tm, tk = _tile(T, 512), _tile(N, 768)
    nk = N // tk

    def body(*refs):
        a_refs, w_refs = refs[:n], refs[n:2 * n]
        dz_ref, o_ref, acc_ref = refs[2 * n:]
        k = pl.program_id(1)

        @pl.when(k == 0)
        def _():
            acc_ref[...] = jnp.zeros_like(acc_ref)

        part = lax.dot_general(a_refs[0][...], w_refs[0][...], NT_DIMS, preferred_element_type=F32)
        for p in range(1, n):
            part += lax.dot_general(a_refs[p][...], w_refs[p][...], NT_DIMS, preferred_element_type=F32)
        acc_ref[...] += part

        @pl.when(k == nk - 1)
        def _():
            o_ref[...] = acc_ref[...] + alpha * dz_ref[...]

    row = pl.BlockSpec((tm, D), lambda i, k: (i, 0))
    return pl.pallas_call(
        body, grid=(T // tm, nk),
        in_specs=[pl.BlockSpec((tm, tk), lambda i, k: (i, k))] * n + [pl.BlockSpec((D, tk), lambda i, k: (0, k))] * n + [row],
        out_specs=row,
        out_shape=jax.ShapeDtypeStruct((T, D), F32),
        scratch_shapes=[pltpu.VMEM((tm, D), F32)],
        compiler_params=_params("parallel", "arbitrary"), name=name)(*[p[0] for p in pairs], *[p[1] for p in pairs], dz)


def mm_tn(a, b, scale, shard_cols, name):
    T, M = a.shape
    N = b.shape[1]
    tm, tk = _tile(M, 512), _tile(T, 512)
    nk = T // tk
    if shard_cols:
        ns = N // N_DEV
        per = 2 if (2 * ns) % 256 == 0 else 1
        tn = per * ns
        out_shape = jax.ShapeDtypeStruct((N_DEV, M, ns), BF16)
        out_spec = pl.BlockSpec((per, tm, ns), lambda i, j, k: (j, i, 0))
    else:
        tn = _tile(N, 1024)
        out_shape = jax.ShapeDtypeStruct((M, N), BF16)
        out_spec = pl.BlockSpec((tm, tn), lambda i, j, k: (i, j))

    def body(a_ref, b_ref, o_ref, acc_ref):
        k = pl.program_id(2)

        @pl.when(k == 0)
        def _():
            acc_ref[...] = jnp.zeros_like(acc_ref)

        acc_ref[...] += lax.dot_general(a_ref[...], b_ref[...], TN_DIMS, preferred_element_type=F32)

        @pl.when(k == nk - 1)
        def _():
            if shard_cols:
                for s in range(per):
                    o_ref[s] = (scale * acc_ref[:, s * ns:(s + 1) * ns]).astype(BF16)
            else:
                o_ref[...] = (scale * acc_ref[...]).astype(BF16)

    return pl.pallas_call(
        body, grid=(M // tm, N // tn, nk),
        in_specs=[pl.BlockSpec((tk, tm), lambda i, j, k: (k, i)), pl.BlockSpec((tk, tn), lambda i, j, k: (k, j))],
        out_specs=out_spec, out_shape=out_shape,
        scratch_shapes=[pltpu.VMEM((tm, tn), F32)],
        compiler_params=_params("parallel", "parallel", "arbitrary"), name=name)(a, b)


def loss_head(y, target, name):
    T, D = y.shape
    tm = _tile(T, 512)

    def body(y_ref, t_ref, l_ref, dy_ref):
        @pl.when(pl.program_id(0) == 0)
        def _():
            l_ref[...] = jnp.zeros_like(l_ref)

        e = y_ref[...] - t_ref[...]
        dy_ref[...] = e * (1.0 / D)
        rows = jnp.sum(e * e, axis=1, keepdims=True) * (0.5 / D)
        l_ref[...] += jnp.sum(rows, axis=0, keepdims=True)

    row = pl.BlockSpec((tm, D), lambda i: (i, 0))
    return pl.pallas_call(
        body, grid=(T // tm,), in_specs=[row, row],
        out_specs=[pl.BlockSpec((1, 1), lambda i: (0, 0)), row],
        out_shape=[jax.ShapeDtypeStruct((1, 1), F32), jax.ShapeDtypeStruct((T, D), F32)],
        compiler_params=_params("arbitrary"), name=name)(y, target)


def _rel_index(i, j):
    return jnp.clip(i - j + LEFT_CHUNKS * CHUNK, -REL_CLIP, REL_CLIP) + REL_CLIP


def bias_band(table_t, name):
    H = table_t.shape[0]
    rows = SUBLANE

    def body(t_ref, o_ref):
        i0 = pl.program_id(0) * rows
        parts = _split3(t_ref[...])
        r = lax.broadcasted_iota(jnp.int32, (REL_PAD, A_W), 0)
        j = lax.broadcasted_iota(jnp.int32, (REL_PAD, A_W), 1)
        for ii in range(rows):
            onehot = jnp.where(r == _rel_index(i0 + ii, j), 1.0, 0.0).astype(BF16)
            o_ref[ii] = sum(jnp.dot(p, onehot, preferred_element_type=F32) for p in parts)

    return pl.pallas_call(
        body, grid=(A_TQ // rows,),
        in_specs=[pl.BlockSpec((H, REL_PAD), lambda i: (0, 0))],
        out_specs=pl.BlockSpec((rows, H, A_W), lambda i: (i, 0, 0)),
        out_shape=jax.ShapeDtypeStruct((A_TQ, H, A_W), F32),
        compiler_params=_params("parallel"), name=name)(table_t)


def bias_band_bwd(dbands, name):
    H = dbands[0].shape[1]
    rows = SUBLANE
    n = len(dbands)

    def body(*refs):
        g_refs, o_ref = refs[:n], refs[n]

        @pl.when(pl.program_id(0) == 0)
        def _():
            o_ref[...] = jnp.zeros_like(o_ref)

        i0 = pl.program_id(0) * rows
        j = lax.broadcasted_iota(jnp.int32, (A_W, REL_PAD), 0)
        r = lax.broadcasted_iota(jnp.int32, (A_W, REL_PAD), 1)
        acc = jnp.zeros((H, REL_PAD), F32)
        for ii in range(rows):
            onehot = jnp.where(r == _rel_index(i0 + ii, j), 1.0, 0.0).astype(BF16)
            g = g_refs[0][ii]
            for q in range(1, n):
                g = g + g_refs[q][ii]
            acc += _dot3(g, onehot)
        o_ref[...] += acc

    spec = pl.BlockSpec((rows, H, A_W), lambda i: (i, 0, 0))
    return pl.pallas_call(
        body, grid=(A_TQ // rows,), in_specs=[spec] * n,
        out_specs=pl.BlockSpec((H, REL_PAD), lambda i: (0, 0)),
        out_shape=jax.ShapeDtypeStruct((H, REL_PAD), F32),
        compiler_params=_params("arbitrary"), name=name)(*dbands)


def _a_window(ref, qi):
    parts = []
    for d in range(A_NWB):
        kb = jnp.maximum(qi - (A_NWB - 1) + d, 0)
        parts.append(ref[pl.ds(pl.multiple_of(kb * A_TQ, A_TQ), A_TQ), :])
    return jnp.concatenate(parts, axis=0)


def _a_valid(qi):
    i = lax.broadcasted_iota(jnp.int32, (A_TQ, A_W), 0)
    j = lax.broadcasted_iota(jnp.int32, (A_TQ, A_W), 1)
    ic, jc = i // CHUNK, j // CHUNK
    return (jc >= ic) & (jc <= ic + LEFT_CHUNKS) & (j >= LEFT_CHUNKS * CHUNK - qi * A_TQ)


def _head_masks():
    lane = lax.broadcasted_iota(jnp.int32, (1, LANE), 1)
    return [lane < HEAD_DIM, lane >= HEAD_DIM]


def _a_probs(qm, kw, bias, valid):
    s = lax.dot_general(qm, kw, NT_DIMS, preferred_element_type=F32) * (HEAD_DIM ** -0.5) + bias
    s = jnp.where(valid, s, NEG)
    p = jnp.exp(s - jnp.max(s, axis=1, keepdims=True))
    return p / jnp.sum(p, axis=1, keepdims=True)


def _attn_specs(S, D, tq):
    hp_n = D // LANE
    nq = S // tq
    q = pl.BlockSpec((tq, LANE), lambda hp, b, qi: (b * nq + qi, hp))
    k = pl.BlockSpec((S, LANE), lambda hp, b, qi: (b, hp_n + hp))
    v = pl.BlockSpec((S, LANE), lambda hp, b, qi: (b, 2 * hp_n + hp))
    tile = pl.BlockSpec((tq, LANE), lambda hp, b, qi: (b * nq + qi, hp))
    seq = pl.BlockSpec((S, LANE), lambda hp, b, qi: (b, hp))
    return q, k, v, tile, seq


def attn_a_fwd(qkv, band, S, name):
    T, D3 = qkv.shape
    D = D3 // 3
    nb, nq = T // S, S // A_TQ

    def body(q_ref, k_ref, v_ref, b_ref, o_ref):
        qi = pl.program_id(2)
        q2 = q_ref[...]
        kw, vw = _a_window(k_ref, qi), _a_window(v_ref, qi)
        valid = _a_valid(qi)
        o = jnp.zeros((A_TQ, LANE), F32)
        for hh, hm in enumerate(_head_masks()):
            p = _a_probs(jnp.where(hm, q2, jnp.zeros_like(q2)), kw, b_ref[hh], valid)
            o = jnp.where(hm, jnp.dot(p.astype(BF16), vw, preferred_element_type=F32), o)
        o_ref[...] = o.astype(BF16)

    q, k, v, tile, _ = _attn_specs(S, D, A_TQ)
    return pl.pallas_call(
        body, grid=(D // LANE, nb, nq),
        in_specs=[q, k, v, pl.BlockSpec((2, A_TQ, A_W), lambda hp, b, qi: (hp, 0, 0))],
        out_specs=tile, out_shape=jax.ShapeDtypeStruct((T, D), BF16),
        compiler_params=_params("parallel", "parallel", "parallel"), name=name)(qkv, qkv, qkv, band)


def attn_a_bwd(qkv, band, do, S, name):
    T, D3 = qkv.shape
    D = D3 // 3
    nb, nq = T // S, S // A_TQ
    scale = HEAD_DIM ** -0.5

    def body(q_ref, k_ref, v_ref, b_ref, do_ref, dq_ref, dk_ref, dv_ref, db_ref, dk_acc, dv_acc):
        b, qi = pl.program_id(1), pl.program_id(2)

        @pl.when((b == 0) & (qi == 0))
        def _():
            db_ref[...] = jnp.zeros_like(db_ref)

        @pl.when(qi == 0)
        def _():
            dk_acc[...] = jnp.zeros_like(dk_acc)
            dv_acc[...] = jnp.zeros_like(dv_acc)

        q2, do2 = q_ref[...], do_ref[...]
        kw, vw = _a_window(k_ref, qi), _a_window(v_ref, qi)
        valid = _a_valid(qi)
        dq = jnp.zeros((A_TQ, LANE), F32)
        dkw = jnp.zeros((A_W, LANE), F32)
        dvw = jnp.zeros((A_W, LANE), F32)
        for hh, hm in enumerate(_head_masks()):
            qm = jnp.where(hm, q2, jnp.zeros_like(q2))
            dom = jnp.where(hm, do2, jnp.zeros_like(do2))
            p = _a_probs(qm, kw, b_ref[hh], valid)
            dp = lax.dot_general(dom, vw, NT_DIMS, preferred_element_type=F32)
            ds = p * (dp - jnp.sum(p * dp, axis=1, keepdims=True))
            db_ref[hh] += ds
            dsb = ds.astype(BF16)
            dq = jnp.where(hm, jnp.dot(dsb, kw, preferred_element_type=F32) * scale, dq)
            dkw += lax.dot_general(dsb, qm, TN_DIMS, preferred_element_type=F32) * scale
            dvw += lax.dot_general(p.astype(BF16), dom, TN_DIMS, preferred_element_type=F32)
        dq_ref[...] = dq.astype(BF16)
        for d in range(A_NWB):
            kb = jnp.maximum(qi - (A_NWB - 1) + d, 0)
            rows = pl.ds(pl.multiple_of(kb * A_TQ, A_TQ), A_TQ)
            dk_acc[rows, :] += dkw[d * A_TQ:(d + 1) * A_TQ]
            dv_acc[rows, :] += dvw[d * A_TQ:(d + 1) * A_TQ]

        @pl.when(qi == nq - 1)
        def _():
            dk_ref[...] = dk_acc[...].astype(BF16)
            dv_ref[...] = dv_acc[...].astype(BF16)

    q, k, v, tile, seq = _attn_specs(S, D, A_TQ)
    bspec = pl.BlockSpec((2, A_TQ, A_W), lambda hp, b, qi: (hp, 0, 0))
    act = jax.ShapeDtypeStruct((T, D), BF16)
    return pl.pallas_call(
        body, grid=(D // LANE, nb, nq),
        in_specs=[q, k, v, bspec, tile], out_specs=[tile, seq, seq, bspec],
        out_shape=[act, act, act, jax.ShapeDtypeStruct(band.shape, F32)],
        scratch_shapes=[pltpu.VMEM((S, LANE), F32), pltpu.VMEM((S, LANE), F32)],
        compiler_params=_params("arbitrary", "arbitrary", "arbitrary"), name=name)(qkv, qkv, qkv, band, do)


def _dot2(v, w):
    h = v.astype(BF16)
    lo = (v - h.astype(F32)).astype(BF16)
    return jnp.dot(h, w, preferred_element_type=F32) + jnp.dot(lo, w, preferred_element_type=F32)


def _b_logits(qm, kt, diagonal):
    z = lax.dot_general(qm, kt, NT_DIMS, preferred_element_type=F32)
    nz = -z
    soft = jnp.log(1.0 + jnp.exp(jnp.minimum(z, nz)))
    lb = jnp.minimum(z, 0.0) - soft
    l1m = jnp.minimum(nz, 0.0) - soft
    causal = None
    if diagonal:
        row = lax.broadcasted_iota(jnp.int32, (B_T, B_T), 0)
        col = lax.broadcasted_iota(jnp.int32, (B_T, B_T), 1)
        causal = col < row
        l1m = jnp.where(causal, l1m, 0.0)
    return z, lb, l1m, causal


def _tri(strict_lower):
    r = lax.broadcasted_iota(jnp.int32, (B_T, B_T), 0)
    c = lax.broadcasted_iota(jnp.int32, (B_T, B_T), 1)
    return jnp.where((r > c) if strict_lower else (r < c), 1.0, 0.0).astype(BF16)


def _scaled_heads(q2):
    qs = q2 * (HEAD_DIM ** -0.5)
    return [jnp.where(hm, qs, jnp.zeros_like(qs)) for hm in _head_masks()]


def attn_b_fwd(qkv, S, name):
    T, D3 = qkv.shape
    D = D3 // 3
    nb, nq = T // S, S // B_T

    def body(q_ref, k_ref, v_ref, o_ref):
        qi = pl.program_id(2)
        after = _tri(True)
        qms = _scaled_heads(q_ref[...])

        def tiles(kb, carry, diagonal):
            rows = pl.ds(pl.multiple_of(kb * B_T, B_T), B_T)
            kt, vt = k_ref[rows, :], v_ref[rows, :]
            out = []
            for hh in range(2):
                right, acc = carry[2 * hh], carry[2 * hh + 1]
                _, lb, l1m, causal = _b_logits(qms[hh], kt, diagonal)
                a = jnp.exp(lb + (right + _dot2(l1m, after)))
                if diagonal:
                    a = jnp.where(causal, a, 0.0)
                acc = acc + jnp.dot(a.astype(BF16), vt, preferred_element_type=F32)
                out += [right + jnp.sum(l1m, axis=1, keepdims=True), acc]
            return tuple(out)

        init = (jnp.zeros((B_T, 1), F32), jnp.zeros((B_T, LANE), F32)) * 2
        res = tiles(qi, init, True)
        res = lax.fori_loop(1, qi + 1, lambda it, carry: tiles(qi - it, carry, False), res)
        o_ref[...] = jnp.where(_head_masks()[0], res[1], res[3]).astype(BF16)

    q, k, v, tile, _ = _attn_specs(S, D, B_T)
    return pl.pallas_call(
        body, grid=(D // LANE, nb, nq), in_specs=[q, k, v], out_specs=tile,
        out_shape=jax.ShapeDtypeStruct((T, D), BF16),
        compiler_params=_params("parallel", "parallel", "parallel"), name=name)(qkv, qkv, qkv)


def attn_b_bwd(qkv, do, S, name):
    T, D3 = qkv.shape
    D = D3 // 3
    nb, nq = T // S, S // B_T
    scale = HEAD_DIM ** -0.5

    def body(q_ref, k_ref, v_ref, do_ref, dq_ref, dk_ref, dv_ref, dk_acc, dv_acc, z_s, g_s):
        qi = pl.program_id(2)

        @pl.when(qi == 0)
        def _():
            dk_acc[...] = jnp.zeros_like(dk_acc)
            dv_acc[...] = jnp.zeros_like(dv_acc)

        do2 = do_ref[...]
        after, before = _tri(True), _tri(False)
        masks = _head_masks()
        qms = _scaled_heads(q_ref[...])
        doms = [jnp.where(hm, do2, jnp.zeros_like(do2)) for hm in masks]

        def sweep_left(kb, rights, diagonal):
            rows = pl.ds(pl.multiple_of(kb * B_T, B_T), B_T)
            kt, vt = k_ref[rows, :], v_ref[rows, :]
            out = []
            dv = jnp.zeros((B_T, LANE), F32)
            for hh in range(2):
                z, lb, l1m, causal = _b_logits(qms[hh], kt, diagonal)
                a = jnp.exp(lb + (rights[hh] + _dot2(l1m, after)))
                if diagonal:
                    a = jnp.where(causal, a, 0.0)
                da = lax.dot_general(doms[hh], vt, NT_DIMS, preferred_element_type=F32)
                z_s[hh, kb] = z
                g_s[hh, kb] = da * a
                dv += lax.dot_general(a.astype(BF16), doms[hh], TN_DIMS, preferred_element_type=F32)
                out.append(rights[hh] + jnp.sum(l1m, axis=1, keepdims=True))
            dv_acc[rows, :] += dv
            return tuple(out)

        zero_col = jnp.zeros((B_T, 1), F32)
        rights = sweep_left(qi, (zero_col, zero_col), True)
        lax.fori_loop(1, qi + 1, lambda it, carry: sweep_left(qi - it, carry, False), rights)

        def sweep_right(kb, carry, diagonal):
            rows = pl.ds(pl.multiple_of(kb * B_T, B_T), B_T)
            kt = k_ref[rows, :]
            out = []
            dk = jnp.zeros((B_T, LANE), F32)
            for hh in range(2):
                left, dq_h = carry[2 * hh], carry[2 * hh + 1]
                z, g = z_s[hh, kb], g_s[hh, kb]
                beta = jax.nn.sigmoid(z)
                dz = g * (1.0 - beta) - beta * (left + _dot2(g, before))
                if diagonal:
                    row = lax.broadcasted_iota(jnp.int32, (B_T, B_T), 0)
                    col = lax.broadcasted_iota(jnp.int32, (B_T, B_T), 1)
                    dz = jnp.where(col < row, dz, 0.0)
                dzb = dz.astype(BF16)
                dk += lax.dot_general(dzb, qms[hh], TN_DIMS, preferred_element_type=F32)
                dq_h = dq_h + jnp.dot(dzb, kt, preferred_element_type=F32)
                out += [left + jnp.sum(g, axis=1, keepdims=True), dq_h]
            dk_acc[rows, :] += dk
            return tuple(out)

        init = (zero_col, jnp.zeros((B_T, LANE), F32)) * 2
        res = lax.fori_loop(0, qi, lambda kb, carry: sweep_right(kb, carry, False), init)
        res = sweep_right(qi, res, True)
        dq_ref[...] = (jnp.where(masks[0], res[1], res[3]) * scale).astype(BF16)

        @pl.when(qi == nq - 1)
        def _():
            dk_ref[...] = dk_acc[...].astype(BF16)
            dv_ref[...] = dv_acc[...].astype(BF16)

    q, k, v, tile, seq = _attn_specs(S, D, B_T)
    act = jax.ShapeDtypeStruct((T, D), BF16)
    return pl.pallas_call(
        body, grid=(D // LANE, nb, nq),
        in_specs=[q, k, v, tile], out_specs=[tile, seq, seq], out_shape=[act, act, act],
        scratch_shapes=[pltpu.VMEM((S, LANE), F32), pltpu.VMEM((S, LANE), F32),
                        pltpu.VMEM((2, nq, B_T, B_T), F32), pltpu.VMEM((2, nq, B_T, B_T), F32)],
        compiler_params=_params("arbitrary", "arbitrary", "arbitrary"), name=name)(qkv, qkv, qkv, do)


HBM = pl.BlockSpec(memory_space=pltpu.HBM)
SEM = pl.BlockSpec(memory_space=pltpu.SEMAPHORE)
N_PEERS = N_DEV - 1
GATHERS_AHEAD = 3
SCATTERS_BEHIND = 1


def _place():
    return lax.axis_index("x"), lax.axis_index("y"), lax.axis_index("c")


def _peers(x, y, c):
    return [(1 - x if r & 4 else x, 1 - y if r & 2 else y, 1 - c if r & 1 else c) for r in range(1, N_DEV)]


def _block(ref, shape, by_cols, j):
    r, w = shape
    if by_cols:
        return ref.at[:, pl.ds(pl.multiple_of(j * w, LANE), w)]
    return ref.at[pl.ds(pl.multiple_of(j * r, SUBLANE), r), :]


def _exchange_copies(gather, src_refs, land_refs, send_sems, recv_sems, by_cols):
    x, y, c = _place()
    me = 4 * x + 2 * y + c
    out = []
    for t, (src, land) in enumerate(zip(src_refs, land_refs)):
        for r, peer in enumerate(_peers(x, y, c)):
            pj = 4 * peer[0] + 2 * peer[1] + peer[2]
            if gather:
                mine, to_me, theirs = src, _block(land, src.shape, by_cols[t], me), _block(land, src.shape, by_cols[t], pj)
            else:
                mine, to_me, theirs = src.at[pj], land.at[me], land.at[pj]
            sems = dict(send_sem=send_sems.at[N_PEERS * t + r], recv_sem=recv_sems.at[N_PEERS * t + r],
                        device_id=peer, device_id_type=MESH_ID)
            out.append((pltpu.make_async_remote_copy(src_ref=mine, dst_ref=to_me, **sems),
                        pltpu.make_async_remote_copy(src_ref=mine, dst_ref=theirs, **sems)))
    return out


def exchange_start(gather, srcs, land_shapes, by_cols, after, name):
    n = len(srcs)

    def body(*refs):
        src_refs, land_refs = refs[:n], refs[n:2 * n]
        send_sems, recv_sems = refs[2 * n + 1], refs[2 * n + 2]
        token = refs[-1]
        for mine, _ in _exchange_copies(gather, src_refs, land_refs, send_sems, recv_sems, by_cols):
            mine.start()
        token[...] = jnp.zeros_like(token)

    lands = [pltpu.with_memory_space_constraint(lax.empty(s.shape, s.dtype), pltpu.HBM) for s in land_shapes]
    srcs = [pltpu.with_memory_space_constraint(s, pltpu.HBM) for s in srcs]
    res = pl.pallas_call(
        body, name=name,
        out_shape=(pltpu.SemaphoreType.DMA((N_PEERS * n,)), pltpu.SemaphoreType.DMA((N_PEERS * n,)),
                   *[pltpu.HBM(s.shape, s.dtype) for s in srcs], *[pltpu.HBM(s.shape, s.dtype) for s in land_shapes],
                   jax.ShapeDtypeStruct((SUBLANE, LANE), F32)),
        in_specs=[HBM] * (2 * n) + [ANY],
        out_specs=(SEM, SEM, *[HBM] * (2 * n), pl.BlockSpec(memory_space=pltpu.VMEM)),
        input_output_aliases={i: 2 + i for i in range(2 * n)},
        compiler_params=pltpu.CompilerParams(has_side_effects=pltpu.SideEffectType.DATAFLOW_SIDE_EFFECTING),
    )(*srcs, *lands, after)
    return dict(gather=gather, n=n, by_cols=by_cols, send=res[0], recv=res[1], srcs=res[2:2 + n],
                lands=res[2 + n:2 + 2 * n], token=res[-1])


def exchange_wait(started, after, name):
    n, gather, by_cols = started["n"], started["gather"], started["by_cols"]

    def body(*refs):
        src_refs, land_refs = refs[:n], refs[n:2 * n]
        send_sems, recv_sems = refs[2 * n], refs[2 * n + 1]
        local_sems = refs[-1]
        x, y, c = _place()
        me = 4 * x + 2 * y + c
        own = []
        for t, (src, land) in enumerate(zip(src_refs, land_refs)):
            if gather:
                own.append(pltpu.make_async_copy(src, _block(land, src.shape, by_cols[t], me), local_sems.at[t]))
            else:
                own.append(pltpu.make_async_copy(src.at[me], land.at[me], local_sems.at[t]))
            own[-1].start()
        for mine, theirs in _exchange_copies(gather, src_refs, land_refs, send_sems, recv_sems, by_cols):
            mine.wait_send()
            theirs.wait_recv()
        for cp in own:
            cp.wait()

    res = pl.pallas_call(
        body, name=name,
        out_shape=tuple(pltpu.HBM(s.shape, s.dtype) for s in (*started["srcs"], *started["lands"])),
        in_specs=[HBM] * (2 * n) + [SEM, SEM, ANY], out_specs=tuple([HBM] * (2 * n)),
        input_output_aliases={i: i for i in range(2 * n)},
        scratch_shapes=[pltpu.SemaphoreType.DMA((n,))],
        compiler_params=pltpu.CompilerParams(has_side_effects=pltpu.SideEffectType.DATAFLOW_SIDE_EFFECTING),
    )(*started["srcs"], *started["lands"], started["send"], started["recv"], after)
    return res[n:]


def gather_small(v, name):
    R, C = v.shape

    def body(v_ref, o_ref, send_sems, recv_sems):
        x, y, c = _place()
        o_ref[4 * x + 2 * y + c] = v_ref[...]
        peers = _peers(x, y, c)

        def copy(r, owner, to):
            slot = o_ref.at[4 * owner[0] + 2 * owner[1] + owner[2]]
            return pltpu.make_async_remote_copy(
                src_ref=slot, dst_ref=slot, send_sem=send_sems.at[r], recv_sem=recv_sems.at[r],
                device_id=to, device_id_type=MESH_ID)

        sends = [copy(r, (x, y, c), peer) for r, peer in enumerate(peers)]
        for cp in sends:
            cp.start()
        for r, peer in enumerate(peers):
            copy(r, peer, (x, y, c)).wait_recv()
        for cp in sends:
            cp.wait_send()

    vm = pl.BlockSpec(memory_space=pltpu.VMEM)
    return pl.pallas_call(
        body, in_specs=[vm], out_specs=vm, out_shape=jax.ShapeDtypeStruct((N_DEV, R, C), F32),
        scratch_shapes=[pltpu.SemaphoreType.DMA((N_PEERS,)), pltpu.SemaphoreType.DMA((N_PEERS,))],
        name=name)(v)


def sum_devices(g, name):
    _, R, C = g.shape

    def body(g_ref, o_ref):
        acc = g_ref[0]
        for j in range(1, N_DEV):
            acc = acc + g_ref[j]
        o_ref[...] = acc

    vm = pl.BlockSpec(memory_space=pltpu.VMEM)
    return pl.pallas_call(body, in_specs=[vm], out_specs=vm, out_shape=jax.ShapeDtypeStruct((R, C), F32), name=name)(g)


def adamw(w, m, v, index, parts, name):
    _, R, C = w.shape
    P, Rp, Cp = parts.shape
    tr = _tile(R, 512, SUBLANE) if Rp == R else R

    def body(w_ref, m_ref, v_ref, p_ref, g_out, d_out, m_out, v_out):
        g = p_ref[0, :tr, :C].astype(F32)
        for q in range(1, P):
            g = g + p_ref[q, :tr, :C].astype(F32)
        mn = ADAM_B1 * m_ref[...] + (1.0 - ADAM_B1) * g
        vn = ADAM_B2 * v_ref[...] + (1.0 - ADAM_B2) * (g * g)
        m_hat = mn / (1.0 - ADAM_B1 ** ADAM_STEP)
        v_hat = vn / (1.0 - ADAM_B2 ** ADAM_STEP)
        g_out[...] = g
        d_out[...] = -ADAM_LR * (m_hat / (jnp.sqrt(v_hat) + ADAM_EPS) + ADAM_WD * w_ref[...])
        m_out[...] = mn
        v_out[...] = vn

    slab = pl.BlockSpec((None, tr, C), lambda r: (index, r, 0))
    flat = pl.BlockSpec((tr, C), lambda r: (r, 0))
    pspec = pl.BlockSpec((P, tr if Rp == R else Rp, Cp), lambda r: (0, r, 0))
    return pl.pallas_call(
        body, grid=(R // tr,), in_specs=[slab] * 3 + [pspec], out_specs=[flat] * 4,
        out_shape=[jax.ShapeDtypeStruct((R, C), F32)] * 4,
        compiler_params=_params("parallel"), name=name)(w, m, v, parts)


def _pad_to(a, axis, size):
    pad = [(0, 0)] * a.ndim
    pad[axis] = (0, size - a.shape[axis])
    return jnp.pad(a, pad)


def kernel(x, w_qkv_a, w_o_a, rel_bias, w_qkv_b, w_o_b, ffn_w_gate, ffn_w_up, ffn_w_down, ln_g, ln_b, loss_target, m_w_qkv_a, m_w_o_a, m_rel_bias, m_w_qkv_b, m_w_o_b, m_ffn_w_gate, m_ffn_w_up, m_ffn_w_down, m_ln_g, m_ln_b, v_w_qkv_a, v_w_o_a, v_rel_bias, v_w_qkv_b, v_w_o_b, v_ffn_w_gate, v_ffn_w_up, v_ffn_w_down, v_ln_g, v_ln_b):
    nb, S, D = x.shape
    T = nb * S
    depth = ffn_w_gate.shape[0]
    H = D // HEAD_DIM
    fs = ffn_w_gate.shape[-1]
    fp = -(-fs // LANE) * LANE
    alpha = (2.0 * depth) ** 0.25
    cx, cy, cc = _place()
    me = 4 * cx + 2 * cy + cc

    ln_local = jnp.concatenate([ln_g.reshape(depth * 3, -1), ln_b.reshape(depth * 3, -1)], axis=0)
    ln_all = gather_small(ln_local, "gather_ln")
    ln_full = jnp.transpose(ln_all, (1, 0, 2)).reshape(2 * depth * 3, D)
    ln_gain = lambda i, s: ln_full[3 * i + s][None, :]
    ln_bias = lambda i, s: ln_full[3 * depth + 3 * i + s][None, :]

    table_t = _pad_to(rel_bias.T, 1, REL_PAD)
    band = jnp.transpose(bias_band(table_t, "bias_band"), (1, 0, 2))

    subs = []
    for i in range(depth):
        for s in (0, 1, 2):
            if s == 1:
                wq, wo = (w_qkv_a, w_o_a) if i % 2 == 0 else (w_qkv_b, w_o_b)
                subs.append(([wq[i // 2].astype(BF16), wo[i // 2].astype(BF16)], [True, False]))
            else:
                f = 0 if s == 0 else 1
                subs.append(([_pad_to(ffn_w_gate[i, f].astype(BF16), 1, fp), _pad_to(ffn_w_up[i, f].astype(BF16), 1, fp),
                              _pad_to(ffn_w_down[i, f].astype(BF16), 0, fp)], [True, True, False]))

    def start_gather(k, after):
        shards, by_cols = subs[k]
        shapes = [jax.ShapeDtypeStruct((s.shape[0], N_DEV * s.shape[1]) if col else (N_DEV * s.shape[0], s.shape[1]), BF16)
                  for s, col in zip(shards, by_cols)]
        return exchange_start(True, shards, shapes, by_cols, after, f"gather_start_{k}")

    xf = x.reshape(T, D)
    act, act_b = xf, xf.astype(BF16)
    gathers = {}
    for k in range(min(GATHERS_AHEAD, len(subs))):
        gathers[k] = start_gather(k, xf if k == 0 else gathers[k - 1]["token"])
    saved = []
    for i in range(depth):
        layer = {}
        for s in (0, 1, 2):
            k = 3 * i + s
            tag = f"L{i}S{s}"
            full = exchange_wait(gathers.pop(k), act_b, f"gather_wait_{k}")
            if s == 1:
                wqkv_f, wo_f = full
                qkv = mm_nn(act_b, wqkv_f, "qkv_" + tag)
                if i % 2 == 0:
                    att = attn_a_fwd(qkv, band, S, "attn_a_fwd_" + tag)
                else:
                    att = attn_b_fwd(qkv, S, "attn_b_fwd_" + tag)
                z, o, ob = mm_res_ln(att, wo_f, act, ln_gain(i, s), ln_bias(i, s), alpha, 1.0, "out_ln_" + tag)
                layer[s] = dict(x_b=act_b, qkv=qkv, att=att, z=z, wqkv=wqkv_f, wo=wo_f)
            else:
                wg_f, wu_f, wd_f = full
                h, u, a = ffn_up(act_b, wg_f, wu_f, "ffn_up_" + tag)
                z, o, ob = mm_res_ln(a, wd_f, act, ln_gain(i, s), ln_bias(i, s), alpha, 0.5, "down_ln_" + tag)
                layer[s] = dict(x_b=act_b, h=h, u=u, a=a, z=z, wg=wg_f, wu=wu_f, wd=wd_f)
            act, act_b = o, ob
            if k + GATHERS_AHEAD < len(subs):
                gathers[k + GATHERS_AHEAD] = start_gather(k + GATHERS_AHEAD, ob)
        saved.append(layer)

    loss_local, d_act = loss_head(act, loss_target.reshape(T, D), "loss_head")
    loss = lax.psum(loss_local[0, 0], ("x", "y", "c"))

    results = {}

    def update(name, w, m, v, index, parts):
        L = w.shape[0] if w.ndim == 3 else w.shape[0] * w.shape[1]
        flat = lambda t: t.reshape((L,) + t.shape[-2:])
        results.setdefault(name, {})[index] = adamw(flat(w), flat(m), flat(v), index, parts, f"adamw_{name}_{index}")

    def finish(entry, after):
        started, targets, tag = entry
        lands = exchange_wait(started, after, "scatter_wait_" + tag)
        for (name, w, m, v, index), parts in zip(targets, lands):
            update(name, w, m, v, index, parts)

    pending = []
    dbands = []
    dln_g = [None] * (3 * depth)
    dln_b = [None] * (3 * depth)
    for i in reversed(range(depth)):
        for s in (2, 1, 0):
            tag = f"L{i}S{s}"
            sv = saved[i][s]
            dz, dzb, dg, db = ln_bwd(sv["z"], ln_gain(i, s), d_act, "ln_bwd_" + tag)
            dln_g[3 * i + s], dln_b[3 * i + s] = dg, db
            if s == 1:
                j = i // 2
                d_att = mm_nt(dzb, sv["wo"], "att_bwd_" + tag)
                g_wo = mm_tn(sv["att"], dzb, 1.0, False, "dwo_" + tag)
                if i % 2 == 0:
                    dq, dk, dv, dband = attn_a_bwd(sv["qkv"], band, d_att, S, "attn_a_bwd_" + tag)
                    dbands.append(jnp.transpose(dband, (1, 0, 2)))
                else:
                    dq, dk, dv = attn_b_bwd(sv["qkv"], d_att, S, "attn_b_bwd_" + tag)
                dqkv = jnp.concatenate([dq, dk, dv], axis=1)
                g_wqkv = mm_tn(sv["x_b"], dqkv, 1.0, True, "dwqkv_" + tag)
                d_act = mm_nt_res([(dqkv, sv["wqkv"])], dz, alpha, "dx_mix_" + tag)
                grads = [g_wqkv, g_wo.reshape(N_DEV, D // N_DEV, D)]
                if i % 2 == 0:
                    targets = [("w_qkv_a", w_qkv_a, m_w_qkv_a, v_w_qkv_a, j), ("w_o_a", w_o_a, m_w_o_a, v_w_o_a, j)]
                else:
                    targets = [("w_qkv_b", w_qkv_b, m_w_qkv_b, v_w_qkv_b, j), ("w_o_b", w_o_b, m_w_o_b, v_w_o_b, j)]
            else:
                f = 0 if s == 0 else 1
                dh, du = ffn_bwd_mid(dzb, sv["wd"], sv["h"], sv["u"], 0.5, "ffn_mid_" + tag)
                g_wd = mm_tn(sv["a"], dzb, 0.5, False, "dwd_" + tag)
                g_wg = mm_tn(sv["x_b"], dh, 1.0, True, "dwg_" + tag)
                g_wu = mm_tn(sv["x_b"], du, 1.0, True, "dwu_" + tag)
                d_act = mm_nt_res([(dh, sv["wg"]), (du, sv["wu"])], dz, alpha, "dx_ffn_" + tag)
                grads = [g_wg, g_wu, g_wd.reshape(N_DEV, fp, D)]
                idx = 2 * i + f
                targets = [("ffn_w_gate", ffn_w_gate, m_ffn_w_gate, v_ffn_w_gate, idx),
                           ("ffn_w_up", ffn_w_up, m_ffn_w_up, v_ffn_w_up, idx),
                           ("ffn_w_down", ffn_w_down, m_ffn_w_down, v_ffn_w_down, idx)]
            shapes = [jax.ShapeDtypeStruct(g.shape, g.dtype) for g in grads]
            pending.append((exchange_start(False, grads, shapes, None, d_act, "scatter_start_" + tag), targets, tag))
            if len(pending) > SCATTERS_BEHIND:
                finish(pending.pop(0), d_act)
    grad_x = d_act.reshape(nb, S, D)
    while pending:
        finish(pending.pop(0), d_act)

    dtable_t = bias_band_bwd(dbands, "bias_band_bwd")
    small = jnp.concatenate(dln_g + dln_b + [_pad_to(dtable_t, 1, D)], axis=0)
    small = _pad_to(small, 0, -(-small.shape[0] // SUBLANE) * SUBLANE)
    total = sum_devices(gather_small(small, "gather_small_grads"), "sum_small_grads")
    n_ln = 3 * depth
    g_ln_g = lax.dynamic_slice_in_dim(total[:n_ln], me * (D // N_DEV), D // N_DEV, axis=1)
    g_ln_b = lax.dynamic_slice_in_dim(total[n_ln:2 * n_ln], me * (D // N_DEV), D // N_DEV, axis=1)
    g_rel = total[2 * n_ln:2 * n_ln + H, :N_REL].T
    as3 = lambda t: t.reshape((1, -1, t.shape[-1]))
    r_ln_g = adamw(as3(ln_g), as3(m_ln_g), as3(v_ln_g), 0, g_ln_g[None], "adamw_ln_g")
    r_ln_b = adamw(as3(ln_b), as3(m_ln_b), as3(v_ln_b), 0, g_ln_b[None], "adamw_ln_b")
    r_rel = adamw(as3(rel_bias), as3(m_rel_bias), as3(v_rel_bias), 0, g_rel[None], "adamw_rel_bias")

    def stacked(name, like, q):
        res = results[name]
        return jnp.stack([res[k][q] for k in range(len(res))]).reshape(like.shape)

    order = [("w_qkv_a", w_qkv_a), ("w_o_a", w_o_a), ("rel_bias", rel_bias), ("w_qkv_b", w_qkv_b), ("w_o_b", w_o_b),
             ("ffn_w_gate", ffn_w_gate), ("ffn_w_up", ffn_w_up), ("ffn_w_down", ffn_w_down), ("ln_g", ln_g), ("ln_b", ln_b)]
    single = {"rel_bias": r_rel, "ln_g": r_ln_g, "ln_b": r_ln_b}
    outs = [loss, grad_x]
    for q in range(4):
        for name, like in order:
            outs.append(single[name][q].reshape(like.shape) if name in single else stacked(name, like, q))
    return tuple(outs)
```

```python
import functools

import jax
import jax.numpy as jnp
from jax import lax
from jax.experimental import pallas as pl
from jax.experimental.pallas import tpu as pltpu

BF16 = jnp.bfloat16
F32 = jnp.float32
MESH_ID = pl.DeviceIdType.MESH
ANY = pl.BlockSpec(memory_space=pl.ANY)

N_DEV = 8
LANE = 128
SUBLANE = 8
VMEM_LIMIT = 56 * 1024 * 1024

HEAD_DIM = 64
CHUNK = 64
LEFT_CHUNKS = 8
REL_CLIP = 128
N_REL = 2 * REL_CLIP + 1
REL_PAD = 384
LN_EPS = 1e-5
ADAM_LR, ADAM_B1, ADAM_B2, ADAM_EPS, ADAM_WD, ADAM_STEP = 0.001, 0.9, 0.999, 1e-08, 0.01, 10
NEG = -1e30

A_TQ = 128
A_NWB = LEFT_CHUNKS * CHUNK // A_TQ + 1
A_W = A_NWB * A_TQ
B_T = 256

NT_DIMS = (((1,), (1,)), ((), ()))
TN_DIMS = (((0,), (0,)), ((), ()))


def _tile(n, pref, unit=LANE):
    if n <= pref:
        return n
    t = pref - pref % unit
    while t > unit and n % t:
        t -= unit
    assert n % t == 0, (n, pref)
    return t


def _params(*sem):
    return pltpu.CompilerParams(dimension_semantics=sem, vmem_limit_bytes=VMEM_LIMIT)


def _split3(v):
    h = v.astype(BF16)
    r = v - h.astype(F32)
    m = r.astype(BF16)
    lo = (r - m.astype(F32)).astype(BF16)
    return h, m, lo


def _dot3(v, w):
    h, m, lo = _split3(v)
    return (jnp.dot(h, w, preferred_element_type=F32) + jnp.dot(m, w, preferred_element_type=F32)
            + jnp.dot(lo, w, preferred_element_type=F32))


def mm_nn(a, w, name):
    T, K = a.shape
    N = w.shape[1]
    tm, tn = _tile(T, 1024), _tile(N, 768)

    def body(a_ref, w_ref, o_ref):
        o_ref[...] = jnp.dot(a_ref[...], w_ref[...], preferred_element_type=F32).astype(o_ref.dtype)

    return pl.pallas_call(
        body, grid=(T // tm, N // tn),
        in_specs=[pl.BlockSpec((tm, K), lambda i, j: (i, 0)), pl.BlockSpec((K, tn), lambda i, j: (0, j))],
        out_specs=pl.BlockSpec((tm, tn), lambda i, j: (i, j)),
        out_shape=jax.ShapeDtypeStruct((T, N), BF16),
        compiler_params=_params("parallel", "parallel"), name=name)(a, w)


def ffn_up(xb, wg, wu, name):
    T, K = xb.shape
    N = wg.shape[1]
    tm, tn = _tile(T, 512), _tile(N, 768)

    def body(x_ref, wg_ref, wu_ref, h_ref, u_ref, a_ref):
        x = x_ref[...]
        h = jnp.dot(x, wg_ref[...], preferred_element_type=F32)
        u = jnp.dot(x, wu_ref[...], preferred_element_type=F32)
        h_ref[...] = h.astype(BF16)
        u_ref[...] = u.astype(BF16)
        a_ref[...] = (h * jax.nn.sigmoid(h) * u).astype(BF16)

    wspec = pl.BlockSpec((K, tn), lambda i, j: (0, j))
    ospec = pl.BlockSpec((tm, tn), lambda i, j: (i, j))
    return pl.pallas_call(
        body, grid=(T // tm, N // tn),
        in_specs=[pl.BlockSpec((tm, K), lambda i, j: (i, 0)), wspec, wspec],
        out_specs=[ospec, ospec, ospec],
        out_shape=[jax.ShapeDtypeStruct((T, N), BF16)] * 3,
        compiler_params=_params("parallel", "parallel"), name=name)(xb, wg, wu)


def mm_res_ln(a, w, x, g, b, alpha, scale, name):
    T, K = a.shape
    D = w.shape[1]
    tm = _tile(T, 256)

    def body(a_ref, w_ref, x_ref, g_ref, b_ref, z_ref, o_ref, ob_ref):
        y = jnp.dot(a_ref[...], w_ref[...], preferred_element_type=F32)
        z = alpha * x_ref[...] + scale * y
        mu = jnp.mean(z, axis=1, keepdims=True)
        zc = z - mu
        var = jnp.mean(zc * zc, axis=1, keepdims=True)
        o = zc * lax.rsqrt(var + LN_EPS) * g_ref[...] + b_ref[...]
        z_ref[...] = z
        o_ref[...] = o
        ob_ref[...] = o.astype(BF16)

    row = pl.BlockSpec((tm, D), lambda i: (i, 0))
    vec = pl.BlockSpec((1, D), lambda i: (0, 0))
    return pl.pallas_call(
        body, grid=(T // tm,),
        in_specs=[pl.BlockSpec((tm, K), lambda i: (i, 0)), pl.BlockSpec((K, D), lambda i: (0, 0)), row, vec, vec],
        out_specs=[row, row, row],
        out_shape=[jax.ShapeDtypeStruct((T, D), F32), jax.ShapeDtypeStruct((T, D), F32),
                   jax.ShapeDtypeStruct((T, D), BF16)],
        compiler_params=_params("parallel"), name=name)(a, w, x, g, b)


def ln_bwd(z, g, do, dep, name):
    T, D = z.shape
    tm = _tile(T, 512)

    def body(z_ref, g_ref, do_ref, dep_ref, dz_ref, dzb_ref, dg_ref, db_ref):
        @pl.when(pl.program_id(0) == 0)
        def _():
            dg_ref[...] = jnp.zeros_like(dg_ref)
            db_ref[...] = jnp.zeros_like(db_ref)

        zv = z_ref[...]
        dov = do_ref[...]
        mu = jnp.mean(zv, axis=1, keepdims=True)
        zc = zv - mu
        var = jnp.mean(zc * zc, axis=1, keepdims=True)
        rstd = lax.rsqrt(var + LN_EPS)
        xhat = zc * rstd
        dxhat = dov * g_ref[...]
        m1 = jnp.mean(dxhat, axis=1, keepdims=True)
        m2 = jnp.mean(dxhat * xhat, axis=1, keepdims=True)
        dz = rstd * (dxhat - m1 - xhat * m2)
        dz_ref[...] = dz
        dzb_ref[...] = dz.astype(BF16)
        dg_ref[...] += jnp.sum(dov * xhat, axis=0, keepdims=True)
        db_ref[...] += jnp.sum(dov, axis=0, keepdims=True)

    row = pl.BlockSpec((tm, D), lambda i: (i, 0))
    vec = pl.BlockSpec((1, D), lambda i: (0, 0))
    return pl.pallas_call(
        body, grid=(T // tm,), in_specs=[row, vec, row, ANY], out_specs=[row, row, vec, vec],
        out_shape=[jax.ShapeDtypeStruct((T, D), F32), jax.ShapeDtypeStruct((T, D), BF16),
                   jax.ShapeDtypeStruct((1, D), F32), jax.ShapeDtypeStruct((1, D), F32)],
        compiler_params=_params("arbitrary"), name=name)(z, g, do, dep)


def mm_nt(a, w, name):
    T, K = a.shape
    N = w.shape[0]
    tm, tn = _tile(T, 1024), _tile(N, 512)

    def body(a_ref, w_ref, o_ref):
        o_ref[...] = lax.dot_general(a_ref[...], w_ref[...], NT_DIMS, preferred_element_type=F32).astype(o_ref.dtype)

    return pl.pallas_call(
        body, grid=(T // tm, N // tn),
        in_specs=[pl.BlockSpec((tm, K), lambda i, j: (i, 0)), pl.BlockSpec((tn, K), lambda i, j: (j, 0))],
        out_specs=pl.BlockSpec((tm, tn), lambda i, j: (i, j)),
        out_shape=jax.ShapeDtypeStruct((T, N), BF16),
        compiler_params=_params("parallel", "parallel"), name=name)(a, w)


def ffn_bwd_mid(dzb, wd, h, u, scale, name):
    T, K = dzb.shape
    N = wd.shape[0]
    tm, tn = _tile(T, 512), _tile(N, 768)

    def body(dz_ref, w_ref, h_ref, u_ref, dh_ref, du_ref):
        da = scale * lax.dot_general(dz_ref[...], w_ref[...], NT_DIMS, preferred_element_type=F32)
        hv = h_ref[...].astype(F32)
        uv = u_ref[...].astype(F32)
        s = jax.nn.sigmoid(hv)
        silu = hv * s
        dh_ref[...] = (da * uv * (s + silu * (1.0 - s))).astype(BF16)
        du_ref[...] = (da * silu).astype(BF16)

    tile = pl.BlockSpec((tm, tn), lambda i, j: (i, j))
    return pl.pallas_call(
        body, grid=(T // tm, N // tn),
        in_specs=[pl.BlockSpec((tm, K), lambda i, j: (i, 0)), pl.BlockSpec((tn, K), lambda i, j: (j, 0)), tile, tile],
        out_specs=[tile, tile],
        out_shape=[jax.ShapeDtypeStruct((T, N), BF16)] * 2,
        compiler_params=_params("parallel", "parallel"), name=name)(dzb, wd, h, u)


def mm_nt_res(pairs, dz, alpha, name):
    T, N = pairs[0][0].shape
    D = pairs[0][1].shape[0]
    n = len(pairs)
    tm, tk = _tile(T, 512), _tile(N, 768)
    nk = N // tk

    def body(*refs):
        a_refs, w_refs = refs[:n], refs[n:2 * n]
        dz_ref, o_ref, acc_ref = refs[2 * n:]
        k = pl.program_id(1)

        @pl.when(k == 0)
        def _():
            acc_ref[...] = jnp.zeros_like(acc_ref)

        part = lax.dot_general(a_refs[0][...], w_refs[0][...], NT_DIMS, preferred_element_type=F32)
        for p in range(1, n):
            part += lax.dot_general(a_refs[p][...], w_refs[p][...], NT_DIMS, preferred_element_type=F32)
        acc_ref[...] += part

        @pl.when(k == nk - 1)
        def _():
            o_ref[...] = acc_ref[...] + alpha * dz_ref[...]

    row = pl.BlockSpec((tm, D), lambda i, k: (i, 0))
    return pl.pallas_call(
        body, grid=(T // tm, nk),
        in_specs=[pl.BlockSpec((tm, tk), lambda i, k: (i, k))] * n + [pl.BlockSpec((D, tk), lambda i, k: (0, k))] * n + [row],
        out_specs=row,
        out_shape=jax.ShapeDtypeStruct((T, D), F32),
        scratch_shapes=[pltpu.VMEM((tm, D), F32)],
        compiler_params=_params("parallel", "arbitrary"), name=name)(*[p[0] for p in pairs], *[p[1] for p in pairs], dz)


def mm_tn(a, b, scale, shard_cols, name):
    T, M = a.shape
    N = b.shape[1]
    tm, tk = _tile(M, 512), _tile(T, 512)
    nk = T // tk
    if shard_cols:
        ns = N // N_DEV
        per = 2 if (2 * ns) % 256 == 0 else 1
        tn = per * ns
        out_shape = jax.ShapeDtypeStruct((N_DEV, M, ns), BF16)
        out_spec = pl.BlockSpec((per, tm, ns), lambda i, j, k: (j, i, 0))
    else:
        tn = _tile(N, 1024)
        out_shape = jax.ShapeDtypeStruct((M, N), BF16)
        out_spec = pl.BlockSpec((tm, tn), lambda i, j, k: (i, j))

    def body(a_ref, b_ref, o_ref, acc_ref):
        k = pl.program_id(2)

        @pl.when(k == 0)
        def _():
            acc_ref[...] = jnp.zeros_like(acc_ref)

        acc_ref[...] += lax.dot_general(a_ref[...], b_ref[...], TN_DIMS, preferred_element_type=F32)

        @pl.when(k == nk - 1)
        def _():
            if shard_cols:
                for s in range(per):
                    o_ref[s] = (scale * acc_ref[:, s * ns:(s + 1) * ns]).astype(BF16)
            else:
                o_ref[...] = (scale * acc_ref[...]).astype(BF16)

    return pl.pallas_call(
        body, grid=(M // tm, N // tn, nk),
        in_specs=[pl.BlockSpec((tk, tm), lambda i, j, k: (k, i)), pl.BlockSpec((tk, tn), lambda i, j, k: (k, j))],
        out_specs=out_spec, out_shape=out_shape,
        scratch_shapes=[pltpu.VMEM((tm, tn), F32)],
        compiler_params=_params("parallel", "parallel", "arbitrary"), name=name)(a, b)


def loss_head(y, target, name):
    T, D = y.shape
    tm = _tile(T, 512)

    def body(y_ref, t_ref, l_ref, dy_ref):
        @pl.when(pl.program_id(0) == 0)
        def _():
            l_ref[...] = jnp.zeros_like(l_ref)

        e = y_ref[...] - t_ref[...]
        dy_ref[...] = e * (1.0 / D)
        rows = jnp.sum(e * e, axis=1, keepdims=True) * (0.5 / D)
        l_ref[...] += jnp.sum(rows, axis=0, keepdims=True)

    row = pl.BlockSpec((tm, D), lambda i: (i, 0))
    return pl.pallas_call(
        body, grid=(T // tm,), in_specs=[row, row],
        out_specs=[pl.BlockSpec((1, 1), lambda i: (0, 0)), row],
        out_shape=[jax.ShapeDtypeStruct((1, 1), F32), jax.ShapeDtypeStruct((T, D), F32)],
        compiler_params=_params("arbitrary"), name=name)(y, target)


def _rel_index(i, j):
    return jnp.clip(i - j + LEFT_CHUNKS * CHUNK, -REL_CLIP, REL_CLIP) + REL_CLIP


def bias_band(table_t, name):
    H = table_t.shape[0]
    rows = SUBLANE

    def body(t_ref, o_ref):
        i0 = pl.program_id(0) * rows
        parts = _split3(t_ref[...])
        r = lax.broadcasted_iota(jnp.int32, (REL_PAD, A_W), 0)
        j = lax.broadcasted_iota(jnp.int32, (REL_PAD, A_W), 1)
        for ii in range(rows):
            onehot = jnp.where(r == _rel_index(i0 + ii, j), 1.0, 0.0).astype(BF16)
            o_ref[ii] = sum(jnp.dot(p, onehot, preferred_element_type=F32) for p in parts)

    return pl.pallas_call(
        body, grid=(A_TQ // rows,),
        in_specs=[pl.BlockSpec((H, REL_PAD), lambda i: (0, 0))],
        out_specs=pl.BlockSpec((rows, H, A_W), lambda i: (i, 0, 0)),
        out_shape=jax.ShapeDtypeStruct((A_TQ, H, A_W), F32),
        compiler_params=_params("parallel"), name=name)(table_t)


def bias_band_bwd(dbands, name):
    H = dbands[0].shape[1]
    rows = SUBLANE
    n = len(dbands)

    def body(*refs):
        g_refs, o_ref = refs[:n], refs[n]

        @pl.when(pl.program_id(0) == 0)
        def _():
            o_ref[...] = jnp.zeros_like(o_ref)

        i0 = pl.program_id(0) * rows
        j = lax.broadcasted_iota(jnp.int32, (A_W, REL_PAD), 0)
        r = lax.broadcasted_iota(jnp.int32, (A_W, REL_PAD), 1)
        acc = jnp.zeros((H, REL_PAD), F32)
        for ii in range(rows):
            onehot = jnp.where(r == _rel_index(i0 + ii, j), 1.0, 0.0).astype(BF16)
            g = g_refs[0][ii]
            for q in range(1, n):
                g = g + g_refs[q][ii]
            acc += _dot3(g, onehot)
        o_ref[...] += acc

    spec = pl.BlockSpec((rows, H, A_W), lambda i: (i, 0, 0))
    return pl.pallas_call(
        body, grid=(A_TQ // rows,), in_specs=[spec] * n,
        out_specs=pl.BlockSpec((H, REL_PAD), lambda i: (0, 0)),
        out_shape=jax.ShapeDtypeStruct((H, REL_PAD), F32),
        compiler_params=_params("arbitrary"), name=name)(*dbands)


def _a_window(ref, qi):
    parts = []
    for d in range(A_NWB):
        kb = jnp.maximum(qi - (A_NWB - 1) + d, 0)
        parts.append(ref[pl.ds(pl.multiple_of(kb * A_TQ, A_TQ), A_TQ), :])
    return jnp.concatenate(parts, axis=0)


def _a_valid(qi):
    i = lax.broadcasted_iota(jnp.int32, (A_TQ, A_W), 0)
    j = lax.broadcasted_iota(jnp.int32, (A_TQ, A_W), 1)
    ic, jc = i // CHUNK, j // CHUNK
    return (jc >= ic) & (jc <= ic + LEFT_CHUNKS) & (j >= LEFT_CHUNKS * CHUNK - qi * A_TQ)


def _head_masks():
    lane = lax.broadcasted_iota(jnp.int32, (1, LANE), 1)
    return [lane < HEAD_DIM, lane >= HEAD_DIM]


def _a_probs(qm, kw, bias, valid):
    s = lax.dot_general(qm, kw, NT_DIMS, preferred_element_type=F32) * (HEAD_DIM ** -0.5) + bias
    s = jnp.where(valid, s, NEG)
    p = jnp.exp(s - jnp.max(s, axis=1, keepdims=True))
    return p / jnp.sum(p, axis=1, keepdims=True)


def _attn_specs(S, D, tq):
    hp_n = D // LANE
    nq = S // tq
    q = pl.BlockSpec((tq, LANE), lambda hp, b, qi: (b * nq + qi, hp))
    k = pl.BlockSpec((S, LANE), lambda hp, b, qi: (b, hp_n + hp))
    v = pl.BlockSpec((S, LANE), lambda hp, b, qi: (b, 2 * hp_n + hp))
    tile = pl.BlockSpec((tq, LANE), lambda hp, b, qi: (b * nq + qi, hp))
    seq = pl.BlockSpec((S, LANE), lambda hp, b, qi: (b, hp))
    return q, k, v, tile, seq


def attn_a_fwd(qkv, band, S, name):
    T, D3 = qkv.shape
    D = D3 // 3
    nb, nq = T // S, S // A_TQ

    def body(q_ref, k_ref, v_ref, b_ref, o_ref):
        qi = pl.program_id(2)
        q2 = q_ref[...]
        kw, vw = _a_window(k_ref, qi), _a_window(v_ref, qi)
        valid = _a_valid(qi)
        o = jnp.zeros((A_TQ, LANE), F32)
        for hh, hm in enumerate(_head_masks()):
            p = _a_probs(jnp.where(hm, q2, jnp.zeros_like(q2)), kw, b_ref[hh], valid)
            o = jnp.where(hm, jnp.dot(p.astype(BF16), vw, preferred_element_type=F32), o)
        o_ref[...] = o.astype(BF16)

    q, k, v, tile, _ = _attn_specs(S, D, A_TQ)
    return pl.pallas_call(
        body, grid=(D // LANE, nb, nq),
        in_specs=[q, k, v, pl.BlockSpec((2, A_TQ, A_W), lambda hp, b, qi: (hp, 0, 0))],
        out_specs=tile, out_shape=jax.ShapeDtypeStruct((T, D), BF16),
        compiler_params=_params("parallel", "parallel", "parallel"), name=name)(qkv, qkv, qkv, band)


def attn_a_bwd(qkv, band, do, S, name):
    T, D3 = qkv.shape
    D = D3 // 3
    nb, nq = T // S, S // A_TQ
    scale = HEAD_DIM ** -0.5

    def body(q_ref, k_ref, v_ref, b_ref, do_ref, dq_ref, dk_ref, dv_ref, db_ref, dk_acc, dv_acc):
        b, qi = pl.program_id(1), pl.program_id(2)

        @pl.when((b == 0) & (qi == 0))
        def _():
            db_ref[...] = jnp.zeros_like(db_ref)

        @pl.when(qi == 0)
        def _():
            dk_acc[...] = jnp.zeros_like(dk_acc)
            dv_acc[...] = jnp.zeros_like(dv_acc)

        q2, do2 = q_ref[...], do_ref[...]
        kw, vw = _a_window(k_ref, qi), _a_window(v_ref, qi)
        valid = _a_valid(qi)
        dq = jnp.zeros((A_TQ, LANE), F32)
        dkw = jnp.zeros((A_W, LANE), F32)
        dvw = jnp.zeros((A_W, LANE), F32)
        for hh, hm in enumerate(_head_masks()):
            qm = jnp.where(hm, q2, jnp.zeros_like(q2))
            dom = jnp.where(hm, do2, jnp.zeros_like(do2))
            p = _a_probs(qm, kw, b_ref[hh], valid)
            dp = lax.dot_general(dom, vw, NT_DIMS, preferred_element_type=F32)
            ds = p * (dp - jnp.sum(p * dp, axis=1, keepdims=True))
            db_ref[hh] += ds
            dsb = ds.astype(BF16)
            dq = jnp.where(hm, jnp.dot(dsb, kw, preferred_element_type=F32) * scale, dq)
            dkw += lax.dot_general(dsb, qm, TN_DIMS, preferred_element_type=F32) * scale
            dvw += lax.dot_general(p.astype(BF16), dom, TN_DIMS, preferred_element_type=F32)
        dq_ref[...] = dq.astype(BF16)
        for d in range(A_NWB):
            kb = jnp.maximum(qi - (A_NWB - 1) + d, 0)
            rows = pl.ds(pl.multiple_of(kb * A_TQ, A_TQ), A_TQ)
            dk_acc[rows, :] += dkw[d * A_TQ:(d + 1) * A_TQ]
            dv_acc[rows, :] += dvw[d * A_TQ:(d + 1) * A_TQ]

        @pl.when(qi == nq - 1)
        def _():
            dk_ref[...] = dk_acc[...].astype(BF16)
            dv_ref[...] = dv_acc[...].astype(BF16)

    q, k, v, tile, seq = _attn_specs(S, D, A_TQ)
    bspec = pl.BlockSpec((2, A_TQ, A_W), lambda hp, b, qi: (hp, 0, 0))
    act = jax.ShapeDtypeStruct((T, D), BF16)
    return pl.pallas_call(
        body, grid=(D // LANE, nb, nq),
        in_specs=[q, k, v, bspec, tile], out_specs=[tile, seq, seq, bspec],
        out_shape=[act, act, act, jax.ShapeDtypeStruct(band.shape, F32)],
        scratch_shapes=[pltpu.VMEM((S, LANE), F32), pltpu.VMEM((S, LANE), F32)],
        compiler_params=_params("arbitrary", "arbitrary", "arbitrary"), name=name)(qkv, qkv, qkv, band, do)


def _dot2(v, w):
    h = v.astype(BF16)
    lo = (v - h.astype(F32)).astype(BF16)
    return jnp.dot(h, w, preferred_element_type=F32) + jnp.dot(lo, w, preferred_element_type=F32)


def _b_logits(qm, kt, diagonal):
    z = lax.dot_general(qm, kt, NT_DIMS, preferred_element_type=F32)
    nz = -z
    soft = jnp.log(1.0 + jnp.exp(jnp.minimum(z, nz)))
    lb = jnp.minimum(z, 0.0) - soft
    l1m = jnp.minimum(nz, 0.0) - soft
    causal = None
    if diagonal:
        row = lax.broadcasted_iota(jnp.int32, (B_T, B_T), 0)
        col = lax.broadcasted_iota(jnp.int32, (B_T, B_T), 1)
        causal = col < row
        l1m = jnp.where(causal, l1m, 0.0)
    return z, lb, l1m, causal


def _tri(strict_lower):
    r = lax.broadcasted_iota(jnp.int32, (B_T, B_T), 0)
    c = lax.broadcasted_iota(jnp.int32, (B_T, B_T), 1)
    return jnp.where((r > c) if strict_lower else (r < c), 1.0, 0.0).astype(BF16)


def _scaled_heads(q2):
    qs = q2 * (HEAD_DIM ** -0.5)
    return [jnp.where(hm, qs, jnp.zeros_like(qs)) for hm in _head_masks()]


def attn_b_fwd(qkv, S, name):
    T, D3 = qkv.shape
    D = D3 // 3
    nb, nq = T // S, S // B_T

    def body(q_ref, k_ref, v_ref, o_ref):
        qi = pl.program_id(2)
        after = _tri(True)
        qms = _scaled_heads(q_ref[...])

        def tiles(kb, carry, diagonal):
            rows = pl.ds(pl.multiple_of(kb * B_T, B_T), B_T)
            kt, vt = k_ref[rows, :], v_ref[rows, :]
            out = []
            for hh in range(2):
                right, acc = carry[2 * hh], carry[2 * hh + 1]
                _, lb, l1m, causal = _b_logits(qms[hh], kt, diagonal)
                a = jnp.exp(lb + (right + _dot2(l1m, after)))
                if diagonal:
                    a = jnp.where(causal, a, 0.0)
                acc = acc + jnp.dot(a.astype(BF16), vt, preferred_element_type=F32)
                out += [right + jnp.sum(l1m, axis=1, keepdims=True), acc]
            return tuple(out)

        init = (jnp.zeros((B_T, 1), F32), jnp.zeros((B_T, LANE), F32)) * 2
        res = tiles(qi, init, True)
        res = lax.fori_loop(1, qi + 1, lambda it, carry: tiles(qi - it, carry, False), res)
        o_ref[...] = jnp.where(_head_masks()[0], res[1], res[3]).astype(BF16)

    q, k, v, tile, _ = _attn_specs(S, D, B_T)
    return pl.pallas_call(
        body, grid=(D // LANE, nb, nq), in_specs=[q, k, v], out_specs=tile,
        out_shape=jax.ShapeDtypeStruct((T, D), BF16),
        compiler_params=_params("parallel", "parallel", "parallel"), name=name)(qkv, qkv, qkv)


def attn_b_bwd(qkv, do, S, name):
    T, D3 = qkv.shape
    D = D3 // 3
    nb, nq = T // S, S // B_T
    scale = HEAD_DIM ** -0.5

    def body(q_ref, k_ref, v_ref, do_ref, dq_ref, dk_ref, dv_ref, dk_acc, dv_acc, z_s, g_s):
        qi = pl.program_id(2)

        @pl.when(qi == 0)
        def _():
            dk_acc[...] = jnp.zeros_like(dk_acc)
            dv_acc[...] = jnp.zeros_like(dv_acc)

        do2 = do_ref[...]
        after, before = _tri(True), _tri(False)
        masks = _head_masks()
        qms = _scaled_heads(q_ref[...])
        doms = [jnp.where(hm, do2, jnp.zeros_like(do2)) for hm in masks]

        def sweep_left(kb, rights, diagonal):
            rows = pl.ds(pl.multiple_of(kb * B_T, B_T), B_T)
            kt, vt = k_ref[rows, :], v_ref[rows, :]
            out = []
            dv = jnp.zeros((B_T, LANE), F32)
            for hh in range(2):
                z, lb, l1m, causal = _b_logits(qms[hh], kt, diagonal)
                a = jnp.exp(lb + (rights[hh] + _dot2(l1m, after)))
                if diagonal:
                    a = jnp.where(causal, a, 0.0)
                da = lax.dot_general(doms[hh], vt, NT_DIMS, preferred_element_type=F32)
                z_s[hh, kb] = z
                g_s[hh, kb] = da * a
                dv += lax.dot_general(a.astype(BF16), doms[hh], TN_DIMS, preferred_element_type=F32)
                out.append(rights[hh] + jnp.sum(l1m, axis=1, keepdims=True))
            dv_acc[rows, :] += dv
            return tuple(out)

        zero_col = jnp.zeros((B_T, 1), F32)
        rights = sweep_left(qi, (zero_col, zero_col), True)
        lax.fori_loop(1, qi + 1, lambda it, carry: sweep_left(qi - it, carry, False), rights)

        def sweep_right(kb, carry, diagonal):
            rows = pl.ds(pl.multiple_of(kb * B_T, B_T), B_T)
            kt = k_ref[rows, :]
            out = []
            dk = jnp.zeros((B_T, LANE), F32)
            for hh in range(2):
                left, dq_h = carry[2 * hh], carry[2 * hh + 1]
                z, g = z_s[hh, kb], g_s[hh, kb]
                beta = jax.nn.sigmoid(z)
                dz = g * (1.0 - beta) - beta * (left + _dot2(g, before))
                if diagonal:
                    row = lax.broadcasted_iota(jnp.int32, (B_T, B_T), 0)
                    col = lax.broadcasted_iota(jnp.int32, (B_T, B_T), 1)
                    dz = jnp.where(col < row, dz, 0.0)
                dzb = dz.astype(BF16)
                dk += lax.dot_general(dzb, qms[hh], TN_DIMS, preferred_element_type=F32)
                dq_h = dq_h + jnp.dot(dzb, kt, preferred_element_type=F32)
                out += [left + jnp.sum(g, axis=1, keepdims=True), dq_h]
            dk_acc[rows, :] += dk
            return tuple(out)

        init = (zero_col, jnp.zeros((B_T, LANE), F32)) * 2
        res = lax.fori_loop(0, qi, lambda kb, carry: sweep_right(kb, carry, False), init)
        res = sweep_right(qi, res, True)
        dq_ref[...] = (jnp.where(masks[0], res[1], res[3]) * scale).astype(BF16)

        @pl.when(qi == nq - 1)
        def _():
            dk_ref[...] = dk_acc[...].astype(BF16)
            dv_ref[...] = dv_acc[...].astype(BF16)

    q, k, v, tile, seq = _attn_specs(S, D, B_T)
    act = jax.ShapeDtypeStruct((T, D), BF16)
    return pl.pallas_call(
        body, grid=(D // LANE, nb, nq),
        in_specs=[q, k, v, tile], out_specs=[tile, seq, seq], out_shape=[act, act, act],
        scratch_shapes=[pltpu.VMEM((S, LANE), F32), pltpu.VMEM((S, LANE), F32),
                        pltpu.VMEM((2, nq, B_T, B_T), F32), pltpu.VMEM((2, nq, B_T, B_T), F32)],
        compiler_params=_params("arbitrary", "arbitrary", "arbitrary"), name=name)(qkv, qkv, qkv, do)


HBM = pl.BlockSpec(memory_space=pltpu.HBM)
SEM = pl.BlockSpec(memory_space=pltpu.SEMAPHORE)
N_PEERS = N_DEV - 1
GATHERS_AHEAD = 3
SCATTERS_BEHIND = 1


def _place():
    return lax.axis_index("x"), lax.axis_index("y"), lax.axis_index("c")


def _peers(x, y, c):
    return [(1 - x if r & 4 else x, 1 - y if r & 2 else y, 1 - c if r & 1 else c) for r in range(1, N_DEV)]


def _block(ref, shape, by_cols, j):
    r, w = shape
    if by_cols:
        return ref.at[:, pl.ds(pl.multiple_of(j * w, LANE), w)]
    return ref.at[pl.ds(pl.multiple_of(j * r, SUBLANE), r), :]


def _exchange_copies(gather, src_refs, land_refs, send_sems, recv_sems, by_cols):
    x, y, c = _place()
    me = 4 * x + 2 * y + c
    out = []
    for t, (src, land) in enumerate(zip(src_refs, land_refs)):
        for r, peer in enumerate(_peers(x, y, c)):
            pj = 4 * peer[0] + 2 * peer[1] + peer[2]
            if gather:
                mine, to_me, theirs = src, _block(land, src.shape, by_cols[t], me), _block(land, src.shape, by_cols[t], pj)
            else:
                mine, to_me, theirs = src.at[pj], land.at[me], land.at[pj]
            sems = dict(send_sem=send_sems.at[N_PEERS * t + r], recv_sem=recv_sems.at[N_PEERS * t + r],
                        device_id=peer, device_id_type=MESH_ID)
            out.append((pltpu.make_async_remote_copy(src_ref=mine, dst_ref=to_me, **sems),
                        pltpu.make_async_remote_copy(src_ref=mine, dst_ref=theirs, **sems)))
    return out


def exchange_start(gather, srcs, land_shapes, by_cols, after, name):
    n = len(srcs)

    def body(*refs):
        src_refs, land_refs = refs[:n], refs[n:2 * n]
        send_sems, recv_sems = refs[2 * n + 1], refs[2 * n + 2]
        token = refs[-1]
        for mine, _ in _exchange_copies(gather, src_refs, land_refs, send_sems, recv_sems, by_cols):
            mine.start()
        token[...] = jnp.zeros_like(token)

    lands = [pltpu.with_memory_space_constraint(lax.empty(s.shape, s.dtype), pltpu.HBM) for s in land_shapes]
    srcs = [pltpu.with_memory_space_constraint(s, pltpu.HBM) for s in srcs]
    res = pl.pallas_call(
        body, name=name,
        out_shape=(pltpu.SemaphoreType.DMA((N_PEERS * n,)), pltpu.SemaphoreType.DMA((N_PEERS * n,)),
                   *[pltpu.HBM(s.shape, s.dtype) for s in srcs], *[pltpu.HBM(s.shape, s.dtype) for s in land_shapes],
                   jax.ShapeDtypeStruct((SUBLANE, LANE), F32)),
        in_specs=[HBM] * (2 * n) + [ANY],
        out_specs=(SEM, SEM, *[HBM] * (2 * n), pl.BlockSpec(memory_space=pltpu.VMEM)),
        input_output_aliases={i: 2 + i for i in range(2 * n)},
        compiler_params=pltpu.CompilerParams(has_side_effects=pltpu.SideEffectType.DATAFLOW_SIDE_EFFECTING),
    )(*srcs, *lands, after)
    return dict(gather=gather, n=n, by_cols=by_cols, send=res[0], recv=res[1], srcs=res[2:2 + n],
                lands=res[2 + n:2 + 2 * n], token=res[-1])


def exchange_wait(started, after, name):
    n, gather, by_cols = started["n"], started["gather"], started["by_cols"]

    def body(*refs):
        src_refs, land_refs = refs[:n], refs[n:2 * n]
        send_sems, recv_sems = refs[2 * n], refs[2 * n + 1]
        local_sems = refs[-1]
        x, y, c = _place()
        me = 4 * x + 2 * y + c
        own = []
        for t, (src, land) in enumerate(zip(src_refs, land_refs)):
            if gather:
                own.append(pltpu.make_async_copy(src, _block(land, src.shape, by_cols[t], me), local_sems.at[t]))
            else:
                own.append(pltpu.make_async_copy(src.at[me], land.at[me], local_sems.at[t]))
            own[-1].start()
        for mine, theirs in _exchange_copies(gather, src_refs, land_refs, send_sems, recv_sems, by_cols):
            mine.wait_send()
            theirs.wait_recv()
        for cp in own:
            cp.wait()

    res = pl.pallas_call(
        body, name=name,
        out_shape=tuple(pltpu.HBM(s.shape, s.dtype) for s in (*started["srcs"], *started["lands"])),
        in_specs=[HBM] * (2 * n) + [SEM, SEM, ANY], out_specs=tuple([HBM] * (2 * n)),
        input_output_aliases={i: i for i in range(2 * n)},
        scratch_shapes=[pltpu.SemaphoreType.DMA((n,))],
        compiler_params=pltpu.CompilerParams(has_side_effects=pltpu.SideEffectType.DATAFLOW_SIDE_EFFECTING),
    )(*started["srcs"], *started["lands"], started["send"], started["recv"], after)
    return res[n:]


def gather_small(v, dep, name):
    R, C = v.shape

    def body(v_ref, dep_ref, o_ref, send_sems, recv_sems):
        x, y, c = _place()
        o_ref[4 * x + 2 * y + c] = v_ref[...]
        peers = _peers(x, y, c)

        def copy(r, owner, to):
            slot = o_ref.at[4 * owner[0] + 2 * owner[1] + owner[2]]
            return pltpu.make_async_remote_copy(
                src_ref=slot, dst_ref=slot, send_sem=send_sems.at[r], recv_sem=recv_sems.at[r],
                device_id=to, device_id_type=MESH_ID)

        sends = [copy(r, (x, y, c), peer) for r, peer in enumerate(peers)]
        for cp in sends:
            cp.start()
        for r, peer in enumerate(peers):
            copy(r, peer, (x, y, c)).wait_recv()
        for cp in sends:
            cp.wait_send()

    vm = pl.BlockSpec(memory_space=pltpu.VMEM)
    return pl.pallas_call(
        body, in_specs=[vm, ANY], out_specs=vm, out_shape=jax.ShapeDtypeStruct((N_DEV, R, C), F32),
        scratch_shapes=[pltpu.SemaphoreType.DMA((N_PEERS,)), pltpu.SemaphoreType.DMA((N_PEERS,))],
        name=name)(v, dep)


def sum_devices(g, name):
    _, R, C = g.shape

    def body(g_ref, o_ref):
        acc = g_ref[0]
        for j in range(1, N_DEV):
            acc = acc + g_ref[j]
        o_ref[...] = acc

    vm = pl.BlockSpec(memory_space=pltpu.VMEM)
    return pl.pallas_call(body, in_specs=[vm], out_specs=vm, out_shape=jax.ShapeDtypeStruct((R, C), F32), name=name)(g)


def adamw(w, m, v, index, parts, name):
    _, R, C = w.shape
    P, Rp, Cp = parts.shape
    tr = _tile(R, 512, SUBLANE) if Rp == R else R

    def body(w_ref, m_ref, v_ref, p_ref, g_out, d_out, m_out, v_out):
        g = p_ref[0, :tr, :C].astype(F32)
        for q in range(1, P):
            g = g + p_ref[q, :tr, :C].astype(F32)
        mn = ADAM_B1 * m_ref[...] + (1.0 - ADAM_B1) * g
        vn = ADAM_B2 * v_ref[...] + (1.0 - ADAM_B2) * (g * g)
        m_hat = mn / (1.0 - ADAM_B1 ** ADAM_STEP)
        v_hat = vn / (1.0 - ADAM_B2 ** ADAM_STEP)
        g_out[...] = g
        d_out[...] = -ADAM_LR * (m_hat / (jnp.sqrt(v_hat) + ADAM_EPS) + ADAM_WD * w_ref[...])
        m_out[...] = mn
        v_out[...] = vn

    slab = pl.BlockSpec((None, tr, C), lambda r: (index, r, 0))
    flat = pl.BlockSpec((tr, C), lambda r: (r, 0))
    pspec = pl.BlockSpec((P, tr if Rp == R else Rp, Cp), lambda r: (0, r, 0))
    return pl.pallas_call(
        body, grid=(R // tr,), in_specs=[slab] * 3 + [pspec], out_specs=[flat] * 4,
        out_shape=[jax.ShapeDtypeStruct((R, C), F32)] * 4,
        compiler_params=_params("parallel"), name=name)(w, m, v, parts)


def _pad_to(a, axis, size):
    pad = [(0, 0)] * a.ndim
    pad[axis] = (0, size - a.shape[axis])
    return jnp.pad(a, pad)


def kernel(x, w_qkv_a, w_o_a, rel_bias, w_qkv_b, w_o_b, ffn_w_gate, ffn_w_up, ffn_w_down, ln_g, ln_b, loss_target, m_w_qkv_a, m_w_o_a, m_rel_bias, m_w_qkv_b, m_w_o_b, m_ffn_w_gate, m_ffn_w_up, m_ffn_w_down, m_ln_g, m_ln_b, v_w_qkv_a, v_w_o_a, v_rel_bias, v_w_qkv_b, v_w_o_b, v_ffn_w_gate, v_ffn_w_up, v_ffn_w_down, v_ln_g, v_ln_b):
    nb, S, D = x.shape
    T = nb * S
    depth = ffn_w_gate.shape[0]
    H = D // HEAD_DIM
    fs = ffn_w_gate.shape[-1]
    fp = -(-fs // LANE) * LANE
    alpha = (2.0 * depth) ** 0.25
    cx, cy, cc = _place()
    me = 4 * cx + 2 * cy + cc

    ln_local = jnp.concatenate([ln_g.reshape(depth * 3, -1), ln_b.reshape(depth * 3, -1)], axis=0)
    ln_all = gather_small(ln_local, ln_local, "gather_ln")
    ln_full = jnp.transpose(ln_all, (1, 0, 2)).reshape(2 * depth * 3, D)
    ln_gain = lambda i, s: ln_full[3 * i + s][None, :]
    ln_bias = lambda i, s: ln_full[3 * depth + 3 * i + s][None, :]

    table_t = _pad_to(rel_bias.T, 1, REL_PAD)
    band = jnp.transpose(bias_band(table_t, "bias_band"), (1, 0, 2))

    subs = []
    for i in range(depth):
        for s in (0, 1, 2):
            if s == 1:
                wq, wo = (w_qkv_a, w_o_a) if i % 2 == 0 else (w_qkv_b, w_o_b)
                subs.append(([wq[i // 2].astype(BF16), wo[i // 2].astype(BF16)], [True, False]))
            else:
                f = 0 if s == 0 else 1
                subs.append(([_pad_to(ffn_w_gate[i, f].astype(BF16), 1, fp), _pad_to(ffn_w_up[i, f].astype(BF16), 1, fp),
                              _pad_to(ffn_w_down[i, f].astype(BF16), 0, fp)], [True, True, False]))

    def start_gather(k, after):
        shards, by_cols = subs[k]
        shapes = [jax.ShapeDtypeStruct((s.shape[0], N_DEV * s.shape[1]) if col else (N_DEV * s.shape[0], s.shape[1]), BF16)
                  for s, col in zip(shards, by_cols)]
        return exchange_start(True, shards, shapes, by_cols, after, f"gather_start_{k}")

    xf = x.reshape(T, D)
    act, act_b = xf, xf.astype(BF16)
    gathers = {}
    for k in range(min(GATHERS_AHEAD, len(subs))):
        gathers[k] = start_gather(k, xf if k == 0 else gathers[k - 1]["token"])
    newest = gathers[k]["token"]
    saved = []
    for i in range(depth):
        layer = {}
        for s in (0, 1, 2):
            k = 3 * i + s
            tag = f"L{i}S{s}"
            if 0 < k and k + GATHERS_AHEAD - 1 < len(subs):
                gathers[k + GATHERS_AHEAD - 1] = start_gather(k + GATHERS_AHEAD - 1, act_b)
                newest = gathers[k + GATHERS_AHEAD - 1]["token"]
            full = exchange_wait(gathers.pop(k), newest, f"gather_wait_{k}")
            if s == 1:
                wqkv_f, wo_f = full
                qkv = mm_nn(act_b, wqkv_f, "qkv_" + tag)
                if i % 2 == 0:
                    att = attn_a_fwd(qkv, band, S, "attn_a_fwd_" + tag)
                else:
                    att = attn_b_fwd(qkv, S, "attn_b_fwd_" + tag)
                z, o, ob = mm_res_ln(att, wo_f, act, ln_gain(i, s), ln_bias(i, s), alpha, 1.0, "out_ln_" + tag)
                layer[s] = dict(x_b=act_b, qkv=qkv, att=att, z=z, wqkv=wqkv_f, wo=wo_f)
            else:
                wg_f, wu_f, wd_f = full
                h, u, a = ffn_up(act_b, wg_f, wu_f, "ffn_up_" + tag)
                z, o, ob = mm_res_ln(a, wd_f, act, ln_gain(i, s), ln_bias(i, s), alpha, 0.5, "down_ln_" + tag)
                layer[s] = dict(x_b=act_b, h=h, u=u, a=a, z=z, wg=wg_f, wu=wu_f, wd=wd_f)
            act, act_b = o, ob
        saved.append(layer)

    loss_local, d_act = loss_head(act, loss_target.reshape(T, D), "loss_head")
    loss = lax.psum(loss_local[0, 0], ("x", "y", "c"))

    results = {}

    def update(name, w, m, v, index, parts):
        L = w.shape[0] if w.ndim == 3 else w.shape[0] * w.shape[1]
        flat = lambda t: t.reshape((L,) + t.shape[-2:])
        results.setdefault(name, {})[index] = adamw(flat(w), flat(m), flat(v), index, parts, f"adamw_{name}_{index}")

    def finish(entry, after):
        started, targets, tag = entry
        lands = exchange_wait(started, after, "scatter_wait_" + tag)
        for (name, w, m, v, index), parts in zip(targets, lands):
            update(name, w, m, v, index, parts)

    pending = []
    started = d_act
    dbands = []
    dln_g = [None] * (3 * depth)
    dln_b = [None] * (3 * depth)
    for i in reversed(range(depth)):
        for s in (2, 1, 0):
            tag = f"L{i}S{s}"
            sv = saved[i][s]
            dz, dzb, dg, db = ln_bwd(sv["z"], ln_gain(i, s), d_act, started, "ln_bwd_" + tag)
            dln_g[3 * i + s], dln_b[3 * i + s] = dg, db
            if s == 1:
                j = i // 2
                d_att = mm_nt(dzb, sv["wo"], "att_bwd_" + tag)
                g_wo = mm_tn(sv["att"], dzb, 1.0, False, "dwo_" + tag)
                if i % 2 == 0:
                    dq, dk, dv, dband = attn_a_bwd(sv["qkv"], band, d_att, S, "attn_a_bwd_" + tag)
                    dbands.append(jnp.transpose(dband, (1, 0, 2)))
                else:
                    dq, dk, dv = attn_b_bwd(sv["qkv"], d_att, S, "attn_b_bwd_" + tag)
                dqkv = jnp.concatenate([dq, dk, dv], axis=1)
                g_wqkv = mm_tn(sv["x_b"], dqkv, 1.0, True, "dwqkv_" + tag)
                d_act = mm_nt_res([(dqkv, sv["wqkv"])], dz, alpha, "dx_mix_" + tag)
                grads = [g_wqkv, g_wo.reshape(N_DEV, D // N_DEV, D)]
                if i % 2 == 0:
                    targets = [("w_qkv_a", w_qkv_a, m_w_qkv_a, v_w_qkv_a, j), ("w_o_a", w_o_a, m_w_o_a, v_w_o_a, j)]
                else:
                    targets = [("w_qkv_b", w_qkv_b, m_w_qkv_b, v_w_qkv_b, j), ("w_o_b", w_o_b, m_w_o_b, v_w_o_b, j)]
            else:
                f = 0 if s == 0 else 1
                dh, du = ffn_bwd_mid(dzb, sv["wd"], sv["h"], sv["u"], 0.5, "ffn_mid_" + tag)
                g_wd = mm_tn(sv["a"], dzb, 0.5, False, "dwd_" + tag)
                g_wg = mm_tn(sv["x_b"], dh, 1.0, True, "dwg_" + tag)
                g_wu = mm_tn(sv["x_b"], du, 1.0, True, "dwu_" + tag)
                d_act = mm_nt_res([(dh, sv["wg"]), (du, sv["wu"])], dz, alpha, "dx_ffn_" + tag)
                grads = [g_wg, g_wu, g_wd.reshape(N_DEV, fp, D)]
                idx = 2 * i + f
                targets = [("ffn_w_gate", ffn_w_gate, m_ffn_w_gate, v_ffn_w_gate, idx),
                           ("ffn_w_up", ffn_w_up, m_ffn_w_up, v_ffn_w_up, idx),
                           ("ffn_w_down", ffn_w_down, m_ffn_w_down, v_ffn_w_down, idx)]
            shapes = [jax.ShapeDtypeStruct(g.shape, g.dtype) for g in grads]
            pending.append((exchange_start(False, grads, shapes, None, d_act, "scatter_start_" + tag), targets, tag))
            started = pending[-1][0]["token"]
            if len(pending) > SCATTERS_BEHIND:
                finish(pending.pop(0), d_act)
    grad_x = d_act.reshape(nb, S, D)

    dtable_t = bias_band_bwd(dbands, "bias_band_bwd")
    small = jnp.concatenate(dln_g + dln_b + [_pad_to(dtable_t, 1, D)], axis=0)
    small = _pad_to(small, 0, -(-small.shape[0] // SUBLANE) * SUBLANE)
    total = sum_devices(gather_small(small, started, "gather_small_grads"), "sum_small_grads")
    while pending:
        finish(pending.pop(0), total)
    n_ln = 3 * depth
    g_ln_g = lax.dynamic_slice_in_dim(total[:n_ln], me * (D // N_DEV), D // N_DEV, axis=1)
    g_ln_b = lax.dynamic_slice_in_dim(total[n_ln:2 * n_ln], me * (D // N_DEV), D // N_DEV, axis=1)
    g_rel = total[2 * n_ln:2 * n_ln + H, :N_REL].T
    as3 = lambda t: t.reshape((1, -1, t.shape[-1]))
    r_ln_g = adamw(as3(ln_g), as3(m_ln_g), as3(v_ln_g), 0, g_ln_g[None], "adamw_ln_g")
    r_ln_b = adamw(as3(ln_b), as3(m_ln_b), as3(v_ln_b), 0, g_ln_b[None], "adamw_ln_b")
    r_rel = adamw(as3(rel_bias), as3(m_rel_bias), as3(v_rel_bias), 0, g_rel[None], "adamw_rel_bias")

    def stacked(name, like, q):
        res = results[name]
        return jnp.stack([res[k][q] for k in range(len(res))]).reshape(like.shape)

    order = [("w_qkv_a", w_qkv_a), ("w_o_a", w_o_a), ("rel_bias", rel_bias), ("w_qkv_b", w_qkv_b), ("w_o_b", w_o_b),
             ("ffn_w_gate", ffn_w_gate), ("ffn_w_up", ffn_w_up), ("ffn_w_down", ffn_w_down), ("ln_g", ln_g), ("ln_b", ln_b)]
    single = {"rel_bias": r_rel, "ln_g": r_ln_g, "ln_b": r_ln_b}
    outs = [loss, grad_x]
    for q in range(4):
        for name, like in order:
            outs.append(single[name][q].reshape(like.shape) if name in single else stacked(name, like, q))
    return tuple(outs)
```

```python
import functools

import jax
import jax.numpy as jnp
from jax import lax
from jax.experimental import pallas as pl
from jax.experimental.pallas import tpu as pltpu

BF16 = jnp.bfloat16
F32 = jnp.float32
MESH_ID = pl.DeviceIdType.MESH
ANY = pl.BlockSpec(memory_space=pl.ANY)

N_DEV = 8
LANE = 128
SUBLANE = 8
VMEM_LIMIT = 56 * 1024 * 1024

HEAD_DIM = 64
CHUNK = 64
LEFT_CHUNKS = 8
REL_CLIP = 128
N_REL = 2 * REL_CLIP + 1
REL_PAD = 384
LN_EPS = 1e-5
ADAM_LR, ADAM_B1, ADAM_B2, ADAM_EPS, ADAM_WD, ADAM_STEP = 0.001, 0.9, 0.999, 1e-08, 0.01, 10
NEG = -1e30

A_TQ = 128
A_NWB = LEFT_CHUNKS * CHUNK // A_TQ + 1
A_W = A_NWB * A_TQ
B_T = 256

NT_DIMS = (((1,), (1,)), ((), ()))
TN_DIMS = (((0,), (0,)), ((), ()))


def _tile(n, pref, unit=LANE):
    if n <= pref:
        return n
    t = pref - pref % unit
    while t > unit and n % t:
        t -= unit
    assert n % t == 0, (n, pref)
    return t


def _params(*sem):
    return pltpu.CompilerParams(dimension_semantics=sem, vmem_limit_bytes=VMEM_LIMIT)


def _split3(v):
    h = v.astype(BF16)
    r = v - h.astype(F32)
    m = r.astype(BF16)
    lo = (r - m.astype(F32)).astype(BF16)
    return h, m, lo


def _dot3(v, w):
    h, m, lo = _split3(v)
    return (jnp.dot(h, w, preferred_element_type=F32) + jnp.dot(m, w, preferred_element_type=F32)
            + jnp.dot(lo, w, preferred_element_type=F32))


def mm_nn(a, w, name):
    T, K = a.shape
    N = w.shape[1]
    tm, tn = _tile(T, 1024), _tile(N, 768)

    def body(a_ref, w_ref, o_ref):
        o_ref[...] = jnp.dot(a_ref[...], w_ref[...], preferred_element_type=F32).astype(o_ref.dtype)

    return pl.pallas_call(
        body, grid=(T // tm, N // tn),
        in_specs=[pl.BlockSpec((tm, K), lambda i, j: (i, 0)), pl.BlockSpec((K, tn), lambda i, j: (0, j))],
        out_specs=pl.BlockSpec((tm, tn), lambda i, j: (i, j)),
        out_shape=jax.ShapeDtypeStruct((T, N), BF16),
        compiler_params=_params("parallel", "parallel"), name=name)(a, w)


def ffn_up(xb, wg, wu, name):
    T, K = xb.shape
    N = wg.shape[1]
    tm, tn = _tile(T, 512), _tile(N, 768)

    def body(x_ref, wg_ref, wu_ref, h_ref, u_ref, a_ref):
        x = x_ref[...]
        h = jnp.dot(x, wg_ref[...], preferred_element_type=F32)
        u = jnp.dot(x, wu_ref[...], preferred_element_type=F32)
        h_ref[...] = h.astype(BF16)
        u_ref[...] = u.astype(BF16)
        a_ref[...] = (h * jax.nn.sigmoid(h) * u).astype(BF16)

    wspec = pl.BlockSpec((K, tn), lambda i, j: (0, j))
    ospec = pl.BlockSpec((tm, tn), lambda i, j: (i, j))
    return pl.pallas_call(
        body, grid=(T // tm, N // tn),
        in_specs=[pl.BlockSpec((tm, K), lambda i, j: (i, 0)), wspec, wspec],
        out_specs=[ospec, ospec, ospec],
        out_shape=[jax.ShapeDtypeStruct((T, N), BF16)] * 3,
        compiler_params=_params("parallel", "parallel"), name=name)(xb, wg, wu)


def mm_res_ln(a, w, x, g, b, alpha, scale, name):
    T, K = a.shape
    D = w.shape[1]
    tm = _tile(T, 256)

    def body(a_ref, w_ref, x_ref, g_ref, b_ref, z_ref, o_ref, ob_ref):
        y = jnp.dot(a_ref[...], w_ref[...], preferred_element_type=F32)
        z = alpha * x_ref[...] + scale * y
        mu = jnp.mean(z, axis=1, keepdims=True)
        zc = z - mu
        var = jnp.mean(zc * zc, axis=1, keepdims=True)
        o = zc * lax.rsqrt(var + LN_EPS) * g_ref[...] + b_ref[...]
        z_ref[...] = z
        o_ref[...] = o
        ob_ref[...] = o.astype(BF16)

    row = pl.BlockSpec((tm, D), lambda i: (i, 0))
    vec = pl.BlockSpec((1, D), lambda i: (0, 0))
    return pl.pallas_call(
        body, grid=(T // tm,),
        in_specs=[pl.BlockSpec((tm, K), lambda i: (i, 0)), pl.BlockSpec((K, D), lambda i: (0, 0)), row, vec, vec],
        out_specs=[row, row, row],
        out_shape=[jax.ShapeDtypeStruct((T, D), F32), jax.ShapeDtypeStruct((T, D), F32),
                   jax.ShapeDtypeStruct((T, D), BF16)],
        compiler_params=_params("parallel"), name=name)(a, w, x, g, b)


def ln_bwd(z, g, do, dep, name):
    T, D = z.shape
    tm = _tile(T, 512)

    def body(z_ref, g_ref, do_ref, dep_ref, dz_ref, dzb_ref, dg_ref, db_ref):
        @pl.when(pl.program_id(0) == 0)
        def _():
            dg_ref[...] = jnp.zeros_like(dg_ref)
            db_ref[...] = jnp.zeros_like(db_ref)

        zv = z_ref[...]
        dov = do_ref[...]
        mu = jnp.mean(zv, axis=1, keepdims=True)
        zc = zv - mu
        var = jnp.mean(zc * zc, axis=1, keepdims=True)
        rstd = lax.rsqrt(var + LN_EPS)
        xhat = zc * rstd
        dxhat = dov * g_ref[...]
        m1 = jnp.mean(dxhat, axis=1, keepdims=True)
        m2 = jnp.mean(dxhat * xhat, axis=1, keepdims=True)
        dz = rstd * (dxhat - m1 - xhat * m2)
        dz_ref[...] = dz
        dzb_ref[...] = dz.astype(BF16)
        dg_ref[...] += jnp.sum(dov * xhat, axis=0, keepdims=True)
        db_ref[...] += jnp.sum(dov, axis=0, keepdims=True)

    row = pl.BlockSpec((tm, D), lambda i: (i, 0))
    vec = pl.BlockSpec((1, D), lambda i: (0, 0))
    return pl.pallas_call(
        body, grid=(T // tm,), in_specs=[row, vec, row, ANY], out_specs=[row, row, vec, vec],
        out_shape=[jax.ShapeDtypeStruct((T, D), F32), jax.ShapeDtypeStruct((T, D), BF16),
                   jax.ShapeDtypeStruct((1, D), F32), jax.ShapeDtypeStruct((1, D), F32)],
        compiler_params=_params("arbitrary"), name=name)(z, g, do, dep)


def mm_nt(a, w, name):
    T, K = a.shape
    N = w.shape[0]
    tm, tn = _tile(T, 1024), _tile(N, 512)

    def body(a_ref, w_ref, o_ref):
        o_ref[...] = lax.dot_general(a_ref[...], w_ref[...], NT_DIMS, preferred_element_type=F32).astype(o_ref.dtype)

    return pl.pallas_call(
        body, grid=(T // tm, N // tn),
        in_specs=[pl.BlockSpec((tm, K), lambda i, j: (i, 0)), pl.BlockSpec((tn, K), lambda i, j: (j, 0))],
        out_specs=pl.BlockSpec((tm, tn), lambda i, j: (i, j)),
        out_shape=jax.ShapeDtypeStruct((T, N), BF16),
        compiler_params=_params("parallel", "parallel"), name=name)(a, w)


def ffn_bwd_mid(dzb, wd, h, u, scale, name):
    T, K = dzb.shape
    N = wd.shape[0]
    tm, tn = _tile(T, 512), _tile(N, 768)

    def body(dz_ref, w_ref, h_ref, u_ref, dh_ref, du_ref):
        da = scale * lax.dot_general(dz_ref[...], w_ref[...], NT_DIMS, preferred_element_type=F32)
        hv = h_ref[...].astype(F32)
        uv = u_ref[...].astype(F32)
        s = jax.nn.sigmoid(hv)
        silu = hv * s
        dh_ref[...] = (da * uv * (s + silu * (1.0 - s))).astype(BF16)
        du_ref[...] = (da * silu).astype(BF16)

    tile = pl.BlockSpec((tm, tn), lambda i, j: (i, j))
    return pl.pallas_call(
        body, grid=(T // tm, N // tn),
        in_specs=[pl.BlockSpec((tm, K), lambda i, j: (i, 0)), pl.BlockSpec((tn, K), lambda i, j: (j, 0)), tile, tile],
        out_specs=[tile, tile],
        out_shape=[jax.ShapeDtypeStruct((T, N), BF16)] * 2,
        compiler_params=_params("parallel", "parallel"), name=name)(dzb, wd, h, u)


def mm_nt_res(pairs, dz, alpha, name):
    T, N = pairs[0][0].shape
    D = pairs[0][1].shape[0]
    n = len(pairs)
    tm, tk = _tile(T, 512), _tile(N, 768)
    nk = N // tk

    def body(*refs):
        a_refs, w_refs = refs[:n], refs[n:2 * n]
        dz_ref, o_ref, acc_ref = refs[2 * n:]
        k = pl.program_id(1)

        @pl.when(k == 0)
        def _():
            acc_ref[...] = jnp.zeros_like(acc_ref)

        part = lax.dot_general(a_refs[0][...], w_refs[0][...], NT_DIMS, preferred_element_type=F32)
        for p in range(1, n):
            part += lax.dot_general(a_refs[p][...], w_refs[p][...], NT_DIMS, preferred_element_type=F32)
        acc_ref[...] += part

        @pl.when(k == nk - 1)
        def _():
            o_ref[...] = acc_ref[...] + alpha * dz_ref[...]

    row = pl.BlockSpec((tm, D), lambda i, k: (i, 0))
    return pl.pallas_call(
        body, grid=(T // tm, nk),
        in_specs=[pl.BlockSpec((tm, tk), lambda i, k: (i, k))] * n + [pl.BlockSpec((D, tk), lambda i, k: (0, k))] * n + [row],
        out_specs=row,
        out_shape=jax.ShapeDtypeStruct((T, D), F32),
        scratch_shapes=[pltpu.VMEM((tm, D), F32)],
        compiler_params=_params("parallel", "arbitrary"), name=name)(*[p[0] for p in pairs], *[p[1] for p in pairs], dz)


def mm_tn(a, b, scale, shard_cols, name):
    T, M = a.shape
    N = b.shape[1]
    tm, tk = _tile(M, 512), _tile(T, 2048)
    nk = T // tk
    if shard_cols:
        ns = N // N_DEV
        per = 2 if (2 * ns) % 256 == 0 else 1
        tn = per * ns
        out_shape = jax.ShapeDtypeStruct((N_DEV, M, ns), BF16)
        out_spec = pl.BlockSpec((per, tm, ns), lambda i, j, k: (j, i, 0))
    else:
        tn = _tile(N, 1024)
        out_shape = jax.ShapeDtypeStruct((M, N), BF16)
        out_spec = pl.BlockSpec((tm, tn), lambda i, j, k: (i, j))

    def body(a_ref, b_ref, o_ref, acc_ref):
        k = pl.program_id(2)

        @pl.when(k == 0)
        def _():
            acc_ref[...] = jnp.zeros_like(acc_ref)

        acc_ref[...] += lax.dot_general(a_ref[...], b_ref[...], TN_DIMS, preferred_element_type=F32)

        @pl.when(k == nk - 1)
        def _():
            if shard_cols:
                for s in range(per):
                    o_ref[s] = (scale * acc_ref[:, s * ns:(s + 1) * ns]).astype(BF16)
            else:
                o_ref[...] = (scale * acc_ref[...]).astype(BF16)

    return pl.pallas_call(
        body, grid=(M // tm, N // tn, nk),
        in_specs=[pl.BlockSpec((tk, tm), lambda i, j, k: (k, i)), pl.BlockSpec((tk, tn), lambda i, j, k: (k, j))],
        out_specs=out_spec, out_shape=out_shape,
        scratch_shapes=[pltpu.VMEM((tm, tn), F32)],
        compiler_params=_params("parallel", "parallel", "arbitrary"), name=name)(a, b)


def loss_head(y, target, name):
    T, D = y.shape
    tm = _tile(T, 512)

    def body(y_ref, t_ref, l_ref, dy_ref):
        @pl.when(pl.program_id(0) == 0)
        def _():
            l_ref[...] = jnp.zeros_like(l_ref)

        e = y_ref[...] - t_ref[...]
        dy_ref[...] = e * (1.0 / D)
        rows = jnp.sum(e * e, axis=1, keepdims=True) * (0.5 / D)
        l_ref[...] += jnp.sum(rows, axis=0, keepdims=True)

    row = pl.BlockSpec((tm, D), lambda i: (i, 0))
    return pl.pallas_call(
        body, grid=(T // tm,), in_specs=[row, row],
        out_specs=[pl.BlockSpec((1, 1), lambda i: (0, 0)), row],
        out_shape=[jax.ShapeDtypeStruct((1, 1), F32), jax.ShapeDtypeStruct((T, D), F32)],
        compiler_params=_params("arbitrary"), name=name)(y, target)


def _rel_index(i, j):
    return jnp.clip(i - j + LEFT_CHUNKS * CHUNK, -REL_CLIP, REL_CLIP) + REL_CLIP


def bias_band(table_t, name):
    H = table_t.shape[0]
    rows = SUBLANE

    def body(t_ref, o_ref):
        i0 = pl.program_id(0) * rows
        parts = _split3(t_ref[...])
        r = lax.broadcasted_iota(jnp.int32, (REL_PAD, A_W), 0)
        j = lax.broadcasted_iota(jnp.int32, (REL_PAD, A_W), 1)
        for ii in range(rows):
            onehot = jnp.where(r == _rel_index(i0 + ii, j), 1.0, 0.0).astype(BF16)
            o_ref[ii] = sum(jnp.dot(p, onehot, preferred_element_type=F32) for p in parts)

    return pl.pallas_call(
        body, grid=(A_TQ // rows,),
        in_specs=[pl.BlockSpec((H, REL_PAD), lambda i: (0, 0))],
        out_specs=pl.BlockSpec((rows, H, A_W), lambda i: (i, 0, 0)),
        out_shape=jax.ShapeDtypeStruct((A_TQ, H, A_W), F32),
        compiler_params=_params("parallel"), name=name)(table_t)


def bias_band_bwd(dbands, name):
    H = dbands[0].shape[1]
    rows = SUBLANE
    n = len(dbands)

    def body(*refs):
        g_refs, o_ref = refs[:n], refs[n]

        @pl.when(pl.program_id(0) == 0)
        def _():
            o_ref[...] = jnp.zeros_like(o_ref)

        i0 = pl.program_id(0) * rows
        j = lax.broadcasted_iota(jnp.int32, (A_W, REL_PAD), 0)
        r = lax.broadcasted_iota(jnp.int32, (A_W, REL_PAD), 1)
        acc = jnp.zeros((H, REL_PAD), F32)
        for ii in range(rows):
            onehot = jnp.where(r == _rel_index(i0 + ii, j), 1.0, 0.0).astype(BF16)
            g = g_refs[0][ii]
            for q in range(1, n):
                g = g + g_refs[q][ii]
            acc += _dot3(g, onehot)
        o_ref[...] += acc

    spec = pl.BlockSpec((rows, H, A_W), lambda i: (i, 0, 0))
    return pl.pallas_call(
        body, grid=(A_TQ // rows,), in_specs=[spec] * n,
        out_specs=pl.BlockSpec((H, REL_PAD), lambda i: (0, 0)),
        out_shape=jax.ShapeDtypeStruct((H, REL_PAD), F32),
        compiler_params=_params("arbitrary"), name=name)(*dbands)


def _a_window(ref, qi):
    parts = []
    for d in range(A_NWB):
        kb = jnp.maximum(qi - (A_NWB - 1) + d, 0)
        parts.append(ref[pl.ds(pl.multiple_of(kb * A_TQ, A_TQ), A_TQ), :])
    return jnp.concatenate(parts, axis=0)


def _a_valid(qi):
    i = lax.broadcasted_iota(jnp.int32, (A_TQ, A_W), 0)
    j = lax.broadcasted_iota(jnp.int32, (A_TQ, A_W), 1)
    ic, jc = i // CHUNK, j // CHUNK
    return (jc >= ic) & (jc <= ic + LEFT_CHUNKS) & (j >= LEFT_CHUNKS * CHUNK - qi * A_TQ)


def _head_masks():
    lane = lax.broadcasted_iota(jnp.int32, (1, LANE), 1)
    return [lane < HEAD_DIM, lane >= HEAD_DIM]


def _a_probs(qm, kw, bias, valid):
    s = lax.dot_general(qm, kw, NT_DIMS, preferred_element_type=F32) * (HEAD_DIM ** -0.5) + bias
    s = jnp.where(valid, s, NEG)
    p = jnp.exp(s - jnp.max(s, axis=1, keepdims=True))
    return p / jnp.sum(p, axis=1, keepdims=True)


def _attn_specs(S, D, tq):
    hp_n = D // LANE
    nq = S // tq
    q = pl.BlockSpec((tq, LANE), lambda hp, b, qi: (b * nq + qi, hp))
    k = pl.BlockSpec((S, LANE), lambda hp, b, qi: (b, hp_n + hp))
    v = pl.BlockSpec((S, LANE), lambda hp, b, qi: (b, 2 * hp_n + hp))
    tile = pl.BlockSpec((tq, LANE), lambda hp, b, qi: (b * nq + qi, hp))
    seq = pl.BlockSpec((S, LANE), lambda hp, b, qi: (b, hp))
    return q, k, v, tile, seq


def attn_a_fwd(qkv, band, S, name):
    T, D3 = qkv.shape
    D = D3 // 3
    nb, nq = T // S, S // A_TQ

    def body(q_ref, k_ref, v_ref, b_ref, o_ref):
        qi = pl.program_id(2)
        q2 = q_ref[...]
        kw, vw = _a_window(k_ref, qi), _a_window(v_ref, qi)
        valid = _a_valid(qi)
        o = jnp.zeros((A_TQ, LANE), F32)
        for hh, hm in enumerate(_head_masks()):
            p = _a_probs(jnp.where(hm, q2, jnp.zeros_like(q2)), kw, b_ref[hh], valid)
            o = jnp.where(hm, jnp.dot(p.astype(BF16), vw, preferred_element_type=F32), o)
        o_ref[...] = o.astype(BF16)

    q, k, v, tile, _ = _attn_specs(S, D, A_TQ)
    return pl.pallas_call(
        body, grid=(D // LANE, nb, nq),
        in_specs=[q, k, v, pl.BlockSpec((2, A_TQ, A_W), lambda hp, b, qi: (hp, 0, 0))],
        out_specs=tile, out_shape=jax.ShapeDtypeStruct((T, D), BF16),
        compiler_params=_params("parallel", "parallel", "parallel"), name=name)(qkv, qkv, qkv, band)


def attn_a_bwd(qkv, band, do, S, name):
    T, D3 = qkv.shape
    D = D3 // 3
    nb, nq = T // S, S // A_TQ
    scale = HEAD_DIM ** -0.5

    def body(q_ref, k_ref, v_ref, b_ref, do_ref, dq_ref, dk_ref, dv_ref, db_ref, dk_acc, dv_acc):
        b, qi = pl.program_id(1), pl.program_id(2)

        @pl.when((b == 0) & (qi == 0))
        def _():
            db_ref[...] = jnp.zeros_like(db_ref)

        @pl.when(qi == 0)
        def _():
            dk_acc[...] = jnp.zeros_like(dk_acc)
            dv_acc[...] = jnp.zeros_like(dv_acc)

        q2, do2 = q_ref[...], do_ref[...]
        kw, vw = _a_window(k_ref, qi), _a_window(v_ref, qi)
        valid = _a_valid(qi)
        dq = jnp.zeros((A_TQ, LANE), F32)
        dkw = jnp.zeros((A_W, LANE), F32)
        dvw = jnp.zeros((A_W, LANE), F32)
        for hh, hm in enumerate(_head_masks()):
            qm = jnp.where(hm, q2, jnp.zeros_like(q2))
            dom = jnp.where(hm, do2, jnp.zeros_like(do2))
            p = _a_probs(qm, kw, b_ref[hh], valid)
            dp = lax.dot_general(dom, vw, NT_DIMS, preferred_element_type=F32)
            ds = p * (dp - jnp.sum(p * dp, axis=1, keepdims=True))
            db_ref[hh] += ds
            dsb = ds.astype(BF16)
            dq = jnp.where(hm, jnp.dot(dsb, kw, preferred_element_type=F32) * scale, dq)
            dkw += lax.dot_general(dsb, qm, TN_DIMS, preferred_element_type=F32) * scale
            dvw += lax.dot_general(p.astype(BF16), dom, TN_DIMS, preferred_element_type=F32)
        dq_ref[...] = dq.astype(BF16)
        for d in range(A_NWB):
            kb = jnp.maximum(qi - (A_NWB - 1) + d, 0)
            rows = pl.ds(pl.multiple_of(kb * A_TQ, A_TQ), A_TQ)
            dk_acc[rows, :] += dkw[d * A_TQ:(d + 1) * A_TQ]
            dv_acc[rows, :] += dvw[d * A_TQ:(d + 1) * A_TQ]

        @pl.when(qi == nq - 1)
        def _():
            dk_ref[...] = dk_acc[...].astype(BF16)
            dv_ref[...] = dv_acc[...].astype(BF16)

    q, k, v, tile, seq = _attn_specs(S, D, A_TQ)
    bspec = pl.BlockSpec((2, A_TQ, A_W), lambda hp, b, qi: (hp, 0, 0))
    act = jax.ShapeDtypeStruct((T, D), BF16)
    return pl.pallas_call(
        body, grid=(D // LANE, nb, nq),
        in_specs=[q, k, v, bspec, tile], out_specs=[tile, seq, seq, bspec],
        out_shape=[act, act, act, jax.ShapeDtypeStruct(band.shape, F32)],
        scratch_shapes=[pltpu.VMEM((S, LANE), F32), pltpu.VMEM((S, LANE), F32)],
        compiler_params=_params("arbitrary", "arbitrary", "arbitrary"), name=name)(qkv, qkv, qkv, band, do)


def _dot2(v, w):
    h = v.astype(BF16)
    lo = (v - h.astype(F32)).astype(BF16)
    return jnp.dot(h, w, preferred_element_type=F32) + jnp.dot(lo, w, preferred_element_type=F32)


def _b_logits(qm, kt, diagonal):
    z = lax.dot_general(qm, kt, NT_DIMS, preferred_element_type=F32)
    nz = -z
    soft = jnp.log(1.0 + jnp.exp(jnp.minimum(z, nz)))
    lb = jnp.minimum(z, 0.0) - soft
    l1m = jnp.minimum(nz, 0.0) - soft
    causal = None
    if diagonal:
        row = lax.broadcasted_iota(jnp.int32, (B_T, B_T), 0)
        col = lax.broadcasted_iota(jnp.int32, (B_T, B_T), 1)
        causal = col < row
        l1m = jnp.where(causal, l1m, 0.0)
    return z, lb, l1m, causal


def _tri(strict_lower):
    r = lax.broadcasted_iota(jnp.int32, (B_T, B_T), 0)
    c = lax.broadcasted_iota(jnp.int32, (B_T, B_T), 1)
    return jnp.where((r > c) if strict_lower else (r < c), 1.0, 0.0).astype(BF16)


def _scaled_heads(q2):
    qs = q2 * (HEAD_DIM ** -0.5)
    return [jnp.where(hm, qs, jnp.zeros_like(qs)) for hm in _head_masks()]


def attn_b_fwd(qkv, S, name):
    T, D3 = qkv.shape
    D = D3 // 3
    nb, nq = T // S, S // B_T

    def body(q_ref, k_ref, v_ref, o_ref):
        qi = pl.program_id(2)
        after = _tri(True)
        qms = _scaled_heads(q_ref[...])

        def tiles(kbs, carry, diagonal):
            carry = list(carry)
            for kb in kbs:
                rows = pl.ds(pl.multiple_of(kb * B_T, B_T), B_T)
                kt, vt = k_ref[rows, :], v_ref[rows, :]
                for hh in range(2):
                    right, acc = carry[2 * hh], carry[2 * hh + 1]
                    _, lb, l1m, causal = _b_logits(qms[hh], kt, diagonal)
                    a = jnp.exp(lb + (right + _dot2(l1m, after)))
                    if diagonal:
                        a = jnp.where(causal, a, 0.0)
                    carry[2 * hh] = right + jnp.sum(l1m, axis=1, keepdims=True)
                    carry[2 * hh + 1] = acc + jnp.dot(a.astype(BF16), vt, preferred_element_type=F32)
            return tuple(carry)

        init = (jnp.zeros((B_T, 1), F32), jnp.zeros((B_T, LANE), F32)) * 2
        res = tiles([qi], init, True)
        res = lax.fori_loop(0, qi // 2, lambda p, c: tiles([qi - 1 - 2 * p, qi - 2 - 2 * p], c, False), res)
        res = lax.cond(qi % 2 == 1, lambda c: tiles([0], c, False), lambda c: c, res)
        o_ref[...] = jnp.where(_head_masks()[0], res[1], res[3]).astype(BF16)

    q, k, v, tile, _ = _attn_specs(S, D, B_T)
    return pl.pallas_call(
        body, grid=(D // LANE, nb, nq), in_specs=[q, k, v], out_specs=tile,
        out_shape=jax.ShapeDtypeStruct((T, D), BF16),
        compiler_params=_params("parallel", "parallel", "parallel"), name=name)(qkv, qkv, qkv)


def attn_b_bwd(qkv, do, S, name):
    T, D3 = qkv.shape
    D = D3 // 3
    nb, nq = T // S, S // B_T
    scale = HEAD_DIM ** -0.5

    def body(q_ref, k_ref, v_ref, do_ref, dq_ref, dk_ref, dv_ref, dk_acc, dv_acc, z_s, g_s):
        qi = pl.program_id(2)

        @pl.when(qi == 0)
        def _():
            dk_acc[...] = jnp.zeros_like(dk_acc)
            dv_acc[...] = jnp.zeros_like(dv_acc)

        do2 = do_ref[...]
        after, before = _tri(True), _tri(False)
        masks = _head_masks()
        qms = _scaled_heads(q_ref[...])
        doms = [jnp.where(hm, do2, jnp.zeros_like(do2)) for hm in masks]

        def sweep_left(kbs, rights, diagonal):
            rights = list(rights)
            for kb in kbs:
                rows = pl.ds(pl.multiple_of(kb * B_T, B_T), B_T)
                kt, vt = k_ref[rows, :], v_ref[rows, :]
                dv = jnp.zeros((B_T, LANE), F32)
                for hh in range(2):
                    z, lb, l1m, causal = _b_logits(qms[hh], kt, diagonal)
                    a = jnp.exp(lb + (rights[hh] + _dot2(l1m, after)))
                    if diagonal:
                        a = jnp.where(causal, a, 0.0)
                    da = lax.dot_general(doms[hh], vt, NT_DIMS, preferred_element_type=F32)
                    z_s[hh, kb] = z
                    g_s[hh, kb] = da * a
                    dv += lax.dot_general(a.astype(BF16), doms[hh], TN_DIMS, preferred_element_type=F32)
                    rights[hh] = rights[hh] + jnp.sum(l1m, axis=1, keepdims=True)
                dv_acc[rows, :] += dv
            return tuple(rights)

        zero_col = jnp.zeros((B_T, 1), F32)
        rights = sweep_left([qi], (zero_col, zero_col), True)
        rights = lax.fori_loop(0, qi // 2, lambda p, c: sweep_left([qi - 1 - 2 * p, qi - 2 - 2 * p], c, False), rights)
        lax.cond(qi % 2 == 1, lambda c: sweep_left([0], c, False), lambda c: c, rights)

        def sweep_right(kbs, carry, diagonal):
            carry = list(carry)
            for kb in kbs:
                rows = pl.ds(pl.multiple_of(kb * B_T, B_T), B_T)
                kt = k_ref[rows, :]
                dk = jnp.zeros((B_T, LANE), F32)
                for hh in range(2):
                    left, dq_h = carry[2 * hh], carry[2 * hh + 1]
                    z, g = z_s[hh, kb], g_s[hh, kb]
                    beta = jax.nn.sigmoid(z)
                    dz = g * (1.0 - beta) - beta * (left + _dot2(g, before))
                    if diagonal:
                        row = lax.broadcasted_iota(jnp.int32, (B_T, B_T), 0)
                        col = lax.broadcasted_iota(jnp.int32, (B_T, B_T), 1)
                        dz = jnp.where(col < row, dz, 0.0)
                    dzb = dz.astype(BF16)
                    dk += lax.dot_general(dzb, qms[hh], TN_DIMS, preferred_element_type=F32)
                    carry[2 * hh] = left + jnp.sum(g, axis=1, keepdims=True)
                    carry[2 * hh + 1] = dq_h + jnp.dot(dzb, kt, preferred_element_type=F32)
                dk_acc[rows, :] += dk
            return tuple(carry)

        init = (zero_col, jnp.zeros((B_T, LANE), F32)) * 2
        res = lax.fori_loop(0, qi // 2, lambda p, c: sweep_right([2 * p, 2 * p + 1], c, False), init)
        res = lax.cond(qi % 2 == 1, lambda c: sweep_right([qi - 1], c, False), lambda c: c, res)
        res = sweep_right([qi], res, True)
        dq_ref[...] = (jnp.where(masks[0], res[1], res[3]) * scale).astype(BF16)

        @pl.when(qi == nq - 1)
        def _():
            dk_ref[...] = dk_acc[...].astype(BF16)
            dv_ref[...] = dv_acc[...].astype(BF16)

    q, k, v, tile, seq = _attn_specs(S, D, B_T)
    act = jax.ShapeDtypeStruct((T, D), BF16)
    return pl.pallas_call(
        body, grid=(D // LANE, nb, nq),
        in_specs=[q, k, v, tile], out_specs=[tile, seq, seq], out_shape=[act, act, act],
        scratch_shapes=[pltpu.VMEM((S, LANE), F32), pltpu.VMEM((S, LANE), F32),
                        pltpu.VMEM((2, nq, B_T, B_T), F32), pltpu.VMEM((2, nq, B_T, B_T), F32)],
        compiler_params=_params("arbitrary", "arbitrary", "arbitrary"), name=name)(qkv, qkv, qkv, do)


HBM = pl.BlockSpec(memory_space=pltpu.HBM)
SEM = pl.BlockSpec(memory_space=pltpu.SEMAPHORE)
N_PEERS = N_DEV - 1
GATHERS_AHEAD = 3


def _place():
    return lax.axis_index("x"), lax.axis_index("y"), lax.axis_index("c")


def _peers(x, y, c):
    return [(1 - x if r & 4 else x, 1 - y if r & 2 else y, 1 - c if r & 1 else c) for r in range(1, N_DEV)]


def _block(ref, shape, by_cols, j):
    r, w = shape
    if by_cols:
        return ref.at[:, pl.ds(pl.multiple_of(j * w, LANE), w)]
    return ref.at[pl.ds(pl.multiple_of(j * r, SUBLANE), r), :]


def _exchange_copies(gather, src_refs, land_refs, send_sems, recv_sems, by_cols):
    x, y, c = _place()
    me = 4 * x + 2 * y + c
    out = []
    for t, (src, land) in enumerate(zip(src_refs, land_refs)):
        for r, peer in enumerate(_peers(x, y, c)):
            pj = 4 * peer[0] + 2 * peer[1] + peer[2]
            if gather:
                mine, to_me, theirs = src, _block(land, src.shape, by_cols[t], me), _block(land, src.shape, by_cols[t], pj)
            else:
                mine, to_me, theirs = src.at[pj], land.at[me], land.at[pj]
            sems = dict(send_sem=send_sems.at[N_PEERS * t + r], recv_sem=recv_sems.at[N_PEERS * t + r],
                        device_id=peer, device_id_type=MESH_ID)
            out.append((pltpu.make_async_remote_copy(src_ref=mine, dst_ref=to_me, **sems),
                        pltpu.make_async_remote_copy(src_ref=mine, dst_ref=theirs, **sems)))
    return out


def _own_copies(gather, src_refs, land_refs, own_sems, by_cols):
    x, y, c = _place()
    me = 4 * x + 2 * y + c
    out = []
    for t, (src, land) in enumerate(zip(src_refs, land_refs)):
        if gather:
            out.append(pltpu.make_async_copy(src, _block(land, src.shape, by_cols[t], me), own_sems.at[t]))
        else:
            out.append(pltpu.make_async_copy(src.at[me], land.at[me], own_sems.at[t]))
    return out


def exchange_start(gather, srcs, land_shapes, by_cols, after, name):
    n = len(srcs)

    def body(*refs):
        src_refs, land_refs = refs[:n], refs[n:2 * n]
        send_sems, recv_sems, own_sems = refs[2 * n + 1:2 * n + 4]
        token = refs[-1]
        for cp in _own_copies(gather, src_refs, land_refs, own_sems, by_cols):
            cp.start()
        for mine, _ in _exchange_copies(gather, src_refs, land_refs, send_sems, recv_sems, by_cols):
            mine.start()
        token[...] = jnp.zeros_like(token)

    lands = [pltpu.with_memory_space_constraint(lax.empty(s.shape, s.dtype), pltpu.HBM) for s in land_shapes]
    srcs = [pltpu.with_memory_space_constraint(s, pltpu.HBM) for s in srcs]
    res = pl.pallas_call(
        body, name=name,
        out_shape=(pltpu.SemaphoreType.DMA((N_PEERS * n,)), pltpu.SemaphoreType.DMA((N_PEERS * n,)),
                   pltpu.SemaphoreType.DMA((n,)),
                   *[pltpu.HBM(s.shape, s.dtype) for s in srcs], *[pltpu.HBM(s.shape, s.dtype) for s in land_shapes],
                   jax.ShapeDtypeStruct((SUBLANE, LANE), F32)),
        in_specs=[HBM] * (2 * n) + [ANY],
        out_specs=(SEM, SEM, SEM, *[HBM] * (2 * n), pl.BlockSpec(memory_space=pltpu.VMEM)),
        input_output_aliases={i: 3 + i for i in range(2 * n)},
        compiler_params=pltpu.CompilerParams(has_side_effects=pltpu.SideEffectType.DATAFLOW_SIDE_EFFECTING),
    )(*srcs, *lands, after)
    return dict(gather=gather, n=n, by_cols=by_cols, sems=res[:3], srcs=res[3:3 + n],
                lands=res[3 + n:3 + 2 * n], token=res[-1])


def exchange_wait(started, after, name):
    n, gather, by_cols = started["n"], started["gather"], started["by_cols"]

    def body(*refs):
        src_refs, land_refs = refs[:n], refs[n:2 * n]
        send_sems, recv_sems, own_sems = refs[2 * n:2 * n + 3]
        for mine, theirs in _exchange_copies(gather, src_refs, land_refs, send_sems, recv_sems, by_cols):
            mine.wait_send()
            theirs.wait_recv()
        for cp in _own_copies(gather, src_refs, land_refs, own_sems, by_cols):
            cp.wait()

    res = pl.pallas_call(
        body, name=name,
        out_shape=tuple(pltpu.HBM(s.shape, s.dtype) for s in (*started["srcs"], *started["lands"])),
        in_specs=[HBM] * (2 * n) + [SEM, SEM, SEM, ANY], out_specs=tuple([HBM] * (2 * n)),
        input_output_aliases={i: i for i in range(2 * n)},
        compiler_params=pltpu.CompilerParams(has_side_effects=pltpu.SideEffectType.DATAFLOW_SIDE_EFFECTING),
    )(*started["srcs"], *started["lands"], *started["sems"], after)
    return res[n:]


def gather_small(v, dep, name):
    R, C = v.shape

    def body(v_ref, dep_ref, o_ref, send_sems, recv_sems):
        x, y, c = _place()
        o_ref[4 * x + 2 * y + c] = v_ref[...]
        peers = _peers(x, y, c)

        def copy(r, owner, to):
            slot = o_ref.at[4 * owner[0] + 2 * owner[1] + owner[2]]
            return pltpu.make_async_remote_copy(
                src_ref=slot, dst_ref=slot, send_sem=send_sems.at[r], recv_sem=recv_sems.at[r],
                device_id=to, device_id_type=MESH_ID)

        sends = [copy(r, (x, y, c), peer) for r, peer in enumerate(peers)]
        for cp in sends:
            cp.start()
        for r, peer in enumerate(peers):
            copy(r, peer, (x, y, c)).wait_recv()
        for cp in sends:
            cp.wait_send()

    vm = pl.BlockSpec(memory_space=pltpu.VMEM)
    return pl.pallas_call(
        body, in_specs=[vm, ANY], out_specs=vm, out_shape=jax.ShapeDtypeStruct((N_DEV, R, C), F32),
        scratch_shapes=[pltpu.SemaphoreType.DMA((N_PEERS,)), pltpu.SemaphoreType.DMA((N_PEERS,))],
        name=name)(v, dep)


def sum_devices(g, name):
    _, R, C = g.shape

    def body(g_ref, o_ref):
        acc = g_ref[0]
        for j in range(1, N_DEV):
            acc = acc + g_ref[j]
        o_ref[...] = acc

    vm = pl.BlockSpec(memory_space=pltpu.VMEM)
    return pl.pallas_call(body, in_specs=[vm], out_specs=vm, out_shape=jax.ShapeDtypeStruct((R, C), F32), name=name)(g)


def adamw(w, m, v, index, parts, name):
    _, R, C = w.shape
    P, Rp, Cp = parts.shape
    tr = _tile(R, 512, SUBLANE) if Rp == R else R

    def body(w_ref, m_ref, v_ref, p_ref, g_out, d_out, m_out, v_out):
        g = p_ref[0, :tr, :C].astype(F32)
        for q in range(1, P):
            g = g + p_ref[q, :tr, :C].astype(F32)
        mn = ADAM_B1 * m_ref[...] + (1.0 - ADAM_B1) * g
        vn = ADAM_B2 * v_ref[...] + (1.0 - ADAM_B2) * (g * g)
        m_hat = mn / (1.0 - ADAM_B1 ** ADAM_STEP)
        v_hat = vn / (1.0 - ADAM_B2 ** ADAM_STEP)
        g_out[...] = g
        d_out[...] = -ADAM_LR * (m_hat / (jnp.sqrt(v_hat) + ADAM_EPS) + ADAM_WD * w_ref[...])
        m_out[...] = mn
        v_out[...] = vn

    slab = pl.BlockSpec((None, tr, C), lambda r: (index, r, 0))
    flat = pl.BlockSpec((tr, C), lambda r: (r, 0))
    pspec = pl.BlockSpec((P, tr if Rp == R else Rp, Cp), lambda r: (0, r, 0))
    return pl.pallas_call(
        body, grid=(R // tr,), in_specs=[slab] * 3 + [pspec], out_specs=[flat] * 4,
        out_shape=[jax.ShapeDtypeStruct((R, C), F32)] * 4,
        compiler_params=_params("parallel"), name=name)(w, m, v, parts)


def _pad_to(a, axis, size):
    pad = [(0, 0)] * a.ndim
    pad[axis] = (0, size - a.shape[axis])
    return jnp.pad(a, pad)


def kernel(x, w_qkv_a, w_o_a, rel_bias, w_qkv_b, w_o_b, ffn_w_gate, ffn_w_up, ffn_w_down, ln_g, ln_b, loss_target, m_w_qkv_a, m_w_o_a, m_rel_bias, m_w_qkv_b, m_w_o_b, m_ffn_w_gate, m_ffn_w_up, m_ffn_w_down, m_ln_g, m_ln_b, v_w_qkv_a, v_w_o_a, v_rel_bias, v_w_qkv_b, v_w_o_b, v_ffn_w_gate, v_ffn_w_up, v_ffn_w_down, v_ln_g, v_ln_b):
    nb, S, D = x.shape
    T = nb * S
    depth = ffn_w_gate.shape[0]
    H = D // HEAD_DIM
    fs = ffn_w_gate.shape[-1]
    fp = -(-fs // LANE) * LANE
    alpha = (2.0 * depth) ** 0.25
    cx, cy, cc = _place()
    me = 4 * cx + 2 * cy + cc

    ln_local = jnp.concatenate([ln_g.reshape(depth * 3, -1), ln_b.reshape(depth * 3, -1)], axis=0)
    ln_all = gather_small(ln_local, ln_local, "gather_ln")
    ln_full = jnp.transpose(ln_all, (1, 0, 2)).reshape(2 * depth * 3, D)
    ln_gain = lambda i, s: ln_full[3 * i + s][None, :]
    ln_bias = lambda i, s: ln_full[3 * depth + 3 * i + s][None, :]

    table_t = _pad_to(rel_bias.T, 1, REL_PAD)
    band = jnp.transpose(bias_band(table_t, "bias_band"), (1, 0, 2))

    subs = []
    for i in range(depth):
        for s in (0, 1, 2):
            if s == 1:
                wq, wo = (w_qkv_a, w_o_a) if i % 2 == 0 else (w_qkv_b, w_o_b)
                subs.append(([wq[i // 2].astype(BF16), wo[i // 2].astype(BF16)], [True, False]))
            else:
                f = 0 if s == 0 else 1
                subs.append(([_pad_to(ffn_w_gate[i, f].astype(BF16), 1, fp), _pad_to(ffn_w_up[i, f].astype(BF16), 1, fp),
                              _pad_to(ffn_w_down[i, f].astype(BF16), 0, fp)], [True, True, False]))

    def start_gather(k, after):
        shards, by_cols = subs[k]
        shapes = [jax.ShapeDtypeStruct((s.shape[0], N_DEV * s.shape[1]) if col else (N_DEV * s.shape[0], s.shape[1]), BF16)
                  for s, col in zip(shards, by_cols)]
        return exchange_start(True, shards, shapes, by_cols, after, f"gather_start_{k}")

    xf = x.reshape(T, D)
    act, act_b = xf, xf.astype(BF16)
    gathers = {}
    for k in range(min(GATHERS_AHEAD, len(subs))):
        gathers[k] = start_gather(k, xf if k == 0 else gathers[k - 1]["token"])
    newest = gathers[k]["token"]
    saved = []
    for i in range(depth):
        layer = {}
        for s in (0, 1, 2):
            k = 3 * i + s
            tag = f"L{i}S{s}"
            if 0 < k and k + GATHERS_AHEAD - 1 < len(subs):
                gathers[k + GATHERS_AHEAD - 1] = start_gather(k + GATHERS_AHEAD - 1, act_b)
                newest = gathers[k + GATHERS_AHEAD - 1]["token"]
            full = exchange_wait(gathers.pop(k), newest, f"gather_wait_{k}")
            if s == 1:
                wqkv_f, wo_f = full
                qkv = mm_nn(act_b, wqkv_f, "qkv_" + tag)
                if i % 2 == 0:
                    att = attn_a_fwd(qkv, band, S, "attn_a_fwd_" + tag)
                else:
                    att = attn_b_fwd(qkv, S, "attn_b_fwd_" + tag)
                z, o, ob = mm_res_ln(att, wo_f, act, ln_gain(i, s), ln_bias(i, s), alpha, 1.0, "out_ln_" + tag)
                layer[s] = dict(x_b=act_b, qkv=qkv, att=att, z=z, wqkv=wqkv_f, wo=wo_f)
            else:
                wg_f, wu_f, wd_f = full
                h, u, a = ffn_up(act_b, wg_f, wu_f, "ffn_up_" + tag)
                z, o, ob = mm_res_ln(a, wd_f, act, ln_gain(i, s), ln_bias(i, s), alpha, 0.5, "down_ln_" + tag)
                layer[s] = dict(x_b=act_b, h=h, u=u, a=a, z=z, wg=wg_f, wu=wu_f, wd=wd_f)
            act, act_b = o, ob
        saved.append(layer)

    loss_local, d_act = loss_head(act, loss_target.reshape(T, D), "loss_head")
    loss = lax.psum(loss_local[0, 0], ("x", "y", "c"))

    results = {}

    def update(name, w, m, v, index, parts):
        L = w.shape[0] if w.ndim == 3 else w.shape[0] * w.shape[1]
        flat = lambda t: t.reshape((L,) + t.shape[-2:])
        results.setdefault(name, {})[index] = adamw(flat(w), flat(m), flat(v), index, parts, f"adamw_{name}_{index}")

    def finish(entry, after):
        exchange, targets, tag = entry
        lands = exchange_wait(exchange, after, "scatter_wait_" + tag)
        for (name, w, m, v, index), parts in zip(targets, lands):
            update(name, w, m, v, index, parts)
        return results[targets[-1][0]][targets[-1][4]][0]

    pending = []
    started = d_act
    dbands = []
    dln_g = [None] * (3 * depth)
    dln_b = [None] * (3 * depth)
    for i in reversed(range(depth)):
        for s in (2, 1, 0):
            tag = f"L{i}S{s}"
            sv = saved[i][s]
            dz, dzb, dg, db = ln_bwd(sv["z"], ln_gain(i, s), d_act, started, "ln_bwd_" + tag)
            dln_g[3 * i + s], dln_b[3 * i + s] = dg, db
            if s == 1:
                j = i // 2
                d_att = mm_nt(dzb, sv["wo"], "att_bwd_" + tag)
                g_wo = mm_tn(sv["att"], dzb, 1.0, False, "dwo_" + tag)
                if i % 2 == 0:
                    dq, dk, dv, dband = attn_a_bwd(sv["qkv"], band, d_att, S, "attn_a_bwd_" + tag)
                    dbands.append(jnp.transpose(dband, (1, 0, 2)))
                else:
                    dq, dk, dv = attn_b_bwd(sv["qkv"], d_att, S, "attn_b_bwd_" + tag)
                dqkv = jnp.concatenate([dq, dk, dv], axis=1)
                g_wqkv = mm_tn(sv["x_b"], dqkv, 1.0, True, "dwqkv_" + tag)
                d_act = mm_nt_res([(dqkv, sv["wqkv"])], dz, alpha, "dx_mix_" + tag)
                grads = [g_wqkv, g_wo.reshape(N_DEV, D // N_DEV, D)]
                if i % 2 == 0:
                    targets = [("w_qkv_a", w_qkv_a, m_w_qkv_a, v_w_qkv_a, j), ("w_o_a", w_o_a, m_w_o_a, v_w_o_a, j)]
                else:
                    targets = [("w_qkv_b", w_qkv_b, m_w_qkv_b, v_w_qkv_b, j), ("w_o_b", w_o_b, m_w_o_b, v_w_o_b, j)]
            else:
                f = 0 if s == 0 else 1
                dh, du = ffn_bwd_mid(dzb, sv["wd"], sv["h"], sv["u"], 0.5, "ffn_mid_" + tag)
                g_wd = mm_tn(sv["a"], dzb, 0.5, False, "dwd_" + tag)
                g_wg = mm_tn(sv["x_b"], dh, 1.0, True, "dwg_" + tag)
                g_wu = mm_tn(sv["x_b"], du, 1.0, True, "dwu_" + tag)
                d_act = mm_nt_res([(dh, sv["wg"]), (du, sv["wu"])], dz, alpha, "dx_ffn_" + tag)
                grads = [g_wg, g_wu, g_wd.reshape(N_DEV, fp, D)]
                idx = 2 * i + f
                targets = [("ffn_w_gate", ffn_w_gate, m_ffn_w_gate, v_ffn_w_gate, idx),
                           ("ffn_w_up", ffn_w_up, m_ffn_w_up, v_ffn_w_up, idx),
                           ("ffn_w_down", ffn_w_down, m_ffn_w_down, v_ffn_w_down, idx)]
            after = d_act
            if i == 0 and s == 0:
                dtable_t = bias_band_bwd(dbands, "bias_band_bwd")
                small = jnp.concatenate(dln_g + dln_b + [_pad_to(dtable_t, 1, D)], axis=0)
                small = _pad_to(small, 0, -(-small.shape[0] // SUBLANE) * SUBLANE)
                total = after = sum_devices(gather_small(small, d_act, "gather_small_grads"), "sum_small_grads")
            shapes = [jax.ShapeDtypeStruct(g.shape, g.dtype) for g in grads]
            pending.append((exchange_start(False, grads, shapes, None, after, "scatter_start_" + tag), targets, tag))
            started = pending[-1][0]["token"]
    grad_x = d_act.reshape(nb, S, D)

    last = pending.pop()
    done = started
    for entry in pending:
        done = finish(entry, done)
    finish(last, done)
    n_ln = 3 * depth
    g_ln_g = lax.dynamic_slice_in_dim(total[:n_ln], me * (D // N_DEV), D // N_DEV, axis=1)
    g_ln_b = lax.dynamic_slice_in_dim(total[n_ln:2 * n_ln], me * (D // N_DEV), D // N_DEV, axis=1)
    g_rel = total[2 * n_ln:2 * n_ln + H, :N_REL].T
    as3 = lambda t: t.reshape((1, -1, t.shape[-1]))
    r_ln_g = adamw(as3(ln_g), as3(m_ln_g), as3(v_ln_g), 0, g_ln_g[None], "adamw_ln_g")
    r_ln_b = adamw(as3(ln_b), as3(m_ln_b), as3(v_ln_b), 0, g_ln_b[None], "adamw_ln_b")
    r_rel = adamw(as3(rel_bias), as3(m_rel_bias), as3(v_rel_bias), 0, g_rel[None], "adamw_rel_bias")

    def stacked(name, like, q):
        res = results[name]
        return jnp.stack([res[k][q] for k in range(len(res))]).reshape(like.shape)

    order = [("w_qkv_a", w_qkv_a), ("w_o_a", w_o_a), ("rel_bias", rel_bias), ("w_qkv_b", w_qkv_b), ("w_o_b", w_o_b),
             ("ffn_w_gate", ffn_w_gate), ("ffn_w_up", ffn_w_up), ("ffn_w_down", ffn_w_down), ("ln_g", ln_g), ("ln_b", ln_b)]
    single = {"rel_bias": r_rel, "ln_g": r_ln_g, "ln_b": r_ln_b}
    outs = [loss, grad_x]
    for q in range(4):
        for name, like in order:
            outs.append(single[name][q].reshape(like.shape) if name in single else stacked(name, like, q))
    return tuple(outs)
```

```python
import functools

import jax
import jax.numpy as jnp
from jax import lax
from jax.experimental import pallas as pl
from jax.experimental.pallas import tpu as pltpu

BF16 = jnp.bfloat16
F32 = jnp.float32
MESH_ID = pl.DeviceIdType.MESH
ANY = pl.BlockSpec(memory_space=pl.ANY)

N_DEV = 8
LANE = 128
MXU_COLS = 256
SUBLANE = 8
VMEM_LIMIT = 56 * 1024 * 1024

HEAD_DIM = 64
CHUNK = 64
LEFT_CHUNKS = 8
REL_CLIP = 128
N_REL = 2 * REL_CLIP + 1
REL_PAD = 384
LN_EPS = 1e-5
ADAM_LR, ADAM_B1, ADAM_B2, ADAM_EPS, ADAM_WD, ADAM_STEP = 0.001, 0.9, 0.999, 1e-08, 0.01, 10
NEG = -1e30

A_TQ = 128
A_NWB = LEFT_CHUNKS * CHUNK // A_TQ + 1
A_W = A_NWB * A_TQ
B_T = 256

NT_DIMS = (((1,), (1,)), ((), ()))
TN_DIMS = (((0,), (0,)), ((), ()))


def _tile(n, pref, unit=LANE):
    if n <= pref:
        return n
    t = pref - pref % unit
    while t > unit and n % t:
        t -= unit
    assert n % t == 0, (n, pref)
    return t


def _params(*sem):
    return pltpu.CompilerParams(dimension_semantics=sem, vmem_limit_bytes=VMEM_LIMIT)


def _split3(v):
    h = v.astype(BF16)
    r = v - h.astype(F32)
    m = r.astype(BF16)
    lo = (r - m.astype(F32)).astype(BF16)
    return h, m, lo


def _dot3(v, w):
    h, m, lo = _split3(v)
    return (jnp.dot(h, w, preferred_element_type=F32) + jnp.dot(m, w, preferred_element_type=F32)
            + jnp.dot(lo, w, preferred_element_type=F32))


def mm_nn(a, w, name):
    T, K = a.shape
    N = w.shape[1]
    tm, tn = _tile(T, 1024), _tile(N, 768)

    def body(a_ref, w_ref, o_ref):
        o_ref[...] = jnp.dot(a_ref[...], w_ref[...], preferred_element_type=F32).astype(o_ref.dtype)

    return pl.pallas_call(
        body, grid=(T // tm, N // tn),
        in_specs=[pl.BlockSpec((tm, K), lambda i, j: (i, 0)), pl.BlockSpec((K, tn), lambda i, j: (0, j))],
        out_specs=pl.BlockSpec((tm, tn), lambda i, j: (i, j)),
        out_shape=jax.ShapeDtypeStruct((T, N), BF16),
        compiler_params=_params("parallel", "parallel"), name=name)(a, w)


def ffn_up(xb, wg, wu, name):
    T, K = xb.shape
    N = wg.shape[1]
    tm, tn = _tile(T, 512), _tile(N, 768)

    def body(x_ref, wg_ref, wu_ref, h_ref, u_ref, a_ref):
        x = x_ref[...]
        chunks = [slice(c, c + MXU_COLS) for c in range(0, tn, MXU_COLS)]
        hs = [jnp.dot(x, wg_ref[:, c], preferred_element_type=F32) for c in chunks]
        us = [jnp.dot(x, wu_ref[:, c], preferred_element_type=F32) for c in chunks]
        for c, h, u in zip(chunks, hs, us):
            h_ref[:, c] = h.astype(BF16)
            u_ref[:, c] = u.astype(BF16)
            a_ref[:, c] = (h * jax.nn.sigmoid(h) * u).astype(BF16)

    wspec = pl.BlockSpec((K, tn), lambda i, j: (0, j))
    ospec = pl.BlockSpec((tm, tn), lambda i, j: (i, j))
    return pl.pallas_call(
        body, grid=(T // tm, N // tn),
        in_specs=[pl.BlockSpec((tm, K), lambda i, j: (i, 0)), wspec, wspec],
        out_specs=[ospec, ospec, ospec],
        out_shape=[jax.ShapeDtypeStruct((T, N), BF16)] * 3,
        compiler_params=_params("parallel", "parallel"), name=name)(xb, wg, wu)


def mm_res_ln(a, w, x, g, b, alpha, scale, name):
    T, K = a.shape
    D = w.shape[1]
    tm = _tile(T, 256)

    def body(a_ref, w_ref, x_ref, g_ref, b_ref, z_ref, o_ref, ob_ref):
        y = jnp.dot(a_ref[...], w_ref[...], preferred_element_type=F32)
        z = alpha * x_ref[...] + scale * y
        mu = jnp.mean(z, axis=1, keepdims=True)
        zc = z - mu
        var = jnp.mean(zc * zc, axis=1, keepdims=True)
        o = zc * lax.rsqrt(var + LN_EPS) * g_ref[...] + b_ref[...]
        z_ref[...] = z
        o_ref[...] = o
        ob_ref[...] = o.astype(BF16)

    row = pl.BlockSpec((tm, D), lambda i: (i, 0))
    vec = pl.BlockSpec((1, D), lambda i: (0, 0))
    return pl.pallas_call(
        body, grid=(T // tm,),
        in_specs=[pl.BlockSpec((tm, K), lambda i: (i, 0)), pl.BlockSpec((K, D), lambda i: (0, 0)), row, vec, vec],
        out_specs=[row, row, row],
        out_shape=[jax.ShapeDtypeStruct((T, D), F32), jax.ShapeDtypeStruct((T, D), F32),
                   jax.ShapeDtypeStruct((T, D), BF16)],
        compiler_params=_params("parallel"), name=name)(a, w, x, g, b)


def ln_bwd(z, g, do, dep, name):
    T, D = z.shape
    tm = _tile(T, 512)

    def body(z_ref, g_ref, do_ref, dep_ref, dz_ref, dzb_ref, dg_ref, db_ref):
        @pl.when(pl.program_id(0) == 0)
        def _():
            dg_ref[...] = jnp.zeros_like(dg_ref)
            db_ref[...] = jnp.zeros_like(db_ref)

        zv = z_ref[...]
        dov = do_ref[...]
        mu = jnp.mean(zv, axis=1, keepdims=True)
        zc = zv - mu
        var = jnp.mean(zc * zc, axis=1, keepdims=True)
        rstd = lax.rsqrt(var + LN_EPS)
        xhat = zc * rstd
        dxhat = dov * g_ref[...]
        m1 = jnp.mean(dxhat, axis=1, keepdims=True)
        m2 = jnp.mean(dxhat * xhat, axis=1, keepdims=True)
        dz = rstd * (dxhat - m1 - xhat * m2)
        dz_ref[...] = dz
        dzb_ref[...] = dz.astype(BF16)
        dg_ref[...] += jnp.sum(dov * xhat, axis=0, keepdims=True)
        db_ref[...] += jnp.sum(dov, axis=0, keepdims=True)

    row = pl.BlockSpec((tm, D), lambda i: (i, 0))
    vec = pl.BlockSpec((1, D), lambda i: (0, 0))
    return pl.pallas_call(
        body, grid=(T // tm,), in_specs=[row, vec, row, ANY], out_specs=[row, row, vec, vec],
        out_shape=[jax.ShapeDtypeStruct((T, D), F32), jax.ShapeDtypeStruct((T, D), BF16),
                   jax.ShapeDtypeStruct((1, D), F32), jax.ShapeDtypeStruct((1, D), F32)],
        compiler_params=_params("arbitrary"), name=name)(z, g, do, dep)


def mm_nt(a, w, name):
    T, K = a.shape
    N = w.shape[0]
    tm, tn = _tile(T, 1024), _tile(N, 512)

    def body(a_ref, w_ref, o_ref):
        o_ref[...] = lax.dot_general(a_ref[...], w_ref[...], NT_DIMS, preferred_element_type=F32).astype(o_ref.dtype)

    return pl.pallas_call(
        body, grid=(T // tm, N // tn),
        in_specs=[pl.BlockSpec((tm, K), lambda i, j: (i, 0)), pl.BlockSpec((tn, K), lambda i, j: (j, 0))],
        out_specs=pl.BlockSpec((tm, tn), lambda i, j: (i, j)),
        out_shape=jax.ShapeDtypeStruct((T, N), BF16),
        compiler_params=_params("parallel", "parallel"), name=name)(a, w)


def ffn_bwd_mid(dzb, wd, h, u, scale, name):
    T, K = dzb.shape
    N = wd.shape[0]
    tm, tn = _tile(T, 512), _tile(N, 768)

    def body(dz_ref, w_ref, h_ref, u_ref, dh_ref, du_ref):
        dz = dz_ref[...]
        chunks = [slice(c, c + MXU_COLS) for c in range(0, tn, MXU_COLS)]
        das = [lax.dot_general(dz, w_ref[c, :], NT_DIMS, preferred_element_type=F32) for c in chunks]
        for c, da in zip(chunks, das):
            da = scale * da
            hv = h_ref[:, c].astype(F32)
            s = jax.nn.sigmoid(hv)
            silu = hv * s
            dh_ref[:, c] = (da * u_ref[:, c].astype(F32) * (s + silu * (1.0 - s))).astype(BF16)
            du_ref[:, c] = (da * silu).astype(BF16)

    tile = pl.BlockSpec((tm, tn), lambda i, j: (i, j))
    return pl.pallas_call(
        body, grid=(T // tm, N // tn),
        in_specs=[pl.BlockSpec((tm, K), lambda i, j: (i, 0)), pl.BlockSpec((tn, K), lambda i, j: (j, 0)), tile, tile],
        out_specs=[tile, tile],
        out_shape=[jax.ShapeDtypeStruct((T, N), BF16)] * 2,
        compiler_params=_params("parallel", "parallel"), name=name)(dzb, wd, h, u)


def mm_nt_res(pairs, dz, alpha, name):
    T, N = pairs[0][0].shape
    D = pairs[0][1].shape[0]
    n = len(pairs)
    tm, tk = _tile(T, 512 // n), _tile(N, 3072)
    nk = N // tk

    def body(*refs):
        a_refs, w_refs = refs[:n], refs[n:2 * n]
        dz_ref, o_ref, acc_ref = refs[2 * n:]
        k = pl.program_id(1)

        @pl.when(k == 0)
        def _():
            acc_ref[...] = jnp.zeros_like(acc_ref)

        part = lax.dot_general(a_refs[0][...], w_refs[0][...], NT_DIMS, preferred_element_type=F32)
        for p in range(1, n):
            part += lax.dot_general(a_refs[p][...], w_refs[p][...], NT_DIMS, preferred_element_type=F32)
        acc_ref[...] += part

        @pl.when(k == nk - 1)
        def _():
            o_ref[...] = acc_ref[...] + alpha * dz_ref[...]

    row = pl.BlockSpec((tm, D), lambda i, k: (i, 0))
    return pl.pallas_call(
        body, grid=(T // tm, nk),
        in_specs=[pl.BlockSpec((tm, tk), lambda i, k: (i, k))] * n + [pl.BlockSpec((D, tk), lambda i, k: (0, k))] * n + [row],
        out_specs=row,
        out_shape=jax.ShapeDtypeStruct((T, D), F32),
        scratch_shapes=[pltpu.VMEM((tm, D), F32)],
        compiler_params=_params("parallel", "arbitrary"), name=name)(*[p[0] for p in pairs], *[p[1] for p in pairs], dz)


def mm_tn(a, b, scale, shard_cols, name):
    T, M = a.shape
    N = b.shape[1]
    tm, tk = _tile(M, 512), _tile(T, 4096)
    nk = T // tk
    if shard_cols:
        ns = N // N_DEV
        per = 2 if (2 * ns) % 256 == 0 else 1
        tn = per * ns
        out_shape = jax.ShapeDtypeStruct((N_DEV, M, ns), BF16)
        out_spec = pl.BlockSpec((per, tm, ns), lambda i, j, k: (j, i, 0))
    else:
        tn = _tile(N, 1024)
        out_shape = jax.ShapeDtypeStruct((M, N), BF16)
        out_spec = pl.BlockSpec((tm, tn), lambda i, j, k: (i, j))

    def body(a_ref, b_ref, o_ref, acc_ref):
        k = pl.program_id(2)

        @pl.when(k == 0)
        def _():
            acc_ref[...] = jnp.zeros_like(acc_ref)

        acc_ref[...] += lax.dot_general(a_ref[...], b_ref[...], TN_DIMS, preferred_element_type=F32)

        @pl.when(k == nk - 1)
        def _():
            if shard_cols:
                for s in range(per):
                    o_ref[s] = (scale * acc_ref[:, s * ns:(s + 1) * ns]).astype(BF16)
            else:
                o_ref[...] = (scale * acc_ref[...]).astype(BF16)

    return pl.pallas_call(
        body, grid=(M // tm, N // tn, nk),
        in_specs=[pl.BlockSpec((tk, tm), lambda i, j, k: (k, i)), pl.BlockSpec((tk, tn), lambda i, j, k: (k, j))],
        out_specs=out_spec, out_shape=out_shape,
        scratch_shapes=[pltpu.VMEM((tm, tn), F32)],
        compiler_params=_params("parallel", "parallel", "arbitrary"), name=name)(a, b)


def loss_head(y, target, name):
    T, D = y.shape
    tm = _tile(T, 512)

    def body(y_ref, t_ref, l_ref, dy_ref):
        @pl.when(pl.program_id(0) == 0)
        def _():
            l_ref[...] = jnp.zeros_like(l_ref)

        e = y_ref[...] - t_ref[...]
        dy_ref[...] = e * (1.0 / D)
        rows = jnp.sum(e * e, axis=1, keepdims=True) * (0.5 / D)
        l_ref[...] += jnp.sum(rows, axis=0, keepdims=True)

    row = pl.BlockSpec((tm, D), lambda i: (i, 0))
    return pl.pallas_call(
        body, grid=(T // tm,), in_specs=[row, row],
        out_specs=[pl.BlockSpec((1, 1), lambda i: (0, 0)), row],
        out_shape=[jax.ShapeDtypeStruct((1, 1), F32), jax.ShapeDtypeStruct((T, D), F32)],
        compiler_params=_params("arbitrary"), name=name)(y, target)


def _rel_index(i, j):
    return jnp.clip(i - j + LEFT_CHUNKS * CHUNK, -REL_CLIP, REL_CLIP) + REL_CLIP


def bias_band(table_t, name):
    H = table_t.shape[0]
    rows = SUBLANE

    def body(t_ref, o_ref):
        i0 = pl.program_id(0) * rows
        parts = _split3(t_ref[...])
        r = lax.broadcasted_iota(jnp.int32, (REL_PAD, A_W), 0)
        j = lax.broadcasted_iota(jnp.int32, (REL_PAD, A_W), 1)
        for ii in range(rows):
            onehot = jnp.where(r == _rel_index(i0 + ii, j), 1.0, 0.0).astype(BF16)
            o_ref[ii] = sum(jnp.dot(p, onehot, preferred_element_type=F32) for p in parts)

    return pl.pallas_call(
        body, grid=(A_TQ // rows,),
        in_specs=[pl.BlockSpec((H, REL_PAD), lambda i: (0, 0))],
        out_specs=pl.BlockSpec((rows, H, A_W), lambda i: (i, 0, 0)),
        out_shape=jax.ShapeDtypeStruct((A_TQ, H, A_W), F32),
        compiler_params=_params("parallel"), name=name)(table_t)


def bias_band_bwd(dbands, name):
    H = dbands[0].shape[1]
    rows = SUBLANE
    n = len(dbands)

    def body(*refs):
        g_refs, o_ref = refs[:n], refs[n]

        @pl.when(pl.program_id(0) == 0)
        def _():
            o_ref[...] = jnp.zeros_like(o_ref)

        i0 = pl.program_id(0) * rows
        j = lax.broadcasted_iota(jnp.int32, (A_W, REL_PAD), 0)
        r = lax.broadcasted_iota(jnp.int32, (A_W, REL_PAD), 1)
        acc = jnp.zeros((H, REL_PAD), F32)
        for ii in range(rows):
            onehot = jnp.where(r == _rel_index(i0 + ii, j), 1.0, 0.0).astype(BF16)
            g = g_refs[0][ii]
            for q in range(1, n):
                g = g + g_refs[q][ii]
            acc += _dot3(g, onehot)
        o_ref[...] += acc

    spec = pl.BlockSpec((rows, H, A_W), lambda i: (i, 0, 0))
    return pl.pallas_call(
        body, grid=(A_TQ // rows,), in_specs=[spec] * n,
        out_specs=pl.BlockSpec((H, REL_PAD), lambda i: (0, 0)),
        out_shape=jax.ShapeDtypeStruct((H, REL_PAD), F32),
        compiler_params=_params("arbitrary"), name=name)(*dbands)


def _a_window(ref, qi):
    parts = []
    for d in range(A_NWB):
        kb = jnp.maximum(qi - (A_NWB - 1) + d, 0)
        parts.append(ref[pl.ds(pl.multiple_of(kb * A_TQ, A_TQ), A_TQ), :])
    return jnp.concatenate(parts, axis=0)


def _a_valid(qi):
    i = lax.broadcasted_iota(jnp.int32, (A_TQ, A_W), 0)
    j = lax.broadcasted_iota(jnp.int32, (A_TQ, A_W), 1)
    ic, jc = i // CHUNK, j // CHUNK
    return (jc >= ic) & (jc <= ic + LEFT_CHUNKS) & (j >= LEFT_CHUNKS * CHUNK - qi * A_TQ)


def _head_masks():
    lane = lax.broadcasted_iota(jnp.int32, (1, LANE), 1)
    return [lane < HEAD_DIM, lane >= HEAD_DIM]


def _a_probs(qms, kw, b_ref, valid):
    scores = [lax.dot_general(qm, kw, NT_DIMS, preferred_element_type=F32) for qm in qms]
    probs = []
    for hh, s in enumerate(scores):
        s = jnp.where(valid, s * (HEAD_DIM ** -0.5) + b_ref[hh], NEG)
        p = jnp.exp(s - jnp.max(s, axis=1, keepdims=True))
        probs.append(p / jnp.sum(p, axis=1, keepdims=True))
    return probs


def _attn_specs(S, D, tq):
    hp_n = D // LANE
    nq = S // tq
    q = pl.BlockSpec((tq, LANE), lambda hp, b, qi: (b * nq + qi, hp))
    k = pl.BlockSpec((S, LANE), lambda hp, b, qi: (b, hp_n + hp))
    v = pl.BlockSpec((S, LANE), lambda hp, b, qi: (b, 2 * hp_n + hp))
    tile = pl.BlockSpec((tq, LANE), lambda hp, b, qi: (b * nq + qi, hp))
    seq = pl.BlockSpec((S, LANE), lambda hp, b, qi: (b, hp))
    return q, k, v, tile, seq


def attn_a_fwd(qkv, band, S, name):
    T, D3 = qkv.shape
    D = D3 // 3
    nb, nq = T // S, S // A_TQ

    def body(q_ref, k_ref, v_ref, b_ref, o_ref):
        qi = pl.program_id(2)
        q2 = q_ref[...]
        kw, vw = _a_window(k_ref, qi), _a_window(v_ref, qi)
        valid = _a_valid(qi)
        masks = _head_masks()
        probs = _a_probs([jnp.where(hm, q2, jnp.zeros_like(q2)) for hm in masks], kw, b_ref, valid)
        outs = [jnp.dot(p.astype(BF16), vw, preferred_element_type=F32) for p in probs]
        o_ref[...] = jnp.where(masks[0], outs[0], outs[1]).astype(BF16)

    q, k, v, tile, _ = _attn_specs(S, D, A_TQ)
    return pl.pallas_call(
        body, grid=(D // LANE, nb, nq),
        in_specs=[q, k, v, pl.BlockSpec((2, A_TQ, A_W), lambda hp, b, qi: (hp, 0, 0))],
        out_specs=tile, out_shape=jax.ShapeDtypeStruct((T, D), BF16),
        compiler_params=_params("parallel", "parallel", "parallel"), name=name)(qkv, qkv, qkv, band)


def attn_a_bwd(qkv, band, do, S, name):
    T, D3 = qkv.shape
    D = D3 // 3
    nb, nq = T // S, S // A_TQ
    scale = HEAD_DIM ** -0.5

    def body(q_ref, k_ref, v_ref, b_ref, do_ref, dq_ref, dk_ref, dv_ref, db_ref, dk_acc, dv_acc):
        b, qi = pl.program_id(1), pl.program_id(2)

        @pl.when((b == 0) & (qi == 0))
        def _():
            db_ref[...] = jnp.zeros_like(db_ref)

        @pl.when(qi == 0)
        def _():
            dk_acc[...] = jnp.zeros_like(dk_acc)
            dv_acc[...] = jnp.zeros_like(dv_acc)

        q2, do2 = q_ref[...], do_ref[...]
        kw, vw = _a_window(k_ref, qi), _a_window(v_ref, qi)
        valid = _a_valid(qi)
        masks = _head_masks()
        qms = [jnp.where(hm, q2, jnp.zeros_like(q2)) for hm in masks]
        doms = [jnp.where(hm, do2, jnp.zeros_like(do2)) for hm in masks]
        dps = [lax.dot_general(dom, vw, NT_DIMS, preferred_element_type=F32) for dom in doms]
        probs = _a_probs(qms, kw, b_ref, valid)
        dsbs = []
        for hh, (p, dp) in enumerate(zip(probs, dps)):
            ds = p * (dp - jnp.sum(p * dp, axis=1, keepdims=True))
            db_ref[hh] += ds
            dsbs.append(ds.astype(BF16))
        dqs = [jnp.dot(dsb, kw, preferred_element_type=F32) for dsb in dsbs]
        dks = [lax.dot_general(dsb, qm, TN_DIMS, preferred_element_type=F32) for dsb, qm in zip(dsbs, qms)]
        dvs = [lax.dot_general(p.astype(BF16), dom, TN_DIMS, preferred_element_type=F32) for p, dom in zip(probs, doms)]
        dq_ref[...] = (jnp.where(masks[0], dqs[0], dqs[1]) * scale).astype(BF16)
        dkw = (dks[0] + dks[1]) * scale
        dvw = dvs[0] + dvs[1]
        for d in range(A_NWB):
            kb = jnp.maximum(qi - (A_NWB - 1) + d, 0)
            rows = pl.ds(pl.multiple_of(kb * A_TQ, A_TQ), A_TQ)
            dk_acc[rows, :] += dkw[d * A_TQ:(d + 1) * A_TQ]
            dv_acc[rows, :] += dvw[d * A_TQ:(d + 1) * A_TQ]

        @pl.when(qi == nq - 1)
        def _():
            dk_ref[...] = dk_acc[...].astype(BF16)
            dv_ref[...] = dv_acc[...].astype(BF16)

    q, k, v, tile, seq = _attn_specs(S, D, A_TQ)
    bspec = pl.BlockSpec((2, A_TQ, A_W), lambda hp, b, qi: (hp, 0, 0))
    act = jax.ShapeDtypeStruct((T, D), BF16)
    return pl.pallas_call(
        body, grid=(D // LANE, nb, nq),
        in_specs=[q, k, v, bspec, tile], out_specs=[tile, seq, seq, bspec],
        out_shape=[act, act, act, jax.ShapeDtypeStruct(band.shape, F32)],
        scratch_shapes=[pltpu.VMEM((S, LANE), F32), pltpu.VMEM((S, LANE), F32)],
        compiler_params=_params("arbitrary", "arbitrary", "arbitrary"), name=name)(qkv, qkv, qkv, band, do)


def _dot2(v, w):
    h = v.astype(BF16)
    lo = (v - h.astype(F32)).astype(BF16)
    return jnp.dot(h, w, preferred_element_type=F32) + jnp.dot(lo, w, preferred_element_type=F32)


def _b_weights(qms, kts, rights, after, diagonal):
    inst = [(t, hh) for t in range(len(kts)) for hh in range(2)]
    zs = [lax.dot_general(qms[hh], kts[t], NT_DIMS, preferred_element_type=F32) for t, hh in inst]
    lbs, l1ms, his, los = [], [], [], []
    causal = None
    if diagonal:
        row = lax.broadcasted_iota(jnp.int32, (B_T, B_T), 0)
        col = lax.broadcasted_iota(jnp.int32, (B_T, B_T), 1)
        causal = col < row
    for z in zs:
        nz = -z
        soft = jnp.log(1.0 + jnp.exp(jnp.minimum(z, nz)))
        lbs.append(jnp.minimum(z, 0.0) - soft)
        l1m = jnp.minimum(nz, 0.0) - soft
        if diagonal:
            l1m = jnp.where(causal, l1m, 0.0)
        hi = l1m.astype(BF16)
        l1ms.append(l1m)
        his.append(hi)
        los.append((l1m - hi.astype(F32)).astype(BF16))
    sums = [jnp.dot(hi, after, preferred_element_type=F32) + jnp.dot(lo, after, preferred_element_type=F32)
            for hi, lo in zip(his, los)]
    rights = list(rights)
    weights = []
    for (t, hh), lb, l1m, c in zip(inst, lbs, l1ms, sums):
        a = jnp.exp(lb + (rights[hh] + c))
        weights.append(jnp.where(causal, a, 0.0) if diagonal else a)
        rights[hh] = rights[hh] + jnp.sum(l1m, axis=1, keepdims=True)
    return zs, weights, rights


def _tri(strict_lower):
    r = lax.broadcasted_iota(jnp.int32, (B_T, B_T), 0)
    c = lax.broadcasted_iota(jnp.int32, (B_T, B_T), 1)
    return jnp.where((r > c) if strict_lower else (r < c), 1.0, 0.0).astype(BF16)


def _scaled_heads(q2):
    qs = q2 * (HEAD_DIM ** -0.5)
    return [jnp.where(hm, qs, jnp.zeros_like(qs)) for hm in _head_masks()]


def attn_b_fwd(qkv, S, name):
    T, D3 = qkv.shape
    D = D3 // 3
    nb, nq = T // S, S // B_T

    def body(q_ref, k_ref, v_ref, o_ref):
        qi = pl.program_id(2)
        after = _tri(True)
        qms = _scaled_heads(q_ref[...])

        def tiles(kbs, carry, diagonal):
            rows = [pl.ds(pl.multiple_of(kb * B_T, B_T), B_T) for kb in kbs]
            kts, vts = [k_ref[r, :] for r in rows], [v_ref[r, :] for r in rows]
            _, weights, rights = _b_weights(qms, kts, (carry[0], carry[2]), after, diagonal)
            accs = [carry[1], carry[3]]
            for i, a in enumerate(weights):
                accs[i % 2] = accs[i % 2] + jnp.dot(a.astype(BF16), vts[i // 2], preferred_element_type=F32)
            return rights[0], accs[0], rights[1], accs[1]

        init = (jnp.zeros((B_T, 1), F32), jnp.zeros((B_T, LANE), F32)) * 2
        res = tiles([qi], init, True)
        res = lax.fori_loop(0, qi // 2, lambda p, c: tiles([qi - 1 - 2 * p, qi - 2 - 2 * p], c, False), res)
        res = lax.cond(qi % 2 == 1, lambda c: tiles([0], c, False), lambda c: c, res)
        o_ref[...] = jnp.where(_head_masks()[0], res[1], res[3]).astype(BF16)

    q, k, v, tile, _ = _attn_specs(S, D, B_T)
    return pl.pallas_call(
        body, grid=(D // LANE, nb, nq), in_specs=[q, k, v], out_specs=tile,
        out_shape=jax.ShapeDtypeStruct((T, D), BF16),
        compiler_params=_params("parallel", "parallel", "parallel"), name=name)(qkv, qkv, qkv)


def attn_b_bwd(qkv, do, S, name):
    T, D3 = qkv.shape
    D = D3 // 3
    nb, nq = T // S, S // B_T
    scale = HEAD_DIM ** -0.5

    def body(q_ref, k_ref, v_ref, do_ref, dq_ref, dk_ref, dv_ref, dk_acc, dv_acc, z_s, g_s):
        qi = pl.program_id(2)

        @pl.when(qi == 0)
        def _():
            dk_acc[...] = jnp.zeros_like(dk_acc)
            dv_acc[...] = jnp.zeros_like(dv_acc)

        do2 = do_ref[...]
        after, before = _tri(True), _tri(False)
        masks = _head_masks()
        qms = _scaled_heads(q_ref[...])
        doms = [jnp.where(hm, do2, jnp.zeros_like(do2)) for hm in masks]

        def sweep_left(kbs, rights, diagonal):
            rows = [pl.ds(pl.multiple_of(kb * B_T, B_T), B_T) for kb in kbs]
            kts, vts = [k_ref[r, :] for r in rows], [v_ref[r, :] for r in rows]
            das = [lax.dot_general(doms[hh], vt, NT_DIMS, preferred_element_type=F32) for vt in vts for hh in range(2)]
            zs, weights, rights = _b_weights(qms, kts, rights, after, diagonal)
            for i, (z, a, da) in enumerate(zip(zs, weights, das)):
                z_s[i % 2, kbs[i // 2]] = z
                g_s[i % 2, kbs[i // 2]] = da * a
            dvs = [lax.dot_general(a.astype(BF16), doms[i % 2], TN_DIMS, preferred_element_type=F32)
                   for i, a in enumerate(weights)]
            for t, r in enumerate(rows):
                dv_acc[r, :] += dvs[2 * t] + dvs[2 * t + 1]
            return tuple(rights)

        zero_col = jnp.zeros((B_T, 1), F32)
        rights = sweep_left([qi], (zero_col, zero_col), True)
        rights = lax.fori_loop(0, qi // 2, lambda p, c: sweep_left([qi - 1 - 2 * p, qi - 2 - 2 * p], c, False), rights)
        lax.cond(qi % 2 == 1, lambda c: sweep_left([0], c, False), lambda c: c, rights)

        def sweep_right(kbs, carry, diagonal):
            rows = [pl.ds(pl.multiple_of(kb * B_T, B_T), B_T) for kb in kbs]
            inst = [(t, hh) for t in range(len(kbs)) for hh in range(2)]
            gs = [g_s[hh, kbs[t]] for t, hh in inst]
            sums = [_dot2(g, before) for g in gs]
            lefts, dqs = [carry[0], carry[2]], [carry[1], carry[3]]
            dzbs = []
            for (t, hh), g, c in zip(inst, gs, sums):
                beta = jax.nn.sigmoid(z_s[hh, kbs[t]])
                dz = g * (1.0 - beta) - beta * (lefts[hh] + c)
                if diagonal:
                    row = lax.broadcasted_iota(jnp.int32, (B_T, B_T), 0)
                    col = lax.broadcasted_iota(jnp.int32, (B_T, B_T), 1)
                    dz = jnp.where(col < row, dz, 0.0)
                dzbs.append(dz.astype(BF16))
                lefts[hh] = lefts[hh] + jnp.sum(g, axis=1, keepdims=True)
            dks = [lax.dot_general(dzb, qms[hh], TN_DIMS, preferred_element_type=F32) for (t, hh), dzb in zip(inst, dzbs)]
            for (t, hh), dzb in zip(inst, dzbs):
                dqs[hh] = dqs[hh] + jnp.dot(dzb, k_ref[rows[t], :], preferred_element_type=F32)
            for t, r in enumerate(rows):
                dk_acc[r, :] += dks[2 * t] + dks[2 * t + 1]
            return lefts[0], dqs[0], lefts[1], dqs[1]

        init = (zero_col, jnp.zeros((B_T, LANE), F32)) * 2
        res = lax.fori_loop(0, qi // 2, lambda p, c: sweep_right([2 * p, 2 * p + 1], c, False), init)
        res = lax.cond(qi % 2 == 1, lambda c: sweep_right([qi - 1], c, False), lambda c: c, res)
        res = sweep_right([qi], res, True)
        dq_ref[...] = (jnp.where(masks[0], res[1], res[3]) * scale).astype(BF16)

        @pl.when(qi == nq - 1)
        def _():
            dk_ref[...] = dk_acc[...].astype(BF16)
            dv_ref[...] = dv_acc[...].astype(BF16)

    q, k, v, tile, seq = _attn_specs(S, D, B_T)
    act = jax.ShapeDtypeStruct((T, D), BF16)
    return pl.pallas_call(
        body, grid=(D // LANE, nb, nq),
        in_specs=[q, k, v, tile], out_specs=[tile, seq, seq], out_shape=[act, act, act],
        scratch_shapes=[pltpu.VMEM((S, LANE), F32), pltpu.VMEM((S, LANE), F32),
                        pltpu.VMEM((2, nq, B_T, B_T), F32), pltpu.VMEM((2, nq, B_T, B_T), F32)],
        compiler_params=_params("arbitrary", "arbitrary", "arbitrary"), name=name)(qkv, qkv, qkv, do)


HBM = pl.BlockSpec(memory_space=pltpu.HBM)
SEM = pl.BlockSpec(memory_space=pltpu.SEMAPHORE)
N_PEERS = N_DEV - 1
GATHERS_AHEAD = 3


def _place():
    return lax.axis_index("x"), lax.axis_index("y"), lax.axis_index("c")


def _peers(x, y, c):
    return [(1 - x if r & 4 else x, 1 - y if r & 2 else y, 1 - c if r & 1 else c) for r in range(1, N_DEV)]


def _block(ref, shape, by_cols, j):
    r, w = shape
    if by_cols:
        return ref.at[:, pl.ds(pl.multiple_of(j * w, LANE), w)]
    return ref.at[pl.ds(pl.multiple_of(j * r, SUBLANE), r), :]


def _exchange_copies(gather, src_refs, land_refs, send_sems, recv_sems, by_cols):
    x, y, c = _place()
    me = 4 * x + 2 * y + c
    out = []
    for t, (src, land) in enumerate(zip(src_refs, land_refs)):
        for r, peer in enumerate(_peers(x, y, c)):
            pj = 4 * peer[0] + 2 * peer[1] + peer[2]
            if gather:
                mine, to_me, theirs = src, _block(land, src.shape, by_cols[t], me), _block(land, src.shape, by_cols[t], pj)
            else:
                mine, to_me, theirs = src.at[pj], land.at[me], land.at[pj]
            sems = dict(send_sem=send_sems.at[N_PEERS * t + r], recv_sem=recv_sems.at[N_PEERS * t + r],
                        device_id=peer, device_id_type=MESH_ID)
            out.append((pltpu.make_async_remote_copy(src_ref=mine, dst_ref=to_me, **sems),
                        pltpu.make_async_remote_copy(src_ref=mine, dst_ref=theirs, **sems)))
    return out


def _own_copies(gather, src_refs, land_refs, own_sems, by_cols):
    x, y, c = _place()
    me = 4 * x + 2 * y + c
    out = []
    for t, (src, land) in enumerate(zip(src_refs, land_refs)):
        if gather:
            out.append(pltpu.make_async_copy(src, _block(land, src.shape, by_cols[t], me), own_sems.at[t]))
        else:
            out.append(pltpu.make_async_copy(src.at[me], land.at[me], own_sems.at[t]))
    return out


def exchange_start(gather, srcs, land_shapes, by_cols, after, name):
    n = len(srcs)

    def body(*refs):
        src_refs, land_refs = refs[:n], refs[n:2 * n]
        send_sems, recv_sems, own_sems = refs[2 * n + 1:2 * n + 4]
        token = refs[-1]
        for cp in _own_copies(gather, src_refs, land_refs, own_sems, by_cols):
            cp.start()
        for mine, _ in _exchange_copies(gather, src_refs, land_refs, send_sems, recv_sems, by_cols):
            mine.start()
        token[...] = jnp.zeros_like(token)

    lands = [pltpu.with_memory_space_constraint(lax.empty(s.shape, s.dtype), pltpu.HBM) for s in land_shapes]
    srcs = [pltpu.with_memory_space_constraint(s, pltpu.HBM) for s in srcs]
    res = pl.pallas_call(
        body, name=name,
        out_shape=(pltpu.SemaphoreType.DMA((N_PEERS * n,)), pltpu.SemaphoreType.DMA((N_PEERS * n,)),
                   pltpu.SemaphoreType.DMA((n,)),
                   *[pltpu.HBM(s.shape, s.dtype) for s in srcs], *[pltpu.HBM(s.shape, s.dtype) for s in land_shapes],
                   jax.ShapeDtypeStruct((SUBLANE, LANE), F32)),
        in_specs=[HBM] * (2 * n) + [ANY],
        out_specs=(SEM, SEM, SEM, *[HBM] * (2 * n), pl.BlockSpec(memory_space=pltpu.VMEM)),
        input_output_aliases={i: 3 + i for i in range(2 * n)},
        compiler_params=pltpu.CompilerParams(has_side_effects=pltpu.SideEffectType.DATAFLOW_SIDE_EFFECTING),
    )(*srcs, *lands, after)
    return dict(gather=gather, n=n, by_cols=by_cols, sems=res[:3], srcs=res[3:3 + n],
                lands=res[3 + n:3 + 2 * n], token=res[-1])


def exchange_wait(started, after, name):
    n, gather, by_cols = started["n"], started["gather"], started["by_cols"]

    def body(*refs):
        src_refs, land_refs = refs[:n], refs[n:2 * n]
        send_sems, recv_sems, own_sems = refs[2 * n:2 * n + 3]
        for mine, theirs in _exchange_copies(gather, src_refs, land_refs, send_sems, recv_sems, by_cols):
            mine.wait_send()
            theirs.wait_recv()
        for cp in _own_copies(gather, src_refs, land_refs, own_sems, by_cols):
            cp.wait()

    res = pl.pallas_call(
        body, name=name,
        out_shape=tuple(pltpu.HBM(s.shape, s.dtype) for s in (*started["srcs"], *started["lands"])),
        in_specs=[HBM] * (2 * n) + [SEM, SEM, SEM, ANY], out_specs=tuple([HBM] * (2 * n)),
        input_output_aliases={i: i for i in range(2 * n)},
        compiler_params=pltpu.CompilerParams(has_side_effects=pltpu.SideEffectType.DATAFLOW_SIDE_EFFECTING),
    )(*started["srcs"], *started["lands"], *started["sems"], after)
    return res[n:]


def gather_small(v, dep, name):
    R, C = v.shape

    def body(v_ref, dep_ref, o_ref, send_sems, recv_sems):
        x, y, c = _place()
        o_ref[4 * x + 2 * y + c] = v_ref[...]
        peers = _peers(x, y, c)

        def copy(r, owner, to):
            slot = o_ref.at[4 * owner[0] + 2 * owner[1] + owner[2]]
            return pltpu.make_async_remote_copy(
                src_ref=slot, dst_ref=slot, send_sem=send_sems.at[r], recv_sem=recv_sems.at[r],
                device_id=to, device_id_type=MESH_ID)

        sends = [copy(r, (x, y, c), peer) for r, peer in enumerate(peers)]
        for cp in sends:
            cp.start()
        for r, peer in enumerate(peers):
            copy(r, peer, (x, y, c)).wait_recv()
        for cp in sends:
            cp.wait_send()

    vm = pl.BlockSpec(memory_space=pltpu.VMEM)
    return pl.pallas_call(
        body, in_specs=[vm, ANY], out_specs=vm, out_shape=jax.ShapeDtypeStruct((N_DEV, R, C), F32),
        scratch_shapes=[pltpu.SemaphoreType.DMA((N_PEERS,)), pltpu.SemaphoreType.DMA((N_PEERS,))],
        name=name)(v, dep)


def sum_devices(g, name):
    _, R, C = g.shape

    def body(g_ref, o_ref):
        acc = g_ref[0]
        for j in range(1, N_DEV):
            acc = acc + g_ref[j]
        o_ref[...] = acc

    vm = pl.BlockSpec(memory_space=pltpu.VMEM)
    return pl.pallas_call(body, in_specs=[vm], out_specs=vm, out_shape=jax.ShapeDtypeStruct((R, C), F32), name=name)(g)


def adamw(w, m, v, index, parts, name):
    _, R, C = w.shape
    P, Rp, Cp = parts.shape
    tr = _tile(R, 512, SUBLANE) if Rp == R else R

    def body(w_ref, m_ref, v_ref, p_ref, g_out, d_out, m_out, v_out):
        g = p_ref[0, :tr, :C].astype(F32)
        for q in range(1, P):
            g = g + p_ref[q, :tr, :C].astype(F32)
        mn = ADAM_B1 * m_ref[...] + (1.0 - ADAM_B1) * g
        vn = ADAM_B2 * v_ref[...] + (1.0 - ADAM_B2) * (g * g)
        m_hat = mn / (1.0 - ADAM_B1 ** ADAM_STEP)
        v_hat = vn / (1.0 - ADAM_B2 ** ADAM_STEP)
        g_out[...] = g
        d_out[...] = -ADAM_LR * (m_hat / (jnp.sqrt(v_hat) + ADAM_EPS) + ADAM_WD * w_ref[...])
        m_out[...] = mn
        v_out[...] = vn

    slab = pl.BlockSpec((None, tr, C), lambda r: (index, r, 0))
    flat = pl.BlockSpec((tr, C), lambda r: (r, 0))
    pspec = pl.BlockSpec((P, tr if Rp == R else Rp, Cp), lambda r: (0, r, 0))
    return pl.pallas_call(
        body, grid=(R // tr,), in_specs=[slab] * 3 + [pspec], out_specs=[flat] * 4,
        out_shape=[jax.ShapeDtypeStruct((R, C), F32)] * 4,
        compiler_params=_params("parallel"), name=name)(w, m, v, parts)


def _pad_to(a, axis, size):
    pad = [(0, 0)] * a.ndim
    pad[axis] = (0, size - a.shape[axis])
    return jnp.pad(a, pad)


def kernel(x, w_qkv_a, w_o_a, rel_bias, w_qkv_b, w_o_b, ffn_w_gate, ffn_w_up, ffn_w_down, ln_g, ln_b, loss_target, m_w_qkv_a, m_w_o_a, m_rel_bias, m_w_qkv_b, m_w_o_b, m_ffn_w_gate, m_ffn_w_up, m_ffn_w_down, m_ln_g, m_ln_b, v_w_qkv_a, v_w_o_a, v_rel_bias, v_w_qkv_b, v_w_o_b, v_ffn_w_gate, v_ffn_w_up, v_ffn_w_down, v_ln_g, v_ln_b):
    nb, S, D = x.shape
    T = nb * S
    depth = ffn_w_gate.shape[0]
    H = D // HEAD_DIM
    fs = ffn_w_gate.shape[-1]
    fp = -(-fs // LANE) * LANE
    alpha = (2.0 * depth) ** 0.25
    cx, cy, cc = _place()
    me = 4 * cx + 2 * cy + cc

    ln_local = jnp.concatenate([ln_g.reshape(depth * 3, -1), ln_b.reshape(depth * 3, -1)], axis=0)
    ln_all = gather_small(ln_local, ln_local, "gather_ln")
    ln_full = jnp.transpose(ln_all, (1, 0, 2)).reshape(2 * depth * 3, D)
    ln_gain = lambda i, s: ln_full[3 * i + s][None, :]
    ln_bias = lambda i, s: ln_full[3 * depth + 3 * i + s][None, :]

    table_t = _pad_to(rel_bias.T, 1, REL_PAD)
    band = jnp.transpose(bias_band(table_t, "bias_band"), (1, 0, 2))

    subs = []
    for i in range(depth):
        for s in (0, 1, 2):
            if s == 1:
                wq, wo = (w_qkv_a, w_o_a) if i % 2 == 0 else (w_qkv_b, w_o_b)
                subs.append(([wq[i // 2].astype(BF16), wo[i // 2].astype(BF16)], [True, False]))
            else:
                f = 0 if s == 0 else 1
                subs.append(([_pad_to(ffn_w_gate[i, f].astype(BF16), 1, fp), _pad_to(ffn_w_up[i, f].astype(BF16), 1, fp),
                              _pad_to(ffn_w_down[i, f].astype(BF16), 0, fp)], [True, True, False]))

    def start_gather(k, after):
        shards, by_cols = subs[k]
        shapes = [jax.ShapeDtypeStruct((s.shape[0], N_DEV * s.shape[1]) if col else (N_DEV * s.shape[0], s.shape[1]), BF16)
                  for s, col in zip(shards, by_cols)]
        return exchange_start(True, shards, shapes, by_cols, after, f"gather_start_{k}")

    xf = x.reshape(T, D)
    act, act_b = xf, xf.astype(BF16)
    gathers = {}
    for k in range(min(GATHERS_AHEAD, len(subs))):
        gathers[k] = start_gather(k, xf if k == 0 else gathers[k - 1]["token"])
    newest = gathers[k]["token"]
    saved = []
    for i in range(depth):
        layer = {}
        for s in (0, 1, 2):
            k = 3 * i + s
            tag = f"L{i}S{s}"
            if 0 < k and k + GATHERS_AHEAD - 1 < len(subs):
                gathers[k + GATHERS_AHEAD - 1] = start_gather(k + GATHERS_AHEAD - 1, act_b)
                newest = gathers[k + GATHERS_AHEAD - 1]["token"]
            full = exchange_wait(gathers.pop(k), newest, f"gather_wait_{k}")
            if s == 1:
                wqkv_f, wo_f = full
                qkv = mm_nn(act_b, wqkv_f, "qkv_" + tag)
                if i % 2 == 0:
                    att = attn_a_fwd(qkv, band, S, "attn_a_fwd_" + tag)
                else:
                    att = attn_b_fwd(qkv, S, "attn_b_fwd_" + tag)
                z, o, ob = mm_res_ln(att, wo_f, act, ln_gain(i, s), ln_bias(i, s), alpha, 1.0, "out_ln_" + tag)
                layer[s] = dict(x_b=act_b, qkv=qkv, att=att, z=z, wqkv=wqkv_f, wo=wo_f)
            else:
                wg_f, wu_f, wd_f = full
                h, u, a = ffn_up(act_b, wg_f, wu_f, "ffn_up_" + tag)
                z, o, ob = mm_res_ln(a, wd_f, act, ln_gain(i, s), ln_bias(i, s), alpha, 0.5, "down_ln_" + tag)
                layer[s] = dict(x_b=act_b, h=h, u=u, a=a, z=z, wg=wg_f, wu=wu_f, wd=wd_f)
            act, act_b = o, ob
        saved.append(layer)

    loss_local, d_act = loss_head(act, loss_target.reshape(T, D), "loss_head")
    loss = lax.psum(loss_local[0, 0], ("x", "y", "c"))

    results = {}

    def update(name, w, m, v, index, parts):
        L = w.shape[0] if w.ndim == 3 else w.shape[0] * w.shape[1]
        flat = lambda t: t.reshape((L,) + t.shape[-2:])
        results.setdefault(name, {})[index] = adamw(flat(w), flat(m), flat(v), index, parts, f"adamw_{name}_{index}")

    def finish(entry, after):
        exchange, targets, tag = entry
        lands = exchange_wait(exchange, after, "scatter_wait_" + tag)
        for (name, w, m, v, index), parts in zip(targets, lands):
            update(name, w, m, v, index, parts)
        return results[targets[-1][0]][targets[-1][4]][0]

    pending = []
    started = d_act
    dbands = []
    dln_g = [None] * (3 * depth)
    dln_b = [None] * (3 * depth)
    for i in reversed(range(depth)):
        for s in (2, 1, 0):
            tag = f"L{i}S{s}"
            sv = saved[i][s]
            dz, dzb, dg, db = ln_bwd(sv["z"], ln_gain(i, s), d_act, started, "ln_bwd_" + tag)
            dln_g[3 * i + s], dln_b[3 * i + s] = dg, db
            if s == 1:
                j = i // 2
                d_att = mm_nt(dzb, sv["wo"], "att_bwd_" + tag)
                g_wo = mm_tn(sv["att"], dzb, 1.0, False, "dwo_" + tag)
                if i % 2 == 0:
                    dq, dk, dv, dband = attn_a_bwd(sv["qkv"], band, d_att, S, "attn_a_bwd_" + tag)
                    dbands.append(jnp.transpose(dband, (1, 0, 2)))
                else:
                    dq, dk, dv = attn_b_bwd(sv["qkv"], d_att, S, "attn_b_bwd_" + tag)
                dqkv = jnp.concatenate([dq, dk, dv], axis=1)
                g_wqkv = mm_tn(sv["x_b"], dqkv, 1.0, True, "dwqkv_" + tag)
                d_act = mm_nt_res([(dqkv, sv["wqkv"])], dz, alpha, "dx_mix_" + tag)
                grads = [g_wqkv, g_wo.reshape(N_DEV, D // N_DEV, D)]
                if i % 2 == 0:
                    targets = [("w_qkv_a", w_qkv_a, m_w_qkv_a, v_w_qkv_a, j), ("w_o_a", w_o_a, m_w_o_a, v_w_o_a, j)]
                else:
                    targets = [("w_qkv_b", w_qkv_b, m_w_qkv_b, v_w_qkv_b, j), ("w_o_b", w_o_b, m_w_o_b, v_w_o_b, j)]
            else:
                f = 0 if s == 0 else 1
                dh, du = ffn_bwd_mid(dzb, sv["wd"], sv["h"], sv["u"], 0.5, "ffn_mid_" + tag)
                g_wd = mm_tn(sv["a"], dzb, 0.5, False, "dwd_" + tag)
                g_wg = mm_tn(sv["x_b"], dh, 1.0, True, "dwg_" + tag)
                g_wu = mm_tn(sv["x_b"], du, 1.0, True, "dwu_" + tag)
                d_act = mm_nt_res([(dh, sv["wg"]), (du, sv["wu"])], dz, alpha, "dx_ffn_" + tag)
                grads = [g_wg, g_wu, g_wd.reshape(N_DEV, fp, D)]
                idx = 2 * i + f
                targets = [("ffn_w_gate", ffn_w_gate, m_ffn_w_gate, v_ffn_w_gate, idx),
                           ("ffn_w_up", ffn_w_up, m_ffn_w_up, v_ffn_w_up, idx),
                           ("ffn_w_down", ffn_w_down, m_ffn_w_down, v_ffn_w_down, idx)]
            after = d_act
            if i == 0 and s == 0:
                dtable_t = bias_band_bwd(dbands, "bias_band_bwd")
                small = jnp.concatenate(dln_g + dln_b + [_pad_to(dtable_t, 1, D)], axis=0)
                small = _pad_to(small, 0, -(-small.shape[0] // SUBLANE) * SUBLANE)
                total = after = sum_devices(gather_small(small, d_act, "gather_small_grads"), "sum_small_grads")
            shapes = [jax.ShapeDtypeStruct(g.shape, g.dtype) for g in grads]
            pending.append((exchange_start(False, grads, shapes, None, after, "scatter_start_" + tag), targets, tag))
            started = pending[-1][0]["token"]
    grad_x = d_act.reshape(nb, S, D)

    last = pending.pop()
    done = started
    for entry in pending:
        done = finish(entry, done)
    finish(last, done)
    n_ln = 3 * depth
    g_ln_g = lax.dynamic_slice_in_dim(total[:n_ln], me * (D // N_DEV), D // N_DEV, axis=1)
    g_ln_b = lax.dynamic_slice_in_dim(total[n_ln:2 * n_ln], me * (D // N_DEV), D // N_DEV, axis=1)
    g_rel = total[2 * n_ln:2 * n_ln + H, :N_REL].T
    as3 = lambda t: t.reshape((1, -1, t.shape[-1]))
    r_ln_g = adamw(as3(ln_g), as3(m_ln_g), as3(v_ln_g), 0, g_ln_g[None], "adamw_ln_g")
    r_ln_b = adamw(as3(ln_b), as3(m_ln_b), as3(v_ln_b), 0, g_ln_b[None], "adamw_ln_b")
    r_rel = adamw(as3(rel_bias), as3(m_rel_bias), as3(v_rel_bias), 0, g_rel[None], "adamw_rel_bias")

    def stacked(name, like, q):
        res = results[name]
        return jnp.stack([res[k][q] for k in range(len(res))]).reshape(like.shape)

    order = [("w_qkv_a", w_qkv_a), ("w_o_a", w_o_a), ("rel_bias", rel_bias), ("w_qkv_b", w_qkv_b), ("w_o_b", w_o_b),
             ("ffn_w_gate", ffn_w_gate), ("ffn_w_up", ffn_w_up), ("ffn_w_down", ffn_w_down), ("ln_g", ln_g), ("ln_b", ln_b)]
    single = {"rel_bias": r_rel, "ln_g": r_ln_g, "ln_b": r_ln_b}
    outs = [loss, grad_x]
    for q in range(4):
        for name, like in order:
            outs.append(single[name][q].reshape(like.shape) if name in single else stacked(name, like, q))
    return tuple(outs)
```

```python
import functools

import jax
import jax.numpy as jnp
from jax import lax
from jax.experimental import pallas as pl
from jax.experimental.pallas import tpu as pltpu

BF16 = jnp.bfloat16
F32 = jnp.float32
MESH_ID = pl.DeviceIdType.MESH
ANY = pl.BlockSpec(memory_space=pl.ANY)

N_DEV = 8
LANE = 128
MXU_COLS = 256
SUBLANE = 8
VMEM_LIMIT = 56 * 1024 * 1024

HEAD_DIM = 64
CHUNK = 64
LEFT_CHUNKS = 8
REL_CLIP = 128
N_REL = 2 * REL_CLIP + 1
REL_PAD = 384
LN_EPS = 1e-5
ADAM_LR, ADAM_B1, ADAM_B2, ADAM_EPS, ADAM_WD, ADAM_STEP = 0.001, 0.9, 0.999, 1e-08, 0.01, 10
NEG = -1e30

A_TQ = 128
A_NWB = LEFT_CHUNKS * CHUNK // A_TQ + 1
A_W = A_NWB * A_TQ
B_T = 256

NT_DIMS = (((1,), (1,)), ((), ()))
TN_DIMS = (((0,), (0,)), ((), ()))


def _tile(n, pref, unit=LANE):
    if n <= pref:
        return n
    t = pref - pref % unit
    while t > unit and n % t:
        t -= unit
    assert n % t == 0, (n, pref)
    return t


def _params(*sem):
    return pltpu.CompilerParams(dimension_semantics=sem, vmem_limit_bytes=VMEM_LIMIT)


def _split3(v):
    h = v.astype(BF16)
    r = v - h.astype(F32)
    m = r.astype(BF16)
    lo = (r - m.astype(F32)).astype(BF16)
    return h, m, lo


def _dot3(v, w):
    h, m, lo = _split3(v)
    return (jnp.dot(h, w, preferred_element_type=F32) + jnp.dot(m, w, preferred_element_type=F32)
            + jnp.dot(lo, w, preferred_element_type=F32))


def mm_nn(a, w, name):
    T, K = a.shape
    N = w.shape[1]
    tm, tn = _tile(T, 1024), _tile(N, 768)

    def body(a_ref, w_ref, o_ref):
        o_ref[...] = jnp.dot(a_ref[...], w_ref[...], preferred_element_type=F32).astype(o_ref.dtype)

    return pl.pallas_call(
        body, grid=(T // tm, N // tn),
        in_specs=[pl.BlockSpec((tm, K), lambda i, j: (i, 0)), pl.BlockSpec((K, tn), lambda i, j: (0, j))],
        out_specs=pl.BlockSpec((tm, tn), lambda i, j: (i, j)),
        out_shape=jax.ShapeDtypeStruct((T, N), BF16),
        compiler_params=_params("parallel", "parallel"), name=name)(a, w)


def ffn_up(xb, wg, wu, name):
    T, K = xb.shape
    N = wg.shape[1]
    tm, tn = _tile(T, 512), _tile(N, 768)

    def body(x_ref, wg_ref, wu_ref, h_ref, u_ref, a_ref):
        x = x_ref[...]
        chunks = [slice(c, c + MXU_COLS) for c in range(0, tn, MXU_COLS)]
        hs = [jnp.dot(x, wg_ref[:, c], preferred_element_type=F32) for c in chunks]
        us = [jnp.dot(x, wu_ref[:, c], preferred_element_type=F32) for c in chunks]
        for c, h, u in zip(chunks, hs, us):
            h_ref[:, c] = h.astype(BF16)
            u_ref[:, c] = u.astype(BF16)
            a_ref[:, c] = (h * jax.nn.sigmoid(h) * u).astype(BF16)

    wspec = pl.BlockSpec((K, tn), lambda j, i: (0, j))
    ospec = pl.BlockSpec((tm, tn), lambda j, i: (i, j))
    return pl.pallas_call(
        body, grid=(N // tn, T // tm),
        in_specs=[pl.BlockSpec((tm, K), lambda j, i: (i, 0)), wspec, wspec],
        out_specs=[ospec, ospec, ospec],
        out_shape=[jax.ShapeDtypeStruct((T, N), BF16)] * 3,
        compiler_params=_params("parallel", "parallel"), name=name)(xb, wg, wu)


def mm_res_ln(a, w, x, g, b, alpha, scale, name):
    T, K = a.shape
    D = w.shape[1]
    tm = _tile(T, 256)

    def body(a_ref, w_ref, x_ref, g_ref, b_ref, z_ref, o_ref, ob_ref):
        y = jnp.dot(a_ref[...], w_ref[...], preferred_element_type=F32)
        z = alpha * x_ref[...] + scale * y
        mu = jnp.mean(z, axis=1, keepdims=True)
        zc = z - mu
        var = jnp.mean(zc * zc, axis=1, keepdims=True)
        o = zc * lax.rsqrt(var + LN_EPS) * g_ref[...] + b_ref[...]
        z_ref[...] = z
        o_ref[...] = o
        ob_ref[...] = o.astype(BF16)

    row = pl.BlockSpec((tm, D), lambda i: (i, 0))
    vec = pl.BlockSpec((1, D), lambda i: (0, 0))
    return pl.pallas_call(
        body, grid=(T // tm,),
        in_specs=[pl.BlockSpec((tm, K), lambda i: (i, 0)), pl.BlockSpec((K, D), lambda i: (0, 0)), row, vec, vec],
        out_specs=[row, row, row],
        out_shape=[jax.ShapeDtypeStruct((T, D), F32), jax.ShapeDtypeStruct((T, D), F32),
                   jax.ShapeDtypeStruct((T, D), BF16)],
        compiler_params=_params("parallel"), name=name)(a, w, x, g, b)


def ln_bwd(z, g, do, dep, name):
    T, D = z.shape
    tm = _tile(T, 512)

    def body(z_ref, g_ref, do_ref, dep_ref, dz_ref, dzb_ref, dg_ref, db_ref):
        @pl.when(pl.program_id(0) == 0)
        def _():
            dg_ref[...] = jnp.zeros_like(dg_ref)
            db_ref[...] = jnp.zeros_like(db_ref)

        zv = z_ref[...]
        dov = do_ref[...]
        mu = jnp.mean(zv, axis=1, keepdims=True)
        zc = zv - mu
        var = jnp.mean(zc * zc, axis=1, keepdims=True)
        rstd = lax.rsqrt(var + LN_EPS)
        xhat = zc * rstd
        dxhat = dov * g_ref[...]
        m1 = jnp.mean(dxhat, axis=1, keepdims=True)
        m2 = jnp.mean(dxhat * xhat, axis=1, keepdims=True)
        dz = rstd * (dxhat - m1 - xhat * m2)
        dz_ref[...] = dz
        dzb_ref[...] = dz.astype(BF16)
        dg_ref[...] += jnp.sum(dov * xhat, axis=0, keepdims=True)
        db_ref[...] += jnp.sum(dov, axis=0, keepdims=True)

    row = pl.BlockSpec((tm, D), lambda i: (i, 0))
    vec = pl.BlockSpec((1, D), lambda i: (0, 0))
    return pl.pallas_call(
        body, grid=(T // tm,), in_specs=[row, vec, row, ANY], out_specs=[row, row, vec, vec],
        out_shape=[jax.ShapeDtypeStruct((T, D), F32), jax.ShapeDtypeStruct((T, D), BF16),
                   jax.ShapeDtypeStruct((1, D), F32), jax.ShapeDtypeStruct((1, D), F32)],
        compiler_params=_params("arbitrary"), name=name)(z, g, do, dep)


def mm_nt(a, w, name):
    T, K = a.shape
    N = w.shape[0]
    tm, tn = _tile(T, 1024), _tile(N, 512)

    def body(a_ref, w_ref, o_ref):
        o_ref[...] = lax.dot_general(a_ref[...], w_ref[...], NT_DIMS, preferred_element_type=F32).astype(o_ref.dtype)

    return pl.pallas_call(
        body, grid=(T // tm, N // tn),
        in_specs=[pl.BlockSpec((tm, K), lambda i, j: (i, 0)), pl.BlockSpec((tn, K), lambda i, j: (j, 0))],
        out_specs=pl.BlockSpec((tm, tn), lambda i, j: (i, j)),
        out_shape=jax.ShapeDtypeStruct((T, N), BF16),
        compiler_params=_params("parallel", "parallel"), name=name)(a, w)


def ffn_bwd_mid(dzb, wd, h, u, scale, name):
    T, K = dzb.shape
    N = wd.shape[0]
    tm, tn = _tile(T, 512), _tile(N, 768)

    def body(dz_ref, w_ref, h_ref, u_ref, dh_ref, du_ref):
        dz = dz_ref[...]
        chunks = [slice(c, c + MXU_COLS) for c in range(0, tn, MXU_COLS)]
        das = [lax.dot_general(dz, w_ref[c, :], NT_DIMS, preferred_element_type=F32) for c in chunks]
        for c, da in zip(chunks, das):
            da = scale * da
            hv = h_ref[:, c].astype(F32)
            s = jax.nn.sigmoid(hv)
            silu = hv * s
            dh_ref[:, c] = (da * u_ref[:, c].astype(F32) * (s + silu * (1.0 - s))).astype(BF16)
            du_ref[:, c] = (da * silu).astype(BF16)

    tile = pl.BlockSpec((tm, tn), lambda j, i: (i, j))
    return pl.pallas_call(
        body, grid=(N // tn, T // tm),
        in_specs=[pl.BlockSpec((tm, K), lambda j, i: (i, 0)), pl.BlockSpec((tn, K), lambda j, i: (j, 0)), tile, tile],
        out_specs=[tile, tile],
        out_shape=[jax.ShapeDtypeStruct((T, N), BF16)] * 2,
        compiler_params=_params("parallel", "parallel"), name=name)(dzb, wd, h, u)


def mm_nt_res(pairs, dz, alpha, name):
    T, N = pairs[0][0].shape
    D = pairs[0][1].shape[0]
    n = len(pairs)
    tm, tk = _tile(T, 512 // n), _tile(N, 3072)
    nk = N // tk

    def body(*refs):
        a_refs, w_refs = refs[:n], refs[n:2 * n]
        dz_ref, o_ref, acc_ref = refs[2 * n:]
        k = pl.program_id(1)

        @pl.when(k == 0)
        def _():
            acc_ref[...] = jnp.zeros_like(acc_ref)

        part = lax.dot_general(a_refs[0][...], w_refs[0][...], NT_DIMS, preferred_element_type=F32)
        for p in range(1, n):
            part += lax.dot_general(a_refs[p][...], w_refs[p][...], NT_DIMS, preferred_element_type=F32)
        acc_ref[...] += part

        @pl.when(k == nk - 1)
        def _():
            o_ref[...] = acc_ref[...] + alpha * dz_ref[...]

    row = pl.BlockSpec((tm, D), lambda i, k: (i, 0))
    return pl.pallas_call(
        body, grid=(T // tm, nk),
        in_specs=[pl.BlockSpec((tm, tk), lambda i, k: (i, k))] * n + [pl.BlockSpec((D, tk), lambda i, k: (0, k))] * n + [row],
        out_specs=row,
        out_shape=jax.ShapeDtypeStruct((T, D), F32),
        scratch_shapes=[pltpu.VMEM((tm, D), F32)],
        compiler_params=_params("parallel", "arbitrary"), name=name)(*[p[0] for p in pairs], *[p[1] for p in pairs], dz)


def mm_tn(a, b, scale, shard_cols, name):
    T, M = a.shape
    N = b.shape[1]
    tm, tk = _tile(M, 512), _tile(T, 4096)
    nk = T // tk
    if shard_cols:
        ns = N // N_DEV
        per = 2 if (2 * ns) % 256 == 0 else 1
        tn = per * ns
        out_shape = jax.ShapeDtypeStruct((N_DEV, M, ns), BF16)
        out_spec = pl.BlockSpec((per, tm, ns), lambda i, j, k: (j, i, 0))
    else:
        tn = _tile(N, 1024)
        out_shape = jax.ShapeDtypeStruct((M, N), BF16)
        out_spec = pl.BlockSpec((tm, tn), lambda i, j, k: (i, j))

    def body(a_ref, b_ref, o_ref, acc_ref):
        k = pl.program_id(2)

        @pl.when(k == 0)
        def _():
            acc_ref[...] = jnp.zeros_like(acc_ref)

        acc_ref[...] += lax.dot_general(a_ref[...], b_ref[...], TN_DIMS, preferred_element_type=F32)

        @pl.when(k == nk - 1)
        def _():
            if shard_cols:
                for s in range(per):
                    o_ref[s] = (scale * acc_ref[:, s * ns:(s + 1) * ns]).astype(BF16)
            else:
                o_ref[...] = (scale * acc_ref[...]).astype(BF16)

    return pl.pallas_call(
        body, grid=(M // tm, N // tn, nk),
        in_specs=[pl.BlockSpec((tk, tm), lambda i, j, k: (k, i)), pl.BlockSpec((tk, tn), lambda i, j, k: (k, j))],
        out_specs=out_spec, out_shape=out_shape,
        scratch_shapes=[pltpu.VMEM((tm, tn), F32)],
        compiler_params=_params("parallel", "parallel", "arbitrary"), name=name)(a, b)


def loss_head(y, target, name):
    T, D = y.shape
    tm = _tile(T, 512)

    def body(y_ref, t_ref, l_ref, dy_ref):
        @pl.when(pl.program_id(0) == 0)
        def _():
            l_ref[...] = jnp.zeros_like(l_ref)

        e = y_ref[...] - t_ref[...]
        dy_ref[...] = e * (1.0 / D)
        rows = jnp.sum(e * e, axis=1, keepdims=True) * (0.5 / D)
        l_ref[...] += jnp.sum(rows, axis=0, keepdims=True)

    row = pl.BlockSpec((tm, D), lambda i: (i, 0))
    return pl.pallas_call(
        body, grid=(T // tm,), in_specs=[row, row],
        out_specs=[pl.BlockSpec((1, 1), lambda i: (0, 0)), row],
        out_shape=[jax.ShapeDtypeStruct((1, 1), F32), jax.ShapeDtypeStruct((T, D), F32)],
        compiler_params=_params("arbitrary"), name=name)(y, target)


def _rel_index(i, j):
    return jnp.clip(i - j + LEFT_CHUNKS * CHUNK, -REL_CLIP, REL_CLIP) + REL_CLIP


def bias_band(table_t, name):
    H = table_t.shape[0]
    rows = SUBLANE

    def body(t_ref, o_ref):
        i0 = pl.program_id(0) * rows
        parts = _split3(t_ref[...])
        r = lax.broadcasted_iota(jnp.int32, (REL_PAD, A_W), 0)
        j = lax.broadcasted_iota(jnp.int32, (REL_PAD, A_W), 1)
        for ii in range(rows):
            onehot = jnp.where(r == _rel_index(i0 + ii, j), 1.0, 0.0).astype(BF16)
            o_ref[ii] = sum(jnp.dot(p, onehot, preferred_element_type=F32) for p in parts)

    return pl.pallas_call(
        body, grid=(A_TQ // rows,),
        in_specs=[pl.BlockSpec((H, REL_PAD), lambda i: (0, 0))],
        out_specs=pl.BlockSpec((rows, H, A_W), lambda i: (i, 0, 0)),
        out_shape=jax.ShapeDtypeStruct((A_TQ, H, A_W), F32),
        compiler_params=_params("parallel"), name=name)(table_t)


def bias_band_bwd(dbands, name):
    H = dbands[0].shape[1]
    rows = SUBLANE
    n = len(dbands)

    def body(*refs):
        g_refs, o_ref = refs[:n], refs[n]

        @pl.when(pl.program_id(0) == 0)
        def _():
            o_ref[...] = jnp.zeros_like(o_ref)

        i0 = pl.program_id(0) * rows
        j = lax.broadcasted_iota(jnp.int32, (A_W, REL_PAD), 0)
        r = lax.broadcasted_iota(jnp.int32, (A_W, REL_PAD), 1)
        acc = jnp.zeros((H, REL_PAD), F32)
        for ii in range(rows):
            onehot = jnp.where(r == _rel_index(i0 + ii, j), 1.0, 0.0).astype(BF16)
            g = g_refs[0][ii]
            for q in range(1, n):
                g = g + g_refs[q][ii]
            acc += _dot3(g, onehot)
        o_ref[...] += acc

    spec = pl.BlockSpec((rows, H, A_W), lambda i: (i, 0, 0))
    return pl.pallas_call(
        body, grid=(A_TQ // rows,), in_specs=[spec] * n,
        out_specs=pl.BlockSpec((H, REL_PAD), lambda i: (0, 0)),
        out_shape=jax.ShapeDtypeStruct((H, REL_PAD), F32),
        compiler_params=_params("arbitrary"), name=name)(*dbands)


def _a_window(ref, qi):
    parts = []
    for d in range(A_NWB):
        kb = jnp.maximum(qi - (A_NWB - 1) + d, 0)
        parts.append(ref[pl.ds(pl.multiple_of(kb * A_TQ, A_TQ), A_TQ), :])
    return jnp.concatenate(parts, axis=0)


def _a_valid(qi):
    i = lax.broadcasted_iota(jnp.int32, (A_TQ, A_W), 0)
    j = lax.broadcasted_iota(jnp.int32, (A_TQ, A_W), 1)
    ic, jc = i // CHUNK, j // CHUNK
    return (jc >= ic) & (jc <= ic + LEFT_CHUNKS) & (j >= LEFT_CHUNKS * CHUNK - qi * A_TQ)


def _head_masks():
    lane = lax.broadcasted_iota(jnp.int32, (1, LANE), 1)
    return [lane < HEAD_DIM, lane >= HEAD_DIM]


def _a_probs(qms, kw, b_ref, valid):
    scores = [lax.dot_general(qm, kw, NT_DIMS, preferred_element_type=F32) for qm in qms]
    probs = []
    for hh, s in enumerate(scores):
        s = jnp.where(valid, s * (HEAD_DIM ** -0.5) + b_ref[hh], NEG)
        p = jnp.exp(s - jnp.max(s, axis=1, keepdims=True))
        probs.append(p / jnp.sum(p, axis=1, keepdims=True))
    return probs


def _attn_specs(S, D, tq):
    hp_n = D // LANE
    nq = S // tq
    q = pl.BlockSpec((tq, LANE), lambda hp, b, qi: (b * nq + qi, hp))
    k = pl.BlockSpec((S, LANE), lambda hp, b, qi: (b, hp_n + hp))
    v = pl.BlockSpec((S, LANE), lambda hp, b, qi: (b, 2 * hp_n + hp))
    tile = pl.BlockSpec((tq, LANE), lambda hp, b, qi: (b * nq + qi, hp))
    seq = pl.BlockSpec((S, LANE), lambda hp, b, qi: (b, hp))
    return q, k, v, tile, seq


def attn_a_fwd(qkv, band, S, name):
    T, D3 = qkv.shape
    D = D3 // 3
    nb, nq = T // S, S // A_TQ

    def body(q_ref, k_ref, v_ref, b_ref, o_ref):
        qi = pl.program_id(2)
        q2 = q_ref[...]
        kw, vw = _a_window(k_ref, qi), _a_window(v_ref, qi)
        valid = _a_valid(qi)
        masks = _head_masks()
        probs = _a_probs([jnp.where(hm, q2, jnp.zeros_like(q2)) for hm in masks], kw, b_ref, valid)
        outs = [jnp.dot(p.astype(BF16), vw, preferred_element_type=F32) for p in probs]
        o_ref[...] = jnp.where(masks[0], outs[0], outs[1]).astype(BF16)

    q, k, v, tile, _ = _attn_specs(S, D, A_TQ)
    return pl.pallas_call(
        body, grid=(D // LANE, nb, nq),
        in_specs=[q, k, v, pl.BlockSpec((2, A_TQ, A_W), lambda hp, b, qi: (hp, 0, 0))],
        out_specs=tile, out_shape=jax.ShapeDtypeStruct((T, D), BF16),
        compiler_params=_params("parallel", "parallel", "parallel"), name=name)(qkv, qkv, qkv, band)


def attn_a_bwd(qkv, band, do, S, name):
    T, D3 = qkv.shape
    D = D3 // 3
    nb, nq = T // S, S // A_TQ
    scale = HEAD_DIM ** -0.5

    def body(q_ref, k_ref, v_ref, b_ref, do_ref, dq_ref, dk_ref, dv_ref, db_ref, dk_acc, dv_acc):
        b, qi = pl.program_id(1), pl.program_id(2)

        @pl.when((b == 0) & (qi == 0))
        def _():
            db_ref[...] = jnp.zeros_like(db_ref)

        @pl.when(qi == 0)
        def _():
            dk_acc[...] = jnp.zeros_like(dk_acc)
            dv_acc[...] = jnp.zeros_like(dv_acc)

        q2, do2 = q_ref[...], do_ref[...]
        kw, vw = _a_window(k_ref, qi), _a_window(v_ref, qi)
        valid = _a_valid(qi)
        masks = _head_masks()
        qms = [jnp.where(hm, q2, jnp.zeros_like(q2)) for hm in masks]
        doms = [jnp.where(hm, do2, jnp.zeros_like(do2)) for hm in masks]
        dps = [lax.dot_general(dom, vw, NT_DIMS, preferred_element_type=F32) for dom in doms]
        probs = _a_probs(qms, kw, b_ref, valid)
        dsbs = []
        for hh, (p, dp) in enumerate(zip(probs, dps)):
            ds = p * (dp - jnp.sum(p * dp, axis=1, keepdims=True))
            db_ref[hh] += ds
            dsbs.append(ds.astype(BF16))
        dqs = [jnp.dot(dsb, kw, preferred_element_type=F32) for dsb in dsbs]
        dks = [lax.dot_general(dsb, qm, TN_DIMS, preferred_element_type=F32) for dsb, qm in zip(dsbs, qms)]
        dvs = [lax.dot_general(p.astype(BF16), dom, TN_DIMS, preferred_element_type=F32) for p, dom in zip(probs, doms)]
        dq_ref[...] = (jnp.where(masks[0], dqs[0], dqs[1]) * scale).astype(BF16)
        dkw = (dks[0] + dks[1]) * scale
        dvw = dvs[0] + dvs[1]
        for d in range(A_NWB):
            kb = jnp.maximum(qi - (A_NWB - 1) + d, 0)
            rows = pl.ds(pl.multiple_of(kb * A_TQ, A_TQ), A_TQ)
            dk_acc[rows, :] += dkw[d * A_TQ:(d + 1) * A_TQ]
            dv_acc[rows, :] += dvw[d * A_TQ:(d + 1) * A_TQ]

        @pl.when(qi == nq - 1)
        def _():
            dk_ref[...] = dk_acc[...].astype(BF16)
            dv_ref[...] = dv_acc[...].astype(BF16)

    q, k, v, tile, seq = _attn_specs(S, D, A_TQ)
    bspec = pl.BlockSpec((2, A_TQ, A_W), lambda hp, b, qi: (hp, 0, 0))
    act = jax.ShapeDtypeStruct((T, D), BF16)
    return pl.pallas_call(
        body, grid=(D // LANE, nb, nq),
        in_specs=[q, k, v, bspec, tile], out_specs=[tile, seq, seq, bspec],
        out_shape=[act, act, act, jax.ShapeDtypeStruct(band.shape, F32)],
        scratch_shapes=[pltpu.VMEM((S, LANE), F32), pltpu.VMEM((S, LANE), F32)],
        compiler_params=_params("arbitrary", "arbitrary", "arbitrary"), name=name)(qkv, qkv, qkv, band, do)


def _dot2(v, w):
    h = v.astype(BF16)
    lo = (v - h.astype(F32)).astype(BF16)
    return jnp.dot(h, w, preferred_element_type=F32) + jnp.dot(lo, w, preferred_element_type=F32)


def _b_weights(qms, kts, rights, after, diagonal):
    inst = [(t, hh) for t in range(len(kts)) for hh in range(2)]
    zs = [lax.dot_general(qms[hh], kts[t], NT_DIMS, preferred_element_type=F32) for t, hh in inst]
    lbs, l1ms, his, los = [], [], [], []
    causal = _strictly_causal() if any(diagonal) else None
    for (t, hh), z in zip(inst, zs):
        nz = -z
        soft = jnp.log(1.0 + jnp.exp(jnp.minimum(z, nz)))
        lbs.append(jnp.minimum(z, 0.0) - soft)
        l1m = jnp.minimum(nz, 0.0) - soft
        if diagonal[t]:
            l1m = jnp.where(causal, l1m, 0.0)
        hi = l1m.astype(BF16)
        l1ms.append(l1m)
        his.append(hi)
        los.append((l1m - hi.astype(F32)).astype(BF16))
    sums = [jnp.dot(hi, after, preferred_element_type=F32) + jnp.dot(lo, after, preferred_element_type=F32)
            for hi, lo in zip(his, los)]
    rights = list(rights)
    weights = []
    for (t, hh), lb, l1m, c in zip(inst, lbs, l1ms, sums):
        a = jnp.exp(lb + (rights[hh] + c))
        weights.append(jnp.where(causal, a, 0.0) if diagonal[t] else a)
        rights[hh] = rights[hh] + jnp.sum(l1m, axis=1, keepdims=True)
    return zs, weights, rights


def _strictly_causal():
    row = lax.broadcasted_iota(jnp.int32, (B_T, B_T), 0)
    col = lax.broadcasted_iota(jnp.int32, (B_T, B_T), 1)
    return col < row


def _tri(strict_lower):
    r = lax.broadcasted_iota(jnp.int32, (B_T, B_T), 0)
    c = lax.broadcasted_iota(jnp.int32, (B_T, B_T), 1)
    return jnp.where((r > c) if strict_lower else (r < c), 1.0, 0.0).astype(BF16)


def _scaled_heads(q2):
    qs = q2 * (HEAD_DIM ** -0.5)
    return [jnp.where(hm, qs, jnp.zeros_like(qs)) for hm in _head_masks()]


def attn_b_fwd(qkv, S, name):
    T, D3 = qkv.shape
    D = D3 // 3
    nb, nq = T // S, S // B_T

    def body(q_ref, k_ref, v_ref, o_ref):
        qi = pl.program_id(2)
        after = _tri(True)
        qms = _scaled_heads(q_ref[...])

        def tiles(kbs, carry, diagonal):
            rows = [pl.ds(pl.multiple_of(kb * B_T, B_T), B_T) for kb in kbs]
            kts, vts = [k_ref[r, :] for r in rows], [v_ref[r, :] for r in rows]
            _, weights, rights = _b_weights(qms, kts, (carry[0], carry[2]), after, diagonal)
            accs = [carry[1], carry[3]]
            for i, a in enumerate(weights):
                accs[i % 2] = accs[i % 2] + jnp.dot(a.astype(BF16), vts[i // 2], preferred_element_type=F32)
            return rights[0], accs[0], rights[1], accs[1]

        init = (jnp.zeros((B_T, 1), F32), jnp.zeros((B_T, LANE), F32)) * 2
        res = lax.cond(qi % 2 == 1, lambda c: tiles([qi, qi - 1], c, [True, False]), lambda c: tiles([qi], c, [True]), init)
        top = qi - 1 - qi % 2
        res = lax.fori_loop(0, qi // 2, lambda p, c: tiles([top - 2 * p, top - 1 - 2 * p], c, [False, False]), res)
        o_ref[...] = jnp.where(_head_masks()[0], res[1], res[3]).astype(BF16)

    q, k, v, tile, _ = _attn_specs(S, D, B_T)
    return pl.pallas_call(
        body, grid=(D // LANE, nb, nq), in_specs=[q, k, v], out_specs=tile,
        out_shape=jax.ShapeDtypeStruct((T, D), BF16),
        compiler_params=_params("parallel", "parallel", "parallel"), name=name)(qkv, qkv, qkv)


def attn_b_bwd(qkv, do, S, name):
    T, D3 = qkv.shape
    D = D3 // 3
    nb, nq = T // S, S // B_T
    scale = HEAD_DIM ** -0.5

    def body(q_ref, k_ref, v_ref, do_ref, dq_ref, dk_ref, dv_ref, dk_acc, dv_acc, z_s, g_s):
        qi = pl.program_id(2)

        @pl.when(qi == 0)
        def _():
            dk_acc[...] = jnp.zeros_like(dk_acc)
            dv_acc[...] = jnp.zeros_like(dv_acc)

        do2 = do_ref[...]
        after, before = _tri(True), _tri(False)
        masks = _head_masks()
        qms = _scaled_heads(q_ref[...])
        doms = [jnp.where(hm, do2, jnp.zeros_like(do2)) for hm in masks]

        def sweep_left(kbs, rights, diagonal):
            rows = [pl.ds(pl.multiple_of(kb * B_T, B_T), B_T) for kb in kbs]
            kts, vts = [k_ref[r, :] for r in rows], [v_ref[r, :] for r in rows]
            das = [lax.dot_general(doms[hh], vt, NT_DIMS, preferred_element_type=F32) for vt in vts for hh in range(2)]
            zs, weights, rights = _b_weights(qms, kts, rights, after, diagonal)
            for i, (z, a, da) in enumerate(zip(zs, weights, das)):
                z_s[i % 2, kbs[i // 2]] = z
                g_s[i % 2, kbs[i // 2]] = da * a
            dvs = [lax.dot_general(a.astype(BF16), doms[i % 2], TN_DIMS, preferred_element_type=F32)
                   for i, a in enumerate(weights)]
            for t, r in enumerate(rows):
                dv_acc[r, :] += dvs[2 * t] + dvs[2 * t + 1]
            return tuple(rights)

        zero_col = jnp.zeros((B_T, 1), F32)
        odd = qi % 2 == 1
        top = qi - 1 - qi % 2
        rights = lax.cond(odd, lambda c: sweep_left([qi, qi - 1], c, [True, False]), lambda c: sweep_left([qi], c, [True]),
                          (zero_col, zero_col))
        lax.fori_loop(0, qi // 2, lambda p, c: sweep_left([top - 2 * p, top - 1 - 2 * p], c, [False, False]), rights)

        def sweep_right(kbs, carry, diagonal):
            rows = [pl.ds(pl.multiple_of(kb * B_T, B_T), B_T) for kb in kbs]
            inst = [(t, hh) for t in range(len(kbs)) for hh in range(2)]
            gs = [g_s[hh, kbs[t]] for t, hh in inst]
            sums = [_dot2(g, before) for g in gs]
            lefts, dqs = [carry[0], carry[2]], [carry[1], carry[3]]
            dzbs = []
            for (t, hh), g, c in zip(inst, gs, sums):
                beta = jax.nn.sigmoid(z_s[hh, kbs[t]])
                dz = g * (1.0 - beta) - beta * (lefts[hh] + c)
                if diagonal[t]:
                    dz = jnp.where(_strictly_causal(), dz, 0.0)
                dzbs.append(dz.astype(BF16))
                lefts[hh] = lefts[hh] + jnp.sum(g, axis=1, keepdims=True)
            dks = [lax.dot_general(dzb, qms[hh], TN_DIMS, preferred_element_type=F32) for (t, hh), dzb in zip(inst, dzbs)]
            for (t, hh), dzb in zip(inst, dzbs):
                dqs[hh] = dqs[hh] + jnp.dot(dzb, k_ref[rows[t], :], preferred_element_type=F32)
            for t, r in enumerate(rows):
                dk_acc[r, :] += dks[2 * t] + dks[2 * t + 1]
            return lefts[0], dqs[0], lefts[1], dqs[1]

        init = (zero_col, jnp.zeros((B_T, LANE), F32)) * 2
        res = lax.fori_loop(0, qi // 2, lambda p, c: sweep_right([2 * p, 2 * p + 1], c, [False, False]), init)
        res = lax.cond(odd, lambda c: sweep_right([qi - 1, qi], c, [False, True]), lambda c: sweep_right([qi], c, [True]), res)
        dq_ref[...] = (jnp.where(masks[0], res[1], res[3]) * scale).astype(BF16)

        @pl.when(qi == nq - 1)
        def _():
            dk_ref[...] = dk_acc[...].astype(BF16)
            dv_ref[...] = dv_acc[...].astype(BF16)

    q, k, v, tile, seq = _attn_specs(S, D, B_T)
    act = jax.ShapeDtypeStruct((T, D), BF16)
    return pl.pallas_call(
        body, grid=(D // LANE, nb, nq),
        in_specs=[q, k, v, tile], out_specs=[tile, seq, seq], out_shape=[act, act, act],
        scratch_shapes=[pltpu.VMEM((S, LANE), F32), pltpu.VMEM((S, LANE), F32),
                        pltpu.VMEM((2, nq, B_T, B_T), F32), pltpu.VMEM((2, nq, B_T, B_T), F32)],
        compiler_params=_params("arbitrary", "arbitrary", "arbitrary"), name=name)(qkv, qkv, qkv, do)


HBM = pl.BlockSpec(memory_space=pltpu.HBM)
SEM = pl.BlockSpec(memory_space=pltpu.SEMAPHORE)
N_PEERS = N_DEV - 1
GATHERS_AHEAD = 3


def _place():
    return lax.axis_index("x"), lax.axis_index("y"), lax.axis_index("c")


def _peers(x, y, c):
    return [(1 - x if r & 4 else x, 1 - y if r & 2 else y, 1 - c if r & 1 else c) for r in range(1, N_DEV)]


def _block(ref, shape, by_cols, j):
    r, w = shape
    if by_cols:
        return ref.at[:, pl.ds(pl.multiple_of(j * w, LANE), w)]
    return ref.at[pl.ds(pl.multiple_of(j * r, SUBLANE), r), :]


def _exchange_copies(gather, src_refs, land_refs, send_sems, recv_sems, by_cols):
    x, y, c = _place()
    me = 4 * x + 2 * y + c
    out = []
    for t, (src, land) in enumerate(zip(src_refs, land_refs)):
        for r, peer in enumerate(_peers(x, y, c)):
            pj = 4 * peer[0] + 2 * peer[1] + peer[2]
            if gather:
                mine, to_me, theirs = src, _block(land, src.shape, by_cols[t], me), _block(land, src.shape, by_cols[t], pj)
            else:
                mine, to_me, theirs = src.at[pj], land.at[me], land.at[pj]
            sems = dict(send_sem=send_sems.at[N_PEERS * t + r], recv_sem=recv_sems.at[N_PEERS * t + r],
                        device_id=peer, device_id_type=MESH_ID)
            out.append((pltpu.make_async_remote_copy(src_ref=mine, dst_ref=to_me, **sems),
                        pltpu.make_async_remote_copy(src_ref=mine, dst_ref=theirs, **sems)))
    return out


def _own_copies(gather, src_refs, land_refs, own_sems, by_cols):
    x, y, c = _place()
    me = 4 * x + 2 * y + c
    out = []
    for t, (src, land) in enumerate(zip(src_refs, land_refs)):
        if gather:
            out.append(pltpu.make_async_copy(src, _block(land, src.shape, by_cols[t], me), own_sems.at[t]))
        else:
            out.append(pltpu.make_async_copy(src.at[me], land.at[me], own_sems.at[t]))
    return out


def exchange_start(gather, srcs, land_shapes, by_cols, after, name):
    n = len(srcs)

    def body(*refs):
        src_refs, land_refs = refs[:n], refs[n:2 * n]
        send_sems, recv_sems, own_sems = refs[2 * n + 1:2 * n + 4]
        token = refs[-1]
        for cp in _own_copies(gather, src_refs, land_refs, own_sems, by_cols):
            cp.start()
        for mine, _ in _exchange_copies(gather, src_refs, land_refs, send_sems, recv_sems, by_cols):
            mine.start()
        token[...] = jnp.zeros_like(token)

    lands = [pltpu.with_memory_space_constraint(lax.empty(s.shape, s.dtype), pltpu.HBM) for s in land_shapes]
    srcs = [pltpu.with_memory_space_constraint(s, pltpu.HBM) for s in srcs]
    res = pl.pallas_call(
        body, name=name,
        out_shape=(pltpu.SemaphoreType.DMA((N_PEERS * n,)), pltpu.SemaphoreType.DMA((N_PEERS * n,)),
                   pltpu.SemaphoreType.DMA((n,)),
                   *[pltpu.HBM(s.shape, s.dtype) for s in srcs], *[pltpu.HBM(s.shape, s.dtype) for s in land_shapes],
                   jax.ShapeDtypeStruct((SUBLANE, LANE), F32)),
        in_specs=[HBM] * (2 * n) + [ANY],
        out_specs=(SEM, SEM, SEM, *[HBM] * (2 * n), pl.BlockSpec(memory_space=pltpu.VMEM)),
        input_output_aliases={i: 3 + i for i in range(2 * n)},
        compiler_params=pltpu.CompilerParams(has_side_effects=pltpu.SideEffectType.DATAFLOW_SIDE_EFFECTING),
    )(*srcs, *lands, after)
    return dict(gather=gather, n=n, by_cols=by_cols, sems=res[:3], srcs=res[3:3 + n],
                lands=res[3 + n:3 + 2 * n], token=res[-1])


def exchange_wait(started, after, name):
    n, gather, by_cols = started["n"], started["gather"], started["by_cols"]

    def body(*refs):
        src_refs, land_refs = refs[:n], refs[n:2 * n]
        send_sems, recv_sems, own_sems = refs[2 * n:2 * n + 3]
        for mine, theirs in _exchange_copies(gather, src_refs, land_refs, send_sems, recv_sems, by_cols):
            mine.wait_send()
            theirs.wait_recv()
        for cp in _own_copies(gather, src_refs, land_refs, own_sems, by_cols):
            cp.wait()

    res = pl.pallas_call(
        body, name=name,
        out_shape=tuple(pltpu.HBM(s.shape, s.dtype) for s in (*started["srcs"], *started["lands"])),
        in_specs=[HBM] * (2 * n) + [SEM, SEM, SEM, ANY], out_specs=tuple([HBM] * (2 * n)),
        input_output_aliases={i: i for i in range(2 * n)},
        compiler_params=pltpu.CompilerParams(has_side_effects=pltpu.SideEffectType.DATAFLOW_SIDE_EFFECTING),
    )(*started["srcs"], *started["lands"], *started["sems"], after)
    return res[n:]


def gather_small(v, dep, name):
    R, C = v.shape

    def body(v_ref, dep_ref, o_ref, send_sems, recv_sems):
        x, y, c = _place()
        o_ref[4 * x + 2 * y + c] = v_ref[...]
        peers = _peers(x, y, c)

        def copy(r, owner, to):
            slot = o_ref.at[4 * owner[0] + 2 * owner[1] + owner[2]]
            return pltpu.make_async_remote_copy(
                src_ref=slot, dst_ref=slot, send_sem=send_sems.at[r], recv_sem=recv_sems.at[r],
                device_id=to, device_id_type=MESH_ID)

        sends = [copy(r, (x, y, c), peer) for r, peer in enumerate(peers)]
        for cp in sends:
            cp.start()
        for r, peer in enumerate(peers):
            copy(r, peer, (x, y, c)).wait_recv()
        for cp in sends:
            cp.wait_send()

    vm = pl.BlockSpec(memory_space=pltpu.VMEM)
    return pl.pallas_call(
        body, in_specs=[vm, ANY], out_specs=vm, out_shape=jax.ShapeDtypeStruct((N_DEV, R, C), F32),
        scratch_shapes=[pltpu.SemaphoreType.DMA((N_PEERS,)), pltpu.SemaphoreType.DMA((N_PEERS,))],
        name=name)(v, dep)


def sum_devices(g, name):
    _, R, C = g.shape

    def body(g_ref, o_ref):
        acc = g_ref[0]
        for j in range(1, N_DEV):
            acc = acc + g_ref[j]
        o_ref[...] = acc

    vm = pl.BlockSpec(memory_space=pltpu.VMEM)
    return pl.pallas_call(body, in_specs=[vm], out_specs=vm, out_shape=jax.ShapeDtypeStruct((R, C), F32), name=name)(g)


def adamw(w, m, v, index, parts, outs, name):
    L, R, C = w.shape
    if outs is None:
        outs = [lax.empty((L, R, C), F32) for _ in range(4)]
    P, Rp, Cp = parts.shape
    tr = _tile(R, 512, SUBLANE) if Rp == R else R

    def body(w_ref, m_ref, v_ref, p_ref, g_in, d_in, m_in, v_in, g_out, d_out, m_out, v_out):
        g = p_ref[0, :tr, :C].astype(F32)
        for q in range(1, P):
            g = g + p_ref[q, :tr, :C].astype(F32)
        mn = ADAM_B1 * m_ref[...] + (1.0 - ADAM_B1) * g
        vn = ADAM_B2 * v_ref[...] + (1.0 - ADAM_B2) * (g * g)
        m_hat = mn / (1.0 - ADAM_B1 ** ADAM_STEP)
        v_hat = vn / (1.0 - ADAM_B2 ** ADAM_STEP)
        g_out[...] = g
        d_out[...] = -ADAM_LR * (m_hat / (jnp.sqrt(v_hat) + ADAM_EPS) + ADAM_WD * w_ref[...])
        m_out[...] = mn
        v_out[...] = vn

    slab = pl.BlockSpec((None, tr, C), lambda r: (index, r, 0))
    pspec = pl.BlockSpec((P, tr if Rp == R else Rp, Cp), lambda r: (0, r, 0))
    return pl.pallas_call(
        body, grid=(R // tr,), in_specs=[slab] * 3 + [pspec] + [ANY] * 4, out_specs=[slab] * 4,
        out_shape=[jax.ShapeDtypeStruct((L, R, C), F32)] * 4,
        input_output_aliases={4 + q: q for q in range(4)},
        compiler_params=_params("parallel"), name=name)(w, m, v, parts, *outs)


def _pad_to(a, axis, size):
    pad = [(0, 0)] * a.ndim
    pad[axis] = (0, size - a.shape[axis])
    return jnp.pad(a, pad)


def kernel(x, w_qkv_a, w_o_a, rel_bias, w_qkv_b, w_o_b, ffn_w_gate, ffn_w_up, ffn_w_down, ln_g, ln_b, loss_target, m_w_qkv_a, m_w_o_a, m_rel_bias, m_w_qkv_b, m_w_o_b, m_ffn_w_gate, m_ffn_w_up, m_ffn_w_down, m_ln_g, m_ln_b, v_w_qkv_a, v_w_o_a, v_rel_bias, v_w_qkv_b, v_w_o_b, v_ffn_w_gate, v_ffn_w_up, v_ffn_w_down, v_ln_g, v_ln_b):
    nb, S, D = x.shape
    T = nb * S
    depth = ffn_w_gate.shape[0]
    H = D // HEAD_DIM
    fs = ffn_w_gate.shape[-1]
    fp = -(-fs // LANE) * LANE
    alpha = (2.0 * depth) ** 0.25
    cx, cy, cc = _place()
    me = 4 * cx + 2 * cy + cc

    ln_local = jnp.concatenate([ln_g.reshape(depth * 3, -1), ln_b.reshape(depth * 3, -1)], axis=0)
    ln_all = gather_small(ln_local, ln_local, "gather_ln")
    ln_full = jnp.transpose(ln_all, (1, 0, 2)).reshape(2 * depth * 3, D)
    ln_gain = lambda i, s: ln_full[3 * i + s][None, :]
    ln_bias = lambda i, s: ln_full[3 * depth + 3 * i + s][None, :]

    table_t = _pad_to(rel_bias.T, 1, REL_PAD)
    band = jnp.transpose(bias_band(table_t, "bias_band"), (1, 0, 2))

    subs = []
    for i in range(depth):
        for s in (0, 1, 2):
            if s == 1:
                wq, wo = (w_qkv_a, w_o_a) if i % 2 == 0 else (w_qkv_b, w_o_b)
                subs.append(([wq[i // 2].astype(BF16), wo[i // 2].astype(BF16)], [True, False]))
            else:
                f = 0 if s == 0 else 1
                subs.append(([_pad_to(ffn_w_gate[i, f].astype(BF16), 1, fp), _pad_to(ffn_w_up[i, f].astype(BF16), 1, fp),
                              _pad_to(ffn_w_down[i, f].astype(BF16), 0, fp)], [True, True, False]))

    def start_gather(k, after):
        shards, by_cols = subs[k]
        shapes = [jax.ShapeDtypeStruct((s.shape[0], N_DEV * s.shape[1]) if col else (N_DEV * s.shape[0], s.shape[1]), BF16)
                  for s, col in zip(shards, by_cols)]
        return exchange_start(True, shards, shapes, by_cols, after, f"gather_start_{k}")

    xf = x.reshape(T, D)
    act, act_b = xf, xf.astype(BF16)
    gathers = {}
    for k in range(min(GATHERS_AHEAD, len(subs))):
        gathers[k] = start_gather(k, xf if k == 0 else gathers[k - 1]["token"])
    newest = gathers[k]["token"]
    saved = []
    for i in range(depth):
        layer = {}
        for s in (0, 1, 2):
            k = 3 * i + s
            tag = f"L{i}S{s}"
            if 0 < k and k + GATHERS_AHEAD - 1 < len(subs):
                gathers[k + GATHERS_AHEAD - 1] = start_gather(k + GATHERS_AHEAD - 1, act_b)
                newest = gathers[k + GATHERS_AHEAD - 1]["token"]
            full = exchange_wait(gathers.pop(k), newest, f"gather_wait_{k}")
            if s == 1:
                wqkv_f, wo_f = full
                qkv = mm_nn(act_b, wqkv_f, "qkv_" + tag)
                if i % 2 == 0:
                    att = attn_a_fwd(qkv, band, S, "attn_a_fwd_" + tag)
                else:
                    att = attn_b_fwd(qkv, S, "attn_b_fwd_" + tag)
                z, o, ob = mm_res_ln(att, wo_f, act, ln_gain(i, s), ln_bias(i, s), alpha, 1.0, "out_ln_" + tag)
                layer[s] = dict(x_b=act_b, qkv=qkv, att=att, z=z, wqkv=wqkv_f, wo=wo_f)
            else:
                wg_f, wu_f, wd_f = full
                h, u, a = ffn_up(act_b, wg_f, wu_f, "ffn_up_" + tag)
                z, o, ob = mm_res_ln(a, wd_f, act, ln_gain(i, s), ln_bias(i, s), alpha, 0.5, "down_ln_" + tag)
                layer[s] = dict(x_b=act_b, h=h, u=u, a=a, z=z, wg=wg_f, wu=wu_f, wd=wd_f)
            act, act_b = o, ob
        saved.append(layer)

    loss_local, d_act = loss_head(act, loss_target.reshape(T, D), "loss_head")
    loss = lax.psum(loss_local[0, 0], ("x", "y", "c"))

    results = {}

    def update(name, w, m, v, index, parts):
        L = w.shape[0] if w.ndim == 3 else w.shape[0] * w.shape[1]
        flat = lambda t: t.reshape((L,) + t.shape[-2:])
        results[name] = adamw(flat(w), flat(m), flat(v), index, parts, results.get(name), f"adamw_{name}_{index}")

    def finish(entry, after):
        exchange, targets, tag = entry
        lands = exchange_wait(exchange, after, "scatter_wait_" + tag)
        for (name, w, m, v, index), parts in zip(targets, lands):
            update(name, w, m, v, index, parts)
        return results[targets[-1][0]][0]

    pending = []
    started = d_act
    dbands = []
    dln_g = [None] * (3 * depth)
    dln_b = [None] * (3 * depth)
    for i in reversed(range(depth)):
        for s in (2, 1, 0):
            tag = f"L{i}S{s}"
            sv = saved[i][s]
            dz, dzb, dg, db = ln_bwd(sv["z"], ln_gain(i, s), d_act, started, "ln_bwd_" + tag)
            dln_g[3 * i + s], dln_b[3 * i + s] = dg, db
            if s == 1:
                j = i // 2
                d_att = mm_nt(dzb, sv["wo"], "att_bwd_" + tag)
                g_wo = mm_tn(sv["att"], dzb, 1.0, False, "dwo_" + tag)
                if i % 2 == 0:
                    dq, dk, dv, dband = attn_a_bwd(sv["qkv"], band, d_att, S, "attn_a_bwd_" + tag)
                    dbands.append(jnp.transpose(dband, (1, 0, 2)))
                else:
                    dq, dk, dv = attn_b_bwd(sv["qkv"], d_att, S, "attn_b_bwd_" + tag)
                dqkv = jnp.concatenate([dq, dk, dv], axis=1)
                g_wqkv = mm_tn(sv["x_b"], dqkv, 1.0, True, "dwqkv_" + tag)
                d_act = mm_nt_res([(dqkv, sv["wqkv"])], dz, alpha, "dx_mix_" + tag)
                grads = [g_wqkv, g_wo.reshape(N_DEV, D // N_DEV, D)]
                if i % 2 == 0:
                    targets = [("w_qkv_a", w_qkv_a, m_w_qkv_a, v_w_qkv_a, j), ("w_o_a", w_o_a, m_w_o_a, v_w_o_a, j)]
                else:
                    targets = [("w_qkv_b", w_qkv_b, m_w_qkv_b, v_w_qkv_b, j), ("w_o_b", w_o_b, m_w_o_b, v_w_o_b, j)]
            else:
                f = 0 if s == 0 else 1
                dh, du = ffn_bwd_mid(dzb, sv["wd"], sv["h"], sv["u"], 0.5, "ffn_mid_" + tag)
                g_wd = mm_tn(sv["a"], dzb, 0.5, False, "dwd_" + tag)
                g_wg = mm_tn(sv["x_b"], dh, 1.0, True, "dwg_" + tag)
                g_wu = mm_tn(sv["x_b"], du, 1.0, True, "dwu_" + tag)
                d_act = mm_nt_res([(dh, sv["wg"]), (du, sv["wu"])], dz, alpha, "dx_ffn_" + tag)
                grads = [g_wg, g_wu, g_wd.reshape(N_DEV, fp, D)]
                idx = 2 * i + f
                targets = [("ffn_w_gate", ffn_w_gate, m_ffn_w_gate, v_ffn_w_gate, idx),
                           ("ffn_w_up", ffn_w_up, m_ffn_w_up, v_ffn_w_up, idx),
                           ("ffn_w_down", ffn_w_down, m_ffn_w_down, v_ffn_w_down, idx)]
            after = d_act
            if i == 0 and s == 0:
                dtable_t = bias_band_bwd(dbands, "bias_band_bwd")
                small = jnp.concatenate(dln_g + dln_b + [_pad_to(dtable_t, 1, D)], axis=0)
                small = _pad_to(small, 0, -(-small.shape[0] // SUBLANE) * SUBLANE)
                total = after = sum_devices(gather_small(small, d_act, "gather_small_grads"), "sum_small_grads")
            shapes = [jax.ShapeDtypeStruct(g.shape, g.dtype) for g in grads]
            pending.append((exchange_start(False, grads, shapes, None, after, "scatter_start_" + tag), targets, tag))
            started = pending[-1][0]["token"]
    grad_x = d_act.reshape(nb, S, D)

    last = pending.pop()
    done = started
    for entry in pending:
        done = finish(entry, done)
    finish(last, done)
    n_ln = 3 * depth
    g_ln_g = lax.dynamic_slice_in_dim(total[:n_ln], me * (D // N_DEV), D // N_DEV, axis=1)
    g_ln_b = lax.dynamic_slice_in_dim(total[n_ln:2 * n_ln], me * (D // N_DEV), D // N_DEV, axis=1)
    g_rel = total[2 * n_ln:2 * n_ln + H, :N_REL].T
    as3 = lambda t: t.reshape((1, -1, t.shape[-1]))
    results["ln_g"] = adamw(as3(ln_g), as3(m_ln_g), as3(v_ln_g), 0, g_ln_g[None], None, "adamw_ln_g")
    results["ln_b"] = adamw(as3(ln_b), as3(m_ln_b), as3(v_ln_b), 0, g_ln_b[None], None, "adamw_ln_b")
    results["rel_bias"] = adamw(as3(rel_bias), as3(m_rel_bias), as3(v_rel_bias), 0, g_rel[None], None, "adamw_rel_bias")

    order = [("w_qkv_a", w_qkv_a), ("w_o_a", w_o_a), ("rel_bias", rel_bias), ("w_qkv_b", w_qkv_b), ("w_o_b", w_o_b),
             ("ffn_w_gate", ffn_w_gate), ("ffn_w_up", ffn_w_up), ("ffn_w_down", ffn_w_down), ("ln_g", ln_g), ("ln_b", ln_b)]
    outs = [loss, grad_x]
    for q in range(4):
        for name, like in order:
            outs.append(results[name][q].reshape(like.shape))
    return tuple(outs)
```

```python
import functools

import jax
import jax.numpy as jnp
from jax import lax
from jax.experimental import pallas as pl
from jax.experimental.pallas import tpu as pltpu

BF16 = jnp.bfloat16
F32 = jnp.float32
MESH_ID = pl.DeviceIdType.MESH
ANY = pl.BlockSpec(memory_space=pl.ANY)

N_DEV = 8
LANE = 128
MXU_COLS = 256
SUBLANE = 8
VMEM_LIMIT = 56 * 1024 * 1024

HEAD_DIM = 64
CHUNK = 64
LEFT_CHUNKS = 8
REL_CLIP = 128
N_REL = 2 * REL_CLIP + 1
REL_PAD = 384
LN_EPS = 1e-5
ADAM_LR, ADAM_B1, ADAM_B2, ADAM_EPS, ADAM_WD, ADAM_STEP = 0.001, 0.9, 0.999, 1e-08, 0.01, 10
NEG = -1e30

A_TQ = 128
A_NWB = LEFT_CHUNKS * CHUNK // A_TQ + 1
A_W = A_NWB * A_TQ
A_SUB = 2
B_T = 256

NT_DIMS = (((1,), (1,)), ((), ()))
TN_DIMS = (((0,), (0,)), ((), ()))


def _tile(n, pref, unit=LANE):
    if n <= pref:
        return n
    t = pref - pref % unit
    while t > unit and n % t:
        t -= unit
    assert n % t == 0, (n, pref)
    return t


def _params(*sem):
    return pltpu.CompilerParams(dimension_semantics=sem, vmem_limit_bytes=VMEM_LIMIT)


def _split3(v):
    h = v.astype(BF16)
    r = v - h.astype(F32)
    m = r.astype(BF16)
    lo = (r - m.astype(F32)).astype(BF16)
    return h, m, lo


def _dot3(v, w):
    h, m, lo = _split3(v)
    return (jnp.dot(h, w, preferred_element_type=F32) + jnp.dot(m, w, preferred_element_type=F32)
            + jnp.dot(lo, w, preferred_element_type=F32))


def mm_nn(a, w, name):
    T, K = a.shape
    N = w.shape[1]
    tm, tn = _tile(T, 1024), _tile(N, 768)

    def body(a_ref, w_ref, o_ref):
        o_ref[...] = jnp.dot(a_ref[...], w_ref[...], preferred_element_type=F32).astype(o_ref.dtype)

    return pl.pallas_call(
        body, grid=(T // tm, N // tn),
        in_specs=[pl.BlockSpec((tm, K), lambda i, j: (i, 0)), pl.BlockSpec((K, tn), lambda i, j: (0, j))],
        out_specs=pl.BlockSpec((tm, tn), lambda i, j: (i, j)),
        out_shape=jax.ShapeDtypeStruct((T, N), BF16),
        compiler_params=_params("parallel", "parallel"), name=name)(a, w)


def ffn_up(xb, wg, wu, name):
    T, K = xb.shape
    N = wg.shape[1]
    tm, tn = _tile(T, 512), _tile(N, 768)

    def body(x_ref, wg_ref, wu_ref, h_ref, u_ref, a_ref):
        x = x_ref[...]
        chunks = [slice(c, c + MXU_COLS) for c in range(0, tn, MXU_COLS)]
        hs = [jnp.dot(x, wg_ref[:, c], preferred_element_type=F32) for c in chunks]
        us = [jnp.dot(x, wu_ref[:, c], preferred_element_type=F32) for c in chunks]
        for c, h, u in zip(chunks, hs, us):
            h_ref[:, c] = h.astype(BF16)
            u_ref[:, c] = u.astype(BF16)
            a_ref[:, c] = (h * jax.nn.sigmoid(h) * u).astype(BF16)

    wspec = pl.BlockSpec((K, tn), lambda j, i: (0, j))
    ospec = pl.BlockSpec((tm, tn), lambda j, i: (i, j))
    return pl.pallas_call(
        body, grid=(N // tn, T // tm),
        in_specs=[pl.BlockSpec((tm, K), lambda j, i: (i, 0)), wspec, wspec],
        out_specs=[ospec, ospec, ospec],
        out_shape=[jax.ShapeDtypeStruct((T, N), BF16)] * 3,
        compiler_params=_params("parallel", "parallel"), name=name)(xb, wg, wu)


def mm_res_ln(a, w, x, g, b, alpha, scale, name):
    T, K = a.shape
    D = w.shape[1]
    tm = _tile(T, 512)
    parts = [slice(r, r + MXU_COLS) for r in range(0, tm, MXU_COLS)] if tm % MXU_COLS == 0 else [slice(0, tm)]

    def body(a_ref, w_ref, x_ref, g_ref, b_ref, z_ref, o_ref, ob_ref):
        ys = [jnp.dot(a_ref[r, :], w_ref[...], preferred_element_type=F32) for r in parts]
        for r, y in zip(parts, ys):
            z = alpha * x_ref[r, :] + scale * y
            mu = jnp.mean(z, axis=1, keepdims=True)
            zc = z - mu
            var = jnp.mean(zc * zc, axis=1, keepdims=True)
            o = zc * lax.rsqrt(var + LN_EPS) * g_ref[...] + b_ref[...]
            z_ref[r, :] = z
            o_ref[r, :] = o
            ob_ref[r, :] = o.astype(BF16)

    row = pl.BlockSpec((tm, D), lambda i: (i, 0))
    vec = pl.BlockSpec((1, D), lambda i: (0, 0))
    return pl.pallas_call(
        body, grid=(T // tm,),
        in_specs=[pl.BlockSpec((tm, K), lambda i: (i, 0)), pl.BlockSpec((K, D), lambda i: (0, 0)), row, vec, vec],
        out_specs=[row, row, row],
        out_shape=[jax.ShapeDtypeStruct((T, D), F32), jax.ShapeDtypeStruct((T, D), F32),
                   jax.ShapeDtypeStruct((T, D), BF16)],
        compiler_params=_params("parallel"), name=name)(a, w, x, g, b)


def ln_bwd(z, g, do, dep, name):
    T, D = z.shape
    tm = _tile(T, 512)

    def body(z_ref, g_ref, do_ref, dep_ref, dz_ref, dzb_ref, dg_ref, db_ref):
        @pl.when(pl.program_id(0) == 0)
        def _():
            dg_ref[...] = jnp.zeros_like(dg_ref)
            db_ref[...] = jnp.zeros_like(db_ref)

        zv = z_ref[...]
        dov = do_ref[...]
        mu = jnp.mean(zv, axis=1, keepdims=True)
        zc = zv - mu
        var = jnp.mean(zc * zc, axis=1, keepdims=True)
        rstd = lax.rsqrt(var + LN_EPS)
        xhat = zc * rstd
        dxhat = dov * g_ref[...]
        m1 = jnp.mean(dxhat, axis=1, keepdims=True)
        m2 = jnp.mean(dxhat * xhat, axis=1, keepdims=True)
        dz = rstd * (dxhat - m1 - xhat * m2)
        dz_ref[...] = dz
        dzb_ref[...] = dz.astype(BF16)
        dg_ref[...] += jnp.sum(dov * xhat, axis=0, keepdims=True)
        db_ref[...] += jnp.sum(dov, axis=0, keepdims=True)

    row = pl.BlockSpec((tm, D), lambda i: (i, 0))
    vec = pl.BlockSpec((1, D), lambda i: (0, 0))
    return pl.pallas_call(
        body, grid=(T // tm,), in_specs=[row, vec, row, ANY], out_specs=[row, row, vec, vec],
        out_shape=[jax.ShapeDtypeStruct((T, D), F32), jax.ShapeDtypeStruct((T, D), BF16),
                   jax.ShapeDtypeStruct((1, D), F32), jax.ShapeDtypeStruct((1, D), F32)],
        compiler_params=_params("arbitrary"), name=name)(z, g, do, dep)


def mm_nt(a, w, name):
    T, K = a.shape
    N = w.shape[0]
    tm, tn = _tile(T, 1024), _tile(N, 512)

    def body(a_ref, w_ref, o_ref):
        o_ref[...] = lax.dot_general(a_ref[...], w_ref[...], NT_DIMS, preferred_element_type=F32).astype(o_ref.dtype)

    return pl.pallas_call(
        body, grid=(T // tm, N // tn),
        in_specs=[pl.BlockSpec((tm, K), lambda i, j: (i, 0)), pl.BlockSpec((tn, K), lambda i, j: (j, 0))],
        out_specs=pl.BlockSpec((tm, tn), lambda i, j: (i, j)),
        out_shape=jax.ShapeDtypeStruct((T, N), BF16),
        compiler_params=_params("parallel", "parallel"), name=name)(a, w)


def ffn_bwd_mid(dzb, wd, h, u, scale, name):
    T, K = dzb.shape
    N = wd.shape[0]
    tm, tn = _tile(T, 512), _tile(N, 768)

    def body(dz_ref, w_ref, h_ref, u_ref, dh_ref, du_ref):
        dz = dz_ref[...]
        chunks = [slice(c, c + MXU_COLS) for c in range(0, tn, MXU_COLS)]
        das = [lax.dot_general(dz, w_ref[c, :], NT_DIMS, preferred_element_type=F32) for c in chunks]
        for c, da in zip(chunks, das):
            da = scale * da
            hv = h_ref[:, c].astype(F32)
            s = jax.nn.sigmoid(hv)
            silu = hv * s
            dh_ref[:, c] = (da * u_ref[:, c].astype(F32) * (s + silu * (1.0 - s))).astype(BF16)
            du_ref[:, c] = (da * silu).astype(BF16)

    tile = pl.BlockSpec((tm, tn), lambda j, i: (i, j))
    return pl.pallas_call(
        body, grid=(N // tn, T // tm),
        in_specs=[pl.BlockSpec((tm, K), lambda j, i: (i, 0)), pl.BlockSpec((tn, K), lambda j, i: (j, 0)), tile, tile],
        out_specs=[tile, tile],
        out_shape=[jax.ShapeDtypeStruct((T, N), BF16)] * 2,
        compiler_params=_params("parallel", "parallel"), name=name)(dzb, wd, h, u)


def mm_nt_res(pairs, dz, alpha, name):
    T, N = pairs[0][0].shape
    D = pairs[0][1].shape[0]
    n = len(pairs)
    tm, tk = _tile(T, 512 // n), _tile(N, 3072)
    nk = N // tk

    def body(*refs):
        a_refs, w_refs = refs[:n], refs[n:2 * n]
        dz_ref, o_ref, acc_ref = refs[2 * n:]
        k = pl.program_id(1)

        @pl.when(k == 0)
        def _():
            acc_ref[...] = jnp.zeros_like(acc_ref)

        part = lax.dot_general(a_refs[0][...], w_refs[0][...], NT_DIMS, preferred_element_type=F32)
        for p in range(1, n):
            part += lax.dot_general(a_refs[p][...], w_refs[p][...], NT_DIMS, preferred_element_type=F32)
        acc_ref[...] += part

        @pl.when(k == nk - 1)
        def _():
            o_ref[...] = acc_ref[...] + alpha * dz_ref[...]

    row = pl.BlockSpec((tm, D), lambda i, k: (i, 0))
    return pl.pallas_call(
        body, grid=(T // tm, nk),
        in_specs=[pl.BlockSpec((tm, tk), lambda i, k: (i, k))] * n + [pl.BlockSpec((D, tk), lambda i, k: (0, k))] * n + [row],
        out_specs=row,
        out_shape=jax.ShapeDtypeStruct((T, D), F32),
        scratch_shapes=[pltpu.VMEM((tm, D), F32)],
        compiler_params=_params("parallel", "arbitrary"), name=name)(*[p[0] for p in pairs], *[p[1] for p in pairs], dz)


def mm_tn(a, b, scale, shard_cols, name):
    T, M = a.shape
    N = b.shape[1]
    tm, tk = _tile(M, 512), _tile(T, 4096)
    nk = T // tk
    if shard_cols:
        ns = N // N_DEV
        per = 2 if (2 * ns) % 256 == 0 else 1
        tn = per * ns
        out_shape = jax.ShapeDtypeStruct((N_DEV, M, ns), BF16)
        out_spec = pl.BlockSpec((per, tm, ns), lambda i, j, k: (j, i, 0))
    else:
        tn = _tile(N, 1024)
        out_shape = jax.ShapeDtypeStruct((M, N), BF16)
        out_spec = pl.BlockSpec((tm, tn), lambda i, j, k: (i, j))

    def body(a_ref, b_ref, o_ref, acc_ref):
        k = pl.program_id(2)

        @pl.when(k == 0)
        def _():
            acc_ref[...] = jnp.zeros_like(acc_ref)

        acc_ref[...] += lax.dot_general(a_ref[...], b_ref[...], TN_DIMS, preferred_element_type=F32)

        @pl.when(k == nk - 1)
        def _():
            if shard_cols:
                for s in range(per):
                    o_ref[s] = (scale * acc_ref[:, s * ns:(s + 1) * ns]).astype(BF16)
            else:
                o_ref[...] = (scale * acc_ref[...]).astype(BF16)

    return pl.pallas_call(
        body, grid=(M // tm, N // tn, nk),
        in_specs=[pl.BlockSpec((tk, tm), lambda i, j, k: (k, i)), pl.BlockSpec((tk, tn), lambda i, j, k: (k, j))],
        out_specs=out_spec, out_shape=out_shape,
        scratch_shapes=[pltpu.VMEM((tm, tn), F32)],
        compiler_params=_params("parallel", "parallel", "arbitrary"), name=name)(a, b)


def loss_head(y, target, name):
    T, D = y.shape
    tm = _tile(T, 512)

    def body(y_ref, t_ref, l_ref, dy_ref):
        @pl.when(pl.program_id(0) == 0)
        def _():
            l_ref[...] = jnp.zeros_like(l_ref)

        e = y_ref[...] - t_ref[...]
        dy_ref[...] = e * (1.0 / D)
        rows = jnp.sum(e * e, axis=1, keepdims=True) * (0.5 / D)
        l_ref[...] += jnp.sum(rows, axis=0, keepdims=True)

    row = pl.BlockSpec((tm, D), lambda i: (i, 0))
    return pl.pallas_call(
        body, grid=(T // tm,), in_specs=[row, row],
        out_specs=[pl.BlockSpec((1, 1), lambda i: (0, 0)), row],
        out_shape=[jax.ShapeDtypeStruct((1, 1), F32), jax.ShapeDtypeStruct((T, D), F32)],
        compiler_params=_params("arbitrary"), name=name)(y, target)


def _rel_index(i, j):
    return jnp.clip(i - j + LEFT_CHUNKS * CHUNK, -REL_CLIP, REL_CLIP) + REL_CLIP


def bias_band(table_t, name):
    H = table_t.shape[0]
    rows = SUBLANE

    def body(t_ref, o_ref):
        i0 = pl.program_id(0) * rows
        parts = _split3(t_ref[...])
        r = lax.broadcasted_iota(jnp.int32, (REL_PAD, A_W), 0)
        j = lax.broadcasted_iota(jnp.int32, (REL_PAD, A_W), 1)
        for ii in range(rows):
            onehot = jnp.where(r == _rel_index(i0 + ii, j), 1.0, 0.0).astype(BF16)
            o_ref[ii] = sum(jnp.dot(p, onehot, preferred_element_type=F32) for p in parts)

    return pl.pallas_call(
        body, grid=(A_TQ // rows,),
        in_specs=[pl.BlockSpec((H, REL_PAD), lambda i: (0, 0))],
        out_specs=pl.BlockSpec((rows, H, A_W), lambda i: (i, 0, 0)),
        out_shape=jax.ShapeDtypeStruct((A_TQ, H, A_W), F32),
        compiler_params=_params("parallel"), name=name)(table_t)


def bias_band_bwd(dbands, name):
    H = dbands[0].shape[1]
    rows = SUBLANE
    n = len(dbands)

    def body(*refs):
        g_refs, o_ref = refs[:n], refs[n]

        @pl.when(pl.program_id(0) == 0)
        def _():
            o_ref[...] = jnp.zeros_like(o_ref)

        i0 = pl.program_id(0) * rows
        j = lax.broadcasted_iota(jnp.int32, (A_W, REL_PAD), 0)
        r = lax.broadcasted_iota(jnp.int32, (A_W, REL_PAD), 1)
        acc = jnp.zeros((H, REL_PAD), F32)
        for ii in range(rows):
            onehot = jnp.where(r == _rel_index(i0 + ii, j), 1.0, 0.0).astype(BF16)
            g = g_refs[0][ii]
            for q in range(1, n):
                g = g + g_refs[q][ii]
            acc += _dot3(g, onehot)
        o_ref[...] += acc

    spec = pl.BlockSpec((rows, H, A_W), lambda i: (i, 0, 0))
    return pl.pallas_call(
        body, grid=(A_TQ // rows,), in_specs=[spec] * n,
        out_specs=pl.BlockSpec((H, REL_PAD), lambda i: (0, 0)),
        out_shape=jax.ShapeDtypeStruct((H, REL_PAD), F32),
        compiler_params=_params("arbitrary"), name=name)(*dbands)


def _a_window(ref, qi):
    parts = []
    for d in range(A_NWB):
        kb = jnp.maximum(qi - (A_NWB - 1) + d, 0)
        parts.append(ref[pl.ds(pl.multiple_of(kb * A_TQ, A_TQ), A_TQ), :])
    return jnp.concatenate(parts, axis=0)


def _a_valid(qi):
    i = lax.broadcasted_iota(jnp.int32, (A_TQ, A_W), 0)
    j = lax.broadcasted_iota(jnp.int32, (A_TQ, A_W), 1)
    ic, jc = i // CHUNK, j // CHUNK
    return (jc >= ic) & (jc <= ic + LEFT_CHUNKS) & (j >= LEFT_CHUNKS * CHUNK - qi * A_TQ)


def _head_masks():
    lane = lax.broadcasted_iota(jnp.int32, (1, LANE), 1)
    return [lane < HEAD_DIM, lane >= HEAD_DIM]


def _a_probs(qms, kws, valids, b_ref):
    scores = [lax.dot_general(qm, kws[i // 2], NT_DIMS, preferred_element_type=F32) for i, qm in enumerate(qms)]
    probs = []
    for i, s in enumerate(scores):
        s = jnp.where(valids[i // 2], s * (HEAD_DIM ** -0.5) + b_ref[i % 2], NEG)
        p = jnp.exp(s - jnp.max(s, axis=1, keepdims=True))
        probs.append(p / jnp.sum(p, axis=1, keepdims=True))
    return probs


def _a_subtiles(ref, masks):
    out = []
    for u in range(A_SUB):
        x = ref[u * A_TQ:(u + 1) * A_TQ, :]
        out += [jnp.where(hm, x, jnp.zeros_like(x)) for hm in masks]
    return out


def _attn_specs(S, D, tq):
    hp_n = D // LANE
    nq = S // tq
    q = pl.BlockSpec((tq, LANE), lambda hp, b, qi: (b * nq + qi, hp))
    k = pl.BlockSpec((S, LANE), lambda hp, b, qi: (b, hp_n + hp))
    v = pl.BlockSpec((S, LANE), lambda hp, b, qi: (b, 2 * hp_n + hp))
    tile = pl.BlockSpec((tq, LANE), lambda hp, b, qi: (b * nq + qi, hp))
    seq = pl.BlockSpec((S, LANE), lambda hp, b, qi: (b, hp))
    return q, k, v, tile, seq


def attn_a_fwd(qkv, band, S, name):
    T, D3 = qkv.shape
    D = D3 // 3
    nb, ng = T // S, S // (A_TQ * A_SUB)

    def body(q_ref, k_ref, v_ref, b_ref, o_ref):
        qis = [pl.program_id(2) * A_SUB + u for u in range(A_SUB)]
        masks = _head_masks()
        kws, vws = [_a_window(k_ref, qi) for qi in qis], [_a_window(v_ref, qi) for qi in qis]
        probs = _a_probs(_a_subtiles(q_ref, masks), kws, [_a_valid(qi) for qi in qis], b_ref)
        outs = [jnp.dot(p.astype(BF16), vws[i // 2], preferred_element_type=F32) for i, p in enumerate(probs)]
        for u in range(A_SUB):
            o_ref[u * A_TQ:(u + 1) * A_TQ, :] = jnp.where(masks[0], outs[2 * u], outs[2 * u + 1]).astype(BF16)

    q, k, v, tile, _ = _attn_specs(S, D, A_TQ * A_SUB)
    return pl.pallas_call(
        body, grid=(D // LANE, nb, ng),
        in_specs=[q, k, v, pl.BlockSpec((2, A_TQ, A_W), lambda hp, b, qi: (hp, 0, 0))],
        out_specs=tile, out_shape=jax.ShapeDtypeStruct((T, D), BF16),
        compiler_params=_params("parallel", "parallel", "parallel"), name=name)(qkv, qkv, qkv, band)


def attn_a_bwd(qkv, band, do, S, name):
    T, D3 = qkv.shape
    D = D3 // 3
    nb, ng = T // S, S // (A_TQ * A_SUB)
    scale = HEAD_DIM ** -0.5

    def body(q_ref, k_ref, v_ref, b_ref, do_ref, dq_ref, dk_ref, dv_ref, db_ref, dk_acc, dv_acc):
        b, qg = pl.program_id(1), pl.program_id(2)

        @pl.when((b == 0) & (qg == 0))
        def _():
            db_ref[...] = jnp.zeros_like(db_ref)

        @pl.when(qg == 0)
        def _():
            dk_acc[...] = jnp.zeros_like(dk_acc)
            dv_acc[...] = jnp.zeros_like(dv_acc)

        qis = [qg * A_SUB + u for u in range(A_SUB)]
        masks = _head_masks()
        kws, vws = [_a_window(k_ref, qi) for qi in qis], [_a_window(v_ref, qi) for qi in qis]
        qms, doms = _a_subtiles(q_ref, masks), _a_subtiles(do_ref, masks)
        dps = [lax.dot_general(dom, vws[i // 2], NT_DIMS, preferred_element_type=F32) for i, dom in enumerate(doms)]
        probs = _a_probs(qms, kws, [_a_valid(qi) for qi in qis], b_ref)
        dss = [p * (dp - jnp.sum(p * dp, axis=1, keepdims=True)) for p, dp in zip(probs, dps)]
        for hh in range(2):
            db_ref[hh] += sum(dss[hh::2])
        dsbs = [ds.astype(BF16) for ds in dss]
        dqs = [jnp.dot(dsb, kws[i // 2], preferred_element_type=F32) for i, dsb in enumerate(dsbs)]
        dks = [lax.dot_general(dsb, qm, TN_DIMS, preferred_element_type=F32) for dsb, qm in zip(dsbs, qms)]
        dvs = [lax.dot_general(p.astype(BF16), dom, TN_DIMS, preferred_element_type=F32) for p, dom in zip(probs, doms)]
        for u, qi in enumerate(qis):
            dq_ref[u * A_TQ:(u + 1) * A_TQ, :] = (jnp.where(masks[0], dqs[2 * u], dqs[2 * u + 1]) * scale).astype(BF16)
            dkw = (dks[2 * u] + dks[2 * u + 1]) * scale
            dvw = dvs[2 * u] + dvs[2 * u + 1]
            for d in range(A_NWB):
                kb = jnp.maximum(qi - (A_NWB - 1) + d, 0)
                rows = pl.ds(pl.multiple_of(kb * A_TQ, A_TQ), A_TQ)
                dk_acc[rows, :] += dkw[d * A_TQ:(d + 1) * A_TQ]
                dv_acc[rows, :] += dvw[d * A_TQ:(d + 1) * A_TQ]

        @pl.when(qg == ng - 1)
        def _():
            dk_ref[...] = dk_acc[...].astype(BF16)
            dv_ref[...] = dv_acc[...].astype(BF16)

    q, k, v, tile, seq = _attn_specs(S, D, A_TQ * A_SUB)
    bspec = pl.BlockSpec((2, A_TQ, A_W), lambda hp, b, qi: (hp, 0, 0))
    act = jax.ShapeDtypeStruct((T, D), BF16)
    return pl.pallas_call(
        body, grid=(D // LANE, nb, ng),
        in_specs=[q, k, v, bspec, tile], out_specs=[tile, seq, seq, bspec],
        out_shape=[act, act, act, jax.ShapeDtypeStruct(band.shape, F32)],
        scratch_shapes=[pltpu.VMEM((S, LANE), F32), pltpu.VMEM((S, LANE), F32)],
        compiler_params=_params("arbitrary", "arbitrary", "arbitrary"), name=name)(qkv, qkv, qkv, band, do)


def _dot2(v, w):
    h = v.astype(BF16)
    lo = (v - h.astype(F32)).astype(BF16)
    return jnp.dot(h, w, preferred_element_type=F32) + jnp.dot(lo, w, preferred_element_type=F32)


def _b_weights(qms, kts, rights, after, diagonal):
    inst = [(t, hh) for t in range(len(kts)) for hh in range(2)]
    zs = [lax.dot_general(qms[hh], kts[t], NT_DIMS, preferred_element_type=F32) for t, hh in inst]
    lbs, l1ms, his, los = [], [], [], []
    causal = _strictly_causal() if any(diagonal) else None
    for (t, hh), z in zip(inst, zs):
        nz = -z
        soft = jnp.log(1.0 + jnp.exp(jnp.minimum(z, nz)))
        lbs.append(jnp.minimum(z, 0.0) - soft)
        l1m = jnp.minimum(nz, 0.0) - soft
        if diagonal[t]:
            l1m = jnp.where(causal, l1m, 0.0)
        hi = l1m.astype(BF16)
        l1ms.append(l1m)
        his.append(hi)
        los.append((l1m - hi.astype(F32)).astype(BF16))
    sums = [jnp.dot(hi, after, preferred_element_type=F32) + jnp.dot(lo, after, preferred_element_type=F32)
            for hi, lo in zip(his, los)]
    rights = list(rights)
    weights = []
    for (t, hh), lb, l1m, c in zip(inst, lbs, l1ms, sums):
        a = jnp.exp(lb + (rights[hh] + c))
        weights.append(jnp.where(causal, a, 0.0) if diagonal[t] else a)
        rights[hh] = rights[hh] + jnp.sum(l1m, axis=1, keepdims=True)
    return zs, weights, rights


def _strictly_causal():
    row = lax.broadcasted_iota(jnp.int32, (B_T, B_T), 0)
    col = lax.broadcasted_iota(jnp.int32, (B_T, B_T), 1)
    return col < row


def _tri(strict_lower):
    r = lax.broadcasted_iota(jnp.int32, (B_T, B_T), 0)
    c = lax.broadcasted_iota(jnp.int32, (B_T, B_T), 1)
    return jnp.where((r > c) if strict_lower else (r < c), 1.0, 0.0).astype(BF16)


def _scaled_heads(q2):
    qs = q2 * (HEAD_DIM ** -0.5)
    return [jnp.where(hm, qs, jnp.zeros_like(qs)) for hm in _head_masks()]


def attn_b_fwd(qkv, S, name):
    T, D3 = qkv.shape
    D = D3 // 3
    nb, nq = T // S, S // B_T

    def body(q_ref, k_ref, v_ref, o_ref):
        qi = pl.program_id(2)
        after = _tri(True)
        qms = _scaled_heads(q_ref[...])

        def tiles(kbs, carry, diagonal):
            rows = [pl.ds(pl.multiple_of(kb * B_T, B_T), B_T) for kb in kbs]
            kts, vts = [k_ref[r, :] for r in rows], [v_ref[r, :] for r in rows]
            _, weights, rights = _b_weights(qms, kts, (carry[0], carry[2]), after, diagonal)
            accs = [carry[1], carry[3]]
            for i, a in enumerate(weights):
                accs[i % 2] = accs[i % 2] + jnp.dot(a.astype(BF16), vts[i // 2], preferred_element_type=F32)
            return rights[0], accs[0], rights[1], accs[1]

        init = (jnp.zeros((B_T, 1), F32), jnp.zeros((B_T, LANE), F32)) * 2
        res = lax.cond(qi % 2 == 1, lambda c: tiles([qi, qi - 1], c, [True, False]), lambda c: tiles([qi], c, [True]), init)
        top = qi - 1 - qi % 2
        res = lax.fori_loop(0, qi // 2, lambda p, c: tiles([top - 2 * p, top - 1 - 2 * p], c, [False, False]), res)
        o_ref[...] = jnp.where(_head_masks()[0], res[1], res[3]).astype(BF16)

    q, k, v, tile, _ = _attn_specs(S, D, B_T)
    return pl.pallas_call(
        body, grid=(D // LANE, nb, nq), in_specs=[q, k, v], out_specs=tile,
        out_shape=jax.ShapeDtypeStruct((T, D), BF16),
        compiler_params=_params("parallel", "parallel", "parallel"), name=name)(qkv, qkv, qkv)


def attn_b_bwd(qkv, do, S, name):
    T, D3 = qkv.shape
    D = D3 // 3
    nb, nq = T // S, S // B_T
    scale = HEAD_DIM ** -0.5

    def body(q_ref, k_ref, v_ref, do_ref, dq_ref, dk_ref, dv_ref, dk_acc, dv_acc, z_s, g_s):
        qi = pl.program_id(2)

        @pl.when(qi == 0)
        def _():
            dk_acc[...] = jnp.zeros_like(dk_acc)
            dv_acc[...] = jnp.zeros_like(dv_acc)

        do2 = do_ref[...]
        after, before = _tri(True), _tri(False)
        masks = _head_masks()
        qms = _scaled_heads(q_ref[...])
        doms = [jnp.where(hm, do2, jnp.zeros_like(do2)) for hm in masks]

        def sweep_left(kbs, rights, diagonal):
            rows = [pl.ds(pl.multiple_of(kb * B_T, B_T), B_T) for kb in kbs]
            kts, vts = [k_ref[r, :] for r in rows], [v_ref[r, :] for r in rows]
            das = [lax.dot_general(doms[hh], vt, NT_DIMS, preferred_element_type=F32) for vt in vts for hh in range(2)]
            zs, weights, rights = _b_weights(qms, kts, rights, after, diagonal)
            for i, (z, a, da) in enumerate(zip(zs, weights, das)):
                z_s[i % 2, kbs[i // 2]] = z
                g_s[i % 2, kbs[i // 2]] = da * a
            dvs = [lax.dot_general(a.astype(BF16), doms[i % 2], TN_DIMS, preferred_element_type=F32)
                   for i, a in enumerate(weights)]
            for t, r in enumerate(rows):
                dv_acc[r, :] += dvs[2 * t] + dvs[2 * t + 1]
            return tuple(rights)

        zero_col = jnp.zeros((B_T, 1), F32)
        odd = qi % 2 == 1
        top = qi - 1 - qi % 2
        rights = lax.cond(odd, lambda c: sweep_left([qi, qi - 1], c, [True, False]), lambda c: sweep_left([qi], c, [True]),
                          (zero_col, zero_col))
        lax.fori_loop(0, qi // 2, lambda p, c: sweep_left([top - 2 * p, top - 1 - 2 * p], c, [False, False]), rights)

        def sweep_right(kbs, carry, diagonal):
            rows = [pl.ds(pl.multiple_of(kb * B_T, B_T), B_T) for kb in kbs]
            inst = [(t, hh) for t in range(len(kbs)) for hh in range(2)]
            gs = [g_s[hh, kbs[t]] for t, hh in inst]
            sums = [_dot2(g, before) for g in gs]
            lefts, dqs = [carry[0], carry[2]], [carry[1], carry[3]]
            dzbs = []
            for (t, hh), g, c in zip(inst, gs, sums):
                beta = jax.nn.sigmoid(z_s[hh, kbs[t]])
                dz = g * (1.0 - beta) - beta * (lefts[hh] + c)
                if diagonal[t]:
                    dz = jnp.where(_strictly_causal(), dz, 0.0)
                dzbs.append(dz.astype(BF16))
                lefts[hh] = lefts[hh] + jnp.sum(g, axis=1, keepdims=True)
            dks = [lax.dot_general(dzb, qms[hh], TN_DIMS, preferred_element_type=F32) for (t, hh), dzb in zip(inst, dzbs)]
            for (t, hh), dzb in zip(inst, dzbs):
                dqs[hh] = dqs[hh] + jnp.dot(dzb, k_ref[rows[t], :], preferred_element_type=F32)
            for t, r in enumerate(rows):
                dk_acc[r, :] += dks[2 * t] + dks[2 * t + 1]
            return lefts[0], dqs[0], lefts[1], dqs[1]

        init = (zero_col, jnp.zeros((B_T, LANE), F32)) * 2
        res = lax.fori_loop(0, qi // 2, lambda p, c: sweep_right([2 * p, 2 * p + 1], c, [False, False]), init)
        res = lax.cond(odd, lambda c: sweep_right([qi - 1, qi], c, [False, True]), lambda c: sweep_right([qi], c, [True]), res)
        dq_ref[...] = (jnp.where(masks[0], res[1], res[3]) * scale).astype(BF16)

        @pl.when(qi == nq - 1)
        def _():
            dk_ref[...] = dk_acc[...].astype(BF16)
            dv_ref[...] = dv_acc[...].astype(BF16)

    q, k, v, tile, seq = _attn_specs(S, D, B_T)
    act = jax.ShapeDtypeStruct((T, D), BF16)
    return pl.pallas_call(
        body, grid=(D // LANE, nb, nq),
        in_specs=[q, k, v, tile], out_specs=[tile, seq, seq], out_shape=[act, act, act],
        scratch_shapes=[pltpu.VMEM((S, LANE), F32), pltpu.VMEM((S, LANE), F32),
                        pltpu.VMEM((2, nq, B_T, B_T), F32), pltpu.VMEM((2, nq, B_T, B_T), F32)],
        compiler_params=_params("arbitrary", "arbitrary", "arbitrary"), name=name)(qkv, qkv, qkv, do)


HBM = pl.BlockSpec(memory_space=pltpu.HBM)
SEM = pl.BlockSpec(memory_space=pltpu.SEMAPHORE)
N_PEERS = N_DEV - 1
GATHERS_AHEAD = 3


def _place():
    return lax.axis_index("x"), lax.axis_index("y"), lax.axis_index("c")


def _peers(x, y, c):
    return [(1 - x if r & 4 else x, 1 - y if r & 2 else y, 1 - c if r & 1 else c) for r in range(1, N_DEV)]


def _block(ref, shape, by_cols, j):
    r, w = shape
    if by_cols:
        return ref.at[:, pl.ds(pl.multiple_of(j * w, LANE), w)]
    return ref.at[pl.ds(pl.multiple_of(j * r, SUBLANE), r), :]


def _exchange_copies(gather, src_refs, land_refs, send_sems, recv_sems, by_cols):
    x, y, c = _place()
    me = 4 * x + 2 * y + c
    out = []
    for t, (src, land) in enumerate(zip(src_refs, land_refs)):
        for r, peer in enumerate(_peers(x, y, c)):
            pj = 4 * peer[0] + 2 * peer[1] + peer[2]
            if gather:
                mine, to_me, theirs = src, _block(land, src.shape, by_cols[t], me), _block(land, src.shape, by_cols[t], pj)
            else:
                mine, to_me, theirs = src.at[pj], land.at[me], land.at[pj]
            sems = dict(send_sem=send_sems.at[N_PEERS * t + r], recv_sem=recv_sems.at[N_PEERS * t + r],
                        device_id=peer, device_id_type=MESH_ID)
            out.append((pltpu.make_async_remote_copy(src_ref=mine, dst_ref=to_me, **sems),
                        pltpu.make_async_remote_copy(src_ref=mine, dst_ref=theirs, **sems)))
    return out


def _own_copies(gather, src_refs, land_refs, own_sems, by_cols):
    x, y, c = _place()
    me = 4 * x + 2 * y + c
    out = []
    for t, (src, land) in enumerate(zip(src_refs, land_refs)):
        if gather:
            out.append(pltpu.make_async_copy(src, _block(land, src.shape, by_cols[t], me), own_sems.at[t]))
        else:
            out.append(pltpu.make_async_copy(src.at[me], land.at[me], own_sems.at[t]))
    return out


def exchange_start(gather, srcs, land_shapes, by_cols, after, name):
    n = len(srcs)

    def body(*refs):
        src_refs, land_refs = refs[:n], refs[n:2 * n]
        send_sems, recv_sems, own_sems = refs[2 * n + 1:2 * n + 4]
        token = refs[-1]
        for cp in _own_copies(gather, src_refs, land_refs, own_sems, by_cols):
            cp.start()
        for mine, _ in _exchange_copies(gather, src_refs, land_refs, send_sems, recv_sems, by_cols):
            mine.start()
        token[...] = jnp.zeros_like(token)

    lands = [pltpu.with_memory_space_constraint(lax.empty(s.shape, s.dtype), pltpu.HBM) for s in land_shapes]
    srcs = [pltpu.with_memory_space_constraint(s, pltpu.HBM) for s in srcs]
    res = pl.pallas_call(
        body, name=name,
        out_shape=(pltpu.SemaphoreType.DMA((N_PEERS * n,)), pltpu.SemaphoreType.DMA((N_PEERS * n,)),
                   pltpu.SemaphoreType.DMA((n,)),
                   *[pltpu.HBM(s.shape, s.dtype) for s in srcs], *[pltpu.HBM(s.shape, s.dtype) for s in land_shapes],
                   jax.ShapeDtypeStruct((SUBLANE, LANE), F32)),
        in_specs=[HBM] * (2 * n) + [ANY],
        out_specs=(SEM, SEM, SEM, *[HBM] * (2 * n), pl.BlockSpec(memory_space=pltpu.VMEM)),
        input_output_aliases={i: 3 + i for i in range(2 * n)},
        compiler_params=pltpu.CompilerParams(has_side_effects=pltpu.SideEffectType.DATAFLOW_SIDE_EFFECTING),
    )(*srcs, *lands, after)
    return dict(gather=gather, n=n, by_cols=by_cols, sems=res[:3], srcs=res[3:3 + n],
                lands=res[3 + n:3 + 2 * n], token=res[-1])


def exchange_wait(started, after, name):
    n, gather, by_cols = started["n"], started["gather"], started["by_cols"]

    def body(*refs):
        src_refs, land_refs = refs[:n], refs[n:2 * n]
        send_sems, recv_sems, own_sems = refs[2 * n:2 * n + 3]
        for mine, theirs in _exchange_copies(gather, src_refs, land_refs, send_sems, recv_sems, by_cols):
            mine.wait_send()
            theirs.wait_recv()
        for cp in _own_copies(gather, src_refs, land_refs, own_sems, by_cols):
            cp.wait()

    res = pl.pallas_call(
        body, name=name,
        out_shape=tuple(pltpu.HBM(s.shape, s.dtype) for s in (*started["srcs"], *started["lands"])),
        in_specs=[HBM] * (2 * n) + [SEM, SEM, SEM, ANY], out_specs=tuple([HBM] * (2 * n)),
        input_output_aliases={i: i for i in range(2 * n)},
        compiler_params=pltpu.CompilerParams(has_side_effects=pltpu.SideEffectType.DATAFLOW_SIDE_EFFECTING),
    )(*started["srcs"], *started["lands"], *started["sems"], after)
    return res[n:]


def gather_small(v, dep, name):
    R, C = v.shape

    def body(v_ref, dep_ref, o_ref, send_sems, recv_sems):
        x, y, c = _place()
        o_ref[4 * x + 2 * y + c] = v_ref[...]
        peers = _peers(x, y, c)

        def copy(r, owner, to):
            slot = o_ref.at[4 * owner[0] + 2 * owner[1] + owner[2]]
            return pltpu.make_async_remote_copy(
                src_ref=slot, dst_ref=slot, send_sem=send_sems.at[r], recv_sem=recv_sems.at[r],
                device_id=to, device_id_type=MESH_ID)

        sends = [copy(r, (x, y, c), peer) for r, peer in enumerate(peers)]
        for cp in sends:
            cp.start()
        for r, peer in enumerate(peers):
            copy(r, peer, (x, y, c)).wait_recv()
        for cp in sends:
            cp.wait_send()

    vm = pl.BlockSpec(memory_space=pltpu.VMEM)
    return pl.pallas_call(
        body, in_specs=[vm, ANY], out_specs=vm, out_shape=jax.ShapeDtypeStruct((N_DEV, R, C), F32),
        scratch_shapes=[pltpu.SemaphoreType.DMA((N_PEERS,)), pltpu.SemaphoreType.DMA((N_PEERS,))],
        name=name)(v, dep)


def sum_devices(g, name):
    _, R, C = g.shape

    def body(g_ref, o_ref):
        acc = g_ref[0]
        for j in range(1, N_DEV):
            acc = acc + g_ref[j]
        o_ref[...] = acc

    vm = pl.BlockSpec(memory_space=pltpu.VMEM)
    return pl.pallas_call(body, in_specs=[vm], out_specs=vm, out_shape=jax.ShapeDtypeStruct((R, C), F32), name=name)(g)


def adamw(w, m, v, index, parts, outs, name):
    L, R, C = w.shape
    if outs is None:
        outs = [lax.empty((L, R, C), F32) for _ in range(4)]
    P, Rp, Cp = parts.shape
    tr = _tile(R, 512, SUBLANE) if Rp == R else R

    def body(w_ref, m_ref, v_ref, p_ref, g_in, d_in, m_in, v_in, g_out, d_out, m_out, v_out):
        g = p_ref[0, :tr, :C].astype(F32)
        for q in range(1, P):
            g = g + p_ref[q, :tr, :C].astype(F32)
        mn = ADAM_B1 * m_ref[...] + (1.0 - ADAM_B1) * g
        vn = ADAM_B2 * v_ref[...] + (1.0 - ADAM_B2) * (g * g)
        m_hat = mn / (1.0 - ADAM_B1 ** ADAM_STEP)
        v_hat = vn / (1.0 - ADAM_B2 ** ADAM_STEP)
        g_out[...] = g
        d_out[...] = -ADAM_LR * (m_hat / (jnp.sqrt(v_hat) + ADAM_EPS) + ADAM_WD * w_ref[...])
        m_out[...] = mn
        v_out[...] = vn

    slab = pl.BlockSpec((None, tr, C), lambda r: (index, r, 0))
    pspec = pl.BlockSpec((P, tr if Rp == R else Rp, Cp), lambda r: (0, r, 0))
    return pl.pallas_call(
        body, grid=(R // tr,), in_specs=[slab] * 3 + [pspec] + [ANY] * 4, out_specs=[slab] * 4,
        out_shape=[jax.ShapeDtypeStruct((L, R, C), F32)] * 4,
        input_output_aliases={4 + q: q for q in range(4)},
        compiler_params=_params("parallel"), name=name)(w, m, v, parts, *outs)


def _pad_to(a, axis, size):
    pad = [(0, 0)] * a.ndim
    pad[axis] = (0, size - a.shape[axis])
    return jnp.pad(a, pad)


def kernel(x, w_qkv_a, w_o_a, rel_bias, w_qkv_b, w_o_b, ffn_w_gate, ffn_w_up, ffn_w_down, ln_g, ln_b, loss_target, m_w_qkv_a, m_w_o_a, m_rel_bias, m_w_qkv_b, m_w_o_b, m_ffn_w_gate, m_ffn_w_up, m_ffn_w_down, m_ln_g, m_ln_b, v_w_qkv_a, v_w_o_a, v_rel_bias, v_w_qkv_b, v_w_o_b, v_ffn_w_gate, v_ffn_w_up, v_ffn_w_down, v_ln_g, v_ln_b):
    nb, S, D = x.shape
    T = nb * S
    depth = ffn_w_gate.shape[0]
    H = D // HEAD_DIM
    fs = ffn_w_gate.shape[-1]
    fp = -(-fs // LANE) * LANE
    alpha = (2.0 * depth) ** 0.25
    cx, cy, cc = _place()
    me = 4 * cx + 2 * cy + cc

    ln_local = jnp.concatenate([ln_g.reshape(depth * 3, -1), ln_b.reshape(depth * 3, -1)], axis=0)
    ln_all = gather_small(ln_local, ln_local, "gather_ln")
    ln_full = jnp.transpose(ln_all, (1, 0, 2)).reshape(2 * depth * 3, D)
    ln_gain = lambda i, s: ln_full[3 * i + s][None, :]
    ln_bias = lambda i, s: ln_full[3 * depth + 3 * i + s][None, :]

    table_t = _pad_to(rel_bias.T, 1, REL_PAD)
    band = jnp.transpose(bias_band(table_t, "bias_band"), (1, 0, 2))

    subs = []
    for i in range(depth):
        for s in (0, 1, 2):
            if s == 1:
                wq, wo = (w_qkv_a, w_o_a) if i % 2 == 0 else (w_qkv_b, w_o_b)
                subs.append(([wq[i // 2].astype(BF16), wo[i // 2].astype(BF16)], [True, False]))
            else:
                f = 0 if s == 0 else 1
                subs.append(([_pad_to(ffn_w_gate[i, f].astype(BF16), 1, fp), _pad_to(ffn_w_up[i, f].astype(BF16), 1, fp),
                              _pad_to(ffn_w_down[i, f].astype(BF16), 0, fp)], [True, True, False]))

    def start_gather(k, after):
        shards, by_cols = subs[k]
        shapes = [jax.ShapeDtypeStruct((s.shape[0], N_DEV * s.shape[1]) if col else (N_DEV * s.shape[0], s.shape[1]), BF16)
                  for s, col in zip(shards, by_cols)]
        return exchange_start(True, shards, shapes, by_cols, after, f"gather_start_{k}")

    xf = x.reshape(T, D)
    act, act_b = xf, xf.astype(BF16)
    gathers = {}
    for k in range(min(GATHERS_AHEAD, len(subs))):
        gathers[k] = start_gather(k, xf if k == 0 else gathers[k - 1]["token"])
    newest = gathers[k]["token"]
    saved = []
    for i in range(depth):
        layer = {}
        for s in (0, 1, 2):
            k = 3 * i + s
            tag = f"L{i}S{s}"
            if 0 < k and k + GATHERS_AHEAD - 1 < len(subs):
                gathers[k + GATHERS_AHEAD - 1] = start_gather(k + GATHERS_AHEAD - 1, act_b)
                newest = gathers[k + GATHERS_AHEAD - 1]["token"]
            full = exchange_wait(gathers.pop(k), newest, f"gather_wait_{k}")
            if s == 1:
                wqkv_f, wo_f = full
                qkv = mm_nn(act_b, wqkv_f, "qkv_" + tag)
                if i % 2 == 0:
                    att = attn_a_fwd(qkv, band, S, "attn_a_fwd_" + tag)
                else:
                    att = attn_b_fwd(qkv, S, "attn_b_fwd_" + tag)
                z, o, ob = mm_res_ln(att, wo_f, act, ln_gain(i, s), ln_bias(i, s), alpha, 1.0, "out_ln_" + tag)
                layer[s] = dict(x_b=act_b, qkv=qkv, att=att, z=z, wqkv=wqkv_f, wo=wo_f)
            else:
                wg_f, wu_f, wd_f = full
                h, u, a = ffn_up(act_b, wg_f, wu_f, "ffn_up_" + tag)
                z, o, ob = mm_res_ln(a, wd_f, act, ln_gain(i, s), ln_bias(i, s), alpha, 0.5, "down_ln_" + tag)
                layer[s] = dict(x_b=act_b, h=h, u=u, a=a, z=z, wg=wg_f, wu=wu_f, wd=wd_f)
            act, act_b = o, ob
        saved.append(layer)

    loss_local, d_act = loss_head(act, loss_target.reshape(T, D), "loss_head")
    loss = lax.psum(loss_local[0, 0], ("x", "y", "c"))

    results = {}

    def update(name, w, m, v, index, parts):
        L = w.shape[0] if w.ndim == 3 else w.shape[0] * w.shape[1]
        flat = lambda t: t.reshape((L,) + t.shape[-2:])
        results[name] = adamw(flat(w), flat(m), flat(v), index, parts, results.get(name), f"adamw_{name}_{index}")

    def finish(entry, after):
        exchange, targets, tag = entry
        lands = exchange_wait(exchange, after, "scatter_wait_" + tag)
        for (name, w, m, v, index), parts in zip(targets, lands):
            update(name, w, m, v, index, parts)
        return results[targets[-1][0]][0]

    pending = []
    started = d_act
    dbands = []
    dln_g = [None] * (3 * depth)
    dln_b = [None] * (3 * depth)
    for i in reversed(range(depth)):
        for s in (2, 1, 0):
            tag = f"L{i}S{s}"
            sv = saved[i][s]
            dz, dzb, dg, db = ln_bwd(sv["z"], ln_gain(i, s), d_act, started, "ln_bwd_" + tag)
            dln_g[3 * i + s], dln_b[3 * i + s] = dg, db
            if s == 1:
                j = i // 2
                d_att = mm_nt(dzb, sv["wo"], "att_bwd_" + tag)
                g_wo = mm_tn(sv["att"], dzb, 1.0, False, "dwo_" + tag)
                if i % 2 == 0:
                    dq, dk, dv, dband = attn_a_bwd(sv["qkv"], band, d_att, S, "attn_a_bwd_" + tag)
                    dbands.append(jnp.transpose(dband, (1, 0, 2)))
                else:
                    dq, dk, dv = attn_b_bwd(sv["qkv"], d_att, S, "attn_b_bwd_" + tag)
                dqkv = jnp.concatenate([dq, dk, dv], axis=1)
                g_wqkv = mm_tn(sv["x_b"], dqkv, 1.0, True, "dwqkv_" + tag)
                d_act = mm_nt_res([(dqkv, sv["wqkv"])], dz, alpha, "dx_mix_" + tag)
                grads = [g_wqkv, g_wo.reshape(N_DEV, D // N_DEV, D)]
                if i % 2 == 0:
                    targets = [("w_qkv_a", w_qkv_a, m_w_qkv_a, v_w_qkv_a, j), ("w_o_a", w_o_a, m_w_o_a, v_w_o_a, j)]
                else:
                    targets = [("w_qkv_b", w_qkv_b, m_w_qkv_b, v_w_qkv_b, j), ("w_o_b", w_o_b, m_w_o_b, v_w_o_b, j)]
            else:
                f = 0 if s == 0 else 1
                dh, du = ffn_bwd_mid(dzb, sv["wd"], sv["h"], sv["u"], 0.5, "ffn_mid_" + tag)
                g_wd = mm_tn(sv["a"], dzb, 0.5, False, "dwd_" + tag)
                g_wg = mm_tn(sv["x_b"], dh, 1.0, True, "dwg_" + tag)
                g_wu = mm_tn(sv["x_b"], du, 1.0, True, "dwu_" + tag)
                d_act = mm_nt_res([(dh, sv["wg"]), (du, sv["wu"])], dz, alpha, "dx_ffn_" + tag)
                grads = [g_wg, g_wu, g_wd.reshape(N_DEV, fp, D)]
                idx = 2 * i + f
                targets = [("ffn_w_gate", ffn_w_gate, m_ffn_w_gate, v_ffn_w_gate, idx),
                           ("ffn_w_up", ffn_w_up, m_ffn_w_up, v_ffn_w_up, idx),
                           ("ffn_w_down", ffn_w_down, m_ffn_w_down, v_ffn_w_down, idx)]
            after = d_act
            if i == 0 and s == 0:
                dtable_t = bias_band_bwd(dbands, "bias_band_bwd")
                small = jnp.concatenate(dln_g + dln_b + [_pad_to(dtable_t, 1, D)], axis=0)
                small = _pad_to(small, 0, -(-small.shape[0] // SUBLANE) * SUBLANE)
                total = after = sum_devices(gather_small(small, d_act, "gather_small_grads"), "sum_small_grads")
            shapes = [jax.ShapeDtypeStruct(g.shape, g.dtype) for g in grads]
            pending.append((exchange_start(False, grads, shapes, None, after, "scatter_start_" + tag), targets, tag))
            started = pending[-1][0]["token"]
    grad_x = d_act.reshape(nb, S, D)

    last = pending.pop()
    done = started
    for entry in pending:
        done = finish(entry, done)
    finish(last, done)
    n_ln = 3 * depth
    g_ln_g = lax.dynamic_slice_in_dim(total[:n_ln], me * (D // N_DEV), D // N_DEV, axis=1)
    g_ln_b = lax.dynamic_slice_in_dim(total[n_ln:2 * n_ln], me * (D // N_DEV), D // N_DEV, axis=1)
    g_rel = total[2 * n_ln:2 * n_ln + H, :N_REL].T
    as3 = lambda t: t.reshape((1, -1, t.shape[-1]))
    results["ln_g"] = adamw(as3(ln_g), as3(m_ln_g), as3(v_ln_g), 0, g_ln_g[None], None, "adamw_ln_g")
    results["ln_b"] = adamw(as3(ln_b), as3(m_ln_b), as3(v_ln_b), 0, g_ln_b[None], None, "adamw_ln_b")
    results["rel_bias"] = adamw(as3(rel_bias), as3(m_rel_bias), as3(v_rel_bias), 0, g_rel[None], None, "adamw_rel_bias")

    order = [("w_qkv_a", w_qkv_a), ("w_o_a", w_o_a), ("rel_bias", rel_bias), ("w_qkv_b", w_qkv_b), ("w_o_b", w_o_b),
             ("ffn_w_gate", ffn_w_gate), ("ffn_w_up", ffn_w_up), ("ffn_w_down", ffn_w_down), ("ln_g", ln_g), ("ln_b", ln_b)]
    outs = [loss, grad_x]
    for q in range(4):
        for name, like in order:
            outs.append(results[name][q].reshape(like.shape))
    return tuple(outs)
```

```python
import functools

import jax
import jax.numpy as jnp
from jax import lax
from jax.experimental import pallas as pl
from jax.experimental.pallas import tpu as pltpu

BF16 = jnp.bfloat16
F32 = jnp.float32
MESH_ID = pl.DeviceIdType.MESH
ANY = pl.BlockSpec(memory_space=pl.ANY)

N_DEV = 8
LANE = 128
MXU_COLS = 256
SUBLANE = 8
VMEM_LIMIT = 56 * 1024 * 1024

HEAD_DIM = 64
CHUNK = 64
LEFT_CHUNKS = 8
REL_CLIP = 128
N_REL = 2 * REL_CLIP + 1
REL_PAD = 384
LN_EPS = 1e-5
ADAM_LR, ADAM_B1, ADAM_B2, ADAM_EPS, ADAM_WD, ADAM_STEP = 0.001, 0.9, 0.999, 1e-08, 0.01, 10
NEG = -1e30

A_TQ = 128
A_NWB = LEFT_CHUNKS * CHUNK // A_TQ + 1
A_W = A_NWB * A_TQ
A_SUB = 4
B_T = 256

NT_DIMS = (((1,), (1,)), ((), ()))
TN_DIMS = (((0,), (0,)), ((), ()))


def _tile(n, pref, unit=LANE):
    if n <= pref:
        return n
    t = pref - pref % unit
    while t > unit and n % t:
        t -= unit
    assert n % t == 0, (n, pref)
    return t


def _params(*sem):
    return pltpu.CompilerParams(dimension_semantics=sem, vmem_limit_bytes=VMEM_LIMIT)


def _split3(v):
    h = v.astype(BF16)
    r = v - h.astype(F32)
    m = r.astype(BF16)
    lo = (r - m.astype(F32)).astype(BF16)
    return h, m, lo


def _dot3(v, w):
    h, m, lo = _split3(v)
    return (jnp.dot(h, w, preferred_element_type=F32) + jnp.dot(m, w, preferred_element_type=F32)
            + jnp.dot(lo, w, preferred_element_type=F32))


def mm_nn(a, w, name):
    T, K = a.shape
    N = w.shape[1]
    tm, tn = _tile(T, 1024), _tile(N, 768)

    def body(a_ref, w_ref, o_ref):
        o_ref[...] = jnp.dot(a_ref[...], w_ref[...], preferred_element_type=F32).astype(o_ref.dtype)

    return pl.pallas_call(
        body, grid=(T // tm, N // tn),
        in_specs=[pl.BlockSpec((tm, K), lambda i, j: (i, 0)), pl.BlockSpec((K, tn), lambda i, j: (0, j))],
        out_specs=pl.BlockSpec((tm, tn), lambda i, j: (i, j)),
        out_shape=jax.ShapeDtypeStruct((T, N), BF16),
        compiler_params=_params("parallel", "parallel"), name=name)(a, w)


def ffn_up(xb, wg, wu, name):
    T, K = xb.shape
    N = wg.shape[1]
    tm, tn = _tile(T, 512), _tile(N, 768)

    def body(x_ref, wg_ref, wu_ref, h_ref, u_ref, a_ref):
        x = x_ref[...]
        chunks = [slice(c, c + MXU_COLS) for c in range(0, tn, MXU_COLS)]
        hs = [jnp.dot(x, wg_ref[:, c], preferred_element_type=F32) for c in chunks]
        us = [jnp.dot(x, wu_ref[:, c], preferred_element_type=F32) for c in chunks]
        for c, h, u in zip(chunks, hs, us):
            h_ref[:, c] = h.astype(BF16)
            u_ref[:, c] = u.astype(BF16)
            a_ref[:, c] = (h * jax.nn.sigmoid(h) * u).astype(BF16)

    wspec = pl.BlockSpec((K, tn), lambda j, i: (0, j))
    ospec = pl.BlockSpec((tm, tn), lambda j, i: (i, j))
    return pl.pallas_call(
        body, grid=(N // tn, T // tm),
        in_specs=[pl.BlockSpec((tm, K), lambda j, i: (i, 0)), wspec, wspec],
        out_specs=[ospec, ospec, ospec],
        out_shape=[jax.ShapeDtypeStruct((T, N), BF16)] * 3,
        compiler_params=_params("parallel", "parallel"), name=name)(xb, wg, wu)


def mm_res_ln(a, w, x, g, b, alpha, scale, name):
    T, K = a.shape
    D = w.shape[1]
    tm = _tile(T, 512)
    parts = [slice(r, r + MXU_COLS) for r in range(0, tm, MXU_COLS)] if tm % MXU_COLS == 0 else [slice(0, tm)]

    def body(a_ref, w_ref, x_ref, g_ref, b_ref, z_ref, o_ref, ob_ref):
        ys = [jnp.dot(a_ref[r, :], w_ref[...], preferred_element_type=F32) for r in parts]
        for r, y in zip(parts, ys):
            z = alpha * x_ref[r, :] + scale * y
            mu = jnp.mean(z, axis=1, keepdims=True)
            zc = z - mu
            var = jnp.mean(zc * zc, axis=1, keepdims=True)
            o = zc * lax.rsqrt(var + LN_EPS) * g_ref[...] + b_ref[...]
            z_ref[r, :] = z
            o_ref[r, :] = o
            ob_ref[r, :] = o.astype(BF16)

    row = pl.BlockSpec((tm, D), lambda i: (i, 0))
    vec = pl.BlockSpec((1, D), lambda i: (0, 0))
    return pl.pallas_call(
        body, grid=(T // tm,),
        in_specs=[pl.BlockSpec((tm, K), lambda i: (i, 0)), pl.BlockSpec((K, D), lambda i: (0, 0)), row, vec, vec],
        out_specs=[row, row, row],
        out_shape=[jax.ShapeDtypeStruct((T, D), F32), jax.ShapeDtypeStruct((T, D), F32),
                   jax.ShapeDtypeStruct((T, D), BF16)],
        compiler_params=_params("parallel"), name=name)(a, w, x, g, b)


def _ln_bwd_rows(zv, gain, dov, first, dz_ref, dzb_ref, dg_ref, db_ref):
    @pl.when(first)
    def _():
        dg_ref[...] = jnp.zeros_like(dg_ref)
        db_ref[...] = jnp.zeros_like(db_ref)

    mu = jnp.mean(zv, axis=1, keepdims=True)
    zc = zv - mu
    var = jnp.mean(zc * zc, axis=1, keepdims=True)
    rstd = lax.rsqrt(var + LN_EPS)
    xhat = zc * rstd
    dxhat = dov * gain
    m1 = jnp.mean(dxhat, axis=1, keepdims=True)
    m2 = jnp.mean(dxhat * xhat, axis=1, keepdims=True)
    dz = rstd * (dxhat - m1 - xhat * m2)
    dz_ref[...] = dz
    dzb_ref[...] = dz.astype(BF16)
    dg_ref[...] += jnp.sum(dov * xhat, axis=0, keepdims=True)
    db_ref[...] += jnp.sum(dov, axis=0, keepdims=True)


def _ln_bwd_outs(T, D):
    return [jax.ShapeDtypeStruct((T, D), F32), jax.ShapeDtypeStruct((T, D), BF16),
            jax.ShapeDtypeStruct((1, D), F32), jax.ShapeDtypeStruct((1, D), F32)]


def ln_bwd(z, g, do, name):
    T, D = z.shape
    tm = _tile(T, 512)

    def body(z_ref, g_ref, do_ref, dz_ref, dzb_ref, dg_ref, db_ref):
        _ln_bwd_rows(z_ref[...], g_ref[...], do_ref[...], pl.program_id(0) == 0, dz_ref, dzb_ref, dg_ref, db_ref)

    row = pl.BlockSpec((tm, D), lambda i: (i, 0))
    vec = pl.BlockSpec((1, D), lambda i: (0, 0))
    return pl.pallas_call(
        body, grid=(T // tm,), in_specs=[row, vec, row], out_specs=[row, row, vec, vec],
        out_shape=_ln_bwd_outs(T, D), compiler_params=_params("arbitrary"), name=name)(z, g, do)


def mm_nt(a, w, name):
    T, K = a.shape
    N = w.shape[0]
    tm, tn = _tile(T, 1024), _tile(N, 512)

    def body(a_ref, w_ref, o_ref):
        o_ref[...] = lax.dot_general(a_ref[...], w_ref[...], NT_DIMS, preferred_element_type=F32).astype(o_ref.dtype)

    return pl.pallas_call(
        body, grid=(T // tm, N // tn),
        in_specs=[pl.BlockSpec((tm, K), lambda i, j: (i, 0)), pl.BlockSpec((tn, K), lambda i, j: (j, 0))],
        out_specs=pl.BlockSpec((tm, tn), lambda i, j: (i, j)),
        out_shape=jax.ShapeDtypeStruct((T, N), BF16),
        compiler_params=_params("parallel", "parallel"), name=name)(a, w)


def ffn_bwd_mid(dzb, wd, h, u, scale, name):
    T, K = dzb.shape
    N = wd.shape[0]
    tm, tn = _tile(T, 512), _tile(N, 768)

    def body(dz_ref, w_ref, h_ref, u_ref, dh_ref, du_ref):
        dz = dz_ref[...]
        chunks = [slice(c, c + MXU_COLS) for c in range(0, tn, MXU_COLS)]
        das = [lax.dot_general(dz, w_ref[c, :], NT_DIMS, preferred_element_type=F32) for c in chunks]
        for c, da in zip(chunks, das):
            da = scale * da
            hv = h_ref[:, c].astype(F32)
            s = jax.nn.sigmoid(hv)
            silu = hv * s
            dh_ref[:, c] = (da * u_ref[:, c].astype(F32) * (s + silu * (1.0 - s))).astype(BF16)
            du_ref[:, c] = (da * silu).astype(BF16)

    tile = pl.BlockSpec((tm, tn), lambda j, i: (i, j))
    return pl.pallas_call(
        body, grid=(N // tn, T // tm),
        in_specs=[pl.BlockSpec((tm, K), lambda j, i: (i, 0)), pl.BlockSpec((tn, K), lambda j, i: (j, 0)), tile, tile],
        out_specs=[tile, tile],
        out_shape=[jax.ShapeDtypeStruct((T, N), BF16)] * 2,
        compiler_params=_params("parallel", "parallel"), name=name)(dzb, wd, h, u)


def mm_nt_res(pairs, dz, alpha, ln_before, name):
    T, N = pairs[0][0].shape
    D = pairs[0][1].shape[0]
    n = len(pairs)
    tm = _tile(T, 512 // n)

    def body(*refs):
        a_refs, w_refs, dz_ref = refs[:n], refs[n:2 * n], refs[2 * n]
        d = lax.dot_general(a_refs[0][...], w_refs[0][...], NT_DIMS, preferred_element_type=F32)
        for p in range(1, n):
            d += lax.dot_general(a_refs[p][...], w_refs[p][...], NT_DIMS, preferred_element_type=F32)
        d = d + alpha * dz_ref[...]
        if ln_before is None:
            refs[2 * n + 1][...] = d
        else:
            z_ref, g_ref = refs[2 * n + 1:2 * n + 3]
            _ln_bwd_rows(z_ref[...], g_ref[...], d, pl.program_id(0) == 0, *refs[2 * n + 3:])

    row = pl.BlockSpec((tm, D), lambda i: (i, 0))
    vec = pl.BlockSpec((1, D), lambda i: (0, 0))
    in_specs = [pl.BlockSpec((tm, N), lambda i: (i, 0))] * n + [pl.BlockSpec((D, N), lambda i: (0, 0))] * n + [row]
    operands = [p[0] for p in pairs] + [p[1] for p in pairs] + [dz]
    if ln_before is None:
        return pl.pallas_call(
            body, grid=(T // tm,), in_specs=in_specs, out_specs=row, out_shape=jax.ShapeDtypeStruct((T, D), F32),
            compiler_params=_params("parallel"), name=name)(*operands)
    return pl.pallas_call(
        body, grid=(T // tm,), in_specs=in_specs + [row, vec], out_specs=[row, row, vec, vec],
        out_shape=_ln_bwd_outs(T, D), compiler_params=_params("arbitrary"), name=name)(*operands, *ln_before)


def mm_tn(a, b, scale, shard_cols, dep, name):
    T, M = a.shape
    N = b.shape[1]
    tm, tk = _tile(M, 512), _tile(T, 4096)
    nk = T // tk
    if shard_cols:
        ns = N // N_DEV
        per = 2 if (2 * ns) % 256 == 0 else 1
        tn = per * ns
        out_shape = jax.ShapeDtypeStruct((N_DEV, M, ns), BF16)
        out_spec = pl.BlockSpec((per, tm, ns), lambda i, j, k: (j, i, 0))
    else:
        tn = _tile(N, 1024)
        out_shape = jax.ShapeDtypeStruct((M, N), BF16)
        out_spec = pl.BlockSpec((tm, tn), lambda i, j, k: (i, j))

    def body(a_ref, b_ref, dep_ref, o_ref, acc_ref):
        k = pl.program_id(2)

        @pl.when(k == 0)
        def _():
            acc_ref[...] = jnp.zeros_like(acc_ref)

        acc_ref[...] += lax.dot_general(a_ref[...], b_ref[...], TN_DIMS, preferred_element_type=F32)

        @pl.when(k == nk - 1)
        def _():
            if shard_cols:
                for s in range(per):
                    o_ref[s] = (scale * acc_ref[:, s * ns:(s + 1) * ns]).astype(BF16)
            else:
                o_ref[...] = (scale * acc_ref[...]).astype(BF16)

    return pl.pallas_call(
        body, grid=(M // tm, N // tn, nk),
        in_specs=[pl.BlockSpec((tk, tm), lambda i, j, k: (k, i)), pl.BlockSpec((tk, tn), lambda i, j, k: (k, j)), ANY],
        out_specs=out_spec, out_shape=out_shape,
        scratch_shapes=[pltpu.VMEM((tm, tn), F32)],
        compiler_params=_params("parallel", "parallel", "arbitrary"), name=name)(a, b, dep)


def loss_head(y, target, name):
    T, D = y.shape
    tm = _tile(T, 512)

    def body(y_ref, t_ref, l_ref, dy_ref):
        @pl.when(pl.program_id(0) == 0)
        def _():
            l_ref[...] = jnp.zeros_like(l_ref)

        e = y_ref[...] - t_ref[...]
        dy_ref[...] = e * (1.0 / D)
        rows = jnp.sum(e * e, axis=1, keepdims=True) * (0.5 / D)
        l_ref[...] += jnp.sum(rows, axis=0, keepdims=True)

    row = pl.BlockSpec((tm, D), lambda i: (i, 0))
    return pl.pallas_call(
        body, grid=(T // tm,), in_specs=[row, row],
        out_specs=[pl.BlockSpec((1, 1), lambda i: (0, 0)), row],
        out_shape=[jax.ShapeDtypeStruct((1, 1), F32), jax.ShapeDtypeStruct((T, D), F32)],
        compiler_params=_params("arbitrary"), name=name)(y, target)


def _rel_index(i, j):
    return jnp.clip(i - j + LEFT_CHUNKS * CHUNK, -REL_CLIP, REL_CLIP) + REL_CLIP


def bias_band(table_t, name):
    H = table_t.shape[0]
    rows = SUBLANE

    def body(t_ref, o_ref):
        i0 = pl.program_id(0) * rows
        parts = _split3(t_ref[...])
        r = lax.broadcasted_iota(jnp.int32, (REL_PAD, A_W), 0)
        j = lax.broadcasted_iota(jnp.int32, (REL_PAD, A_W), 1)
        for ii in range(rows):
            onehot = jnp.where(r == _rel_index(i0 + ii, j), 1.0, 0.0).astype(BF16)
            o_ref[ii] = sum(jnp.dot(p, onehot, preferred_element_type=F32) for p in parts)

    return pl.pallas_call(
        body, grid=(A_TQ // rows,),
        in_specs=[pl.BlockSpec((H, REL_PAD), lambda i: (0, 0))],
        out_specs=pl.BlockSpec((rows, H, A_W), lambda i: (i, 0, 0)),
        out_shape=jax.ShapeDtypeStruct((A_TQ, H, A_W), F32),
        compiler_params=_params("parallel"), name=name)(table_t)


def bias_band_bwd(dbands, name):
    H = dbands[0].shape[1]
    rows = SUBLANE
    n = len(dbands)

    def body(*refs):
        g_refs, o_ref = refs[:n], refs[n]

        @pl.when(pl.program_id(0) == 0)
        def _():
            o_ref[...] = jnp.zeros_like(o_ref)

        i0 = pl.program_id(0) * rows
        j = lax.broadcasted_iota(jnp.int32, (A_W, REL_PAD), 0)
        r = lax.broadcasted_iota(jnp.int32, (A_W, REL_PAD), 1)
        acc = jnp.zeros((H, REL_PAD), F32)
        for ii in range(rows):
            onehot = jnp.where(r == _rel_index(i0 + ii, j), 1.0, 0.0).astype(BF16)
            g = g_refs[0][ii]
            for q in range(1, n):
                g = g + g_refs[q][ii]
            acc += _dot3(g, onehot)
        o_ref[...] += acc

    spec = pl.BlockSpec((rows, H, A_W), lambda i: (i, 0, 0))
    return pl.pallas_call(
        body, grid=(A_TQ // rows,), in_specs=[spec] * n,
        out_specs=pl.BlockSpec((H, REL_PAD), lambda i: (0, 0)),
        out_shape=jax.ShapeDtypeStruct((H, REL_PAD), F32),
        compiler_params=_params("arbitrary"), name=name)(*dbands)


def _a_window(ref, qi):
    parts = []
    for d in range(A_NWB):
        kb = jnp.maximum(qi - (A_NWB - 1) + d, 0)
        parts.append(ref[pl.ds(pl.multiple_of(kb * A_TQ, A_TQ), A_TQ), :])
    return jnp.concatenate(parts, axis=0)


def _a_valid(qi):
    i = lax.broadcasted_iota(jnp.int32, (A_TQ, A_W), 0)
    j = lax.broadcasted_iota(jnp.int32, (A_TQ, A_W), 1)
    ic, jc = i // CHUNK, j // CHUNK
    return (jc >= ic) & (jc <= ic + LEFT_CHUNKS) & (j >= LEFT_CHUNKS * CHUNK - qi * A_TQ)


def _head_masks():
    lane = lax.broadcasted_iota(jnp.int32, (1, LANE), 1)
    return [lane < HEAD_DIM, lane >= HEAD_DIM]


def _a_probs(qms, kws, valids, b_ref):
    scores = [lax.dot_general(qm, kws[i // 2], NT_DIMS, preferred_element_type=F32) for i, qm in enumerate(qms)]
    probs = []
    for i, s in enumerate(scores):
        s = jnp.where(valids[i // 2], s * (HEAD_DIM ** -0.5) + b_ref[i % 2], NEG)
        p = jnp.exp(s - jnp.max(s, axis=1, keepdims=True))
        probs.append(p / jnp.sum(p, axis=1, keepdims=True))
    return probs


def _a_subtiles(ref, masks):
    out = []
    for u in range(A_SUB):
        x = ref[u * A_TQ:(u + 1) * A_TQ, :]
        out += [jnp.where(hm, x, jnp.zeros_like(x)) for hm in masks]
    return out


def _attn_specs(S, D, tq):
    hp_n = D // LANE
    nq = S // tq
    q = pl.BlockSpec((tq, LANE), lambda hp, b, qi: (b * nq + qi, hp))
    k = pl.BlockSpec((S, LANE), lambda hp, b, qi: (b, hp_n + hp))
    v = pl.BlockSpec((S, LANE), lambda hp, b, qi: (b, 2 * hp_n + hp))
    tile = pl.BlockSpec((tq, LANE), lambda hp, b, qi: (b * nq + qi, hp))
    seq = pl.BlockSpec((S, LANE), lambda hp, b, qi: (b, hp))
    return q, k, v, tile, seq


def attn_a_fwd(qkv, band, S, name):
    T, D3 = qkv.shape
    D = D3 // 3
    nb, ng = T // S, S // (A_TQ * A_SUB)

    def body(q_ref, k_ref, v_ref, b_ref, o_ref):
        qis = [pl.program_id(2) * A_SUB + u for u in range(A_SUB)]
        masks = _head_masks()
        kws, vws = [_a_window(k_ref, qi) for qi in qis], [_a_window(v_ref, qi) for qi in qis]
        probs = _a_probs(_a_subtiles(q_ref, masks), kws, [_a_valid(qi) for qi in qis], b_ref)
        outs = [jnp.dot(p.astype(BF16), vws[i // 2], preferred_element_type=F32) for i, p in enumerate(probs)]
        for u in range(A_SUB):
            o_ref[u * A_TQ:(u + 1) * A_TQ, :] = jnp.where(masks[0], outs[2 * u], outs[2 * u + 1]).astype(BF16)

    q, k, v, tile, _ = _attn_specs(S, D, A_TQ * A_SUB)
    return pl.pallas_call(
        body, grid=(D // LANE, nb, ng),
        in_specs=[q, k, v, pl.BlockSpec((2, A_TQ, A_W), lambda hp, b, qi: (hp, 0, 0))],
        out_specs=tile, out_shape=jax.ShapeDtypeStruct((T, D), BF16),
        compiler_params=_params("parallel", "parallel", "parallel"), name=name)(qkv, qkv, qkv, band)


def attn_a_bwd(qkv, band, do, S, name):
    T, D3 = qkv.shape
    D = D3 // 3
    nb, ng = T // S, S // (A_TQ * A_SUB)
    scale = HEAD_DIM ** -0.5

    def body(q_ref, k_ref, v_ref, b_ref, do_ref, dq_ref, dk_ref, dv_ref, db_ref, dk_acc, dv_acc):
        b, qg = pl.program_id(1), pl.program_id(2)

        @pl.when((b == 0) & (qg == 0))
        def _():
            db_ref[...] = jnp.zeros_like(db_ref)

        @pl.when(qg == 0)
        def _():
            dk_acc[...] = jnp.zeros_like(dk_acc)
            dv_acc[...] = jnp.zeros_like(dv_acc)

        qis = [qg * A_SUB + u for u in range(A_SUB)]
        masks = _head_masks()
        kws, vws = [_a_window(k_ref, qi) for qi in qis], [_a_window(v_ref, qi) for qi in qis]
        qms, doms = _a_subtiles(q_ref, masks), _a_subtiles(do_ref, masks)
        dps = [lax.dot_general(dom, vws[i // 2], NT_DIMS, preferred_element_type=F32) for i, dom in enumerate(doms)]
        probs = _a_probs(qms, kws, [_a_valid(qi) for qi in qis], b_ref)
        dss = [p * (dp - jnp.sum(p * dp, axis=1, keepdims=True)) for p, dp in zip(probs, dps)]
        for hh in range(2):
            db_ref[hh] += sum(dss[hh::2])
        dsbs = [ds.astype(BF16) for ds in dss]
        dqs = [jnp.dot(dsb, kws[i // 2], preferred_element_type=F32) for i, dsb in enumerate(dsbs)]
        dks = [lax.dot_general(dsb, qm, TN_DIMS, preferred_element_type=F32) for dsb, qm in zip(dsbs, qms)]
        dvs = [lax.dot_general(p.astype(BF16), dom, TN_DIMS, preferred_element_type=F32) for p, dom in zip(probs, doms)]
        for u, qi in enumerate(qis):
            dq_ref[u * A_TQ:(u + 1) * A_TQ, :] = (jnp.where(masks[0], dqs[2 * u], dqs[2 * u + 1]) * scale).astype(BF16)
            dkw = (dks[2 * u] + dks[2 * u + 1]) * scale
            dvw = dvs[2 * u] + dvs[2 * u + 1]
            for d in range(A_NWB):
                kb = jnp.maximum(qi - (A_NWB - 1) + d, 0)
                rows = pl.ds(pl.multiple_of(kb * A_TQ, A_TQ), A_TQ)
                dk_acc[rows, :] += dkw[d * A_TQ:(d + 1) * A_TQ]
                dv_acc[rows, :] += dvw[d * A_TQ:(d + 1) * A_TQ]

        @pl.when(qg == ng - 1)
        def _():
            dk_ref[...] = dk_acc[...].astype(BF16)
            dv_ref[...] = dv_acc[...].astype(BF16)

    q, k, v, tile, seq = _attn_specs(S, D, A_TQ * A_SUB)
    bspec = pl.BlockSpec((2, A_TQ, A_W), lambda hp, b, qi: (hp, 0, 0))
    act = jax.ShapeDtypeStruct((T, D), BF16)
    return pl.pallas_call(
        body, grid=(D // LANE, nb, ng),
        in_specs=[q, k, v, bspec, tile], out_specs=[tile, seq, seq, bspec],
        out_shape=[act, act, act, jax.ShapeDtypeStruct(band.shape, F32)],
        scratch_shapes=[pltpu.VMEM((S, LANE), F32), pltpu.VMEM((S, LANE), F32)],
        compiler_params=_params("arbitrary", "arbitrary", "arbitrary"), name=name)(qkv, qkv, qkv, band, do)


def _dot2(v, w):
    h = v.astype(BF16)
    lo = (v - h.astype(F32)).astype(BF16)
    return jnp.dot(h, w, preferred_element_type=F32) + jnp.dot(lo, w, preferred_element_type=F32)


def _b_weights(qms, kts, rights, after, diagonal):
    inst = [(t, hh) for t in range(len(kts)) for hh in range(2)]
    zs = [lax.dot_general(qms[hh], kts[t], NT_DIMS, preferred_element_type=F32) for t, hh in inst]
    lbs, l1ms, his, los = [], [], [], []
    causal = _strictly_causal() if any(diagonal) else None
    for (t, hh), z in zip(inst, zs):
        nz = -z
        soft = jnp.log(1.0 + jnp.exp(jnp.minimum(z, nz)))
        lbs.append(jnp.minimum(z, 0.0) - soft)
        l1m = jnp.minimum(nz, 0.0) - soft
        if diagonal[t]:
            l1m = jnp.where(causal, l1m, 0.0)
        hi = l1m.astype(BF16)
        l1ms.append(l1m)
        his.append(hi)
        los.append((l1m - hi.astype(F32)).astype(BF16))
    sums = [jnp.dot(hi, after, preferred_element_type=F32) + jnp.dot(lo, after, preferred_element_type=F32)
            for hi, lo in zip(his, los)]
    rights = list(rights)
    weights = []
    for (t, hh), lb, l1m, c in zip(inst, lbs, l1ms, sums):
        a = jnp.exp(lb + (rights[hh] + c))
        weights.append(jnp.where(causal, a, 0.0) if diagonal[t] else a)
        rights[hh] = rights[hh] + jnp.sum(l1m, axis=1, keepdims=True)
    return zs, weights, rights


def _strictly_causal():
    row = lax.broadcasted_iota(jnp.int32, (B_T, B_T), 0)
    col = lax.broadcasted_iota(jnp.int32, (B_T, B_T), 1)
    return col < row


def _tri(strict_lower):
    r = lax.broadcasted_iota(jnp.int32, (B_T, B_T), 0)
    c = lax.broadcasted_iota(jnp.int32, (B_T, B_T), 1)
    return jnp.where((r > c) if strict_lower else (r < c), 1.0, 0.0).astype(BF16)


def _scaled_heads(q2):
    qs = q2 * (HEAD_DIM ** -0.5)
    return [jnp.where(hm, qs, jnp.zeros_like(qs)) for hm in _head_masks()]


def attn_b_fwd(qkv, S, name):
    T, D3 = qkv.shape
    D = D3 // 3
    nb, nq = T // S, S // B_T

    def body(q_ref, k_ref, v_ref, o_ref):
        qi = pl.program_id(2)
        after = _tri(True)
        qms = _scaled_heads(q_ref[...])

        def tiles(kbs, carry, diagonal):
            rows = [pl.ds(pl.multiple_of(kb * B_T, B_T), B_T) for kb in kbs]
            kts, vts = [k_ref[r, :] for r in rows], [v_ref[r, :] for r in rows]
            _, weights, rights = _b_weights(qms, kts, (carry[0], carry[2]), after, diagonal)
            accs = [carry[1], carry[3]]
            for i, a in enumerate(weights):
                accs[i % 2] = accs[i % 2] + jnp.dot(a.astype(BF16), vts[i // 2], preferred_element_type=F32)
            return rights[0], accs[0], rights[1], accs[1]

        init = (jnp.zeros((B_T, 1), F32), jnp.zeros((B_T, LANE), F32)) * 2
        res = lax.cond(qi % 2 == 1, lambda c: tiles([qi, qi - 1], c, [True, False]), lambda c: tiles([qi], c, [True]), init)
        top = qi - 1 - qi % 2
        res = lax.fori_loop(0, qi // 2, lambda p, c: tiles([top - 2 * p, top - 1 - 2 * p], c, [False, False]), res)
        o_ref[...] = jnp.where(_head_masks()[0], res[1], res[3]).astype(BF16)

    q, k, v, tile, _ = _attn_specs(S, D, B_T)
    return pl.pallas_call(
        body, grid=(D // LANE, nb, nq), in_specs=[q, k, v], out_specs=tile,
        out_shape=jax.ShapeDtypeStruct((T, D), BF16),
        compiler_params=_params("parallel", "parallel", "parallel"), name=name)(qkv, qkv, qkv)


def attn_b_bwd(qkv, do, S, name):
    T, D3 = qkv.shape
    D = D3 // 3
    nb, nq = T // S, S // B_T
    scale = HEAD_DIM ** -0.5

    def body(q_ref, k_ref, v_ref, do_ref, dq_ref, dk_ref, dv_ref, dk_acc, dv_acc, z_s, g_s):
        qi = pl.program_id(2)

        @pl.when(qi == 0)
        def _():
            dk_acc[...] = jnp.zeros_like(dk_acc)
            dv_acc[...] = jnp.zeros_like(dv_acc)

        do2 = do_ref[...]
        after, before = _tri(True), _tri(False)
        masks = _head_masks()
        qms = _scaled_heads(q_ref[...])
        doms = [jnp.where(hm, do2, jnp.zeros_like(do2)) for hm in masks]

        def sweep_left(kbs, rights, diagonal):
            rows = [pl.ds(pl.multiple_of(kb * B_T, B_T), B_T) for kb in kbs]
            kts, vts = [k_ref[r, :] for r in rows], [v_ref[r, :] for r in rows]
            das = [lax.dot_general(doms[hh], vt, NT_DIMS, preferred_element_type=F32) for vt in vts for hh in range(2)]
            zs, weights, rights = _b_weights(qms, kts, rights, after, diagonal)
            for i, (z, a, da) in enumerate(zip(zs, weights, das)):
                z_s[i % 2, kbs[i // 2]] = z
                g_s[i % 2, kbs[i // 2]] = da * a
            dvs = [lax.dot_general(a.astype(BF16), doms[i % 2], TN_DIMS, preferred_element_type=F32)
                   for i, a in enumerate(weights)]
            for t, r in enumerate(rows):
                dv_acc[r, :] += dvs[2 * t] + dvs[2 * t + 1]
            return tuple(rights)

        zero_col = jnp.zeros((B_T, 1), F32)
        odd = qi % 2 == 1
        top = qi - 1 - qi % 2
        rights = lax.cond(odd, lambda c: sweep_left([qi, qi - 1], c, [True, False]), lambda c: sweep_left([qi], c, [True]),
                          (zero_col, zero_col))
        lax.fori_loop(0, qi // 2, lambda p, c: sweep_left([top - 2 * p, top - 1 - 2 * p], c, [False, False]), rights)

        def sweep_right(kbs, carry, diagonal):
            rows = [pl.ds(pl.multiple_of(kb * B_T, B_T), B_T) for kb in kbs]
            inst = [(t, hh) for t in range(len(kbs)) for hh in range(2)]
            gs = [g_s[hh, kbs[t]] for t, hh in inst]
            sums = [_dot2(g, before) for g in gs]
            lefts, dqs = [carry[0], carry[2]], [carry[1], carry[3]]
            dzbs = []
            for (t, hh), g, c in zip(inst, gs, sums):
                beta = jax.nn.sigmoid(z_s[hh, kbs[t]])
                dz = g * (1.0 - beta) - beta * (lefts[hh] + c)
                if diagonal[t]:
                    dz = jnp.where(_strictly_causal(), dz, 0.0)
                dzbs.append(dz.astype(BF16))
                lefts[hh] = lefts[hh] + jnp.sum(g, axis=1, keepdims=True)
            dks = [lax.dot_general(dzb, qms[hh], TN_DIMS, preferred_element_type=F32) for (t, hh), dzb in zip(inst, dzbs)]
            for (t, hh), dzb in zip(inst, dzbs):
                dqs[hh] = dqs[hh] + jnp.dot(dzb, k_ref[rows[t], :], preferred_element_type=F32)
            for t, r in enumerate(rows):
                dk_acc[r, :] += dks[2 * t] + dks[2 * t + 1]
            return lefts[0], dqs[0], lefts[1], dqs[1]

        init = (zero_col, jnp.zeros((B_T, LANE), F32)) * 2
        res = lax.fori_loop(0, qi // 2, lambda p, c: sweep_right([2 * p, 2 * p + 1], c, [False, False]), init)
        res = lax.cond(odd, lambda c: sweep_right([qi - 1, qi], c, [False, True]), lambda c: sweep_right([qi], c, [True]), res)
        dq_ref[...] = (jnp.where(masks[0], res[1], res[3]) * scale).astype(BF16)

        @pl.when(qi == nq - 1)
        def _():
            dk_ref[...] = dk_acc[...].astype(BF16)
            dv_ref[...] = dv_acc[...].astype(BF16)

    q, k, v, tile, seq = _attn_specs(S, D, B_T)
    act = jax.ShapeDtypeStruct((T, D), BF16)
    return pl.pallas_call(
        body, grid=(D // LANE, nb, nq),
        in_specs=[q, k, v, tile], out_specs=[tile, seq, seq], out_shape=[act, act, act],
        scratch_shapes=[pltpu.VMEM((S, LANE), F32), pltpu.VMEM((S, LANE), F32),
                        pltpu.VMEM((2, nq, B_T, B_T), F32), pltpu.VMEM((2, nq, B_T, B_T), F32)],
        compiler_params=_params("arbitrary", "arbitrary", "arbitrary"), name=name)(qkv, qkv, qkv, do)


HBM = pl.BlockSpec(memory_space=pltpu.HBM)
SEM = pl.BlockSpec(memory_space=pltpu.SEMAPHORE)
N_PEERS = N_DEV - 1
GATHERS_AHEAD = 3


def _place():
    return lax.axis_index("x"), lax.axis_index("y"), lax.axis_index("c")


def _peers(x, y, c):
    return [(1 - x if r & 4 else x, 1 - y if r & 2 else y, 1 - c if r & 1 else c) for r in range(1, N_DEV)]


def _block(ref, shape, by_cols, j):
    r, w = shape
    if by_cols:
        return ref.at[:, pl.ds(pl.multiple_of(j * w, LANE), w)]
    return ref.at[pl.ds(pl.multiple_of(j * r, SUBLANE), r), :]


def _exchange_copies(gather, src_refs, land_refs, send_sems, recv_sems, by_cols):
    x, y, c = _place()
    me = 4 * x + 2 * y + c
    out = []
    for t, (src, land) in enumerate(zip(src_refs, land_refs)):
        for r, peer in enumerate(_peers(x, y, c)):
            pj = 4 * peer[0] + 2 * peer[1] + peer[2]
            if gather:
                mine, to_me, theirs = src, _block(land, src.shape, by_cols[t], me), _block(land, src.shape, by_cols[t], pj)
            else:
                mine, to_me, theirs = src.at[pj], land.at[me], land.at[pj]
            sems = dict(send_sem=send_sems.at[N_PEERS * t + r], recv_sem=recv_sems.at[N_PEERS * t + r],
                        device_id=peer, device_id_type=MESH_ID)
            out.append((pltpu.make_async_remote_copy(src_ref=mine, dst_ref=to_me, **sems),
                        pltpu.make_async_remote_copy(src_ref=mine, dst_ref=theirs, **sems)))
    return out


def _own_copies(gather, src_refs, land_refs, own_sems, by_cols):
    x, y, c = _place()
    me = 4 * x + 2 * y + c
    out = []
    for t, (src, land) in enumerate(zip(src_refs, land_refs)):
        if gather:
            out.append(pltpu.make_async_copy(src, _block(land, src.shape, by_cols[t], me), own_sems.at[t]))
        else:
            out.append(pltpu.make_async_copy(src.at[me], land.at[me], own_sems.at[t]))
    return out


def exchange_start(gather, srcs, land_shapes, by_cols, after, name):
    n = len(srcs)

    def body(*refs):
        src_refs, land_refs = refs[:n], refs[n:2 * n]
        send_sems, recv_sems, own_sems = refs[2 * n + 1:2 * n + 4]
        token = refs[-1]
        for cp in _own_copies(gather, src_refs, land_refs, own_sems, by_cols):
            cp.start()
        for mine, _ in _exchange_copies(gather, src_refs, land_refs, send_sems, recv_sems, by_cols):
            mine.start()
        token[...] = jnp.zeros_like(token)

    lands = [pltpu.with_memory_space_constraint(lax.empty(s.shape, s.dtype), pltpu.HBM) for s in land_shapes]
    srcs = [pltpu.with_memory_space_constraint(s, pltpu.HBM) for s in srcs]
    res = pl.pallas_call(
        body, name=name,
        out_shape=(pltpu.SemaphoreType.DMA((N_PEERS * n,)), pltpu.SemaphoreType.DMA((N_PEERS * n,)),
                   pltpu.SemaphoreType.DMA((n,)),
                   *[pltpu.HBM(s.shape, s.dtype) for s in srcs], *[pltpu.HBM(s.shape, s.dtype) for s in land_shapes],
                   jax.ShapeDtypeStruct((SUBLANE, LANE), F32)),
        in_specs=[HBM] * (2 * n) + [ANY],
        out_specs=(SEM, SEM, SEM, *[HBM] * (2 * n), pl.BlockSpec(memory_space=pltpu.VMEM)),
        input_output_aliases={i: 3 + i for i in range(2 * n)},
        compiler_params=pltpu.CompilerParams(has_side_effects=pltpu.SideEffectType.DATAFLOW_SIDE_EFFECTING),
    )(*srcs, *lands, after)
    return dict(gather=gather, n=n, by_cols=by_cols, sems=res[:3], srcs=res[3:3 + n],
                lands=res[3 + n:3 + 2 * n], token=res[-1])


def exchange_wait(started, after, name):
    n, gather, by_cols = started["n"], started["gather"], started["by_cols"]

    def body(*refs):
        src_refs, land_refs = refs[:n], refs[n:2 * n]
        send_sems, recv_sems, own_sems = refs[2 * n:2 * n + 3]
        for mine, theirs in _exchange_copies(gather, src_refs, land_refs, send_sems, recv_sems, by_cols):
            mine.wait_send()
            theirs.wait_recv()
        for cp in _own_copies(gather, src_refs, land_refs, own_sems, by_cols):
            cp.wait()

    res = pl.pallas_call(
        body, name=name,
        out_shape=tuple(pltpu.HBM(s.shape, s.dtype) for s in (*started["srcs"], *started["lands"])),
        in_specs=[HBM] * (2 * n) + [SEM, SEM, SEM, ANY], out_specs=tuple([HBM] * (2 * n)),
        input_output_aliases={i: i for i in range(2 * n)},
        compiler_params=pltpu.CompilerParams(has_side_effects=pltpu.SideEffectType.DATAFLOW_SIDE_EFFECTING),
    )(*started["srcs"], *started["lands"], *started["sems"], after)
    return res[n:]


def gather_small(v, dep, name):
    R, C = v.shape

    def body(v_ref, dep_ref, o_ref, send_sems, recv_sems):
        x, y, c = _place()
        o_ref[4 * x + 2 * y + c] = v_ref[...]
        peers = _peers(x, y, c)

        def copy(r, owner, to):
            slot = o_ref.at[4 * owner[0] + 2 * owner[1] + owner[2]]
            return pltpu.make_async_remote_copy(
                src_ref=slot, dst_ref=slot, send_sem=send_sems.at[r], recv_sem=recv_sems.at[r],
                device_id=to, device_id_type=MESH_ID)

        sends = [copy(r, (x, y, c), peer) for r, peer in enumerate(peers)]
        for cp in sends:
            cp.start()
        for r, peer in enumerate(peers):
            copy(r, peer, (x, y, c)).wait_recv()
        for cp in sends:
            cp.wait_send()

    vm = pl.BlockSpec(memory_space=pltpu.VMEM)
    return pl.pallas_call(
        body, in_specs=[vm, ANY], out_specs=vm, out_shape=jax.ShapeDtypeStruct((N_DEV, R, C), F32),
        scratch_shapes=[pltpu.SemaphoreType.DMA((N_PEERS,)), pltpu.SemaphoreType.DMA((N_PEERS,))],
        name=name)(v, dep)


def sum_devices(g, name):
    _, R, C = g.shape

    def body(g_ref, o_ref):
        acc = g_ref[0]
        for j in range(1, N_DEV):
            acc = acc + g_ref[j]
        o_ref[...] = acc

    vm = pl.BlockSpec(memory_space=pltpu.VMEM)
    return pl.pallas_call(body, in_specs=[vm], out_specs=vm, out_shape=jax.ShapeDtypeStruct((R, C), F32), name=name)(g)


def adamw(w, m, v, index, parts, outs, name):
    L, R, C = w.shape
    if outs is None:
        outs = [lax.empty((L, R, C), F32) for _ in range(4)]
    P, Rp, Cp = parts.shape
    tr = _tile(R, 512, SUBLANE) if Rp == R else R

    def body(w_ref, m_ref, v_ref, p_ref, g_in, d_in, m_in, v_in, g_out, d_out, m_out, v_out):
        g = p_ref[0, :tr, :C].astype(F32)
        for q in range(1, P):
            g = g + p_ref[q, :tr, :C].astype(F32)
        mn = ADAM_B1 * m_ref[...] + (1.0 - ADAM_B1) * g
        vn = ADAM_B2 * v_ref[...] + (1.0 - ADAM_B2) * (g * g)
        m_hat = mn / (1.0 - ADAM_B1 ** ADAM_STEP)
        v_hat = vn / (1.0 - ADAM_B2 ** ADAM_STEP)
        g_out[...] = g
        d_out[...] = -ADAM_LR * (m_hat / (jnp.sqrt(v_hat) + ADAM_EPS) + ADAM_WD * w_ref[...])
        m_out[...] = mn
        v_out[...] = vn

    slab = pl.BlockSpec((None, tr, C), lambda r: (index, r, 0))
    pspec = pl.BlockSpec((P, tr if Rp == R else Rp, Cp), lambda r: (0, r, 0))
    return pl.pallas_call(
        body, grid=(R // tr,), in_specs=[slab] * 3 + [pspec] + [ANY] * 4, out_specs=[slab] * 4,
        out_shape=[jax.ShapeDtypeStruct((L, R, C), F32)] * 4,
        input_output_aliases={4 + q: q for q in range(4)},
        compiler_params=_params("parallel"), name=name)(w, m, v, parts, *outs)


def _pad_to(a, axis, size):
    pad = [(0, 0)] * a.ndim
    pad[axis] = (0, size - a.shape[axis])
    return jnp.pad(a, pad)


def kernel(x, w_qkv_a, w_o_a, rel_bias, w_qkv_b, w_o_b, ffn_w_gate, ffn_w_up, ffn_w_down, ln_g, ln_b, loss_target, m_w_qkv_a, m_w_o_a, m_rel_bias, m_w_qkv_b, m_w_o_b, m_ffn_w_gate, m_ffn_w_up, m_ffn_w_down, m_ln_g, m_ln_b, v_w_qkv_a, v_w_o_a, v_rel_bias, v_w_qkv_b, v_w_o_b, v_ffn_w_gate, v_ffn_w_up, v_ffn_w_down, v_ln_g, v_ln_b):
    nb, S, D = x.shape
    T = nb * S
    depth = ffn_w_gate.shape[0]
    H = D // HEAD_DIM
    fs = ffn_w_gate.shape[-1]
    fp = -(-fs // LANE) * LANE
    alpha = (2.0 * depth) ** 0.25
    cx, cy, cc = _place()
    me = 4 * cx + 2 * cy + cc

    ln_local = jnp.concatenate([ln_g.reshape(depth * 3, -1), ln_b.reshape(depth * 3, -1)], axis=0)
    ln_all = gather_small(ln_local, ln_local, "gather_ln")
    ln_full = jnp.transpose(ln_all, (1, 0, 2)).reshape(2 * depth * 3, D)
    ln_gain = lambda i, s: ln_full[3 * i + s][None, :]
    ln_bias = lambda i, s: ln_full[3 * depth + 3 * i + s][None, :]

    table_t = _pad_to(rel_bias.T, 1, REL_PAD)
    band = jnp.transpose(bias_band(table_t, "bias_band"), (1, 0, 2))

    subs = []
    for i in range(depth):
        for s in (0, 1, 2):
            if s == 1:
                wq, wo = (w_qkv_a, w_o_a) if i % 2 == 0 else (w_qkv_b, w_o_b)
                subs.append(([wq[i // 2].astype(BF16), wo[i // 2].astype(BF16)], [True, False]))
            else:
                f = 0 if s == 0 else 1
                subs.append(([_pad_to(ffn_w_gate[i, f].astype(BF16), 1, fp), _pad_to(ffn_w_up[i, f].astype(BF16), 1, fp),
                              _pad_to(ffn_w_down[i, f].astype(BF16), 0, fp)], [True, True, False]))

    def start_gather(k, after):
        shards, by_cols = subs[k]
        shapes = [jax.ShapeDtypeStruct((s.shape[0], N_DEV * s.shape[1]) if col else (N_DEV * s.shape[0], s.shape[1]), BF16)
                  for s, col in zip(shards, by_cols)]
        return exchange_start(True, shards, shapes, by_cols, after, f"gather_start_{k}")

    xf = x.reshape(T, D)
    act, act_b = xf, xf.astype(BF16)
    gathers = {}
    for k in range(min(GATHERS_AHEAD, len(subs))):
        gathers[k] = start_gather(k, xf if k == 0 else gathers[k - 1]["token"])
    newest = gathers[k]["token"]
    saved = []
    for i in range(depth):
        layer = {}
        for s in (0, 1, 2):
            k = 3 * i + s
            tag = f"L{i}S{s}"
            if 0 < k and k + GATHERS_AHEAD - 1 < len(subs):
                gathers[k + GATHERS_AHEAD - 1] = start_gather(k + GATHERS_AHEAD - 1, act_b)
                newest = gathers[k + GATHERS_AHEAD - 1]["token"]
            full = exchange_wait(gathers.pop(k), newest, f"gather_wait_{k}")
            if s == 1:
                wqkv_f, wo_f = full
                qkv = mm_nn(act_b, wqkv_f, "qkv_" + tag)
                if i % 2 == 0:
                    att = attn_a_fwd(qkv, band, S, "attn_a_fwd_" + tag)
                else:
                    att = attn_b_fwd(qkv, S, "attn_b_fwd_" + tag)
                z, o, ob = mm_res_ln(att, wo_f, act, ln_gain(i, s), ln_bias(i, s), alpha, 1.0, "out_ln_" + tag)
                layer[s] = dict(x_b=act_b, qkv=qkv, att=att, z=z, wqkv=wqkv_f, wo=wo_f)
            else:
                wg_f, wu_f, wd_f = full
                h, u, a = ffn_up(act_b, wg_f, wu_f, "ffn_up_" + tag)
                z, o, ob = mm_res_ln(a, wd_f, act, ln_gain(i, s), ln_bias(i, s), alpha, 0.5, "down_ln_" + tag)
                layer[s] = dict(x_b=act_b, h=h, u=u, a=a, z=z, wg=wg_f, wu=wu_f, wd=wd_f)
            act, act_b = o, ob
        saved.append(layer)

    loss_local, d_act = loss_head(act, loss_target.reshape(T, D), "loss_head")
    loss = lax.psum(loss_local[0, 0], ("x", "y", "c"))

    results = {}

    def update(name, w, m, v, index, parts):
        L = w.shape[0] if w.ndim == 3 else w.shape[0] * w.shape[1]
        flat = lambda t: t.reshape((L,) + t.shape[-2:])
        results[name] = adamw(flat(w), flat(m), flat(v), index, parts, results.get(name), f"adamw_{name}_{index}")

    def finish(entry, after):
        exchange, targets, tag = entry
        lands = exchange_wait(exchange, after, "scatter_wait_" + tag)
        for (name, w, m, v, index), parts in zip(targets, lands):
            update(name, w, m, v, index, parts)
        return results[targets[-1][0]][0]

    pending = []
    started = d_act
    dbands = []
    dln_g = [None] * (3 * depth)
    dln_b = [None] * (3 * depth)
    ln_grads = ln_bwd(saved[depth - 1][2]["z"], ln_gain(depth - 1, 2), d_act, "ln_bwd_last")
    for i in reversed(range(depth)):
        for s in (2, 1, 0):
            tag = f"L{i}S{s}"
            sv = saved[i][s]
            dz, dzb, dln_g[3 * i + s], dln_b[3 * i + s] = ln_grads
            before = (i, s - 1) if s > 0 else (i - 1, 2) if i > 0 else None
            ln_before = None if before is None else (saved[before[0]][before[1]]["z"], ln_gain(*before))
            if s == 1:
                j = i // 2
                d_att = mm_nt(dzb, sv["wo"], "att_bwd_" + tag)
                g_wo = mm_tn(sv["att"], dzb, 1.0, False, started, "dwo_" + tag)
                if i % 2 == 0:
                    dq, dk, dv, dband = attn_a_bwd(sv["qkv"], band, d_att, S, "attn_a_bwd_" + tag)
                    dbands.append(jnp.transpose(dband, (1, 0, 2)))
                else:
                    dq, dk, dv = attn_b_bwd(sv["qkv"], d_att, S, "attn_b_bwd_" + tag)
                dqkv = jnp.concatenate([dq, dk, dv], axis=1)
                g_wqkv = mm_tn(sv["x_b"], dqkv, 1.0, True, started, "dwqkv_" + tag)
                ln_grads = mm_nt_res([(dqkv, sv["wqkv"])], dz, alpha, ln_before, "dx_mix_" + tag)
                grads = [g_wqkv, g_wo.reshape(N_DEV, D // N_DEV, D)]
                if i % 2 == 0:
                    targets = [("w_qkv_a", w_qkv_a, m_w_qkv_a, v_w_qkv_a, j), ("w_o_a", w_o_a, m_w_o_a, v_w_o_a, j)]
                else:
                    targets = [("w_qkv_b", w_qkv_b, m_w_qkv_b, v_w_qkv_b, j), ("w_o_b", w_o_b, m_w_o_b, v_w_o_b, j)]
            else:
                f = 0 if s == 0 else 1
                dh, du = ffn_bwd_mid(dzb, sv["wd"], sv["h"], sv["u"], 0.5, "ffn_mid_" + tag)
                g_wd = mm_tn(sv["a"], dzb, 0.5, False, started, "dwd_" + tag)
                g_wg = mm_tn(sv["x_b"], dh, 1.0, True, started, "dwg_" + tag)
                g_wu = mm_tn(sv["x_b"], du, 1.0, True, started, "dwu_" + tag)
                ln_grads = mm_nt_res([(dh, sv["wg"]), (du, sv["wu"])], dz, alpha, ln_before, "dx_ffn_" + tag)
                grads = [g_wg, g_wu, g_wd.reshape(N_DEV, fp, D)]
                idx = 2 * i + f
                targets = [("ffn_w_gate", ffn_w_gate, m_ffn_w_gate, v_ffn_w_gate, idx),
                           ("ffn_w_up", ffn_w_up, m_ffn_w_up, v_ffn_w_up, idx),
                           ("ffn_w_down", ffn_w_down, m_ffn_w_down, v_ffn_w_down, idx)]
            after = ln_grads[0] if before is not None else ln_grads
            if before is None:
                grad_x = ln_grads.reshape(nb, S, D)
                dtable_t = bias_band_bwd(dbands, "bias_band_bwd")
                small = jnp.concatenate(dln_g + dln_b + [_pad_to(dtable_t, 1, D)], axis=0)
                small = _pad_to(small, 0, -(-small.shape[0] // SUBLANE) * SUBLANE)
                total = after = sum_devices(gather_small(small, ln_grads, "gather_small_grads"), "sum_small_grads")
            shapes = [jax.ShapeDtypeStruct(g.shape, g.dtype) for g in grads]
            pending.append((exchange_start(False, grads, shapes, None, after, "scatter_start_" + tag), targets, tag))
            started = pending[-1][0]["token"]

    last = pending.pop()
    done = started
    for entry in pending:
        done = finish(entry, done)
    finish(last, done)
    n_ln = 3 * depth
    g_ln_g = lax.dynamic_slice_in_dim(total[:n_ln], me * (D // N_DEV), D // N_DEV, axis=1)
    g_ln_b = lax.dynamic_slice_in_dim(total[n_ln:2 * n_ln], me * (D // N_DEV), D // N_DEV, axis=1)
    g_rel = total[2 * n_ln:2 * n_ln + H, :N_REL].T
    as3 = lambda t: t.reshape((1, -1, t.shape[-1]))
    results["ln_g"] = adamw(as3(ln_g), as3(m_ln_g), as3(v_ln_g), 0, g_ln_g[None], None, "adamw_ln_g")
    results["ln_b"] = adamw(as3(ln_b), as3(m_ln_b), as3(v_ln_b), 0, g_ln_b[None], None, "adamw_ln_b")
    results["rel_bias"] = adamw(as3(rel_bias), as3(m_rel_bias), as3(v_rel_bias), 0, g_rel[None], None, "adamw_rel_bias")

    order = [("w_qkv_a", w_qkv_a), ("w_o_a", w_o_a), ("rel_bias", rel_bias), ("w_qkv_b", w_qkv_b), ("w_o_b", w_o_b),
             ("ffn_w_gate", ffn_w_gate), ("ffn_w_up", ffn_w_up), ("ffn_w_down", ffn_w_down), ("ln_g", ln_g), ("ln_b", ln_b)]
    outs = [loss, grad_x]
    for q in range(4):
        for name, like in order:
            outs.append(results[name][q].reshape(like.shape))
    return tuple(outs)
```

```python
import functools

import jax
import jax.numpy as jnp
from jax import lax
from jax.experimental import pallas as pl
from jax.experimental.pallas import tpu as pltpu

BF16 = jnp.bfloat16
F32 = jnp.float32
MESH_ID = pl.DeviceIdType.MESH
ANY = pl.BlockSpec(memory_space=pl.ANY)

N_DEV = 8
LANE = 128
MXU_COLS = 256
SUBLANE = 8
VMEM_LIMIT = 56 * 1024 * 1024

HEAD_DIM = 64
CHUNK = 64
LEFT_CHUNKS = 8
REL_CLIP = 128
N_REL = 2 * REL_CLIP + 1
REL_PAD = 384
LN_EPS = 1e-5
ADAM_LR, ADAM_B1, ADAM_B2, ADAM_EPS, ADAM_WD, ADAM_STEP = 0.001, 0.9, 0.999, 1e-08, 0.01, 10
NEG = -1e30

A_TQ = 128
A_NWB = LEFT_CHUNKS * CHUNK // A_TQ + 1
A_W = A_NWB * A_TQ
A_SUB = 4
B_T = 256

NT_DIMS = (((1,), (1,)), ((), ()))
TN_DIMS = (((0,), (0,)), ((), ()))


def _tile(n, pref, unit=LANE):
    if n <= pref:
        return n
    t = pref - pref % unit
    while t > unit and n % t:
        t -= unit
    assert n % t == 0, (n, pref)
    return t


def _params(*sem):
    return pltpu.CompilerParams(dimension_semantics=sem, vmem_limit_bytes=VMEM_LIMIT)


def _split3(v):
    h = v.astype(BF16)
    r = v - h.astype(F32)
    m = r.astype(BF16)
    lo = (r - m.astype(F32)).astype(BF16)
    return h, m, lo


def _dot3(v, w):
    h, m, lo = _split3(v)
    return (jnp.dot(h, w, preferred_element_type=F32) + jnp.dot(m, w, preferred_element_type=F32)
            + jnp.dot(lo, w, preferred_element_type=F32))


def mm_nn(a, w, name):
    T, K = a.shape
    N = w.shape[1]
    tm, tn = _tile(T, 1024), _tile(N, 768)

    def body(a_ref, w_ref, o_ref):
        o_ref[...] = jnp.dot(a_ref[...], w_ref[...], preferred_element_type=F32).astype(o_ref.dtype)

    return pl.pallas_call(
        body, grid=(T // tm, N // tn),
        in_specs=[pl.BlockSpec((tm, K), lambda i, j: (i, 0)), pl.BlockSpec((K, tn), lambda i, j: (0, j))],
        out_specs=pl.BlockSpec((tm, tn), lambda i, j: (i, j)),
        out_shape=jax.ShapeDtypeStruct((T, N), BF16),
        compiler_params=_params("parallel", "parallel"), name=name)(a, w)


def ffn_up(xb, wg, wu, name):
    T, K = xb.shape
    N = wg.shape[1]
    tm, tn = _tile(T, 512), _tile(N, 768)

    def body(x_ref, wg_ref, wu_ref, h_ref, u_ref, a_ref):
        x = x_ref[...]
        chunks = [slice(c, c + MXU_COLS) for c in range(0, tn, MXU_COLS)]
        hs = [jnp.dot(x, wg_ref[:, c], preferred_element_type=F32) for c in chunks]
        us = [jnp.dot(x, wu_ref[:, c], preferred_element_type=F32) for c in chunks]
        for c, h, u in zip(chunks, hs, us):
            h_ref[:, c] = h.astype(BF16)
            u_ref[:, c] = u.astype(BF16)
            a_ref[:, c] = (h * jax.nn.sigmoid(h) * u).astype(BF16)

    wspec = pl.BlockSpec((K, tn), lambda j, i: (0, j))
    ospec = pl.BlockSpec((tm, tn), lambda j, i: (i, j))
    return pl.pallas_call(
        body, grid=(N // tn, T // tm),
        in_specs=[pl.BlockSpec((tm, K), lambda j, i: (i, 0)), wspec, wspec],
        out_specs=[ospec, ospec, ospec],
        out_shape=[jax.ShapeDtypeStruct((T, N), BF16)] * 3,
        compiler_params=_params("parallel", "parallel"), name=name)(xb, wg, wu)


def mm_res_ln(a, w, x, g, b, alpha, scale, name):
    T, K = a.shape
    D = w.shape[1]
    tm = _tile(T, 512)
    parts = [slice(r, r + MXU_COLS) for r in range(0, tm, MXU_COLS)] if tm % MXU_COLS == 0 else [slice(0, tm)]

    def body(a_ref, w_ref, x_ref, g_ref, b_ref, z_ref, o_ref, ob_ref):
        ys = [jnp.dot(a_ref[r, :], w_ref[...], preferred_element_type=F32) for r in parts]
        for r, y in zip(parts, ys):
            z = alpha * x_ref[r, :] + scale * y
            mu = jnp.mean(z, axis=1, keepdims=True)
            zc = z - mu
            var = jnp.mean(zc * zc, axis=1, keepdims=True)
            o = zc * lax.rsqrt(var + LN_EPS) * g_ref[...] + b_ref[...]
            z_ref[r, :] = z
            o_ref[r, :] = o
            ob_ref[r, :] = o.astype(BF16)

    row = pl.BlockSpec((tm, D), lambda i: (i, 0))
    vec = pl.BlockSpec((1, D), lambda i: (0, 0))
    return pl.pallas_call(
        body, grid=(T // tm,),
        in_specs=[pl.BlockSpec((tm, K), lambda i: (i, 0)), pl.BlockSpec((K, D), lambda i: (0, 0)), row, vec, vec],
        out_specs=[row, row, row],
        out_shape=[jax.ShapeDtypeStruct((T, D), F32), jax.ShapeDtypeStruct((T, D), F32),
                   jax.ShapeDtypeStruct((T, D), BF16)],
        compiler_params=_params("parallel"), name=name)(a, w, x, g, b)


def _ln_bwd_rows(zv, gain, dov, first, dz_ref, dzb_ref, dg_ref, db_ref):
    @pl.when(first)
    def _():
        dg_ref[...] = jnp.zeros_like(dg_ref)
        db_ref[...] = jnp.zeros_like(db_ref)

    mu = jnp.mean(zv, axis=1, keepdims=True)
    zc = zv - mu
    var = jnp.mean(zc * zc, axis=1, keepdims=True)
    rstd = lax.rsqrt(var + LN_EPS)
    xhat = zc * rstd
    dxhat = dov * gain
    m1 = jnp.mean(dxhat, axis=1, keepdims=True)
    m2 = jnp.mean(dxhat * xhat, axis=1, keepdims=True)
    dz = rstd * (dxhat - m1 - xhat * m2)
    dz_ref[...] = dz
    dzb_ref[...] = dz.astype(BF16)
    dg_ref[...] += jnp.sum(dov * xhat, axis=0, keepdims=True)
    db_ref[...] += jnp.sum(dov, axis=0, keepdims=True)


def _ln_bwd_outs(T, D):
    return [jax.ShapeDtypeStruct((T, D), F32), jax.ShapeDtypeStruct((T, D), BF16),
            jax.ShapeDtypeStruct((1, D), F32), jax.ShapeDtypeStruct((1, D), F32)]


def ln_bwd(z, g, do, name):
    T, D = z.shape
    tm = _tile(T, 512)

    def body(z_ref, g_ref, do_ref, dz_ref, dzb_ref, dg_ref, db_ref):
        _ln_bwd_rows(z_ref[...], g_ref[...], do_ref[...], pl.program_id(0) == 0, dz_ref, dzb_ref, dg_ref, db_ref)

    row = pl.BlockSpec((tm, D), lambda i: (i, 0))
    vec = pl.BlockSpec((1, D), lambda i: (0, 0))
    return pl.pallas_call(
        body, grid=(T // tm,), in_specs=[row, vec, row], out_specs=[row, row, vec, vec],
        out_shape=_ln_bwd_outs(T, D), compiler_params=_params("arbitrary"), name=name)(z, g, do)


def mm_nt(a, w, name):
    T, K = a.shape
    N = w.shape[0]
    tm, tn = _tile(T, 1024), _tile(N, 512)

    def body(a_ref, w_ref, o_ref):
        o_ref[...] = lax.dot_general(a_ref[...], w_ref[...], NT_DIMS, preferred_element_type=F32).astype(o_ref.dtype)

    return pl.pallas_call(
        body, grid=(T // tm, N // tn),
        in_specs=[pl.BlockSpec((tm, K), lambda i, j: (i, 0)), pl.BlockSpec((tn, K), lambda i, j: (j, 0))],
        out_specs=pl.BlockSpec((tm, tn), lambda i, j: (i, j)),
        out_shape=jax.ShapeDtypeStruct((T, N), BF16),
        compiler_params=_params("parallel", "parallel"), name=name)(a, w)


def ffn_bwd_mid(dzb, wd, h, u, scale, name):
    T, K = dzb.shape
    N = wd.shape[0]
    tm, tn = _tile(T, 512), _tile(N, 768)

    def body(dz_ref, w_ref, h_ref, u_ref, dh_ref, du_ref):
        dz = dz_ref[...]
        chunks = [slice(c, c + MXU_COLS) for c in range(0, tn, MXU_COLS)]
        das = [lax.dot_general(dz, w_ref[c, :], NT_DIMS, preferred_element_type=F32) for c in chunks]
        for c, da in zip(chunks, das):
            da = scale * da
            hv = h_ref[:, c].astype(F32)
            s = jax.nn.sigmoid(hv)
            silu = hv * s
            dh_ref[:, c] = (da * u_ref[:, c].astype(F32) * (s + silu * (1.0 - s))).astype(BF16)
            du_ref[:, c] = (da * silu).astype(BF16)

    tile = pl.BlockSpec((tm, tn), lambda j, i: (i, j))
    return pl.pallas_call(
        body, grid=(N // tn, T // tm),
        in_specs=[pl.BlockSpec((tm, K), lambda j, i: (i, 0)), pl.BlockSpec((tn, K), lambda j, i: (j, 0)), tile, tile],
        out_specs=[tile, tile],
        out_shape=[jax.ShapeDtypeStruct((T, N), BF16)] * 2,
        compiler_params=_params("parallel", "parallel"), name=name)(dzb, wd, h, u)


def mm_nt_res(pairs, dz, alpha, ln_before, name):
    T, N = pairs[0][0].shape
    D = pairs[0][1].shape[0]
    n = len(pairs)
    tm = _tile(T, 512 if n == 1 else 256)

    def body(*refs):
        a_refs, w_refs, dz_ref = refs[:n], refs[n:2 * n], refs[2 * n]
        d = lax.dot_general(a_refs[0][...], w_refs[0][...], NT_DIMS, preferred_element_type=F32)
        for p in range(1, n):
            d += lax.dot_general(a_refs[p][...], w_refs[p][...], NT_DIMS, preferred_element_type=F32)
        d = d + alpha * dz_ref[...]
        if ln_before is None:
            refs[2 * n + 1][...] = d
        else:
            z_ref, g_ref = refs[2 * n + 1:2 * n + 3]
            _ln_bwd_rows(z_ref[...], g_ref[...], d, pl.program_id(0) == 0, *refs[2 * n + 3:])

    row = pl.BlockSpec((tm, D), lambda i: (i, 0))
    vec = pl.BlockSpec((1, D), lambda i: (0, 0))
    in_specs = ([pl.BlockSpec((tm, N), lambda i: (i, 0))] * n
                + [pl.BlockSpec((D, N), functools.partial(lambda i, col: (0, col), col=p[2])) for p in pairs] + [row])
    operands = [p[0] for p in pairs] + [p[1] for p in pairs] + [dz]
    if ln_before is None:
        return pl.pallas_call(
            body, grid=(T // tm,), in_specs=in_specs, out_specs=row, out_shape=jax.ShapeDtypeStruct((T, D), F32),
            compiler_params=_params("parallel"), name=name)(*operands)
    return pl.pallas_call(
        body, grid=(T // tm,), in_specs=in_specs + [row, vec], out_specs=[row, row, vec, vec],
        out_shape=_ln_bwd_outs(T, D), compiler_params=_params("arbitrary"), name=name)(*operands, *ln_before)


def mm_tn(a, b, scale, dep, name, into=None, col=0):
    T, M = a.shape
    N = b.shape[1]
    tm, tn, tk = _tile(M, 512), _tile(N, 1024), _tile(T, 4096)
    nk = T // tk
    if into is None:
        into = lax.empty((M, N), BF16)

    def body(a_ref, b_ref, dep_ref, into_ref, o_ref, acc_ref):
        k = pl.program_id(2)

        @pl.when(k == 0)
        def _():
            acc_ref[...] = jnp.zeros_like(acc_ref)

        acc_ref[...] += lax.dot_general(a_ref[...], b_ref[...], TN_DIMS, preferred_element_type=F32)

        @pl.when(k == nk - 1)
        def _():
            o_ref[...] = (scale * acc_ref[...]).astype(BF16)

    return pl.pallas_call(
        body, grid=(M // tm, N // tn, nk),
        in_specs=[pl.BlockSpec((tk, tm), lambda i, j, k: (k, i)), pl.BlockSpec((tk, tn), lambda i, j, k: (k, j)), ANY, ANY],
        out_specs=pl.BlockSpec((tm, tn), lambda i, j, k: (i, col * (N // tn) + j)),
        out_shape=jax.ShapeDtypeStruct(into.shape, BF16), input_output_aliases={3: 0},
        scratch_shapes=[pltpu.VMEM((tm, tn), F32)],
        compiler_params=_params("parallel", "parallel", "arbitrary"), name=name)(a, b, dep, into)


def loss_head(y, target, name):
    T, D = y.shape
    tm = _tile(T, 512)

    def body(y_ref, t_ref, l_ref, dy_ref):
        @pl.when(pl.program_id(0) == 0)
        def _():
            l_ref[...] = jnp.zeros_like(l_ref)

        e = y_ref[...] - t_ref[...]
        dy_ref[...] = e * (1.0 / D)
        rows = jnp.sum(e * e, axis=1, keepdims=True) * (0.5 / D)
        l_ref[...] += jnp.sum(rows, axis=0, keepdims=True)

    row = pl.BlockSpec((tm, D), lambda i: (i, 0))
    return pl.pallas_call(
        body, grid=(T // tm,), in_specs=[row, row],
        out_specs=[pl.BlockSpec((1, 1), lambda i: (0, 0)), row],
        out_shape=[jax.ShapeDtypeStruct((1, 1), F32), jax.ShapeDtypeStruct((T, D), F32)],
        compiler_params=_params("arbitrary"), name=name)(y, target)


def _rel_index(i, j):
    return jnp.clip(i - j + LEFT_CHUNKS * CHUNK, -REL_CLIP, REL_CLIP) + REL_CLIP


def bias_band(table_t, name):
    H = table_t.shape[0]
    rows = SUBLANE

    def body(t_ref, o_ref):
        i0 = pl.program_id(0) * rows
        parts = _split3(t_ref[...])
        r = lax.broadcasted_iota(jnp.int32, (REL_PAD, A_W), 0)
        j = lax.broadcasted_iota(jnp.int32, (REL_PAD, A_W), 1)
        for ii in range(rows):
            onehot = jnp.where(r == _rel_index(i0 + ii, j), 1.0, 0.0).astype(BF16)
            o_ref[ii] = sum(jnp.dot(p, onehot, preferred_element_type=F32) for p in parts)

    return pl.pallas_call(
        body, grid=(A_TQ // rows,),
        in_specs=[pl.BlockSpec((H, REL_PAD), lambda i: (0, 0))],
        out_specs=pl.BlockSpec((rows, H, A_W), lambda i: (i, 0, 0)),
        out_shape=jax.ShapeDtypeStruct((A_TQ, H, A_W), F32),
        compiler_params=_params("parallel"), name=name)(table_t)


def bias_band_bwd(dbands, name):
    H = dbands[0].shape[1]
    rows = SUBLANE
    n = len(dbands)

    def body(*refs):
        g_refs, o_ref = refs[:n], refs[n]

        @pl.when(pl.program_id(0) == 0)
        def _():
            o_ref[...] = jnp.zeros_like(o_ref)

        i0 = pl.program_id(0) * rows
        j = lax.broadcasted_iota(jnp.int32, (A_W, REL_PAD), 0)
        r = lax.broadcasted_iota(jnp.int32, (A_W, REL_PAD), 1)
        acc = jnp.zeros((H, REL_PAD), F32)
        for ii in range(rows):
            onehot = jnp.where(r == _rel_index(i0 + ii, j), 1.0, 0.0).astype(BF16)
            g = g_refs[0][ii]
            for q in range(1, n):
                g = g + g_refs[q][ii]
            acc += _dot3(g, onehot)
        o_ref[...] += acc

    spec = pl.BlockSpec((rows, H, A_W), lambda i: (i, 0, 0))
    return pl.pallas_call(
        body, grid=(A_TQ // rows,), in_specs=[spec] * n,
        out_specs=pl.BlockSpec((H, REL_PAD), lambda i: (0, 0)),
        out_shape=jax.ShapeDtypeStruct((H, REL_PAD), F32),
        compiler_params=_params("arbitrary"), name=name)(*dbands)


def _a_window(ref, qi):
    parts = []
    for d in range(A_NWB):
        kb = jnp.maximum(qi - (A_NWB - 1) + d, 0)
        parts.append(ref[pl.ds(pl.multiple_of(kb * A_TQ, A_TQ), A_TQ), :])
    return jnp.concatenate(parts, axis=0)


def _a_valid(qi):
    i = lax.broadcasted_iota(jnp.int32, (A_TQ, A_W), 0)
    j = lax.broadcasted_iota(jnp.int32, (A_TQ, A_W), 1)
    ic, jc = i // CHUNK, j // CHUNK
    return (jc >= ic) & (jc <= ic + LEFT_CHUNKS) & (j >= LEFT_CHUNKS * CHUNK - qi * A_TQ)


def _head_masks():
    lane = lax.broadcasted_iota(jnp.int32, (1, LANE), 1)
    return [lane < HEAD_DIM, lane >= HEAD_DIM]


def _a_probs(qms, kws, valids, b_ref):
    scores = [lax.dot_general(qm, kws[i // 2], NT_DIMS, preferred_element_type=F32) for i, qm in enumerate(qms)]
    probs = []
    for i, s in enumerate(scores):
        s = jnp.where(valids[i // 2], s * (HEAD_DIM ** -0.5) + b_ref[i % 2], NEG)
        p = jnp.exp(s - jnp.max(s, axis=1, keepdims=True))
        probs.append(p / jnp.sum(p, axis=1, keepdims=True))
    return probs


def _a_subtiles(ref, masks):
    out = []
    for u in range(A_SUB):
        x = ref[u * A_TQ:(u + 1) * A_TQ, :]
        out += [jnp.where(hm, x, jnp.zeros_like(x)) for hm in masks]
    return out


def _attn_specs(S, D, tq):
    hp_n = D // LANE
    nq = S // tq
    q = pl.BlockSpec((tq, LANE), lambda hp, b, qi: (b * nq + qi, hp))
    k = pl.BlockSpec((S, LANE), lambda hp, b, qi: (b, hp_n + hp))
    v = pl.BlockSpec((S, LANE), lambda hp, b, qi: (b, 2 * hp_n + hp))
    tile = pl.BlockSpec((tq, LANE), lambda hp, b, qi: (b * nq + qi, hp))
    seq = pl.BlockSpec((S, LANE), lambda hp, b, qi: (b, hp))
    return q, k, v, tile, seq


def attn_a_fwd(qkv, band, S, name):
    T, D3 = qkv.shape
    D = D3 // 3
    nb, ng = T // S, S // (A_TQ * A_SUB)

    def body(q_ref, k_ref, v_ref, b_ref, o_ref):
        qis = [pl.program_id(2) * A_SUB + u for u in range(A_SUB)]
        masks = _head_masks()
        kws, vws = [_a_window(k_ref, qi) for qi in qis], [_a_window(v_ref, qi) for qi in qis]
        probs = _a_probs(_a_subtiles(q_ref, masks), kws, [_a_valid(qi) for qi in qis], b_ref)
        outs = [jnp.dot(p.astype(BF16), vws[i // 2], preferred_element_type=F32) for i, p in enumerate(probs)]
        for u in range(A_SUB):
            o_ref[u * A_TQ:(u + 1) * A_TQ, :] = jnp.where(masks[0], outs[2 * u], outs[2 * u + 1]).astype(BF16)

    q, k, v, tile, _ = _attn_specs(S, D, A_TQ * A_SUB)
    return pl.pallas_call(
        body, grid=(D // LANE, nb, ng),
        in_specs=[q, k, v, pl.BlockSpec((2, A_TQ, A_W), lambda hp, b, qi: (hp, 0, 0))],
        out_specs=tile, out_shape=jax.ShapeDtypeStruct((T, D), BF16),
        compiler_params=_params("parallel", "parallel", "parallel"), name=name)(qkv, qkv, qkv, band)


def attn_a_bwd(qkv, band, do, S, name):
    T, D3 = qkv.shape
    D = D3 // 3
    nb, ng = T // S, S // (A_TQ * A_SUB)
    scale = HEAD_DIM ** -0.5

    def body(q_ref, k_ref, v_ref, b_ref, do_ref, dq_ref, dk_ref, dv_ref, db_ref, dk_acc, dv_acc):
        b, qg = pl.program_id(1), pl.program_id(2)

        @pl.when((b == 0) & (qg == 0))
        def _():
            db_ref[...] = jnp.zeros_like(db_ref)

        @pl.when(qg == 0)
        def _():
            dk_acc[...] = jnp.zeros_like(dk_acc)
            dv_acc[...] = jnp.zeros_like(dv_acc)

        qis = [qg * A_SUB + u for u in range(A_SUB)]
        masks = _head_masks()
        kws, vws = [_a_window(k_ref, qi) for qi in qis], [_a_window(v_ref, qi) for qi in qis]
        qms, doms = _a_subtiles(q_ref, masks), _a_subtiles(do_ref, masks)
        dps = [lax.dot_general(dom, vws[i // 2], NT_DIMS, preferred_element_type=F32) for i, dom in enumerate(doms)]
        probs = _a_probs(qms, kws, [_a_valid(qi) for qi in qis], b_ref)
        dss = [p * (dp - jnp.sum(p * dp, axis=1, keepdims=True)) for p, dp in zip(probs, dps)]
        for hh in range(2):
            db_ref[hh] += sum(dss[hh::2])
        dsbs = [ds.astype(BF16) for ds in dss]
        dqs = [jnp.dot(dsb, kws[i // 2], preferred_element_type=F32) for i, dsb in enumerate(dsbs)]
        dks = [lax.dot_general(dsb, qm, TN_DIMS, preferred_element_type=F32) for dsb, qm in zip(dsbs, qms)]
        dvs = [lax.dot_general(p.astype(BF16), dom, TN_DIMS, preferred_element_type=F32) for p, dom in zip(probs, doms)]
        for u, qi in enumerate(qis):
            dq_ref[u * A_TQ:(u + 1) * A_TQ, :] = (jnp.where(masks[0], dqs[2 * u], dqs[2 * u + 1]) * scale).astype(BF16)
            dkw = (dks[2 * u] + dks[2 * u + 1]) * scale
            dvw = dvs[2 * u] + dvs[2 * u + 1]
            for d in range(A_NWB):
                kb = jnp.maximum(qi - (A_NWB - 1) + d, 0)
                rows = pl.ds(pl.multiple_of(kb * A_TQ, A_TQ), A_TQ)
                dk_acc[rows, :] += dkw[d * A_TQ:(d + 1) * A_TQ]
                dv_acc[rows, :] += dvw[d * A_TQ:(d + 1) * A_TQ]

        @pl.when(qg == ng - 1)
        def _():
            dk_ref[...] = dk_acc[...].astype(BF16)
            dv_ref[...] = dv_acc[...].astype(BF16)

    q, k, v, tile, seq = _attn_specs(S, D, A_TQ * A_SUB)
    bspec = pl.BlockSpec((2, A_TQ, A_W), lambda hp, b, qi: (hp, 0, 0))
    act = jax.ShapeDtypeStruct((T, D), BF16)
    return pl.pallas_call(
        body, grid=(D // LANE, nb, ng),
        in_specs=[q, k, v, bspec, tile], out_specs=[tile, seq, seq, bspec],
        out_shape=[act, act, act, jax.ShapeDtypeStruct(band.shape, F32)],
        scratch_shapes=[pltpu.VMEM((S, LANE), F32), pltpu.VMEM((S, LANE), F32)],
        compiler_params=_params("arbitrary", "arbitrary", "arbitrary"), name=name)(qkv, qkv, qkv, band, do)


def _dot2(v, w):
    h = v.astype(BF16)
    lo = (v - h.astype(F32)).astype(BF16)
    return jnp.dot(h, w, preferred_element_type=F32) + jnp.dot(lo, w, preferred_element_type=F32)


def _b_weights(qms, kts, rights, after, diagonal):
    inst = [(t, hh) for t in range(len(kts)) for hh in range(2)]
    zs = [lax.dot_general(qms[hh], kts[t], NT_DIMS, preferred_element_type=F32) for t, hh in inst]
    lbs, l1ms, his, los = [], [], [], []
    causal = _strictly_causal() if any(diagonal) else None
    for (t, hh), z in zip(inst, zs):
        lb = jnp.minimum(z, 0.0) - jnp.log(1.0 + jnp.exp(jnp.minimum(z, -z)))
        lbs.append(lb)
        l1m = lb - z
        if diagonal[t]:
            l1m = jnp.where(causal, l1m, 0.0)
        hi = l1m.astype(BF16)
        l1ms.append(l1m)
        his.append(hi)
        los.append((l1m - hi.astype(F32)).astype(BF16))
    sums = [jnp.dot(hi, after, preferred_element_type=F32) + jnp.dot(lo, after, preferred_element_type=F32)
            for hi, lo in zip(his, los)]
    rights = list(rights)
    weights = []
    for (t, hh), lb, l1m, c in zip(inst, lbs, l1ms, sums):
        a = jnp.exp(lb + (rights[hh] + c))
        weights.append(jnp.where(causal, a, 0.0) if diagonal[t] else a)
        rights[hh] = rights[hh] + (c[:, :1] + l1m[:, :1])
    return zs, weights, rights


def _strictly_causal():
    row = lax.broadcasted_iota(jnp.int32, (B_T, B_T), 0)
    col = lax.broadcasted_iota(jnp.int32, (B_T, B_T), 1)
    return col < row


def _tri(strict_lower):
    r = lax.broadcasted_iota(jnp.int32, (B_T, B_T), 0)
    c = lax.broadcasted_iota(jnp.int32, (B_T, B_T), 1)
    return jnp.where((r > c) if strict_lower else (r < c), 1.0, 0.0).astype(BF16)


def _scaled_heads(q2):
    qs = q2 * (HEAD_DIM ** -0.5)
    return [jnp.where(hm, qs, jnp.zeros_like(qs)) for hm in _head_masks()]


def attn_b_fwd(qkv, S, name):
    T, D3 = qkv.shape
    D = D3 // 3
    nb, nq = T // S, S // B_T

    def body(q_ref, k_ref, v_ref, o_ref):
        qi = pl.program_id(2)
        after = _tri(True)
        qms = _scaled_heads(q_ref[...])

        def tiles(kbs, carry, diagonal):
            rows = [pl.ds(pl.multiple_of(kb * B_T, B_T), B_T) for kb in kbs]
            kts, vts = [k_ref[r, :] for r in rows], [v_ref[r, :] for r in rows]
            _, weights, rights = _b_weights(qms, kts, (carry[0], carry[2]), after, diagonal)
            accs = [carry[1], carry[3]]
            for i, a in enumerate(weights):
                accs[i % 2] = accs[i % 2] + jnp.dot(a.astype(BF16), vts[i // 2], preferred_element_type=F32)
            return rights[0], accs[0], rights[1], accs[1]

        init = (jnp.zeros((B_T, 1), F32), jnp.zeros((B_T, LANE), F32)) * 2
        res = lax.cond(qi % 2 == 1, lambda c: tiles([qi, qi - 1], c, [True, False]), lambda c: tiles([qi], c, [True]), init)
        top = qi - 1 - qi % 2
        res = lax.fori_loop(0, qi // 2, lambda p, c: tiles([top - 2 * p, top - 1 - 2 * p], c, [False, False]), res)
        o_ref[...] = jnp.where(_head_masks()[0], res[1], res[3]).astype(BF16)

    q, k, v, tile, _ = _attn_specs(S, D, B_T)
    return pl.pallas_call(
        body, grid=(D // LANE, nb, nq), in_specs=[q, k, v], out_specs=tile,
        out_shape=jax.ShapeDtypeStruct((T, D), BF16),
        compiler_params=_params("parallel", "parallel", "parallel"), name=name)(qkv, qkv, qkv)


def attn_b_bwd(qkv, do, S, name):
    T, D3 = qkv.shape
    D = D3 // 3
    nb, nq = T // S, S // B_T
    scale = HEAD_DIM ** -0.5

    def body(q_ref, k_ref, v_ref, do_ref, dq_ref, dk_ref, dv_ref, dk_acc, dv_acc, z_s, g_s):
        qi = pl.program_id(2)

        @pl.when(qi == 0)
        def _():
            dk_acc[...] = jnp.zeros_like(dk_acc)
            dv_acc[...] = jnp.zeros_like(dv_acc)

        do2 = do_ref[...]
        after, before = _tri(True), _tri(False)
        masks = _head_masks()
        qms = _scaled_heads(q_ref[...])
        doms = [jnp.where(hm, do2, jnp.zeros_like(do2)) for hm in masks]

        def sweep_left(kbs, rights, diagonal):
            rows = [pl.ds(pl.multiple_of(kb * B_T, B_T), B_T) for kb in kbs]
            kts, vts = [k_ref[r, :] for r in rows], [v_ref[r, :] for r in rows]
            das = [lax.dot_general(doms[hh], vt, NT_DIMS, preferred_element_type=F32) for vt in vts for hh in range(2)]
            zs, weights, rights = _b_weights(qms, kts, rights, after, diagonal)
            for i, (z, a, da) in enumerate(zip(zs, weights, das)):
                z_s[i % 2, kbs[i // 2]] = z
                g_s[i % 2, kbs[i // 2]] = da * a
            dvs = [lax.dot_general(a.astype(BF16), doms[i % 2], TN_DIMS, preferred_element_type=F32)
                   for i, a in enumerate(weights)]
            for t, r in enumerate(rows):
                dv_acc[r, :] += dvs[2 * t] + dvs[2 * t + 1]
            return tuple(rights)

        zero_col = jnp.zeros((B_T, 1), F32)
        odd = qi % 2 == 1
        top = qi - 1 - qi % 2
        rights = lax.cond(odd, lambda c: sweep_left([qi, qi - 1], c, [True, False]), lambda c: sweep_left([qi], c, [True]),
                          (zero_col, zero_col))
        lax.fori_loop(0, qi // 2, lambda p, c: sweep_left([top - 2 * p, top - 1 - 2 * p], c, [False, False]), rights)

        def sweep_right(kbs, carry, diagonal):
            rows = [pl.ds(pl.multiple_of(kb * B_T, B_T), B_T) for kb in kbs]
            inst = [(t, hh) for t in range(len(kbs)) for hh in range(2)]
            gs = [g_s[hh, kbs[t]] for t, hh in inst]
            sums = [_dot2(g, before) for g in gs]
            lefts, dqs = [carry[0], carry[2]], [carry[1], carry[3]]
            dzbs = []
            for (t, hh), g, c in zip(inst, gs, sums):
                beta = jax.nn.sigmoid(z_s[hh, kbs[t]])
                dz = g * (1.0 - beta) - beta * (lefts[hh] + c)
                if diagonal[t]:
                    dz = jnp.where(_strictly_causal(), dz, 0.0)
                dzbs.append(dz.astype(BF16))
                lefts[hh] = lefts[hh] + jnp.sum(g, axis=1, keepdims=True)
            dks = [lax.dot_general(dzb, qms[hh], TN_DIMS, preferred_element_type=F32) for (t, hh), dzb in zip(inst, dzbs)]
            for (t, hh), dzb in zip(inst, dzbs):
                dqs[hh] = dqs[hh] + jnp.dot(dzb, k_ref[rows[t], :], preferred_element_type=F32)
            for t, r in enumerate(rows):
                dk_acc[r, :] += dks[2 * t] + dks[2 * t + 1]
            return lefts[0], dqs[0], lefts[1], dqs[1]

        init = (zero_col, jnp.zeros((B_T, LANE), F32)) * 2
        res = lax.fori_loop(0, qi // 2, lambda p, c: sweep_right([2 * p, 2 * p + 1], c, [False, False]), init)
        res = lax.cond(odd, lambda c: sweep_right([qi - 1, qi], c, [False, True]), lambda c: sweep_right([qi], c, [True]), res)
        dq_ref[...] = (jnp.where(masks[0], res[1], res[3]) * scale).astype(BF16)

        @pl.when(qi == nq - 1)
        def _():
            dk_ref[...] = dk_acc[...].astype(BF16)
            dv_ref[...] = dv_acc[...].astype(BF16)

    q, k, v, tile, seq = _attn_specs(S, D, B_T)
    act = jax.ShapeDtypeStruct((T, D), BF16)
    return pl.pallas_call(
        body, grid=(D // LANE, nb, nq),
        in_specs=[q, k, v, tile], out_specs=[tile, seq, seq], out_shape=[act, act, act],
        scratch_shapes=[pltpu.VMEM((S, LANE), F32), pltpu.VMEM((S, LANE), F32),
                        pltpu.VMEM((2, nq, B_T, B_T), F32), pltpu.VMEM((2, nq, B_T, B_T), F32)],
        compiler_params=_params("arbitrary", "arbitrary", "arbitrary"), name=name)(qkv, qkv, qkv, do)


HBM = pl.BlockSpec(memory_space=pltpu.HBM)
SEM = pl.BlockSpec(memory_space=pltpu.SEMAPHORE)
N_PEERS = N_DEV - 1
GATHERS_AHEAD = 3


def _place():
    return lax.axis_index("x"), lax.axis_index("y"), lax.axis_index("c")


def _peers(x, y, c):
    return [(1 - x if r & 4 else x, 1 - y if r & 2 else y, 1 - c if r & 1 else c) for r in range(1, N_DEV)]


def _block(ref, shape, by_cols, j):
    r, w = shape
    if by_cols:
        return ref.at[:, pl.ds(pl.multiple_of(j * w, LANE), w)]
    return ref.at[pl.ds(pl.multiple_of(j * r, SUBLANE), r), :]


def _exchange_copies(gather, src_refs, land_refs, send_sems, recv_sems, by_cols):
    x, y, c = _place()
    me = 4 * x + 2 * y + c
    out = []
    for t, (src, land) in enumerate(zip(src_refs, land_refs)):
        for r, peer in enumerate(_peers(x, y, c)):
            pj = 4 * peer[0] + 2 * peer[1] + peer[2]
            if gather:
                mine, to_me, theirs = src, _block(land, src.shape, by_cols[t], me), _block(land, src.shape, by_cols[t], pj)
            else:
                mine, to_me, theirs = _block(src, land.shape[1:], by_cols[t], pj), land.at[me], land.at[pj]
            sems = dict(send_sem=send_sems.at[N_PEERS * t + r], recv_sem=recv_sems.at[N_PEERS * t + r],
                        device_id=peer, device_id_type=MESH_ID)
            out.append((pltpu.make_async_remote_copy(src_ref=mine, dst_ref=to_me, **sems),
                        pltpu.make_async_remote_copy(src_ref=mine, dst_ref=theirs, **sems)))
    return out


def _own_copies(gather, src_refs, land_refs, own_sems, by_cols):
    x, y, c = _place()
    me = 4 * x + 2 * y + c
    out = []
    for t, (src, land) in enumerate(zip(src_refs, land_refs)):
        if gather:
            out.append(pltpu.make_async_copy(src, _block(land, src.shape, by_cols[t], me), own_sems.at[t]))
        else:
            out.append(pltpu.make_async_copy(_block(src, land.shape[1:], by_cols[t], me), land.at[me], own_sems.at[t]))
    return out


def exchange_start(gather, srcs, land_shapes, by_cols, after, name):
    n = len(srcs)

    def body(*refs):
        src_refs, land_refs = refs[:n], refs[n:2 * n]
        send_sems, recv_sems, own_sems = refs[2 * n + 1:2 * n + 4]
        token = refs[-1]
        for cp in _own_copies(gather, src_refs, land_refs, own_sems, by_cols):
            cp.start()
        for mine, _ in _exchange_copies(gather, src_refs, land_refs, send_sems, recv_sems, by_cols):
            mine.start()
        token[...] = jnp.zeros_like(token)

    lands = [pltpu.with_memory_space_constraint(lax.empty(s.shape, s.dtype), pltpu.HBM) for s in land_shapes]
    srcs = [pltpu.with_memory_space_constraint(s, pltpu.HBM) for s in srcs]
    res = pl.pallas_call(
        body, name=name,
        out_shape=(pltpu.SemaphoreType.DMA((N_PEERS * n,)), pltpu.SemaphoreType.DMA((N_PEERS * n,)),
                   pltpu.SemaphoreType.DMA((n,)),
                   *[pltpu.HBM(s.shape, s.dtype) for s in srcs], *[pltpu.HBM(s.shape, s.dtype) for s in land_shapes],
                   jax.ShapeDtypeStruct((SUBLANE, LANE), F32)),
        in_specs=[HBM] * (2 * n) + [ANY],
        out_specs=(SEM, SEM, SEM, *[HBM] * (2 * n), pl.BlockSpec(memory_space=pltpu.VMEM)),
        input_output_aliases={i: 3 + i for i in range(2 * n)},
        compiler_params=pltpu.CompilerParams(has_side_effects=pltpu.SideEffectType.DATAFLOW_SIDE_EFFECTING),
    )(*srcs, *lands, after)
    return dict(gather=gather, n=n, by_cols=by_cols, sems=res[:3], srcs=res[3:3 + n],
                lands=res[3 + n:3 + 2 * n], token=res[-1])


def exchange_wait(started, after, name):
    n, gather, by_cols = started["n"], started["gather"], started["by_cols"]

    def body(*refs):
        src_refs, land_refs = refs[:n], refs[n:2 * n]
        send_sems, recv_sems, own_sems = refs[2 * n:2 * n + 3]
        for mine, theirs in _exchange_copies(gather, src_refs, land_refs, send_sems, recv_sems, by_cols):
            mine.wait_send()
            theirs.wait_recv()
        for cp in _own_copies(gather, src_refs, land_refs, own_sems, by_cols):
            cp.wait()

    res = pl.pallas_call(
        body, name=name,
        out_shape=tuple(pltpu.HBM(s.shape, s.dtype) for s in (*started["srcs"], *started["lands"])),
        in_specs=[HBM] * (2 * n) + [SEM, SEM, SEM, ANY], out_specs=tuple([HBM] * (2 * n)),
        input_output_aliases={i: i for i in range(2 * n)},
        compiler_params=pltpu.CompilerParams(has_side_effects=pltpu.SideEffectType.DATAFLOW_SIDE_EFFECTING),
    )(*started["srcs"], *started["lands"], *started["sems"], after)
    return res[n:]


def gather_small(v, dep, name):
    R, C = v.shape

    def body(v_ref, dep_ref, o_ref, send_sems, recv_sems):
        x, y, c = _place()
        o_ref[4 * x + 2 * y + c] = v_ref[...]
        peers = _peers(x, y, c)

        def copy(r, owner, to):
            slot = o_ref.at[4 * owner[0] + 2 * owner[1] + owner[2]]
            return pltpu.make_async_remote_copy(
                src_ref=slot, dst_ref=slot, send_sem=send_sems.at[r], recv_sem=recv_sems.at[r],
                device_id=to, device_id_type=MESH_ID)

        sends = [copy(r, (x, y, c), peer) for r, peer in enumerate(peers)]
        for cp in sends:
            cp.start()
        for r, peer in enumerate(peers):
            copy(r, peer, (x, y, c)).wait_recv()
        for cp in sends:
            cp.wait_send()

    vm = pl.BlockSpec(memory_space=pltpu.VMEM)
    return pl.pallas_call(
        body, in_specs=[vm, ANY], out_specs=vm, out_shape=jax.ShapeDtypeStruct((N_DEV, R, C), F32),
        scratch_shapes=[pltpu.SemaphoreType.DMA((N_PEERS,)), pltpu.SemaphoreType.DMA((N_PEERS,))],
        name=name)(v, dep)


def sum_devices(g, name):
    _, R, C = g.shape

    def body(g_ref, o_ref):
        acc = g_ref[0]
        for j in range(1, N_DEV):
            acc = acc + g_ref[j]
        o_ref[...] = acc

    vm = pl.BlockSpec(memory_space=pltpu.VMEM)
    return pl.pallas_call(body, in_specs=[vm], out_specs=vm, out_shape=jax.ShapeDtypeStruct((R, C), F32), name=name)(g)


def adamw(w, m, v, index, parts, outs, name):
    L, R, C = w.shape
    if outs is None:
        outs = [lax.empty((L, R, C), F32) for _ in range(4)]
    P, Rp, Cp = parts.shape
    tr = _tile(R, 512, SUBLANE) if Rp == R else R

    def body(w_ref, m_ref, v_ref, p_ref, g_in, d_in, m_in, v_in, g_out, d_out, m_out, v_out):
        g = p_ref[0, :tr, :C].astype(F32)
        for q in range(1, P):
            g = g + p_ref[q, :tr, :C].astype(F32)
        mn = ADAM_B1 * m_ref[...] + (1.0 - ADAM_B1) * g
        vn = ADAM_B2 * v_ref[...] + (1.0 - ADAM_B2) * (g * g)
        m_hat = mn / (1.0 - ADAM_B1 ** ADAM_STEP)
        v_hat = vn / (1.0 - ADAM_B2 ** ADAM_STEP)
        g_out[...] = g
        d_out[...] = -ADAM_LR * (m_hat / (jnp.sqrt(v_hat) + ADAM_EPS) + ADAM_WD * w_ref[...])
        m_out[...] = mn
        v_out[...] = vn

    slab = pl.BlockSpec((None, tr, C), lambda r: (index, r, 0))
    pspec = pl.BlockSpec((P, tr if Rp == R else Rp, Cp), lambda r: (0, r, 0))
    return pl.pallas_call(
        body, grid=(R // tr,), in_specs=[slab] * 3 + [pspec] + [ANY] * 4, out_specs=[slab] * 4,
        out_shape=[jax.ShapeDtypeStruct((L, R, C), F32)] * 4,
        input_output_aliases={4 + q: q for q in range(4)},
        compiler_params=_params("parallel"), name=name)(w, m, v, parts, *outs)


def _pad_to(a, axis, size):
    pad = [(0, 0)] * a.ndim
    pad[axis] = (0, size - a.shape[axis])
    return jnp.pad(a, pad)


def kernel(x, w_qkv_a, w_o_a, rel_bias, w_qkv_b, w_o_b, ffn_w_gate, ffn_w_up, ffn_w_down, ln_g, ln_b, loss_target, m_w_qkv_a, m_w_o_a, m_rel_bias, m_w_qkv_b, m_w_o_b, m_ffn_w_gate, m_ffn_w_up, m_ffn_w_down, m_ln_g, m_ln_b, v_w_qkv_a, v_w_o_a, v_rel_bias, v_w_qkv_b, v_w_o_b, v_ffn_w_gate, v_ffn_w_up, v_ffn_w_down, v_ln_g, v_ln_b):
    nb, S, D = x.shape
    T = nb * S
    depth = ffn_w_gate.shape[0]
    H = D // HEAD_DIM
    fs = ffn_w_gate.shape[-1]
    fp = -(-fs // LANE) * LANE
    alpha = (2.0 * depth) ** 0.25
    cx, cy, cc = _place()
    me = 4 * cx + 2 * cy + cc

    ln_local = jnp.concatenate([ln_g.reshape(depth * 3, -1), ln_b.reshape(depth * 3, -1)], axis=0)
    ln_all = gather_small(ln_local, ln_local, "gather_ln")
    ln_full = jnp.transpose(ln_all, (1, 0, 2)).reshape(2 * depth * 3, D)
    ln_gain = lambda i, s: ln_full[3 * i + s][None, :]
    ln_bias = lambda i, s: ln_full[3 * depth + 3 * i + s][None, :]

    table_t = _pad_to(rel_bias.T, 1, REL_PAD)
    band = jnp.transpose(bias_band(table_t, "bias_band"), (1, 0, 2))

    subs = []
    for i in range(depth):
        for s in (0, 1, 2):
            if s == 1:
                wq, wo = (w_qkv_a, w_o_a) if i % 2 == 0 else (w_qkv_b, w_o_b)
                subs.append(([wq[i // 2].astype(BF16), wo[i // 2].astype(BF16)], [True, False]))
            else:
                f = 0 if s == 0 else 1
                subs.append(([_pad_to(ffn_w_gate[i, f].astype(BF16), 1, fp), _pad_to(ffn_w_up[i, f].astype(BF16), 1, fp),
                              _pad_to(ffn_w_down[i, f].astype(BF16), 0, fp)], [True, True, False]))

    def start_gather(k, after):
        shards, by_cols = subs[k]
        groups = [(0, 2), (2, 3)] if k == 0 else [(0, len(shards))]
        out = []
        for part, (lo, hi) in enumerate(groups):
            shapes = [jax.ShapeDtypeStruct((s.shape[0], N_DEV * s.shape[1]) if col else (N_DEV * s.shape[0], s.shape[1]), BF16)
                      for s, col in zip(shards[lo:hi], by_cols[lo:hi])]
            out.append(exchange_start(True, shards[lo:hi], shapes, by_cols[lo:hi], after, f"gather_start_{k}_{part}"))
            after = out[-1]["token"]
        return out

    xf = x.reshape(T, D)
    act, act_b = xf, xf.astype(BF16)
    gathers = {}
    for k in range(min(GATHERS_AHEAD, len(subs))):
        gathers[k] = start_gather(k, xf if k == 0 else gathers[k - 1][-1]["token"])
    newest = gathers[k][-1]["token"]
    saved = []
    for i in range(depth):
        layer = {}
        for s in (0, 1, 2):
            k = 3 * i + s
            tag = f"L{i}S{s}"
            if 0 < k and k + GATHERS_AHEAD - 1 < len(subs):
                gathers[k + GATHERS_AHEAD - 1] = start_gather(k + GATHERS_AHEAD - 1, act_b)
                newest = gathers[k + GATHERS_AHEAD - 1][-1]["token"]
            parts = gathers.pop(k)
            full = list(exchange_wait(parts[0], newest, f"gather_wait_{k}_0"))
            if s == 1:
                wqkv_f, wo_f = full
                qkv = mm_nn(act_b, wqkv_f, "qkv_" + tag)
                if i % 2 == 0:
                    att = attn_a_fwd(qkv, band, S, "attn_a_fwd_" + tag)
                else:
                    att = attn_b_fwd(qkv, S, "attn_b_fwd_" + tag)
                z, o, ob = mm_res_ln(att, wo_f, act, ln_gain(i, s), ln_bias(i, s), alpha, 1.0, "out_ln_" + tag)
                layer[s] = dict(x_b=act_b, qkv=qkv, att=att, z=z, wqkv=wqkv_f, wo=wo_f)
            else:
                wg_f, wu_f = full[:2]
                h, u, a = ffn_up(act_b, wg_f, wu_f, "ffn_up_" + tag)
                wd_f = full[2] if len(parts) == 1 else exchange_wait(parts[1], a, f"gather_wait_{k}_1")[0]
                z, o, ob = mm_res_ln(a, wd_f, act, ln_gain(i, s), ln_bias(i, s), alpha, 0.5, "down_ln_" + tag)
                layer[s] = dict(x_b=act_b, h=h, u=u, a=a, z=z, wg=wg_f, wu=wu_f, wd=wd_f)
            act, act_b = o, ob
        saved.append(layer)

    loss_local, d_act = loss_head(act, loss_target.reshape(T, D), "loss_head")
    loss = lax.psum(loss_local[0, 0], ("x", "y", "c"))

    results = {}

    def update(name, w, m, v, index, parts):
        L = w.shape[0] if w.ndim == 3 else w.shape[0] * w.shape[1]
        flat = lambda t: t.reshape((L,) + t.shape[-2:])
        results[name] = adamw(flat(w), flat(m), flat(v), index, parts, results.get(name), f"adamw_{name}_{index}")

    def finish(entry, after):
        exchange, targets, tag = entry
        lands = exchange_wait(exchange, after, "scatter_wait_" + tag)
        for (name, w, m, v, index), parts in zip(targets, lands):
            update(name, w, m, v, index, parts)
        return results[targets[-1][0]][0]

    pending = []
    started = d_act
    dbands = []
    dln_g = [None] * (3 * depth)
    dln_b = [None] * (3 * depth)
    ln_grads = ln_bwd(saved[depth - 1][2]["z"], ln_gain(depth - 1, 2), d_act, "ln_bwd_last")
    for i in reversed(range(depth)):
        for s in (2, 1, 0):
            tag = f"L{i}S{s}"
            sv = saved[i][s]
            dz, dzb, dln_g[3 * i + s], dln_b[3 * i + s] = ln_grads
            before = (i, s - 1) if s > 0 else (i - 1, 2) if i > 0 else None
            ln_before = None if before is None else (saved[before[0]][before[1]]["z"], ln_gain(*before))
            if s == 1:
                j = i // 2
                d_att = mm_nt(dzb, sv["wo"], "att_bwd_" + tag)
                g_wo = mm_tn(sv["att"], dzb, 1.0, started, "dwo_" + tag)
                if i % 2 == 0:
                    dq, dk, dv, dband = attn_a_bwd(sv["qkv"], band, d_att, S, "attn_a_bwd_" + tag)
                    dbands.append(jnp.transpose(dband, (1, 0, 2)))
                else:
                    dq, dk, dv = attn_b_bwd(sv["qkv"], d_att, S, "attn_b_bwd_" + tag)
                g_wqkv = lax.empty((D, 3 * D), BF16)
                for col, (piece, d_piece) in enumerate(zip("qkv", (dq, dk, dv))):
                    g_wqkv = mm_tn(sv["x_b"], d_piece, 1.0, started, f"dw{piece}_" + tag, g_wqkv, col)
                ln_grads = mm_nt_res([(dq, sv["wqkv"], 0), (dk, sv["wqkv"], 1), (dv, sv["wqkv"], 2)], dz, alpha, ln_before,
                                     "dx_mix_" + tag)
                grads, by_cols = [g_wqkv, g_wo], [True, False]
                slabs = [(D, 3 * D // N_DEV), (D // N_DEV, D)]
                if i % 2 == 0:
                    targets = [("w_qkv_a", w_qkv_a, m_w_qkv_a, v_w_qkv_a, j), ("w_o_a", w_o_a, m_w_o_a, v_w_o_a, j)]
                else:
                    targets = [("w_qkv_b", w_qkv_b, m_w_qkv_b, v_w_qkv_b, j), ("w_o_b", w_o_b, m_w_o_b, v_w_o_b, j)]
            else:
                f = 0 if s == 0 else 1
                dh, du = ffn_bwd_mid(dzb, sv["wd"], sv["h"], sv["u"], 0.5, "ffn_mid_" + tag)
                g_wd = mm_tn(sv["a"], dzb, 0.5, started, "dwd_" + tag)
                g_wg = mm_tn(sv["x_b"], dh, 1.0, started, "dwg_" + tag)
                g_wu = mm_tn(sv["x_b"], du, 1.0, started, "dwu_" + tag)
                ln_grads = mm_nt_res([(dh, sv["wg"], 0), (du, sv["wu"], 0)], dz, alpha, ln_before, "dx_ffn_" + tag)
                grads, by_cols = [g_wg, g_wu, g_wd], [True, True, False]
                slabs = [(D, fp), (D, fp), (fp, D)]
                idx = 2 * i + f
                targets = [("ffn_w_gate", ffn_w_gate, m_ffn_w_gate, v_ffn_w_gate, idx),
                           ("ffn_w_up", ffn_w_up, m_ffn_w_up, v_ffn_w_up, idx),
                           ("ffn_w_down", ffn_w_down, m_ffn_w_down, v_ffn_w_down, idx)]
            after = ln_grads[0] if before is not None else ln_grads
            if before is None:
                grad_x = ln_grads.reshape(nb, S, D)
                dtable_t = bias_band_bwd(dbands, "bias_band_bwd")
                small = jnp.concatenate(dln_g + dln_b + [_pad_to(dtable_t, 1, D)], axis=0)
                small = _pad_to(small, 0, -(-small.shape[0] // SUBLANE) * SUBLANE)
                total = after = sum_devices(gather_small(small, ln_grads, "gather_small_grads"), "sum_small_grads")
            shapes = [jax.ShapeDtypeStruct((N_DEV,) + slab, BF16) for slab in slabs]
            pending.append((exchange_start(False, grads, shapes, by_cols, after, "scatter_start_" + tag), targets, tag))
            started = pending[-1][0]["token"]

    last = pending.pop()
    done = started
    for entry in pending:
        done = finish(entry, done)
    finish(last, done)
    n_ln = 3 * depth
    g_ln_g = lax.dynamic_slice_in_dim(total[:n_ln], me * (D // N_DEV), D // N_DEV, axis=1)
    g_ln_b = lax.dynamic_slice_in_dim(total[n_ln:2 * n_ln], me * (D // N_DEV), D // N_DEV, axis=1)
    g_rel = total[2 * n_ln:2 * n_ln + H, :N_REL].T
    as3 = lambda t: t.reshape((1, -1, t.shape[-1]))
    results["ln_g"] = adamw(as3(ln_g), as3(m_ln_g), as3(v_ln_g), 0, g_ln_g[None], None, "adamw_ln_g")
    results["ln_b"] = adamw(as3(ln_b), as3(m_ln_b), as3(v_ln_b), 0, g_ln_b[None], None, "adamw_ln_b")
    results["rel_bias"] = adamw(as3(rel_bias), as3(m_rel_bias), as3(v_rel_bias), 0, g_rel[None], None, "adamw_rel_bias")

    order = [("w_qkv_a", w_qkv_a), ("w_o_a", w_o_a), ("rel_bias", rel_bias), ("w_qkv_b", w_qkv_b), ("w_o_b", w_o_b),
             ("ffn_w_gate", ffn_w_gate), ("ffn_w_up", ffn_w_up), ("ffn_w_down", ffn_w_down), ("ln_g", ln_g), ("ln_b", ln_b)]
    outs = [loss, grad_x]
    for q in range(4):
        for name, like in order:
            outs.append(results[name][q].reshape(like.shape))
    return tuple(outs)
```

```python
import functools

import jax
import jax.numpy as jnp
from jax import lax
from jax.experimental import pallas as pl
from jax.experimental.pallas import tpu as pltpu

BF16 = jnp.bfloat16
F32 = jnp.float32
MESH_ID = pl.DeviceIdType.MESH
ANY = pl.BlockSpec(memory_space=pl.ANY)

N_DEV = 8
LANE = 128
MXU_COLS = 256
SUBLANE = 8
VMEM_LIMIT = 56 * 1024 * 1024

HEAD_DIM = 64
CHUNK = 64
LEFT_CHUNKS = 8
REL_CLIP = 128
N_REL = 2 * REL_CLIP + 1
REL_PAD = 384
LN_EPS = 1e-5
ADAM_LR, ADAM_B1, ADAM_B2, ADAM_EPS, ADAM_WD, ADAM_STEP = 0.001, 0.9, 0.999, 1e-08, 0.01, 10
NEG = -1e30

A_TQ = 128
A_NWB = LEFT_CHUNKS * CHUNK // A_TQ + 1
A_W = A_NWB * A_TQ
A_SUB = 4
B_T = 256

NT_DIMS = (((1,), (1,)), ((), ()))
TN_DIMS = (((0,), (0,)), ((), ()))


def _tile(n, pref, unit=LANE):
    if n <= pref:
        return n
    t = pref - pref % unit
    while t > unit and n % t:
        t -= unit
    assert n % t == 0, (n, pref)
    return t


def _params(*sem):
    return pltpu.CompilerParams(dimension_semantics=sem, vmem_limit_bytes=VMEM_LIMIT)


def _split3(v):
    h = v.astype(BF16)
    r = v - h.astype(F32)
    m = r.astype(BF16)
    lo = (r - m.astype(F32)).astype(BF16)
    return h, m, lo


def _dot3(v, w):
    h, m, lo = _split3(v)
    return (jnp.dot(h, w, preferred_element_type=F32) + jnp.dot(m, w, preferred_element_type=F32)
            + jnp.dot(lo, w, preferred_element_type=F32))


def mm_nn(a, w, name):
    T, K = a.shape
    N = w.shape[1]
    tm, tn = _tile(T, 1024), _tile(N, 768)

    def body(a_ref, w_ref, o_ref):
        o_ref[...] = jnp.dot(a_ref[...], w_ref[...], preferred_element_type=F32).astype(o_ref.dtype)

    return pl.pallas_call(
        body, grid=(T // tm, N // tn),
        in_specs=[pl.BlockSpec((tm, K), lambda i, j: (i, 0)), pl.BlockSpec((K, tn), lambda i, j: (0, j))],
        out_specs=pl.BlockSpec((tm, tn), lambda i, j: (i, j)),
        out_shape=jax.ShapeDtypeStruct((T, N), BF16),
        compiler_params=_params("parallel", "parallel"), name=name)(a, w)


def ffn_up(xb, wg, wu, name):
    T, K = xb.shape
    N = wg.shape[1]
    tm, tn = _tile(T, 512), _tile(N, 768)

    def body(x_ref, wg_ref, wu_ref, h_ref, u_ref, a_ref):
        x = x_ref[...]
        chunks = [slice(c, c + MXU_COLS) for c in range(0, tn, MXU_COLS)]
        hs = [jnp.dot(x, wg_ref[:, c], preferred_element_type=F32) for c in chunks]
        us = [jnp.dot(x, wu_ref[:, c], preferred_element_type=F32) for c in chunks]
        for c, h, u in zip(chunks, hs, us):
            h_ref[:, c] = h.astype(BF16)
            u_ref[:, c] = u.astype(BF16)
            a_ref[:, c] = (h * jax.nn.sigmoid(h) * u).astype(BF16)

    wspec = pl.BlockSpec((K, tn), lambda j, i: (0, j))
    ospec = pl.BlockSpec((tm, tn), lambda j, i: (i, j))
    return pl.pallas_call(
        body, grid=(N // tn, T // tm),
        in_specs=[pl.BlockSpec((tm, K), lambda j, i: (i, 0)), wspec, wspec],
        out_specs=[ospec, ospec, ospec],
        out_shape=[jax.ShapeDtypeStruct((T, N), BF16)] * 3,
        compiler_params=_params("parallel", "parallel"), name=name)(xb, wg, wu)


def mm_res_ln(a, w, x, g, b, alpha, scale, name):
    T, K = a.shape
    D = w.shape[1]
    tm = _tile(T, 512)
    parts = [slice(r, r + MXU_COLS) for r in range(0, tm, MXU_COLS)] if tm % MXU_COLS == 0 else [slice(0, tm)]

    def body(a_ref, w_ref, x_ref, g_ref, b_ref, z_ref, o_ref, ob_ref):
        ys = [jnp.dot(a_ref[r, :], w_ref[...], preferred_element_type=F32) for r in parts]
        for r, y in zip(parts, ys):
            z = alpha * x_ref[r, :] + scale * y
            mu = jnp.mean(z, axis=1, keepdims=True)
            zc = z - mu
            var = jnp.mean(zc * zc, axis=1, keepdims=True)
            o = zc * lax.rsqrt(var + LN_EPS) * g_ref[...] + b_ref[...]
            z_ref[r, :] = z
            o_ref[r, :] = o
            ob_ref[r, :] = o.astype(BF16)

    row = pl.BlockSpec((tm, D), lambda i: (i, 0))
    vec = pl.BlockSpec((1, D), lambda i: (0, 0))
    return pl.pallas_call(
        body, grid=(T // tm,),
        in_specs=[pl.BlockSpec((tm, K), lambda i: (i, 0)), pl.BlockSpec((K, D), lambda i: (0, 0)), row, vec, vec],
        out_specs=[row, row, row],
        out_shape=[jax.ShapeDtypeStruct((T, D), F32), jax.ShapeDtypeStruct((T, D), F32),
                   jax.ShapeDtypeStruct((T, D), BF16)],
        compiler_params=_params("parallel"), name=name)(a, w, x, g, b)


def _ln_bwd_rows(zv, gain, dov, first, dz_ref, dzb_ref, dg_ref, db_ref):
    @pl.when(first)
    def _():
        dg_ref[...] = jnp.zeros_like(dg_ref)
        db_ref[...] = jnp.zeros_like(db_ref)

    mu = jnp.mean(zv, axis=1, keepdims=True)
    zc = zv - mu
    var = jnp.mean(zc * zc, axis=1, keepdims=True)
    rstd = lax.rsqrt(var + LN_EPS)
    xhat = zc * rstd
    dxhat = dov * gain
    m1 = jnp.mean(dxhat, axis=1, keepdims=True)
    m2 = jnp.mean(dxhat * xhat, axis=1, keepdims=True)
    dz = rstd * (dxhat - m1 - xhat * m2)
    dz_ref[...] = dz
    dzb_ref[...] = dz.astype(BF16)
    dg_ref[...] += jnp.sum(dov * xhat, axis=0, keepdims=True)
    db_ref[...] += jnp.sum(dov, axis=0, keepdims=True)


def _ln_bwd_outs(T, D):
    return [jax.ShapeDtypeStruct((T, D), F32), jax.ShapeDtypeStruct((T, D), BF16),
            jax.ShapeDtypeStruct((1, D), F32), jax.ShapeDtypeStruct((1, D), F32)]


def ln_bwd(z, g, do, name):
    T, D = z.shape
    tm = _tile(T, 512)

    def body(z_ref, g_ref, do_ref, dz_ref, dzb_ref, dg_ref, db_ref):
        _ln_bwd_rows(z_ref[...], g_ref[...], do_ref[...], pl.program_id(0) == 0, dz_ref, dzb_ref, dg_ref, db_ref)

    row = pl.BlockSpec((tm, D), lambda i: (i, 0))
    vec = pl.BlockSpec((1, D), lambda i: (0, 0))
    return pl.pallas_call(
        body, grid=(T // tm,), in_specs=[row, vec, row], out_specs=[row, row, vec, vec],
        out_shape=_ln_bwd_outs(T, D), compiler_params=_params("arbitrary"), name=name)(z, g, do)


def mm_nt(a, w, dep, name):
    T, K = a.shape
    N = w.shape[0]
    tm, tn = _tile(T, 1024), _tile(N, 512)

    def body(a_ref, w_ref, dep_ref, o_ref):
        o_ref[...] = lax.dot_general(a_ref[...], w_ref[...], NT_DIMS, preferred_element_type=F32).astype(o_ref.dtype)

    return pl.pallas_call(
        body, grid=(T // tm, N // tn),
        in_specs=[pl.BlockSpec((tm, K), lambda i, j: (i, 0)), pl.BlockSpec((tn, K), lambda i, j: (j, 0)), ANY],
        out_specs=pl.BlockSpec((tm, tn), lambda i, j: (i, j)),
        out_shape=jax.ShapeDtypeStruct((T, N), BF16),
        compiler_params=_params("parallel", "parallel"), name=name)(a, w, dep)


def ffn_bwd_mid(dzb, wd, h, u, scale, dep, name):
    T, K = dzb.shape
    N = wd.shape[0]
    tm, tn = _tile(T, 512), _tile(N, 768)

    def body(dz_ref, w_ref, h_ref, u_ref, dep_ref, dh_ref, du_ref):
        dz = dz_ref[...]
        chunks = [slice(c, c + MXU_COLS) for c in range(0, tn, MXU_COLS)]
        das = [lax.dot_general(dz, w_ref[c, :], NT_DIMS, preferred_element_type=F32) for c in chunks]
        for c, da in zip(chunks, das):
            da = scale * da
            hv = h_ref[:, c].astype(F32)
            s = jax.nn.sigmoid(hv)
            silu = hv * s
            dh_ref[:, c] = (da * u_ref[:, c].astype(F32) * (s + silu * (1.0 - s))).astype(BF16)
            du_ref[:, c] = (da * silu).astype(BF16)

    tile = pl.BlockSpec((tm, tn), lambda j, i: (i, j))
    return pl.pallas_call(
        body, grid=(N // tn, T // tm),
        in_specs=[pl.BlockSpec((tm, K), lambda j, i: (i, 0)), pl.BlockSpec((tn, K), lambda j, i: (j, 0)), tile, tile, ANY],
        out_specs=[tile, tile],
        out_shape=[jax.ShapeDtypeStruct((T, N), BF16)] * 2,
        compiler_params=_params("parallel", "parallel"), name=name)(dzb, wd, h, u, dep)


def mm_nt_res(pairs, dz, alpha, ln_before, name):
    T, N = pairs[0][0].shape
    D = pairs[0][1].shape[0]
    n = len(pairs)
    tm = _tile(T, 512 if n == 1 else 256)

    def body(*refs):
        a_refs, w_refs, dz_ref = refs[:n], refs[n:2 * n], refs[2 * n]
        d = lax.dot_general(a_refs[0][...], w_refs[0][...], NT_DIMS, preferred_element_type=F32)
        for p in range(1, n):
            d += lax.dot_general(a_refs[p][...], w_refs[p][...], NT_DIMS, preferred_element_type=F32)
        d = d + alpha * dz_ref[...]
        if ln_before is None:
            refs[2 * n + 1][...] = d
        else:
            z_ref, g_ref = refs[2 * n + 1:2 * n + 3]
            _ln_bwd_rows(z_ref[...], g_ref[...], d, pl.program_id(0) == 0, *refs[2 * n + 3:])

    row = pl.BlockSpec((tm, D), lambda i: (i, 0))
    vec = pl.BlockSpec((1, D), lambda i: (0, 0))
    in_specs = ([pl.BlockSpec((tm, N), lambda i: (i, 0))] * n
                + [pl.BlockSpec((D, N), functools.partial(lambda i, col: (0, col), col=p[2])) for p in pairs] + [row])
    operands = [p[0] for p in pairs] + [p[1] for p in pairs] + [dz]
    if ln_before is None:
        return pl.pallas_call(
            body, grid=(T // tm,), in_specs=in_specs, out_specs=row, out_shape=jax.ShapeDtypeStruct((T, D), F32),
            compiler_params=_params("parallel"), name=name)(*operands)
    return pl.pallas_call(
        body, grid=(T // tm,), in_specs=in_specs + [row, vec], out_specs=[row, row, vec, vec],
        out_shape=_ln_bwd_outs(T, D), compiler_params=_params("arbitrary"), name=name)(*operands, *ln_before)


def mm_tn(a, b, scale, dep, name, into=None, col=0):
    T, M = a.shape
    N = b.shape[1]
    tm, tn, tk = _tile(M, 512), _tile(N, 1024), _tile(T, 4096)
    nk = T // tk
    if into is None:
        into = lax.empty((M, N), BF16)

    def body(a_ref, b_ref, dep_ref, into_ref, o_ref, acc_ref):
        k = pl.program_id(2)

        @pl.when(k == 0)
        def _():
            acc_ref[...] = jnp.zeros_like(acc_ref)

        acc_ref[...] += lax.dot_general(a_ref[...], b_ref[...], TN_DIMS, preferred_element_type=F32)

        @pl.when(k == nk - 1)
        def _():
            o_ref[...] = (scale * acc_ref[...]).astype(BF16)

    return pl.pallas_call(
        body, grid=(M // tm, N // tn, nk),
        in_specs=[pl.BlockSpec((tk, tm), lambda i, j, k: (k, i)), pl.BlockSpec((tk, tn), lambda i, j, k: (k, j)), ANY, ANY],
        out_specs=pl.BlockSpec((tm, tn), lambda i, j, k: (i, col * (N // tn) + j)),
        out_shape=jax.ShapeDtypeStruct(into.shape, BF16), input_output_aliases={3: 0},
        scratch_shapes=[pltpu.VMEM((tm, tn), F32)],
        compiler_params=_params("parallel", "parallel", "arbitrary"), name=name)(a, b, dep, into)


def loss_head(y, target, name):
    T, D = y.shape
    tm = _tile(T, 512)

    def body(y_ref, t_ref, l_ref, dy_ref):
        @pl.when(pl.program_id(0) == 0)
        def _():
            l_ref[...] = jnp.zeros_like(l_ref)

        e = y_ref[...] - t_ref[...]
        dy_ref[...] = e * (1.0 / D)
        rows = jnp.sum(e * e, axis=1, keepdims=True) * (0.5 / D)
        l_ref[...] += jnp.sum(rows, axis=0, keepdims=True)

    row = pl.BlockSpec((tm, D), lambda i: (i, 0))
    return pl.pallas_call(
        body, grid=(T // tm,), in_specs=[row, row],
        out_specs=[pl.BlockSpec((1, 1), lambda i: (0, 0)), row],
        out_shape=[jax.ShapeDtypeStruct((1, 1), F32), jax.ShapeDtypeStruct((T, D), F32)],
        compiler_params=_params("arbitrary"), name=name)(y, target)


def _rel_index(i, j):
    return jnp.clip(i - j + LEFT_CHUNKS * CHUNK, -REL_CLIP, REL_CLIP) + REL_CLIP


def bias_band(table_t, name):
    H = table_t.shape[0]
    rows = SUBLANE

    def body(t_ref, o_ref):
        i0 = pl.program_id(0) * rows
        parts = _split3(t_ref[...])
        r = lax.broadcasted_iota(jnp.int32, (REL_PAD, A_W), 0)
        j = lax.broadcasted_iota(jnp.int32, (REL_PAD, A_W), 1)
        for ii in range(rows):
            onehot = jnp.where(r == _rel_index(i0 + ii, j), 1.0, 0.0).astype(BF16)
            o_ref[ii] = sum(jnp.dot(p, onehot, preferred_element_type=F32) for p in parts)

    return pl.pallas_call(
        body, grid=(A_TQ // rows,),
        in_specs=[pl.BlockSpec((H, REL_PAD), lambda i: (0, 0))],
        out_specs=pl.BlockSpec((rows, H, A_W), lambda i: (i, 0, 0)),
        out_shape=jax.ShapeDtypeStruct((A_TQ, H, A_W), F32),
        compiler_params=_params("parallel"), name=name)(table_t)


def bias_band_bwd(dbands, name):
    H = dbands[0].shape[1]
    rows = SUBLANE
    n = len(dbands)

    def body(*refs):
        g_refs, o_ref = refs[:n], refs[n]

        @pl.when(pl.program_id(0) == 0)
        def _():
            o_ref[...] = jnp.zeros_like(o_ref)

        i0 = pl.program_id(0) * rows
        j = lax.broadcasted_iota(jnp.int32, (A_W, REL_PAD), 0)
        r = lax.broadcasted_iota(jnp.int32, (A_W, REL_PAD), 1)
        acc = jnp.zeros((H, REL_PAD), F32)
        for ii in range(rows):
            onehot = jnp.where(r == _rel_index(i0 + ii, j), 1.0, 0.0).astype(BF16)
            g = g_refs[0][ii]
            for q in range(1, n):
                g = g + g_refs[q][ii]
            acc += _dot3(g, onehot)
        o_ref[...] += acc

    spec = pl.BlockSpec((rows, H, A_W), lambda i: (i, 0, 0))
    return pl.pallas_call(
        body, grid=(A_TQ // rows,), in_specs=[spec] * n,
        out_specs=pl.BlockSpec((H, REL_PAD), lambda i: (0, 0)),
        out_shape=jax.ShapeDtypeStruct((H, REL_PAD), F32),
        compiler_params=_params("arbitrary"), name=name)(*dbands)


def _a_window(ref, qi):
    parts = []
    for d in range(A_NWB):
        kb = jnp.maximum(qi - (A_NWB - 1) + d, 0)
        parts.append(ref[pl.ds(pl.multiple_of(kb * A_TQ, A_TQ), A_TQ), :])
    return jnp.concatenate(parts, axis=0)


def _a_valid(qi):
    i = lax.broadcasted_iota(jnp.int32, (A_TQ, A_W), 0)
    j = lax.broadcasted_iota(jnp.int32, (A_TQ, A_W), 1)
    ic, jc = i // CHUNK, j // CHUNK
    return (jc >= ic) & (jc <= ic + LEFT_CHUNKS) & (j >= LEFT_CHUNKS * CHUNK - qi * A_TQ)


def _head_masks():
    lane = lax.broadcasted_iota(jnp.int32, (1, LANE), 1)
    return [lane < HEAD_DIM, lane >= HEAD_DIM]


def _a_probs(qms, kws, valids, b_ref):
    scores = [lax.dot_general(qm, kws[i // 2], NT_DIMS, preferred_element_type=F32) for i, qm in enumerate(qms)]
    probs = []
    for i, s in enumerate(scores):
        s = jnp.where(valids[i // 2], s * (HEAD_DIM ** -0.5) + b_ref[i % 2], NEG)
        p = jnp.exp(s - jnp.max(s, axis=1, keepdims=True))
        probs.append(p / jnp.sum(p, axis=1, keepdims=True))
    return probs


def _a_subtiles(ref, masks):
    out = []
    for u in range(A_SUB):
        x = ref[u * A_TQ:(u + 1) * A_TQ, :]
        out += [jnp.where(hm, x, jnp.zeros_like(x)) for hm in masks]
    return out


def _attn_specs(S, D, tq):
    hp_n = D // LANE
    nq = S // tq
    q = pl.BlockSpec((tq, LANE), lambda hp, b, qi: (b * nq + qi, hp))
    k = pl.BlockSpec((S, LANE), lambda hp, b, qi: (b, hp_n + hp))
    v = pl.BlockSpec((S, LANE), lambda hp, b, qi: (b, 2 * hp_n + hp))
    tile = pl.BlockSpec((tq, LANE), lambda hp, b, qi: (b * nq + qi, hp))
    seq = pl.BlockSpec((S, LANE), lambda hp, b, qi: (b, hp))
    return q, k, v, tile, seq


def attn_a_fwd(qkv, band, S, name):
    T, D3 = qkv.shape
    D = D3 // 3
    nb, ng = T // S, S // (A_TQ * A_SUB)

    def body(q_ref, k_ref, v_ref, b_ref, o_ref):
        qis = [pl.program_id(2) * A_SUB + u for u in range(A_SUB)]
        masks = _head_masks()
        kws, vws = [_a_window(k_ref, qi) for qi in qis], [_a_window(v_ref, qi) for qi in qis]
        probs = _a_probs(_a_subtiles(q_ref, masks), kws, [_a_valid(qi) for qi in qis], b_ref)
        outs = [jnp.dot(p.astype(BF16), vws[i // 2], preferred_element_type=F32) for i, p in enumerate(probs)]
        for u in range(A_SUB):
            o_ref[u * A_TQ:(u + 1) * A_TQ, :] = jnp.where(masks[0], outs[2 * u], outs[2 * u + 1]).astype(BF16)

    q, k, v, tile, _ = _attn_specs(S, D, A_TQ * A_SUB)
    return pl.pallas_call(
        body, grid=(D // LANE, nb, ng),
        in_specs=[q, k, v, pl.BlockSpec((2, A_TQ, A_W), lambda hp, b, qi: (hp, 0, 0))],
        out_specs=tile, out_shape=jax.ShapeDtypeStruct((T, D), BF16),
        compiler_params=_params("parallel", "parallel", "parallel"), name=name)(qkv, qkv, qkv, band)


def attn_a_bwd(qkv, band, do, S, name):
    T, D3 = qkv.shape
    D = D3 // 3
    nb, ng = T // S, S // (A_TQ * A_SUB)
    scale = HEAD_DIM ** -0.5

    def body(q_ref, k_ref, v_ref, b_ref, do_ref, dq_ref, dk_ref, dv_ref, db_ref, dk_acc, dv_acc):
        b, qg = pl.program_id(1), pl.program_id(2)

        @pl.when((b == 0) & (qg == 0))
        def _():
            db_ref[...] = jnp.zeros_like(db_ref)

        @pl.when(qg == 0)
        def _():
            dk_acc[...] = jnp.zeros_like(dk_acc)
            dv_acc[...] = jnp.zeros_like(dv_acc)

        qis = [qg * A_SUB + u for u in range(A_SUB)]
        masks = _head_masks()
        kws, vws = [_a_window(k_ref, qi) for qi in qis], [_a_window(v_ref, qi) for qi in qis]
        qms, doms = _a_subtiles(q_ref, masks), _a_subtiles(do_ref, masks)
        dps = [lax.dot_general(dom, vws[i // 2], NT_DIMS, preferred_element_type=F32) for i, dom in enumerate(doms)]
        probs = _a_probs(qms, kws, [_a_valid(qi) for qi in qis], b_ref)
        dss = [p * (dp - jnp.sum(p * dp, axis=1, keepdims=True)) for p, dp in zip(probs, dps)]
        for hh in range(2):
            db_ref[hh] += sum(dss[hh::2])
        dsbs = [ds.astype(BF16) for ds in dss]
        dqs = [jnp.dot(dsb, kws[i // 2], preferred_element_type=F32) for i, dsb in enumerate(dsbs)]
        dks = [lax.dot_general(dsb, qm, TN_DIMS, preferred_element_type=F32) for dsb, qm in zip(dsbs, qms)]
        dvs = [lax.dot_general(p.astype(BF16), dom, TN_DIMS, preferred_element_type=F32) for p, dom in zip(probs, doms)]
        for u, qi in enumerate(qis):
            dq_ref[u * A_TQ:(u + 1) * A_TQ, :] = (jnp.where(masks[0], dqs[2 * u], dqs[2 * u + 1]) * scale).astype(BF16)
            dkw = (dks[2 * u] + dks[2 * u + 1]) * scale
            dvw = dvs[2 * u] + dvs[2 * u + 1]
            for d in range(A_NWB):
                kb = jnp.maximum(qi - (A_NWB - 1) + d, 0)
                rows = pl.ds(pl.multiple_of(kb * A_TQ, A_TQ), A_TQ)
                dk_acc[rows, :] += dkw[d * A_TQ:(d + 1) * A_TQ]
                dv_acc[rows, :] += dvw[d * A_TQ:(d + 1) * A_TQ]

        @pl.when(qg == ng - 1)
        def _():
            dk_ref[...] = dk_acc[...].astype(BF16)
            dv_ref[...] = dv_acc[...].astype(BF16)

    q, k, v, tile, seq = _attn_specs(S, D, A_TQ * A_SUB)
    bspec = pl.BlockSpec((2, A_TQ, A_W), lambda hp, b, qi: (hp, 0, 0))
    act = jax.ShapeDtypeStruct((T, D), BF16)
    return pl.pallas_call(
        body, grid=(D // LANE, nb, ng),
        in_specs=[q, k, v, bspec, tile], out_specs=[tile, seq, seq, bspec],
        out_shape=[act, act, act, jax.ShapeDtypeStruct(band.shape, F32)],
        scratch_shapes=[pltpu.VMEM((S, LANE), F32), pltpu.VMEM((S, LANE), F32)],
        compiler_params=_params("arbitrary", "arbitrary", "arbitrary"), name=name)(qkv, qkv, qkv, band, do)


def _dot2(v, w):
    h = v.astype(BF16)
    lo = (v - h.astype(F32)).astype(BF16)
    return jnp.dot(h, w, preferred_element_type=F32) + jnp.dot(lo, w, preferred_element_type=F32)


def _b_weights(qms, kts, rights, after, diagonal):
    inst = [(t, hh) for t in range(len(kts)) for hh in range(2)]
    zs = [lax.dot_general(qms[hh], kts[t], NT_DIMS, preferred_element_type=F32) for t, hh in inst]
    lbs, l1ms, his, los = [], [], [], []
    causal = _strictly_causal() if any(diagonal) else None
    for (t, hh), z in zip(inst, zs):
        lb = jnp.minimum(z, 0.0) - jnp.log(1.0 + jnp.exp(jnp.minimum(z, -z)))
        lbs.append(lb)
        l1m = lb - z
        if diagonal[t]:
            l1m = jnp.where(causal, l1m, 0.0)
        hi = l1m.astype(BF16)
        l1ms.append(l1m)
        his.append(hi)
        los.append((l1m - hi.astype(F32)).astype(BF16))
    sums = [jnp.dot(hi, after, preferred_element_type=F32) + jnp.dot(lo, after, preferred_element_type=F32)
            for hi, lo in zip(his, los)]
    rights = list(rights)
    weights = []
    for (t, hh), lb, l1m, c in zip(inst, lbs, l1ms, sums):
        a = jnp.exp(lb + (rights[hh] + c))
        weights.append(jnp.where(causal, a, 0.0) if diagonal[t] else a)
        rights[hh] = rights[hh] + (c[:, :1] + l1m[:, :1])
    return zs, weights, rights


def _strictly_causal():
    row = lax.broadcasted_iota(jnp.int32, (B_T, B_T), 0)
    col = lax.broadcasted_iota(jnp.int32, (B_T, B_T), 1)
    return col < row


def _tri(strict_lower):
    r = lax.broadcasted_iota(jnp.int32, (B_T, B_T), 0)
    c = lax.broadcasted_iota(jnp.int32, (B_T, B_T), 1)
    return jnp.where((r > c) if strict_lower else (r < c), 1.0, 0.0).astype(BF16)


def _scaled_heads(q2):
    qs = q2 * (HEAD_DIM ** -0.5)
    return [jnp.where(hm, qs, jnp.zeros_like(qs)) for hm in _head_masks()]


def attn_b_fwd(qkv, S, name):
    T, D3 = qkv.shape
    D = D3 // 3
    nb, nq = T // S, S // B_T

    def body(q_ref, k_ref, v_ref, o_ref):
        qi = pl.program_id(2)
        after = _tri(True)
        qms = _scaled_heads(q_ref[...])

        def tiles(kbs, carry, diagonal):
            rows = [pl.ds(pl.multiple_of(kb * B_T, B_T), B_T) for kb in kbs]
            kts, vts = [k_ref[r, :] for r in rows], [v_ref[r, :] for r in rows]
            _, weights, rights = _b_weights(qms, kts, (carry[0], carry[2]), after, diagonal)
            accs = [carry[1], carry[3]]
            for i, a in enumerate(weights):
                accs[i % 2] = accs[i % 2] + jnp.dot(a.astype(BF16), vts[i // 2], preferred_element_type=F32)
            return rights[0], accs[0], rights[1], accs[1]

        init = (jnp.zeros((B_T, 1), F32), jnp.zeros((B_T, LANE), F32)) * 2
        res = lax.cond(qi % 2 == 1, lambda c: tiles([qi, qi - 1], c, [True, False]), lambda c: tiles([qi], c, [True]), init)
        top = qi - 1 - qi % 2
        res = lax.fori_loop(0, qi // 2, lambda p, c: tiles([top - 2 * p, top - 1 - 2 * p], c, [False, False]), res)
        o_ref[...] = jnp.where(_head_masks()[0], res[1], res[3]).astype(BF16)

    q, k, v, tile, _ = _attn_specs(S, D, B_T)
    return pl.pallas_call(
        body, grid=(D // LANE, nb, nq), in_specs=[q, k, v], out_specs=tile,
        out_shape=jax.ShapeDtypeStruct((T, D), BF16),
        compiler_params=_params("parallel", "parallel", "parallel"), name=name)(qkv, qkv, qkv)


def attn_b_bwd(qkv, do, S, name):
    T, D3 = qkv.shape
    D = D3 // 3
    nb, nq = T // S, S // B_T
    scale = HEAD_DIM ** -0.5

    def body(q_ref, k_ref, v_ref, do_ref, dq_ref, dk_ref, dv_ref, dk_acc, dv_acc, z_s, g_s):
        qi = pl.program_id(2)

        @pl.when(qi == 0)
        def _():
            dk_acc[...] = jnp.zeros_like(dk_acc)
            dv_acc[...] = jnp.zeros_like(dv_acc)

        do2 = do_ref[...]
        after, before = _tri(True), _tri(False)
        masks = _head_masks()
        qms = _scaled_heads(q_ref[...])
        doms = [jnp.where(hm, do2, jnp.zeros_like(do2)) for hm in masks]

        def sweep_left(kbs, rights, diagonal):
            rows = [pl.ds(pl.multiple_of(kb * B_T, B_T), B_T) for kb in kbs]
            kts, vts = [k_ref[r, :] for r in rows], [v_ref[r, :] for r in rows]
            das = [lax.dot_general(doms[hh], vt, NT_DIMS, preferred_element_type=F32) for vt in vts for hh in range(2)]
            zs, weights, rights = _b_weights(qms, kts, rights, after, diagonal)
            for i, (z, a, da) in enumerate(zip(zs, weights, das)):
                z_s[i % 2, kbs[i // 2]] = z
                g_s[i % 2, kbs[i // 2]] = da * a
            dvs = [lax.dot_general(a.astype(BF16), doms[i % 2], TN_DIMS, preferred_element_type=F32)
                   for i, a in enumerate(weights)]
            for t, r in enumerate(rows):
                dv_acc[r, :] += dvs[2 * t] + dvs[2 * t + 1]
            return tuple(rights)

        zero_col = jnp.zeros((B_T, 1), F32)
        odd = qi % 2 == 1
        top = qi - 1 - qi % 2
        rights = lax.cond(odd, lambda c: sweep_left([qi, qi - 1], c, [True, False]), lambda c: sweep_left([qi], c, [True]),
                          (zero_col, zero_col))
        lax.fori_loop(0, qi // 2, lambda p, c: sweep_left([top - 2 * p, top - 1 - 2 * p], c, [False, False]), rights)

        def sweep_right(kbs, carry, diagonal):
            rows = [pl.ds(pl.multiple_of(kb * B_T, B_T), B_T) for kb in kbs]
            inst = [(t, hh) for t in range(len(kbs)) for hh in range(2)]
            gs = [g_s[hh, kbs[t]] for t, hh in inst]
            sums = [_dot2(g, before) for g in gs]
            lefts, dqs = [carry[0], carry[2]], [carry[1], carry[3]]
            dzbs = []
            for (t, hh), g, c in zip(inst, gs, sums):
                beta = jax.nn.sigmoid(z_s[hh, kbs[t]])
                dz = g * (1.0 - beta) - beta * (lefts[hh] + c)
                if diagonal[t]:
                    dz = jnp.where(_strictly_causal(), dz, 0.0)
                dzbs.append(dz.astype(BF16))
                lefts[hh] = lefts[hh] + jnp.sum(g, axis=1, keepdims=True)
            dks = [lax.dot_general(dzb, qms[hh], TN_DIMS, preferred_element_type=F32) for (t, hh), dzb in zip(inst, dzbs)]
            for (t, hh), dzb in zip(inst, dzbs):
                dqs[hh] = dqs[hh] + jnp.dot(dzb, k_ref[rows[t], :], preferred_element_type=F32)
            for t, r in enumerate(rows):
                dk_acc[r, :] += dks[2 * t] + dks[2 * t + 1]
            return lefts[0], dqs[0], lefts[1], dqs[1]

        init = (zero_col, jnp.zeros((B_T, LANE), F32)) * 2
        res = lax.fori_loop(0, qi // 2, lambda p, c: sweep_right([2 * p, 2 * p + 1], c, [False, False]), init)
        res = lax.cond(odd, lambda c: sweep_right([qi - 1, qi], c, [False, True]), lambda c: sweep_right([qi], c, [True]), res)
        dq_ref[...] = (jnp.where(masks[0], res[1], res[3]) * scale).astype(BF16)

        @pl.when(qi == nq - 1)
        def _():
            dk_ref[...] = dk_acc[...].astype(BF16)
            dv_ref[...] = dv_acc[...].astype(BF16)

    q, k, v, tile, seq = _attn_specs(S, D, B_T)
    act = jax.ShapeDtypeStruct((T, D), BF16)
    return pl.pallas_call(
        body, grid=(D // LANE, nb, nq),
        in_specs=[q, k, v, tile], out_specs=[tile, seq, seq], out_shape=[act, act, act],
        scratch_shapes=[pltpu.VMEM((S, LANE), F32), pltpu.VMEM((S, LANE), F32),
                        pltpu.VMEM((2, nq, B_T, B_T), F32), pltpu.VMEM((2, nq, B_T, B_T), F32)],
        compiler_params=_params("arbitrary", "arbitrary", "arbitrary"), name=name)(qkv, qkv, qkv, do)


HBM = pl.BlockSpec(memory_space=pltpu.HBM)
SEM = pl.BlockSpec(memory_space=pltpu.SEMAPHORE)
N_PEERS = N_DEV - 1
GATHERS_AHEAD = 3


def _place():
    return lax.axis_index("x"), lax.axis_index("y"), lax.axis_index("c")


def _peers(x, y, c):
    return [(1 - x if r & 4 else x, 1 - y if r & 2 else y, 1 - c if r & 1 else c) for r in range(1, N_DEV)]


def _block(ref, shape, by_cols, j):
    r, w = shape
    if by_cols:
        return ref.at[:, pl.ds(pl.multiple_of(j * w, LANE), w)]
    return ref.at[pl.ds(pl.multiple_of(j * r, SUBLANE), r), :]


def _exchange_copies(gather, src_refs, land_refs, send_sems, recv_sems, by_cols):
    x, y, c = _place()
    me = 4 * x + 2 * y + c
    out = []
    for t, (src, land) in enumerate(zip(src_refs, land_refs)):
        for r, peer in enumerate(_peers(x, y, c)):
            pj = 4 * peer[0] + 2 * peer[1] + peer[2]
            if gather:
                mine, to_me, theirs = src, _block(land, src.shape, by_cols[t], me), _block(land, src.shape, by_cols[t], pj)
            else:
                mine, to_me, theirs = _block(src, land.shape[1:], by_cols[t], pj), land.at[me], land.at[pj]
            sems = dict(send_sem=send_sems.at[N_PEERS * t + r], recv_sem=recv_sems.at[N_PEERS * t + r],
                        device_id=peer, device_id_type=MESH_ID)
            out.append((pltpu.make_async_remote_copy(src_ref=mine, dst_ref=to_me, **sems),
                        pltpu.make_async_remote_copy(src_ref=mine, dst_ref=theirs, **sems)))
    return out


def _own_copies(gather, src_refs, land_refs, own_sems, by_cols):
    x, y, c = _place()
    me = 4 * x + 2 * y + c
    out = []
    for t, (src, land) in enumerate(zip(src_refs, land_refs)):
        if gather:
            out.append(pltpu.make_async_copy(src, _block(land, src.shape, by_cols[t], me), own_sems.at[t]))
        else:
            out.append(pltpu.make_async_copy(_block(src, land.shape[1:], by_cols[t], me), land.at[me], own_sems.at[t]))
    return out


def exchange_start(gather, srcs, land_shapes, by_cols, after, name):
    n = len(srcs)

    def body(*refs):
        src_refs, land_refs = refs[:n], refs[n:2 * n]
        send_sems, recv_sems, own_sems = refs[2 * n + 1:2 * n + 4]
        token = refs[-1]
        for cp in _own_copies(gather, src_refs, land_refs, own_sems, by_cols):
            cp.start()
        for mine, _ in _exchange_copies(gather, src_refs, land_refs, send_sems, recv_sems, by_cols):
            mine.start()
        token[...] = jnp.zeros_like(token)

    lands = [pltpu.with_memory_space_constraint(lax.empty(s.shape, s.dtype), pltpu.HBM) for s in land_shapes]
    srcs = [pltpu.with_memory_space_constraint(s, pltpu.HBM) for s in srcs]
    res = pl.pallas_call(
        body, name=name,
        out_shape=(pltpu.SemaphoreType.DMA((N_PEERS * n,)), pltpu.SemaphoreType.DMA((N_PEERS * n,)),
                   pltpu.SemaphoreType.DMA((n,)),
                   *[pltpu.HBM(s.shape, s.dtype) for s in srcs], *[pltpu.HBM(s.shape, s.dtype) for s in land_shapes],
                   jax.ShapeDtypeStruct((SUBLANE, LANE), F32)),
        in_specs=[HBM] * (2 * n) + [ANY],
        out_specs=(SEM, SEM, SEM, *[HBM] * (2 * n), pl.BlockSpec(memory_space=pltpu.VMEM)),
        input_output_aliases={i: 3 + i for i in range(2 * n)},
        compiler_params=pltpu.CompilerParams(has_side_effects=pltpu.SideEffectType.DATAFLOW_SIDE_EFFECTING),
    )(*srcs, *lands, after)
    return dict(gather=gather, n=n, by_cols=by_cols, sems=res[:3], srcs=res[3:3 + n],
                lands=res[3 + n:3 + 2 * n], token=res[-1])


def exchange_wait(started, after, name):
    n, gather, by_cols = started["n"], started["gather"], started["by_cols"]

    def body(*refs):
        src_refs, land_refs = refs[:n], refs[n:2 * n]
        send_sems, recv_sems, own_sems = refs[2 * n:2 * n + 3]
        for mine, theirs in _exchange_copies(gather, src_refs, land_refs, send_sems, recv_sems, by_cols):
            mine.wait_send()
            theirs.wait_recv()
        for cp in _own_copies(gather, src_refs, land_refs, own_sems, by_cols):
            cp.wait()

    res = pl.pallas_call(
        body, name=name,
        out_shape=tuple(pltpu.HBM(s.shape, s.dtype) for s in (*started["srcs"], *started["lands"])),
        in_specs=[HBM] * (2 * n) + [SEM, SEM, SEM, ANY], out_specs=tuple([HBM] * (2 * n)),
        input_output_aliases={i: i for i in range(2 * n)},
        compiler_params=pltpu.CompilerParams(has_side_effects=pltpu.SideEffectType.DATAFLOW_SIDE_EFFECTING),
    )(*started["srcs"], *started["lands"], *started["sems"], after)
    return res[n:]


def gather_small(v, dep, name):
    R, C = v.shape

    def body(v_ref, dep_ref, o_ref, send_sems, recv_sems):
        x, y, c = _place()
        o_ref[4 * x + 2 * y + c] = v_ref[...]
        peers = _peers(x, y, c)

        def copy(r, owner, to):
            slot = o_ref.at[4 * owner[0] + 2 * owner[1] + owner[2]]
            return pltpu.make_async_remote_copy(
                src_ref=slot, dst_ref=slot, send_sem=send_sems.at[r], recv_sem=recv_sems.at[r],
                device_id=to, device_id_type=MESH_ID)

        sends = [copy(r, (x, y, c), peer) for r, peer in enumerate(peers)]
        for cp in sends:
            cp.start()
        for r, peer in enumerate(peers):
            copy(r, peer, (x, y, c)).wait_recv()
        for cp in sends:
            cp.wait_send()

    vm = pl.BlockSpec(memory_space=pltpu.VMEM)
    return pl.pallas_call(
        body, in_specs=[vm, ANY], out_specs=vm, out_shape=jax.ShapeDtypeStruct((N_DEV, R, C), F32),
        scratch_shapes=[pltpu.SemaphoreType.DMA((N_PEERS,)), pltpu.SemaphoreType.DMA((N_PEERS,))],
        name=name)(v, dep)


def sum_devices(g, name):
    _, R, C = g.shape

    def body(g_ref, o_ref):
        acc = g_ref[0]
        for j in range(1, N_DEV):
            acc = acc + g_ref[j]
        o_ref[...] = acc

    vm = pl.BlockSpec(memory_space=pltpu.VMEM)
    return pl.pallas_call(body, in_specs=[vm], out_specs=vm, out_shape=jax.ShapeDtypeStruct((R, C), F32), name=name)(g)


def adamw(w, m, v, index, parts, outs, name):
    L, R, C = w.shape
    if outs is None:
        outs = [lax.empty((L, R, C), F32) for _ in range(4)]
    P, Rp, Cp = parts.shape
    tr = _tile(R, 512, SUBLANE) if Rp == R else R

    def body(w_ref, m_ref, v_ref, p_ref, g_in, d_in, m_in, v_in, g_out, d_out, m_out, v_out):
        g = p_ref[0, :tr, :C].astype(F32)
        for q in range(1, P):
            g = g + p_ref[q, :tr, :C].astype(F32)
        mn = ADAM_B1 * m_ref[...] + (1.0 - ADAM_B1) * g
        vn = ADAM_B2 * v_ref[...] + (1.0 - ADAM_B2) * (g * g)
        m_hat = mn / (1.0 - ADAM_B1 ** ADAM_STEP)
        v_hat = vn / (1.0 - ADAM_B2 ** ADAM_STEP)
        g_out[...] = g
        d_out[...] = -ADAM_LR * (m_hat / (jnp.sqrt(v_hat) + ADAM_EPS) + ADAM_WD * w_ref[...])
        m_out[...] = mn
        v_out[...] = vn

    slab = pl.BlockSpec((None, tr, C), lambda r: (index, r, 0))
    pspec = pl.BlockSpec((P, tr if Rp == R else Rp, Cp), lambda r: (0, r, 0))
    return pl.pallas_call(
        body, grid=(R // tr,), in_specs=[slab] * 3 + [pspec] + [ANY] * 4, out_specs=[slab] * 4,
        out_shape=[jax.ShapeDtypeStruct((L, R, C), F32)] * 4,
        input_output_aliases={4 + q: q for q in range(4)},
        compiler_params=_params("parallel"), name=name)(w, m, v, parts, *outs)


def _pad_to(a, axis, size):
    pad = [(0, 0)] * a.ndim
    pad[axis] = (0, size - a.shape[axis])
    return jnp.pad(a, pad)


def kernel(x, w_qkv_a, w_o_a, rel_bias, w_qkv_b, w_o_b, ffn_w_gate, ffn_w_up, ffn_w_down, ln_g, ln_b, loss_target, m_w_qkv_a, m_w_o_a, m_rel_bias, m_w_qkv_b, m_w_o_b, m_ffn_w_gate, m_ffn_w_up, m_ffn_w_down, m_ln_g, m_ln_b, v_w_qkv_a, v_w_o_a, v_rel_bias, v_w_qkv_b, v_w_o_b, v_ffn_w_gate, v_ffn_w_up, v_ffn_w_down, v_ln_g, v_ln_b):
    nb, S, D = x.shape
    T = nb * S
    depth = ffn_w_gate.shape[0]
    H = D // HEAD_DIM
    fs = ffn_w_gate.shape[-1]
    fp = -(-fs // LANE) * LANE
    alpha = (2.0 * depth) ** 0.25
    cx, cy, cc = _place()
    me = 4 * cx + 2 * cy + cc

    ln_local = jnp.concatenate([ln_g.reshape(depth * 3, -1), ln_b.reshape(depth * 3, -1)], axis=0)
    ln_all = gather_small(ln_local, ln_local, "gather_ln")
    ln_full = jnp.transpose(ln_all, (1, 0, 2)).reshape(2 * depth * 3, D)
    ln_gain = lambda i, s: ln_full[3 * i + s][None, :]
    ln_bias = lambda i, s: ln_full[3 * depth + 3 * i + s][None, :]

    table_t = _pad_to(rel_bias.T, 1, REL_PAD)
    band = jnp.transpose(bias_band(table_t, "bias_band"), (1, 0, 2))

    subs = []
    for i in range(depth):
        for s in (0, 1, 2):
            if s == 1:
                wq, wo = (w_qkv_a, w_o_a) if i % 2 == 0 else (w_qkv_b, w_o_b)
                subs.append(([wq[i // 2].astype(BF16), wo[i // 2].astype(BF16)], [True, False]))
            else:
                f = 0 if s == 0 else 1
                subs.append(([_pad_to(ffn_w_gate[i, f].astype(BF16), 1, fp), _pad_to(ffn_w_up[i, f].astype(BF16), 1, fp),
                              _pad_to(ffn_w_down[i, f].astype(BF16), 0, fp)], [True, True, False]))

    def start_gather(k, after):
        shards, by_cols = subs[k]
        groups = [(0, 2), (2, 3)] if k == 0 else [(0, len(shards))]
        out = []
        for part, (lo, hi) in enumerate(groups):
            shapes = [jax.ShapeDtypeStruct((s.shape[0], N_DEV * s.shape[1]) if col else (N_DEV * s.shape[0], s.shape[1]), BF16)
                      for s, col in zip(shards[lo:hi], by_cols[lo:hi])]
            out.append(exchange_start(True, shards[lo:hi], shapes, by_cols[lo:hi], after, f"gather_start_{k}_{part}"))
            after = out[-1]["token"]
        return out

    xf = x.reshape(T, D)
    act, act_b = xf, xf.astype(BF16)
    gathers = {}
    for k in range(min(GATHERS_AHEAD, len(subs))):
        gathers[k] = start_gather(k, xf if k == 0 else gathers[k - 1][-1]["token"])
    newest = gathers[k][-1]["token"]
    saved = []
    for i in range(depth):
        layer = {}
        for s in (0, 1, 2):
            k = 3 * i + s
            tag = f"L{i}S{s}"
            if 0 < k and k + GATHERS_AHEAD - 1 < len(subs):
                gathers[k + GATHERS_AHEAD - 1] = start_gather(k + GATHERS_AHEAD - 1, act_b)
                newest = gathers[k + GATHERS_AHEAD - 1][-1]["token"]
            parts = gathers.pop(k)
            full = list(exchange_wait(parts[0], newest, f"gather_wait_{k}_0"))
            if s == 1:
                wqkv_f, wo_f = full
                qkv = mm_nn(act_b, wqkv_f, "qkv_" + tag)
                if i % 2 == 0:
                    att = attn_a_fwd(qkv, band, S, "attn_a_fwd_" + tag)
                else:
                    att = attn_b_fwd(qkv, S, "attn_b_fwd_" + tag)
                z, o, ob = mm_res_ln(att, wo_f, act, ln_gain(i, s), ln_bias(i, s), alpha, 1.0, "out_ln_" + tag)
                layer[s] = dict(x_b=act_b, qkv=qkv, att=att, z=z, wqkv=wqkv_f, wo=wo_f)
            else:
                wg_f, wu_f = full[:2]
                h, u, a = ffn_up(act_b, wg_f, wu_f, "ffn_up_" + tag)
                wd_f = full[2] if len(parts) == 1 else exchange_wait(parts[1], a, f"gather_wait_{k}_1")[0]
                z, o, ob = mm_res_ln(a, wd_f, act, ln_gain(i, s), ln_bias(i, s), alpha, 0.5, "down_ln_" + tag)
                layer[s] = dict(x_b=act_b, h=h, u=u, a=a, z=z, wg=wg_f, wu=wu_f, wd=wd_f)
            act, act_b = o, ob
        saved.append(layer)

    loss_local, d_act = loss_head(act, loss_target.reshape(T, D), "loss_head")
    loss = lax.psum(loss_local[0, 0], ("x", "y", "c"))

    results = {}

    def update(name, w, m, v, index, parts):
        L = w.shape[0] if w.ndim == 3 else w.shape[0] * w.shape[1]
        flat = lambda t: t.reshape((L,) + t.shape[-2:])
        results[name] = adamw(flat(w), flat(m), flat(v), index, parts, results.get(name), f"adamw_{name}_{index}")

    def finish(entry, after):
        exchange, targets, tag = entry
        lands = exchange_wait(exchange, after, "scatter_wait_" + tag)
        for (name, w, m, v, index), parts in zip(targets, lands):
            update(name, w, m, v, index, parts)
        return results[targets[-1][0]][0]

    pending = []
    started = d_act
    dbands = []
    dln_g = [None] * (3 * depth)
    dln_b = [None] * (3 * depth)
    ln_grads = ln_bwd(saved[depth - 1][2]["z"], ln_gain(depth - 1, 2), d_act, "ln_bwd_last")
    for i in reversed(range(depth)):
        for s in (2, 1, 0):
            tag = f"L{i}S{s}"
            sv = saved[i][s]
            dz, dzb, dln_g[3 * i + s], dln_b[3 * i + s] = ln_grads
            before = (i, s - 1) if s > 0 else (i - 1, 2) if i > 0 else None
            ln_before = None if before is None else (saved[before[0]][before[1]]["z"], ln_gain(*before))
            if s == 1:
                j = i // 2
                d_att = mm_nt(dzb, sv["wo"], started, "att_bwd_" + tag)
                g_wo = mm_tn(sv["att"], dzb, 1.0, started, "dwo_" + tag)
                if i % 2 == 0:
                    dq, dk, dv, dband = attn_a_bwd(sv["qkv"], band, d_att, S, "attn_a_bwd_" + tag)
                    dbands.append(jnp.transpose(dband, (1, 0, 2)))
                else:
                    dq, dk, dv = attn_b_bwd(sv["qkv"], d_att, S, "attn_b_bwd_" + tag)
                g_wqkv = lax.empty((D, 3 * D), BF16)
                for col, (piece, d_piece) in enumerate(zip("qkv", (dq, dk, dv))):
                    g_wqkv = mm_tn(sv["x_b"], d_piece, 1.0, started, f"dw{piece}_" + tag, g_wqkv, col)
                ln_grads = mm_nt_res([(dq, sv["wqkv"], 0), (dk, sv["wqkv"], 1), (dv, sv["wqkv"], 2)], dz, alpha, ln_before,
                                     "dx_mix_" + tag)
                grads, by_cols = [g_wqkv, g_wo], [True, False]
                slabs = [(D, 3 * D // N_DEV), (D // N_DEV, D)]
                if i % 2 == 0:
                    targets = [("w_qkv_a", w_qkv_a, m_w_qkv_a, v_w_qkv_a, j), ("w_o_a", w_o_a, m_w_o_a, v_w_o_a, j)]
                else:
                    targets = [("w_qkv_b", w_qkv_b, m_w_qkv_b, v_w_qkv_b, j), ("w_o_b", w_o_b, m_w_o_b, v_w_o_b, j)]
            else:
                f = 0 if s == 0 else 1
                dh, du = ffn_bwd_mid(dzb, sv["wd"], sv["h"], sv["u"], 0.5, started, "ffn_mid_" + tag)
                g_wd = mm_tn(sv["a"], dzb, 0.5, started, "dwd_" + tag)
                g_wg = mm_tn(sv["x_b"], dh, 1.0, started, "dwg_" + tag)
                g_wu = mm_tn(sv["x_b"], du, 1.0, started, "dwu_" + tag)
                ln_grads = mm_nt_res([(dh, sv["wg"], 0), (du, sv["wu"], 0)], dz, alpha, ln_before, "dx_ffn_" + tag)
                grads, by_cols = [g_wg, g_wu, g_wd], [True, True, False]
                slabs = [(D, fp), (D, fp), (fp, D)]
                idx = 2 * i + f
                targets = [("ffn_w_gate", ffn_w_gate, m_ffn_w_gate, v_ffn_w_gate, idx),
                           ("ffn_w_up", ffn_w_up, m_ffn_w_up, v_ffn_w_up, idx),
                           ("ffn_w_down", ffn_w_down, m_ffn_w_down, v_ffn_w_down, idx)]
            after = ln_grads[0] if before is not None else ln_grads
            if before is None:
                grad_x = ln_grads.reshape(nb, S, D)
                dtable_t = bias_band_bwd(dbands, "bias_band_bwd")
                small = jnp.concatenate(dln_g + dln_b + [_pad_to(dtable_t, 1, D)], axis=0)
                small = _pad_to(small, 0, -(-small.shape[0] // SUBLANE) * SUBLANE)
                total = after = sum_devices(gather_small(small, grads[0], "gather_small_grads"), "sum_small_grads")
            shapes = [jax.ShapeDtypeStruct((N_DEV,) + slab, BF16) for slab in slabs]
            pending.append((exchange_start(False, grads, shapes, by_cols, after, "scatter_start_" + tag), targets, tag))
            started = pending[-1][0]["token"]

    last = pending.pop()
    done = started
    for entry in pending:
        done = finish(entry, done)
    finish(last, done)
    n_ln = 3 * depth
    g_ln_g = lax.dynamic_slice_in_dim(total[:n_ln], me * (D // N_DEV), D // N_DEV, axis=1)
    g_ln_b = lax.dynamic_slice_in_dim(total[n_ln:2 * n_ln], me * (D // N_DEV), D // N_DEV, axis=1)
    g_rel = total[2 * n_ln:2 * n_ln + H, :N_REL].T
    as3 = lambda t: t.reshape((1, -1, t.shape[-1]))
    results["ln_g"] = adamw(as3(ln_g), as3(m_ln_g), as3(v_ln_g), 0, g_ln_g[None], None, "adamw_ln_g")
    results["ln_b"] = adamw(as3(ln_b), as3(m_ln_b), as3(v_ln_b), 0, g_ln_b[None], None, "adamw_ln_b")
    results["rel_bias"] = adamw(as3(rel_bias), as3(m_rel_bias), as3(v_rel_bias), 0, g_rel[None], None, "adamw_rel_bias")

    order = [("w_qkv_a", w_qkv_a), ("w_o_a", w_o_a), ("rel_bias", rel_bias), ("w_qkv_b", w_qkv_b), ("w_o_b", w_o_b),
             ("ffn_w_gate", ffn_w_gate), ("ffn_w_up", ffn_w_up), ("ffn_w_down", ffn_w_down), ("ln_g", ln_g), ("ln_b", ln_b)]
    outs = [loss, grad_x]
    for q in range(4):
        for name, like in order:
            outs.append(results[name][q].reshape(like.shape))
    return tuple(outs)
```

```python
import functools

import jax
import jax.numpy as jnp
from jax import lax
from jax.experimental import pallas as pl
from jax.experimental.pallas import tpu as pltpu

BF16 = jnp.bfloat16
F32 = jnp.float32
MESH_ID = pl.DeviceIdType.MESH
ANY = pl.BlockSpec(memory_space=pl.ANY)

N_DEV = 8
LANE = 128
MXU_COLS = 256
SUBLANE = 8
VMEM_LIMIT = 56 * 1024 * 1024

HEAD_DIM = 64
CHUNK = 64
LEFT_CHUNKS = 8
REL_CLIP = 128
N_REL = 2 * REL_CLIP + 1
REL_PAD = 384
LN_EPS = 1e-5
ADAM_LR, ADAM_B1, ADAM_B2, ADAM_EPS, ADAM_WD, ADAM_STEP = 0.001, 0.9, 0.999, 1e-08, 0.01, 10
NEG = -1e30

A_TQ = 128
A_NWB = LEFT_CHUNKS * CHUNK // A_TQ + 1
A_W = A_NWB * A_TQ
A_SUB = 4
B_T = 256

NT_DIMS = (((1,), (1,)), ((), ()))
TN_DIMS = (((0,), (0,)), ((), ()))


def _tile(n, pref, unit=LANE):
    if n <= pref:
        return n
    t = pref - pref % unit
    while t > unit and n % t:
        t -= unit
    assert n % t == 0, (n, pref)
    return t


def _params(*sem):
    return pltpu.CompilerParams(dimension_semantics=sem, vmem_limit_bytes=VMEM_LIMIT)


def _split3(v):
    h = v.astype(BF16)
    r = v - h.astype(F32)
    m = r.astype(BF16)
    lo = (r - m.astype(F32)).astype(BF16)
    return h, m, lo


def _dot3(v, w):
    h, m, lo = _split3(v)
    return (jnp.dot(h, w, preferred_element_type=F32) + jnp.dot(m, w, preferred_element_type=F32)
            + jnp.dot(lo, w, preferred_element_type=F32))


def mm_nn(a, w, name):
    T, K = a.shape
    N = w.shape[1]
    tm, tn = _tile(T, 1024), _tile(N, 768)

    def body(a_ref, w_ref, o_ref):
        o_ref[...] = jnp.dot(a_ref[...], w_ref[...], preferred_element_type=F32).astype(o_ref.dtype)

    return pl.pallas_call(
        body, grid=(T // tm, N // tn),
        in_specs=[pl.BlockSpec((tm, K), lambda i, j: (i, 0)), pl.BlockSpec((K, tn), lambda i, j: (0, j))],
        out_specs=pl.BlockSpec((tm, tn), lambda i, j: (i, j)),
        out_shape=jax.ShapeDtypeStruct((T, N), BF16),
        compiler_params=_params("parallel", "parallel"), name=name)(a, w)


def ffn_up(xb, wg, wu, name):
    T, K = xb.shape
    N = wg.shape[1]
    tm, tn = _tile(T, 1024), _tile(N, 768)

    def body(x_ref, wg_ref, wu_ref, h_ref, u_ref, a_ref):
        x = x_ref[...]
        chunks = [slice(c, c + MXU_COLS) for c in range(0, tn, MXU_COLS)]
        hs = [jnp.dot(x, wg_ref[:, c], preferred_element_type=F32) for c in chunks]
        us = [jnp.dot(x, wu_ref[:, c], preferred_element_type=F32) for c in chunks]
        for c, h, u in zip(chunks, hs, us):
            h_ref[:, c] = h.astype(BF16)
            u_ref[:, c] = u.astype(BF16)
            a_ref[:, c] = (h * jax.nn.sigmoid(h) * u).astype(BF16)

    wspec = pl.BlockSpec((K, tn), lambda j, i: (0, j))
    ospec = pl.BlockSpec((tm, tn), lambda j, i: (i, j))
    return pl.pallas_call(
        body, grid=(N // tn, T // tm),
        in_specs=[pl.BlockSpec((tm, K), lambda j, i: (i, 0)), wspec, wspec],
        out_specs=[ospec, ospec, ospec],
        out_shape=[jax.ShapeDtypeStruct((T, N), BF16)] * 3,
        compiler_params=_params("parallel", "parallel"), name=name)(xb, wg, wu)


def mm_res_ln(a, w, x, g, b, alpha, scale, name):
    T, K = a.shape
    D = w.shape[1]
    tm = _tile(T, 512)
    parts = [slice(r, r + MXU_COLS) for r in range(0, tm, MXU_COLS)] if tm % MXU_COLS == 0 else [slice(0, tm)]

    def body(a_ref, w_ref, x_ref, g_ref, b_ref, z_ref, o_ref, ob_ref):
        ys = [jnp.dot(a_ref[r, :], w_ref[...], preferred_element_type=F32) for r in parts]
        for r, y in zip(parts, ys):
            z = alpha * x_ref[r, :] + scale * y
            mu = jnp.mean(z, axis=1, keepdims=True)
            zc = z - mu
            var = jnp.mean(zc * zc, axis=1, keepdims=True)
            o = zc * lax.rsqrt(var + LN_EPS) * g_ref[...] + b_ref[...]
            z_ref[r, :] = z
            o_ref[r, :] = o
            ob_ref[r, :] = o.astype(BF16)

    row = pl.BlockSpec((tm, D), lambda i: (i, 0))
    vec = pl.BlockSpec((1, D), lambda i: (0, 0))
    return pl.pallas_call(
        body, grid=(T // tm,),
        in_specs=[pl.BlockSpec((tm, K), lambda i: (i, 0)), pl.BlockSpec((K, D), lambda i: (0, 0)), row, vec, vec],
        out_specs=[row, row, row],
        out_shape=[jax.ShapeDtypeStruct((T, D), F32), jax.ShapeDtypeStruct((T, D), F32),
                   jax.ShapeDtypeStruct((T, D), BF16)],
        compiler_params=_params("parallel"), name=name)(a, w, x, g, b)


def _ln_bwd_rows(zv, gain, dov, first, dz_ref, dzb_ref, dg_ref, db_ref):
    @pl.when(first)
    def _():
        dg_ref[...] = jnp.zeros_like(dg_ref)
        db_ref[...] = jnp.zeros_like(db_ref)

    mu = jnp.mean(zv, axis=1, keepdims=True)
    zc = zv - mu
    var = jnp.mean(zc * zc, axis=1, keepdims=True)
    rstd = lax.rsqrt(var + LN_EPS)
    xhat = zc * rstd
    dxhat = dov * gain
    m1 = jnp.mean(dxhat, axis=1, keepdims=True)
    m2 = jnp.mean(dxhat * xhat, axis=1, keepdims=True)
    dz = rstd * (dxhat - m1 - xhat * m2)
    dz_ref[...] = dz
    dzb_ref[...] = dz.astype(BF16)
    dg_ref[...] += jnp.sum(dov * xhat, axis=0, keepdims=True)
    db_ref[...] += jnp.sum(dov, axis=0, keepdims=True)


def _ln_bwd_outs(T, D):
    return [jax.ShapeDtypeStruct((T, D), F32), jax.ShapeDtypeStruct((T, D), BF16),
            jax.ShapeDtypeStruct((1, D), F32), jax.ShapeDtypeStruct((1, D), F32)]


def ln_bwd(z, g, do, name):
    T, D = z.shape
    tm = _tile(T, 512)

    def body(z_ref, g_ref, do_ref, dz_ref, dzb_ref, dg_ref, db_ref):
        _ln_bwd_rows(z_ref[...], g_ref[...], do_ref[...], pl.program_id(0) == 0, dz_ref, dzb_ref, dg_ref, db_ref)

    row = pl.BlockSpec((tm, D), lambda i: (i, 0))
    vec = pl.BlockSpec((1, D), lambda i: (0, 0))
    return pl.pallas_call(
        body, grid=(T // tm,), in_specs=[row, vec, row], out_specs=[row, row, vec, vec],
        out_shape=_ln_bwd_outs(T, D), compiler_params=_params("arbitrary"), name=name)(z, g, do)


def mm_nt(a, w, dep, name):
    T, K = a.shape
    N = w.shape[0]
    tm, tn = _tile(T, 1024), _tile(N, 512)

    def body(a_ref, w_ref, dep_ref, o_ref):
        o_ref[...] = lax.dot_general(a_ref[...], w_ref[...], NT_DIMS, preferred_element_type=F32).astype(o_ref.dtype)

    return pl.pallas_call(
        body, grid=(T // tm, N // tn),
        in_specs=[pl.BlockSpec((tm, K), lambda i, j: (i, 0)), pl.BlockSpec((tn, K), lambda i, j: (j, 0)), ANY],
        out_specs=pl.BlockSpec((tm, tn), lambda i, j: (i, j)),
        out_shape=jax.ShapeDtypeStruct((T, N), BF16),
        compiler_params=_params("parallel", "parallel"), name=name)(a, w, dep)


def ffn_bwd_mid(dzb, wd, h, u, scale, dep, name):
    T, K = dzb.shape
    N = wd.shape[0]
    tm, tn = _tile(T, 1024), _tile(N, 768)

    def body(dz_ref, w_ref, h_ref, u_ref, dep_ref, dh_ref, du_ref):
        dz = dz_ref[...]
        chunks = [slice(c, c + MXU_COLS) for c in range(0, tn, MXU_COLS)]
        das = [lax.dot_general(dz, w_ref[c, :], NT_DIMS, preferred_element_type=F32) for c in chunks]
        for c, da in zip(chunks, das):
            da = scale * da
            hv = h_ref[:, c].astype(F32)
            s = jax.nn.sigmoid(hv)
            silu = hv * s
            dh_ref[:, c] = (da * u_ref[:, c].astype(F32) * (s + silu * (1.0 - s))).astype(BF16)
            du_ref[:, c] = (da * silu).astype(BF16)

    tile = pl.BlockSpec((tm, tn), lambda j, i: (i, j))
    return pl.pallas_call(
        body, grid=(N // tn, T // tm),
        in_specs=[pl.BlockSpec((tm, K), lambda j, i: (i, 0)), pl.BlockSpec((tn, K), lambda j, i: (j, 0)), tile, tile, ANY],
        out_specs=[tile, tile],
        out_shape=[jax.ShapeDtypeStruct((T, N), BF16)] * 2,
        compiler_params=_params("parallel", "parallel"), name=name)(dzb, wd, h, u, dep)


def mm_nt_res(pairs, dz, alpha, ln_before, name):
    T, N = pairs[0][0].shape
    D = pairs[0][1].shape[0]
    n = len(pairs)
    tm = _tile(T, 512 if n == 1 else 256)

    def body(*refs):
        a_refs, w_refs, dz_ref = refs[:n], refs[n:2 * n], refs[2 * n]
        d = lax.dot_general(a_refs[0][...], w_refs[0][...], NT_DIMS, preferred_element_type=F32)
        for p in range(1, n):
            d += lax.dot_general(a_refs[p][...], w_refs[p][...], NT_DIMS, preferred_element_type=F32)
        d = d + alpha * dz_ref[...]
        if ln_before is None:
            refs[2 * n + 1][...] = d
        else:
            z_ref, g_ref = refs[2 * n + 1:2 * n + 3]
            _ln_bwd_rows(z_ref[...], g_ref[...], d, pl.program_id(0) == 0, *refs[2 * n + 3:])

    row = pl.BlockSpec((tm, D), lambda i: (i, 0))
    vec = pl.BlockSpec((1, D), lambda i: (0, 0))
    in_specs = ([pl.BlockSpec((tm, N), lambda i: (i, 0))] * n
                + [pl.BlockSpec((D, N), functools.partial(lambda i, col: (0, col), col=p[2])) for p in pairs] + [row])
    operands = [p[0] for p in pairs] + [p[1] for p in pairs] + [dz]
    if ln_before is None:
        return pl.pallas_call(
            body, grid=(T // tm,), in_specs=in_specs, out_specs=row, out_shape=jax.ShapeDtypeStruct((T, D), F32),
            compiler_params=_params("parallel"), name=name)(*operands)
    return pl.pallas_call(
        body, grid=(T // tm,), in_specs=in_specs + [row, vec], out_specs=[row, row, vec, vec],
        out_shape=_ln_bwd_outs(T, D), compiler_params=_params("arbitrary"), name=name)(*operands, *ln_before)


def mm_tn(a, b, scale, dep, name, into=None, col=0):
    T, M = a.shape
    N = b.shape[1]
    tm, tn, tk = _tile(M, 512), _tile(N, 1024), _tile(T, 4096)
    nk = T // tk
    if into is None:
        into = lax.empty((M, N), BF16)

    def body(a_ref, b_ref, dep_ref, into_ref, o_ref, acc_ref):
        k = pl.program_id(2)

        @pl.when(k == 0)
        def _():
            acc_ref[...] = jnp.zeros_like(acc_ref)

        acc_ref[...] += lax.dot_general(a_ref[...], b_ref[...], TN_DIMS, preferred_element_type=F32)

        @pl.when(k == nk - 1)
        def _():
            o_ref[...] = (scale * acc_ref[...]).astype(BF16)

    return pl.pallas_call(
        body, grid=(M // tm, N // tn, nk),
        in_specs=[pl.BlockSpec((tk, tm), lambda i, j, k: (k, i)), pl.BlockSpec((tk, tn), lambda i, j, k: (k, j)), ANY, ANY],
        out_specs=pl.BlockSpec((tm, tn), lambda i, j, k: (i, col * (N // tn) + j)),
        out_shape=jax.ShapeDtypeStruct(into.shape, BF16), input_output_aliases={3: 0},
        scratch_shapes=[pltpu.VMEM((tm, tn), F32)],
        compiler_params=_params("parallel", "parallel", "arbitrary"), name=name)(a, b, dep, into)


def loss_head(y, target, name):
    T, D = y.shape
    tm = _tile(T, 512)

    def body(y_ref, t_ref, l_ref, dy_ref):
        @pl.when(pl.program_id(0) == 0)
        def _():
            l_ref[...] = jnp.zeros_like(l_ref)

        e = y_ref[...] - t_ref[...]
        dy_ref[...] = e * (1.0 / D)
        rows = jnp.sum(e * e, axis=1, keepdims=True) * (0.5 / D)
        l_ref[...] += jnp.sum(rows, axis=0, keepdims=True)

    row = pl.BlockSpec((tm, D), lambda i: (i, 0))
    return pl.pallas_call(
        body, grid=(T // tm,), in_specs=[row, row],
        out_specs=[pl.BlockSpec((1, 1), lambda i: (0, 0)), row],
        out_shape=[jax.ShapeDtypeStruct((1, 1), F32), jax.ShapeDtypeStruct((T, D), F32)],
        compiler_params=_params("arbitrary"), name=name)(y, target)


def _rel_index(i, j):
    return jnp.clip(i - j + LEFT_CHUNKS * CHUNK, -REL_CLIP, REL_CLIP) + REL_CLIP


def bias_band(table_t, name):
    H = table_t.shape[0]
    rows = SUBLANE

    def body(t_ref, o_ref):
        i0 = pl.program_id(0) * rows
        parts = _split3(t_ref[...])
        r = lax.broadcasted_iota(jnp.int32, (REL_PAD, A_W), 0)
        j = lax.broadcasted_iota(jnp.int32, (REL_PAD, A_W), 1)
        for ii in range(rows):
            onehot = jnp.where(r == _rel_index(i0 + ii, j), 1.0, 0.0).astype(BF16)
            o_ref[ii] = sum(jnp.dot(p, onehot, preferred_element_type=F32) for p in parts)

    return pl.pallas_call(
        body, grid=(A_TQ // rows,),
        in_specs=[pl.BlockSpec((H, REL_PAD), lambda i: (0, 0))],
        out_specs=pl.BlockSpec((rows, H, A_W), lambda i: (i, 0, 0)),
        out_shape=jax.ShapeDtypeStruct((A_TQ, H, A_W), F32),
        compiler_params=_params("parallel"), name=name)(table_t)


def bias_band_bwd(dbands, name):
    H = dbands[0].shape[1]
    rows = SUBLANE
    n = len(dbands)

    def body(*refs):
        g_refs, o_ref = refs[:n], refs[n]

        @pl.when(pl.program_id(0) == 0)
        def _():
            o_ref[...] = jnp.zeros_like(o_ref)

        i0 = pl.program_id(0) * rows
        j = lax.broadcasted_iota(jnp.int32, (A_W, REL_PAD), 0)
        r = lax.broadcasted_iota(jnp.int32, (A_W, REL_PAD), 1)
        acc = jnp.zeros((H, REL_PAD), F32)
        for ii in range(rows):
            onehot = jnp.where(r == _rel_index(i0 + ii, j), 1.0, 0.0).astype(BF16)
            g = g_refs[0][ii]
            for q in range(1, n):
                g = g + g_refs[q][ii]
            acc += _dot3(g, onehot)
        o_ref[...] += acc

    spec = pl.BlockSpec((rows, H, A_W), lambda i: (i, 0, 0))
    return pl.pallas_call(
        body, grid=(A_TQ // rows,), in_specs=[spec] * n,
        out_specs=pl.BlockSpec((H, REL_PAD), lambda i: (0, 0)),
        out_shape=jax.ShapeDtypeStruct((H, REL_PAD), F32),
        compiler_params=_params("arbitrary"), name=name)(*dbands)


def _a_window(ref, qi):
    parts = []
    for d in range(A_NWB):
        kb = jnp.maximum(qi - (A_NWB - 1) + d, 0)
        parts.append(ref[pl.ds(pl.multiple_of(kb * A_TQ, A_TQ), A_TQ), :])
    return jnp.concatenate(parts, axis=0)


def _a_valid(qi):
    i = lax.broadcasted_iota(jnp.int32, (A_TQ, A_W), 0)
    j = lax.broadcasted_iota(jnp.int32, (A_TQ, A_W), 1)
    ic, jc = i // CHUNK, j // CHUNK
    return (jc >= ic) & (jc <= ic + LEFT_CHUNKS) & (j >= LEFT_CHUNKS * CHUNK - qi * A_TQ)


def _head_masks():
    lane = lax.broadcasted_iota(jnp.int32, (1, LANE), 1)
    return [lane < HEAD_DIM, lane >= HEAD_DIM]


def _a_probs(qms, kws, valids, b_ref):
    scores = [lax.dot_general(qm, kws[i // 2], NT_DIMS, preferred_element_type=F32) for i, qm in enumerate(qms)]
    probs = []
    for i, s in enumerate(scores):
        s = jnp.where(valids[i // 2], s * (HEAD_DIM ** -0.5) + b_ref[i % 2], NEG)
        p = jnp.exp(s - jnp.max(s, axis=1, keepdims=True))
        probs.append(p / jnp.sum(p, axis=1, keepdims=True))
    return probs


def _a_subtiles(ref, masks):
    out = []
    for u in range(A_SUB):
        x = ref[u * A_TQ:(u + 1) * A_TQ, :]
        out += [jnp.where(hm, x, jnp.zeros_like(x)) for hm in masks]
    return out


def _attn_specs(S, D, tq):
    hp_n = D // LANE
    nq = S // tq
    q = pl.BlockSpec((tq, LANE), lambda hp, b, qi: (b * nq + qi, hp))
    k = pl.BlockSpec((S, LANE), lambda hp, b, qi: (b, hp_n + hp))
    v = pl.BlockSpec((S, LANE), lambda hp, b, qi: (b, 2 * hp_n + hp))
    tile = pl.BlockSpec((tq, LANE), lambda hp, b, qi: (b * nq + qi, hp))
    seq = pl.BlockSpec((S, LANE), lambda hp, b, qi: (b, hp))
    return q, k, v, tile, seq


def attn_a_fwd(qkv, band, S, name):
    T, D3 = qkv.shape
    D = D3 // 3
    nb, ng = T // S, S // (A_TQ * A_SUB)

    def body(q_ref, k_ref, v_ref, b_ref, o_ref):
        qis = [pl.program_id(2) * A_SUB + u for u in range(A_SUB)]
        masks = _head_masks()
        kws, vws = [_a_window(k_ref, qi) for qi in qis], [_a_window(v_ref, qi) for qi in qis]
        probs = _a_probs(_a_subtiles(q_ref, masks), kws, [_a_valid(qi) for qi in qis], b_ref)
        outs = [jnp.dot(p.astype(BF16), vws[i // 2], preferred_element_type=F32) for i, p in enumerate(probs)]
        for u in range(A_SUB):
            o_ref[u * A_TQ:(u + 1) * A_TQ, :] = jnp.where(masks[0], outs[2 * u], outs[2 * u + 1]).astype(BF16)

    q, k, v, tile, _ = _attn_specs(S, D, A_TQ * A_SUB)
    return pl.pallas_call(
        body, grid=(D // LANE, nb, ng),
        in_specs=[q, k, v, pl.BlockSpec((2, A_TQ, A_W), lambda hp, b, qi: (hp, 0, 0))],
        out_specs=tile, out_shape=jax.ShapeDtypeStruct((T, D), BF16),
        compiler_params=_params("parallel", "parallel", "parallel"), name=name)(qkv, qkv, qkv, band)


def attn_a_bwd(qkv, band, do, S, name):
    T, D3 = qkv.shape
    D = D3 // 3
    nb, ng = T // S, S // (A_TQ * A_SUB)
    scale = HEAD_DIM ** -0.5

    def body(q_ref, k_ref, v_ref, b_ref, do_ref, dq_ref, dk_ref, dv_ref, db_ref, dk_acc, dv_acc):
        b, qg = pl.program_id(1), pl.program_id(2)

        @pl.when((b == 0) & (qg == 0))
        def _():
            db_ref[...] = jnp.zeros_like(db_ref)

        @pl.when(qg == 0)
        def _():
            dk_acc[...] = jnp.zeros_like(dk_acc)
            dv_acc[...] = jnp.zeros_like(dv_acc)

        qis = [qg * A_SUB + u for u in range(A_SUB)]
        masks = _head_masks()
        kws, vws = [_a_window(k_ref, qi) for qi in qis], [_a_window(v_ref, qi) for qi in qis]
        qms, doms = _a_subtiles(q_ref, masks), _a_subtiles(do_ref, masks)
        dps = [lax.dot_general(dom, vws[i // 2], NT_DIMS, preferred_element_type=F32) for i, dom in enumerate(doms)]
        probs = _a_probs(qms, kws, [_a_valid(qi) for qi in qis], b_ref)
        dss = [p * (dp - jnp.sum(p * dp, axis=1, keepdims=True)) for p, dp in zip(probs, dps)]
        for hh in range(2):
            db_ref[hh] += sum(dss[hh::2])
        dsbs = [ds.astype(BF16) for ds in dss]
        dqs = [jnp.dot(dsb, kws[i // 2], preferred_element_type=F32) for i, dsb in enumerate(dsbs)]
        dks = [lax.dot_general(dsb, qm, TN_DIMS, preferred_element_type=F32) for dsb, qm in zip(dsbs, qms)]
        dvs = [lax.dot_general(p.astype(BF16), dom, TN_DIMS, preferred_element_type=F32) for p, dom in zip(probs, doms)]
        for u, qi in enumerate(qis):
            dq_ref[u * A_TQ:(u + 1) * A_TQ, :] = (jnp.where(masks[0], dqs[2 * u], dqs[2 * u + 1]) * scale).astype(BF16)
            dkw = (dks[2 * u] + dks[2 * u + 1]) * scale
            dvw = dvs[2 * u] + dvs[2 * u + 1]
            for d in range(A_NWB):
                kb = jnp.maximum(qi - (A_NWB - 1) + d, 0)
                rows = pl.ds(pl.multiple_of(kb * A_TQ, A_TQ), A_TQ)
                dk_acc[rows, :] += dkw[d * A_TQ:(d + 1) * A_TQ]
                dv_acc[rows, :] += dvw[d * A_TQ:(d + 1) * A_TQ]

        @pl.when(qg == ng - 1)
        def _():
            dk_ref[...] = dk_acc[...].astype(BF16)
            dv_ref[...] = dv_acc[...].astype(BF16)

    q, k, v, tile, seq = _attn_specs(S, D, A_TQ * A_SUB)
    bspec = pl.BlockSpec((2, A_TQ, A_W), lambda hp, b, qi: (hp, 0, 0))
    act = jax.ShapeDtypeStruct((T, D), BF16)
    return pl.pallas_call(
        body, grid=(D // LANE, nb, ng),
        in_specs=[q, k, v, bspec, tile], out_specs=[tile, seq, seq, bspec],
        out_shape=[act, act, act, jax.ShapeDtypeStruct(band.shape, F32)],
        scratch_shapes=[pltpu.VMEM((S, LANE), F32), pltpu.VMEM((S, LANE), F32)],
        compiler_params=_params("arbitrary", "arbitrary", "arbitrary"), name=name)(qkv, qkv, qkv, band, do)


def _dot2(v, w):
    h = v.astype(BF16)
    lo = (v - h.astype(F32)).astype(BF16)
    return jnp.dot(h, w, preferred_element_type=F32) + jnp.dot(lo, w, preferred_element_type=F32)


def _b_weights(qms, kts, rights, after, diagonal):
    inst = [(t, hh) for t in range(len(kts)) for hh in range(2)]
    zs = [lax.dot_general(qms[hh], kts[t], NT_DIMS, preferred_element_type=F32) for t, hh in inst]
    lbs, l1ms, his, los = [], [], [], []
    causal = _strictly_causal() if any(diagonal) else None
    for (t, hh), z in zip(inst, zs):
        lb = jnp.minimum(z, 0.0) - jnp.log(1.0 + jnp.exp(jnp.minimum(z, -z)))
        lbs.append(lb)
        l1m = lb - z
        if diagonal[t]:
            l1m = jnp.where(causal, l1m, 0.0)
        hi = l1m.astype(BF16)
        l1ms.append(l1m)
        his.append(hi)
        los.append((l1m - hi.astype(F32)).astype(BF16))
    sums = [jnp.dot(hi, after, preferred_element_type=F32) + jnp.dot(lo, after, preferred_element_type=F32)
            for hi, lo in zip(his, los)]
    rights = list(rights)
    weights = []
    for (t, hh), lb, l1m, c in zip(inst, lbs, l1ms, sums):
        a = jnp.exp(lb + (rights[hh] + c))
        weights.append(jnp.where(causal, a, 0.0) if diagonal[t] else a)
        rights[hh] = rights[hh] + (c[:, :1] + l1m[:, :1])
    return zs, weights, rights


def _strictly_causal():
    row = lax.broadcasted_iota(jnp.int32, (B_T, B_T), 0)
    col = lax.broadcasted_iota(jnp.int32, (B_T, B_T), 1)
    return col < row


def _tri(strict_lower):
    r = lax.broadcasted_iota(jnp.int32, (B_T, B_T), 0)
    c = lax.broadcasted_iota(jnp.int32, (B_T, B_T), 1)
    return jnp.where((r > c) if strict_lower else (r < c), 1.0, 0.0).astype(BF16)


def _scaled_heads(q2):
    qs = q2 * (HEAD_DIM ** -0.5)
    return [jnp.where(hm, qs, jnp.zeros_like(qs)) for hm in _head_masks()]


def attn_b_fwd(qkv, S, name):
    T, D3 = qkv.shape
    D = D3 // 3
    nb, nq = T // S, S // B_T

    def body(q_ref, k_ref, v_ref, o_ref):
        qi = pl.program_id(2)
        after = _tri(True)
        qms = _scaled_heads(q_ref[...])

        def tiles(kbs, carry, diagonal):
            rows = [pl.ds(pl.multiple_of(kb * B_T, B_T), B_T) for kb in kbs]
            kts, vts = [k_ref[r, :] for r in rows], [v_ref[r, :] for r in rows]
            _, weights, rights = _b_weights(qms, kts, (carry[0], carry[2]), after, diagonal)
            accs = [carry[1], carry[3]]
            for i, a in enumerate(weights):
                accs[i % 2] = accs[i % 2] + jnp.dot(a.astype(BF16), vts[i // 2], preferred_element_type=F32)
            return rights[0], accs[0], rights[1], accs[1]

        init = (jnp.zeros((B_T, 1), F32), jnp.zeros((B_T, LANE), F32)) * 2
        res = lax.cond(qi % 2 == 1, lambda c: tiles([qi, qi - 1], c, [True, False]), lambda c: tiles([qi], c, [True]), init)
        top = qi - 1 - qi % 2
        res = lax.fori_loop(0, qi // 2, lambda p, c: tiles([top - 2 * p, top - 1 - 2 * p], c, [False, False]), res)
        o_ref[...] = jnp.where(_head_masks()[0], res[1], res[3]).astype(BF16)

    q, k, v, tile, _ = _attn_specs(S, D, B_T)
    return pl.pallas_call(
        body, grid=(D // LANE, nb, nq), in_specs=[q, k, v], out_specs=tile,
        out_shape=jax.ShapeDtypeStruct((T, D), BF16),
        compiler_params=_params("parallel", "parallel", "parallel"), name=name)(qkv, qkv, qkv)


def attn_b_bwd(qkv, do, S, name):
    T, D3 = qkv.shape
    D = D3 // 3
    nb, nq = T // S, S // B_T
    scale = HEAD_DIM ** -0.5

    def body(q_ref, k_ref, v_ref, do_ref, dq_ref, dk_ref, dv_ref, dk_acc, dv_acc, z_s, g_s):
        qi = pl.program_id(2)

        @pl.when(qi == 0)
        def _():
            dk_acc[...] = jnp.zeros_like(dk_acc)
            dv_acc[...] = jnp.zeros_like(dv_acc)

        do2 = do_ref[...]
        after, before = _tri(True), _tri(False)
        masks = _head_masks()
        qms = _scaled_heads(q_ref[...])
        doms = [jnp.where(hm, do2, jnp.zeros_like(do2)) for hm in masks]

        def sweep_left(kbs, rights, diagonal):
            rows = [pl.ds(pl.multiple_of(kb * B_T, B_T), B_T) for kb in kbs]
            kts, vts = [k_ref[r, :] for r in rows], [v_ref[r, :] for r in rows]
            das = [lax.dot_general(doms[hh], vt, NT_DIMS, preferred_element_type=F32) for vt in vts for hh in range(2)]
            zs, weights, rights = _b_weights(qms, kts, rights, after, diagonal)
            for i, (z, a, da) in enumerate(zip(zs, weights, das)):
                z_s[i % 2, kbs[i // 2]] = z
                g_s[i % 2, kbs[i // 2]] = da * a
            dvs = [lax.dot_general(a.astype(BF16), doms[i % 2], TN_DIMS, preferred_element_type=F32)
                   for i, a in enumerate(weights)]
            for t, r in enumerate(rows):
                dv_acc[r, :] += dvs[2 * t] + dvs[2 * t + 1]
            return tuple(rights)

        zero_col = jnp.zeros((B_T, 1), F32)
        odd = qi % 2 == 1
        top = qi - 1 - qi % 2
        rights = lax.cond(odd, lambda c: sweep_left([qi, qi - 1], c, [True, False]), lambda c: sweep_left([qi], c, [True]),
                          (zero_col, zero_col))
        lax.fori_loop(0, qi // 2, lambda p, c: sweep_left([top - 2 * p, top - 1 - 2 * p], c, [False, False]), rights)

        def sweep_right(kbs, carry, diagonal):
            rows = [pl.ds(pl.multiple_of(kb * B_T, B_T), B_T) for kb in kbs]
            inst = [(t, hh) for t in range(len(kbs)) for hh in range(2)]
            gs = [g_s[hh, kbs[t]] for t, hh in inst]
            sums = [_dot2(g, before) for g in gs]
            lefts, dqs = [carry[0], carry[2]], [carry[1], carry[3]]
            dzbs = []
            for (t, hh), g, c in zip(inst, gs, sums):
                beta = jax.nn.sigmoid(z_s[hh, kbs[t]])
                dz = g * (1.0 - beta) - beta * (lefts[hh] + c)
                if diagonal[t]:
                    dz = jnp.where(_strictly_causal(), dz, 0.0)
                dzbs.append(dz.astype(BF16))
                lefts[hh] = lefts[hh] + jnp.sum(g, axis=1, keepdims=True)
            dks = [lax.dot_general(dzb, qms[hh], TN_DIMS, preferred_element_type=F32) for (t, hh), dzb in zip(inst, dzbs)]
            for (t, hh), dzb in zip(inst, dzbs):
                dqs[hh] = dqs[hh] + jnp.dot(dzb, k_ref[rows[t], :], preferred_element_type=F32)
            for t, r in enumerate(rows):
                dk_acc[r, :] += dks[2 * t] + dks[2 * t + 1]
            return lefts[0], dqs[0], lefts[1], dqs[1]

        init = (zero_col, jnp.zeros((B_T, LANE), F32)) * 2
        res = lax.fori_loop(0, qi // 2, lambda p, c: sweep_right([2 * p, 2 * p + 1], c, [False, False]), init)
        res = lax.cond(odd, lambda c: sweep_right([qi - 1, qi], c, [False, True]), lambda c: sweep_right([qi], c, [True]), res)
        dq_ref[...] = (jnp.where(masks[0], res[1], res[3]) * scale).astype(BF16)

        @pl.when(qi == nq - 1)
        def _():
            dk_ref[...] = dk_acc[...].astype(BF16)
            dv_ref[...] = dv_acc[...].astype(BF16)

    q, k, v, tile, seq = _attn_specs(S, D, B_T)
    act = jax.ShapeDtypeStruct((T, D), BF16)
    return pl.pallas_call(
        body, grid=(D // LANE, nb, nq),
        in_specs=[q, k, v, tile], out_specs=[tile, seq, seq], out_shape=[act, act, act],
        scratch_shapes=[pltpu.VMEM((S, LANE), F32), pltpu.VMEM((S, LANE), F32),
                        pltpu.VMEM((2, nq, B_T, B_T), F32), pltpu.VMEM((2, nq, B_T, B_T), F32)],
        compiler_params=_params("arbitrary", "arbitrary", "arbitrary"), name=name)(qkv, qkv, qkv, do)


HBM = pl.BlockSpec(memory_space=pltpu.HBM)
SEM = pl.BlockSpec(memory_space=pltpu.SEMAPHORE)
N_PEERS = N_DEV - 1
GATHERS_AHEAD = 3


def _place():
    return lax.axis_index("x"), lax.axis_index("y"), lax.axis_index("c")


def _peers(x, y, c):
    return [(1 - x if r & 4 else x, 1 - y if r & 2 else y, 1 - c if r & 1 else c) for r in range(1, N_DEV)]


def _block(ref, shape, by_cols, j):
    r, w = shape
    if by_cols:
        return ref.at[:, pl.ds(pl.multiple_of(j * w, LANE), w)]
    return ref.at[pl.ds(pl.multiple_of(j * r, SUBLANE), r), :]


def _exchange_copies(gather, src_refs, land_refs, send_sems, recv_sems, by_cols):
    x, y, c = _place()
    me = 4 * x + 2 * y + c
    out = []
    for t, (src, land) in enumerate(zip(src_refs, land_refs)):
        for r, peer in enumerate(_peers(x, y, c)):
            pj = 4 * peer[0] + 2 * peer[1] + peer[2]
            if gather:
                mine, to_me, theirs = src, _block(land, src.shape, by_cols[t], me), _block(land, src.shape, by_cols[t], pj)
            else:
                mine, to_me, theirs = _block(src, land.shape[1:], by_cols[t], pj), land.at[me], land.at[pj]
            sems = dict(send_sem=send_sems.at[N_PEERS * t + r], recv_sem=recv_sems.at[N_PEERS * t + r],
                        device_id=peer, device_id_type=MESH_ID)
            out.append((pltpu.make_async_remote_copy(src_ref=mine, dst_ref=to_me, **sems),
                        pltpu.make_async_remote_copy(src_ref=mine, dst_ref=theirs, **sems)))
    return out


def _own_copies(gather, src_refs, land_refs, own_sems, by_cols):
    x, y, c = _place()
    me = 4 * x + 2 * y + c
    out = []
    for t, (src, land) in enumerate(zip(src_refs, land_refs)):
        if gather:
            out.append(pltpu.make_async_copy(src, _block(land, src.shape, by_cols[t], me), own_sems.at[t]))
        else:
            out.append(pltpu.make_async_copy(_block(src, land.shape[1:], by_cols[t], me), land.at[me], own_sems.at[t]))
    return out


def exchange_start(gather, srcs, land_shapes, by_cols, after, name):
    n = len(srcs)

    def body(*refs):
        src_refs, land_refs = refs[:n], refs[n:2 * n]
        send_sems, recv_sems, own_sems = refs[2 * n + 1:2 * n + 4]
        token = refs[-1]
        for cp in _own_copies(gather, src_refs, land_refs, own_sems, by_cols):
            cp.start()
        for mine, _ in _exchange_copies(gather, src_refs, land_refs, send_sems, recv_sems, by_cols):
            mine.start()
        token[...] = jnp.zeros_like(token)

    lands = [pltpu.with_memory_space_constraint(lax.empty(s.shape, s.dtype), pltpu.HBM) for s in land_shapes]
    srcs = [pltpu.with_memory_space_constraint(s, pltpu.HBM) for s in srcs]
    res = pl.pallas_call(
        body, name=name,
        out_shape=(pltpu.SemaphoreType.DMA((N_PEERS * n,)), pltpu.SemaphoreType.DMA((N_PEERS * n,)),
                   pltpu.SemaphoreType.DMA((n,)),
                   *[pltpu.HBM(s.shape, s.dtype) for s in srcs], *[pltpu.HBM(s.shape, s.dtype) for s in land_shapes],
                   jax.ShapeDtypeStruct((SUBLANE, LANE), F32)),
        in_specs=[HBM] * (2 * n) + [ANY],
        out_specs=(SEM, SEM, SEM, *[HBM] * (2 * n), pl.BlockSpec(memory_space=pltpu.VMEM)),
        input_output_aliases={i: 3 + i for i in range(2 * n)},
        compiler_params=pltpu.CompilerParams(has_side_effects=pltpu.SideEffectType.DATAFLOW_SIDE_EFFECTING),
    )(*srcs, *lands, after)
    return dict(gather=gather, n=n, by_cols=by_cols, sems=res[:3], srcs=res[3:3 + n],
                lands=res[3 + n:3 + 2 * n], token=res[-1])


def exchange_wait(started, after, name):
    n, gather, by_cols = started["n"], started["gather"], started["by_cols"]

    def body(*refs):
        src_refs, land_refs = refs[:n], refs[n:2 * n]
        send_sems, recv_sems, own_sems = refs[2 * n:2 * n + 3]
        for mine, theirs in _exchange_copies(gather, src_refs, land_refs, send_sems, recv_sems, by_cols):
            mine.wait_send()
            theirs.wait_recv()
        for cp in _own_copies(gather, src_refs, land_refs, own_sems, by_cols):
            cp.wait()

    res = pl.pallas_call(
        body, name=name,
        out_shape=tuple(pltpu.HBM(s.shape, s.dtype) for s in (*started["srcs"], *started["lands"])),
        in_specs=[HBM] * (2 * n) + [SEM, SEM, SEM, ANY], out_specs=tuple([HBM] * (2 * n)),
        input_output_aliases={i: i for i in range(2 * n)},
        compiler_params=pltpu.CompilerParams(has_side_effects=pltpu.SideEffectType.DATAFLOW_SIDE_EFFECTING),
    )(*started["srcs"], *started["lands"], *started["sems"], after)
    return res[n:]


def gather_small(v, dep, name):
    R, C = v.shape

    def body(v_ref, dep_ref, o_ref, send_sems, recv_sems):
        x, y, c = _place()
        o_ref[4 * x + 2 * y + c] = v_ref[...]
        peers = _peers(x, y, c)

        def copy(r, owner, to):
            slot = o_ref.at[4 * owner[0] + 2 * owner[1] + owner[2]]
            return pltpu.make_async_remote_copy(
                src_ref=slot, dst_ref=slot, send_sem=send_sems.at[r], recv_sem=recv_sems.at[r],
                device_id=to, device_id_type=MESH_ID)

        sends = [copy(r, (x, y, c), peer) for r, peer in enumerate(peers)]
        for cp in sends:
            cp.start()
        for r, peer in enumerate(peers):
            copy(r, peer, (x, y, c)).wait_recv()
        for cp in sends:
            cp.wait_send()

    vm = pl.BlockSpec(memory_space=pltpu.VMEM)
    return pl.pallas_call(
        body, in_specs=[vm, ANY], out_specs=vm, out_shape=jax.ShapeDtypeStruct((N_DEV, R, C), F32),
        scratch_shapes=[pltpu.SemaphoreType.DMA((N_PEERS,)), pltpu.SemaphoreType.DMA((N_PEERS,))],
        name=name)(v, dep)


def sum_devices(g, name):
    _, R, C = g.shape

    def body(g_ref, o_ref):
        acc = g_ref[0]
        for j in range(1, N_DEV):
            acc = acc + g_ref[j]
        o_ref[...] = acc

    vm = pl.BlockSpec(memory_space=pltpu.VMEM)
    return pl.pallas_call(body, in_specs=[vm], out_specs=vm, out_shape=jax.ShapeDtypeStruct((R, C), F32), name=name)(g)


def adamw(w, m, v, index, parts, outs, name):
    L, R, C = w.shape
    if outs is None:
        outs = [lax.empty((L, R, C), F32) for _ in range(4)]
    P, Rp, Cp = parts.shape
    tr = _tile(R, 512, SUBLANE) if Rp == R else R

    def body(w_ref, m_ref, v_ref, p_ref, g_in, d_in, m_in, v_in, g_out, d_out, m_out, v_out):
        g = p_ref[0, :tr, :C].astype(F32)
        for q in range(1, P):
            g = g + p_ref[q, :tr, :C].astype(F32)
        mn = ADAM_B1 * m_ref[...] + (1.0 - ADAM_B1) * g
        vn = ADAM_B2 * v_ref[...] + (1.0 - ADAM_B2) * (g * g)
        m_hat = mn / (1.0 - ADAM_B1 ** ADAM_STEP)
        v_hat = vn / (1.0 - ADAM_B2 ** ADAM_STEP)
        g_out[...] = g
        d_out[...] = -ADAM_LR * (m_hat / (jnp.sqrt(v_hat) + ADAM_EPS) + ADAM_WD * w_ref[...])
        m_out[...] = mn
        v_out[...] = vn

    slab = pl.BlockSpec((None, tr, C), lambda r: (index, r, 0))
    pspec = pl.BlockSpec((P, tr if Rp == R else Rp, Cp), lambda r: (0, r, 0))
    return pl.pallas_call(
        body, grid=(R // tr,), in_specs=[slab] * 3 + [pspec] + [ANY] * 4, out_specs=[slab] * 4,
        out_shape=[jax.ShapeDtypeStruct((L, R, C), F32)] * 4,
        input_output_aliases={4 + q: q for q in range(4)},
        compiler_params=_params("parallel"), name=name)(w, m, v, parts, *outs)


def _pad_to(a, axis, size):
    pad = [(0, 0)] * a.ndim
    pad[axis] = (0, size - a.shape[axis])
    return jnp.pad(a, pad)


def kernel(x, w_qkv_a, w_o_a, rel_bias, w_qkv_b, w_o_b, ffn_w_gate, ffn_w_up, ffn_w_down, ln_g, ln_b, loss_target, m_w_qkv_a, m_w_o_a, m_rel_bias, m_w_qkv_b, m_w_o_b, m_ffn_w_gate, m_ffn_w_up, m_ffn_w_down, m_ln_g, m_ln_b, v_w_qkv_a, v_w_o_a, v_rel_bias, v_w_qkv_b, v_w_o_b, v_ffn_w_gate, v_ffn_w_up, v_ffn_w_down, v_ln_g, v_ln_b):
    nb, S, D = x.shape
    T = nb * S
    depth = ffn_w_gate.shape[0]
    H = D // HEAD_DIM
    fs = ffn_w_gate.shape[-1]
    fp = -(-fs // LANE) * LANE
    alpha = (2.0 * depth) ** 0.25
    cx, cy, cc = _place()
    me = 4 * cx + 2 * cy + cc

    ln_local = jnp.concatenate([ln_g.reshape(depth * 3, -1), ln_b.reshape(depth * 3, -1)], axis=0)
    ln_all = gather_small(ln_local, ln_local, "gather_ln")
    ln_full = jnp.transpose(ln_all, (1, 0, 2)).reshape(2 * depth * 3, D)
    ln_gain = lambda i, s: ln_full[3 * i + s][None, :]
    ln_bias = lambda i, s: ln_full[3 * depth + 3 * i + s][None, :]

    table_t = _pad_to(rel_bias.T, 1, REL_PAD)
    band = jnp.transpose(bias_band(table_t, "bias_band"), (1, 0, 2))

    subs = []
    for i in range(depth):
        for s in (0, 1, 2):
            if s == 1:
                wq, wo = (w_qkv_a, w_o_a) if i % 2 == 0 else (w_qkv_b, w_o_b)
                subs.append(([wq[i // 2].astype(BF16), wo[i // 2].astype(BF16)], [True, False]))
            else:
                f = 0 if s == 0 else 1
                subs.append(([_pad_to(ffn_w_gate[i, f].astype(BF16), 1, fp), _pad_to(ffn_w_up[i, f].astype(BF16), 1, fp),
                              _pad_to(ffn_w_down[i, f].astype(BF16), 0, fp)], [True, True, False]))

    def start_gather(k, after):
        shards, by_cols = subs[k]
        groups = [(0, 2), (2, 3)] if k == 0 else [(0, len(shards))]
        out = []
        for part, (lo, hi) in enumerate(groups):
            shapes = [jax.ShapeDtypeStruct((s.shape[0], N_DEV * s.shape[1]) if col else (N_DEV * s.shape[0], s.shape[1]), BF16)
                      for s, col in zip(shards[lo:hi], by_cols[lo:hi])]
            out.append(exchange_start(True, shards[lo:hi], shapes, by_cols[lo:hi], after, f"gather_start_{k}_{part}"))
            after = out[-1]["token"]
        return out

    xf = x.reshape(T, D)
    act, act_b = xf, xf.astype(BF16)
    gathers = {}
    for k in range(min(GATHERS_AHEAD, len(subs))):
        gathers[k] = start_gather(k, xf if k == 0 else gathers[k - 1][-1]["token"])
    newest = gathers[k][-1]["token"]
    saved = []
    for i in range(depth):
        layer = {}
        for s in (0, 1, 2):
            k = 3 * i + s
            tag = f"L{i}S{s}"
            if 0 < k and k + GATHERS_AHEAD - 1 < len(subs):
                gathers[k + GATHERS_AHEAD - 1] = start_gather(k + GATHERS_AHEAD - 1, act_b)
                newest = gathers[k + GATHERS_AHEAD - 1][-1]["token"]
            parts = gathers.pop(k)
            full = list(exchange_wait(parts[0], newest, f"gather_wait_{k}_0"))
            if s == 1:
                wqkv_f, wo_f = full
                qkv = mm_nn(act_b, wqkv_f, "qkv_" + tag)
                if i % 2 == 0:
                    att = attn_a_fwd(qkv, band, S, "attn_a_fwd_" + tag)
                else:
                    att = attn_b_fwd(qkv, S, "attn_b_fwd_" + tag)
                z, o, ob = mm_res_ln(att, wo_f, act, ln_gain(i, s), ln_bias(i, s), alpha, 1.0, "out_ln_" + tag)
                layer[s] = dict(x_b=act_b, qkv=qkv, att=att, z=z, wqkv=wqkv_f, wo=wo_f)
            else:
                wg_f, wu_f = full[:2]
                h, u, a = ffn_up(act_b, wg_f, wu_f, "ffn_up_" + tag)
                wd_f = full[2] if len(parts) == 1 else exchange_wait(parts[1], a, f"gather_wait_{k}_1")[0]
                z, o, ob = mm_res_ln(a, wd_f, act, ln_gain(i, s), ln_bias(i, s), alpha, 0.5, "down_ln_" + tag)
                layer[s] = dict(x_b=act_b, h=h, u=u, a=a, z=z, wg=wg_f, wu=wu_f, wd=wd_f)
            act, act_b = o, ob
        saved.append(layer)

    loss_local, d_act = loss_head(act, loss_target.reshape(T, D), "loss_head")
    loss = lax.psum(loss_local[0, 0], ("x", "y", "c"))

    results = {}

    def update(name, w, m, v, index, parts):
        L = w.shape[0] if w.ndim == 3 else w.shape[0] * w.shape[1]
        flat = lambda t: t.reshape((L,) + t.shape[-2:])
        results[name] = adamw(flat(w), flat(m), flat(v), index, parts, results.get(name), f"adamw_{name}_{index}")

    def finish(entry, after):
        exchange, targets, tag = entry
        lands = exchange_wait(exchange, after, "scatter_wait_" + tag)
        for (name, w, m, v, index), parts in zip(targets, lands):
            update(name, w, m, v, index, parts)
        return results[targets[-1][0]][0]

    pending = []
    started = d_act
    dbands = []
    dln_g = [None] * (3 * depth)
    dln_b = [None] * (3 * depth)
    ln_grads = ln_bwd(saved[depth - 1][2]["z"], ln_gain(depth - 1, 2), d_act, "ln_bwd_last")
    for i in reversed(range(depth)):
        for s in (2, 1, 0):
            tag = f"L{i}S{s}"
            sv = saved[i][s]
            dz, dzb, dln_g[3 * i + s], dln_b[3 * i + s] = ln_grads
            before = (i, s - 1) if s > 0 else (i - 1, 2) if i > 0 else None
            ln_before = None if before is None else (saved[before[0]][before[1]]["z"], ln_gain(*before))
            if s == 1:
                j = i // 2
                d_att = mm_nt(dzb, sv["wo"], started, "att_bwd_" + tag)
                g_wo = mm_tn(sv["att"], dzb, 1.0, started, "dwo_" + tag)
                if i % 2 == 0:
                    dq, dk, dv, dband = attn_a_bwd(sv["qkv"], band, d_att, S, "attn_a_bwd_" + tag)
                    dbands.append(jnp.transpose(dband, (1, 0, 2)))
                else:
                    dq, dk, dv = attn_b_bwd(sv["qkv"], d_att, S, "attn_b_bwd_" + tag)
                g_wqkv = lax.empty((D, 3 * D), BF16)
                for col, (piece, d_piece) in enumerate(zip("qkv", (dq, dk, dv))):
                    g_wqkv = mm_tn(sv["x_b"], d_piece, 1.0, started, f"dw{piece}_" + tag, g_wqkv, col)
                ln_grads = mm_nt_res([(dq, sv["wqkv"], 0), (dk, sv["wqkv"], 1), (dv, sv["wqkv"], 2)], dz, alpha, ln_before,
                                     "dx_mix_" + tag)
                grads, by_cols = [g_wqkv, g_wo], [True, False]
                slabs = [(D, 3 * D // N_DEV), (D // N_DEV, D)]
                if i % 2 == 0:
                    targets = [("w_qkv_a", w_qkv_a, m_w_qkv_a, v_w_qkv_a, j), ("w_o_a", w_o_a, m_w_o_a, v_w_o_a, j)]
                else:
                    targets = [("w_qkv_b", w_qkv_b, m_w_qkv_b, v_w_qkv_b, j), ("w_o_b", w_o_b, m_w_o_b, v_w_o_b, j)]
            else:
                f = 0 if s == 0 else 1
                dh, du = ffn_bwd_mid(dzb, sv["wd"], sv["h"], sv["u"], 0.5, started, "ffn_mid_" + tag)
                g_wd = mm_tn(sv["a"], dzb, 0.5, started, "dwd_" + tag)
                g_wg = mm_tn(sv["x_b"], dh, 1.0, started, "dwg_" + tag)
                g_wu = mm_tn(sv["x_b"], du, 1.0, started, "dwu_" + tag)
                ln_grads = mm_nt_res([(dh, sv["wg"], 0), (du, sv["wu"], 0)], dz, alpha, ln_before, "dx_ffn_" + tag)
                grads, by_cols = [g_wg, g_wu, g_wd], [True, True, False]
                slabs = [(D, fp), (D, fp), (fp, D)]
                idx = 2 * i + f
                targets = [("ffn_w_gate", ffn_w_gate, m_ffn_w_gate, v_ffn_w_gate, idx),
                           ("ffn_w_up", ffn_w_up, m_ffn_w_up, v_ffn_w_up, idx),
                           ("ffn_w_down", ffn_w_down, m_ffn_w_down, v_ffn_w_down, idx)]
            after = ln_grads[0] if before is not None else ln_grads
            if before is None:
                grad_x = ln_grads.reshape(nb, S, D)
                dtable_t = bias_band_bwd(dbands, "bias_band_bwd")
                small = jnp.concatenate(dln_g + dln_b + [_pad_to(dtable_t, 1, D)], axis=0)
                small = _pad_to(small, 0, -(-small.shape[0] // SUBLANE) * SUBLANE)
                total = after = sum_devices(gather_small(small, grads[0], "gather_small_grads"), "sum_small_grads")
            shapes = [jax.ShapeDtypeStruct((N_DEV,) + slab, BF16) for slab in slabs]
            pending.append((exchange_start(False, grads, shapes, by_cols, after, "scatter_start_" + tag), targets, tag))
            started = pending[-1][0]["token"]

    last = pending.pop()
    done = started
    for entry in pending:
        done = finish(entry, done)
    finish(last, done)
    n_ln = 3 * depth
    g_ln_g = lax.dynamic_slice_in_dim(total[:n_ln], me * (D // N_DEV), D // N_DEV, axis=1)
    g_ln_b = lax.dynamic_slice_in_dim(total[n_ln:2 * n_ln], me * (D // N_DEV), D // N_DEV, axis=1)
    g_rel = total[2 * n_ln:2 * n_ln + H, :N_REL].T
    as3 = lambda t: t.reshape((1, -1, t.shape[-1]))
    results["ln_g"] = adamw(as3(ln_g), as3(m_ln_g), as3(v_ln_g), 0, g_ln_g[None], None, "adamw_ln_g")
    results["ln_b"] = adamw(as3(ln_b), as3(m_ln_b), as3(v_ln_b), 0, g_ln_b[None], None, "adamw_ln_b")
    results["rel_bias"] = adamw(as3(rel_bias), as3(m_rel_bias), as3(v_rel_bias), 0, g_rel[None], None, "adamw_rel_bias")

    order = [("w_qkv_a", w_qkv_a), ("w_o_a", w_o_a), ("rel_bias", rel_bias), ("w_qkv_b", w_qkv_b), ("w_o_b", w_o_b),
             ("ffn_w_gate", ffn_w_gate), ("ffn_w_up", ffn_w_up), ("ffn_w_down", ffn_w_down), ("ln_g", ln_g), ("ln_b", ln_b)]
    outs = [loss, grad_x]
    for q in range(4):
        for name, like in order:
            outs.append(results[name][q].reshape(like.shape))
    return tuple(outs)
```

```python
import functools

import jax
import jax.numpy as jnp
from jax import lax
from jax.experimental import pallas as pl
from jax.experimental.pallas import tpu as pltpu

BF16 = jnp.bfloat16
F32 = jnp.float32
MESH_ID = pl.DeviceIdType.MESH
ANY = pl.BlockSpec(memory_space=pl.ANY)

N_DEV = 8
LANE = 128
MXU_COLS = 256
SUBLANE = 8
VMEM_LIMIT = 56 * 1024 * 1024

HEAD_DIM = 64
CHUNK = 64
LEFT_CHUNKS = 8
REL_CLIP = 128
N_REL = 2 * REL_CLIP + 1
REL_PAD = 384
LN_EPS = 1e-5
ADAM_LR, ADAM_B1, ADAM_B2, ADAM_EPS, ADAM_WD, ADAM_STEP = 0.001, 0.9, 0.999, 1e-08, 0.01, 10
NEG = -1e30

A_TQ = 128
A_NWB = LEFT_CHUNKS * CHUNK // A_TQ + 1
A_W = A_NWB * A_TQ
A_SUB = 8
B_T = 256

NT_DIMS = (((1,), (1,)), ((), ()))
TN_DIMS = (((0,), (0,)), ((), ()))


def _tile(n, pref, unit=LANE):
    if n <= pref:
        return n
    t = pref - pref % unit
    while t > unit and n % t:
        t -= unit
    assert n % t == 0, (n, pref)
    return t


def _params(*sem):
    return pltpu.CompilerParams(dimension_semantics=sem, vmem_limit_bytes=VMEM_LIMIT)


def _split3(v):
    h = v.astype(BF16)
    r = v - h.astype(F32)
    m = r.astype(BF16)
    lo = (r - m.astype(F32)).astype(BF16)
    return h, m, lo


def _dot3(v, w):
    h, m, lo = _split3(v)
    return (jnp.dot(h, w, preferred_element_type=F32) + jnp.dot(m, w, preferred_element_type=F32)
            + jnp.dot(lo, w, preferred_element_type=F32))


def mm_nn(a, w, name):
    T, K = a.shape
    N = w.shape[1]
    tm, tn = _tile(T, 1024), _tile(N, 768)

    def body(a_ref, w_ref, o_ref):
        o_ref[...] = jnp.dot(a_ref[...], w_ref[...], preferred_element_type=F32).astype(o_ref.dtype)

    return pl.pallas_call(
        body, grid=(T // tm, N // tn),
        in_specs=[pl.BlockSpec((tm, K), lambda i, j: (i, 0)), pl.BlockSpec((K, tn), lambda i, j: (0, j))],
        out_specs=pl.BlockSpec((tm, tn), lambda i, j: (i, j)),
        out_shape=jax.ShapeDtypeStruct((T, N), BF16),
        compiler_params=_params("parallel", "parallel"), name=name)(a, w)


def ffn_up(xb, wg, wu, name):
    T, K = xb.shape
    N = wg.shape[1]
    tm, tn = _tile(T, 1024), _tile(N, 768)

    def body(x_ref, wg_ref, wu_ref, h_ref, u_ref, a_ref):
        x = x_ref[...]
        chunks = [slice(c, c + MXU_COLS) for c in range(0, tn, MXU_COLS)]
        hs = [jnp.dot(x, wg_ref[:, c], preferred_element_type=F32) for c in chunks]
        us = [jnp.dot(x, wu_ref[:, c], preferred_element_type=F32) for c in chunks]
        for c, h, u in zip(chunks, hs, us):
            h_ref[:, c] = h.astype(BF16)
            u_ref[:, c] = u.astype(BF16)
            a_ref[:, c] = (h * jax.nn.sigmoid(h) * u).astype(BF16)

    wspec = pl.BlockSpec((K, tn), lambda j, i: (0, j))
    ospec = pl.BlockSpec((tm, tn), lambda j, i: (i, j))
    return pl.pallas_call(
        body, grid=(N // tn, T // tm),
        in_specs=[pl.BlockSpec((tm, K), lambda j, i: (i, 0)), wspec, wspec],
        out_specs=[ospec, ospec, ospec],
        out_shape=[jax.ShapeDtypeStruct((T, N), BF16)] * 3,
        compiler_params=_params("parallel", "parallel"), name=name)(xb, wg, wu)


def mm_res_ln(a, w, x, g, b, alpha, scale, name):
    T, K = a.shape
    D = w.shape[1]
    tm = _tile(T, 512)
    parts = [slice(r, r + MXU_COLS) for r in range(0, tm, MXU_COLS)] if tm % MXU_COLS == 0 else [slice(0, tm)]

    def body(a_ref, w_ref, x_ref, g_ref, b_ref, z_ref, o_ref, ob_ref):
        ys = [jnp.dot(a_ref[r, :], w_ref[...], preferred_element_type=F32) for r in parts]
        for r, y in zip(parts, ys):
            z = alpha * x_ref[r, :] + scale * y
            mu = jnp.mean(z, axis=1, keepdims=True)
            zc = z - mu
            var = jnp.mean(zc * zc, axis=1, keepdims=True)
            o = zc * lax.rsqrt(var + LN_EPS) * g_ref[...] + b_ref[...]
            z_ref[r, :] = z
            o_ref[r, :] = o
            ob_ref[r, :] = o.astype(BF16)

    row = pl.BlockSpec((tm, D), lambda i: (i, 0))
    vec = pl.BlockSpec((1, D), lambda i: (0, 0))
    return pl.pallas_call(
        body, grid=(T // tm,),
        in_specs=[pl.BlockSpec((tm, K), lambda i: (i, 0)), pl.BlockSpec((K, D), lambda i: (0, 0)), row, vec, vec],
        out_specs=[row, row, row],
        out_shape=[jax.ShapeDtypeStruct((T, D), F32), jax.ShapeDtypeStruct((T, D), F32),
                   jax.ShapeDtypeStruct((T, D), BF16)],
        compiler_params=_params("parallel"), name=name)(a, w, x, g, b)


def _ln_bwd_rows(zv, gain, dov, first, dz_ref, dzb_ref, dg_ref, db_ref):
    @pl.when(first)
    def _():
        dg_ref[...] = jnp.zeros_like(dg_ref)
        db_ref[...] = jnp.zeros_like(db_ref)

    mu = jnp.mean(zv, axis=1, keepdims=True)
    zc = zv - mu
    var = jnp.mean(zc * zc, axis=1, keepdims=True)
    rstd = lax.rsqrt(var + LN_EPS)
    xhat = zc * rstd
    dxhat = dov * gain
    m1 = jnp.mean(dxhat, axis=1, keepdims=True)
    m2 = jnp.mean(dxhat * xhat, axis=1, keepdims=True)
    dz = rstd * (dxhat - m1 - xhat * m2)
    dz_ref[...] = dz
    dzb_ref[...] = dz.astype(BF16)
    dg_ref[...] += jnp.sum(dov * xhat, axis=0, keepdims=True)
    db_ref[...] += jnp.sum(dov, axis=0, keepdims=True)


def _ln_bwd_outs(T, D):
    return [jax.ShapeDtypeStruct((T, D), F32), jax.ShapeDtypeStruct((T, D), BF16),
            jax.ShapeDtypeStruct((1, D), F32), jax.ShapeDtypeStruct((1, D), F32)]


def ln_bwd(z, g, do, name):
    T, D = z.shape
    tm = _tile(T, 512)

    def body(z_ref, g_ref, do_ref, dz_ref, dzb_ref, dg_ref, db_ref):
        _ln_bwd_rows(z_ref[...], g_ref[...], do_ref[...], pl.program_id(0) == 0, dz_ref, dzb_ref, dg_ref, db_ref)

    row = pl.BlockSpec((tm, D), lambda i: (i, 0))
    vec = pl.BlockSpec((1, D), lambda i: (0, 0))
    return pl.pallas_call(
        body, grid=(T // tm,), in_specs=[row, vec, row], out_specs=[row, row, vec, vec],
        out_shape=_ln_bwd_outs(T, D), compiler_params=_params("arbitrary"), name=name)(z, g, do)


def mm_nt(a, w, dep, name):
    T, K = a.shape
    N = w.shape[0]
    tm, tn = _tile(T, 1024), _tile(N, 512)

    def body(a_ref, w_ref, dep_ref, o_ref):
        o_ref[...] = lax.dot_general(a_ref[...], w_ref[...], NT_DIMS, preferred_element_type=F32).astype(o_ref.dtype)

    return pl.pallas_call(
        body, grid=(T // tm, N // tn),
        in_specs=[pl.BlockSpec((tm, K), lambda i, j: (i, 0)), pl.BlockSpec((tn, K), lambda i, j: (j, 0)), ANY],
        out_specs=pl.BlockSpec((tm, tn), lambda i, j: (i, j)),
        out_shape=jax.ShapeDtypeStruct((T, N), BF16),
        compiler_params=_params("parallel", "parallel"), name=name)(a, w, dep)


def ffn_bwd_mid(dzb, wd, h, u, scale, dep, name):
    T, K = dzb.shape
    N = wd.shape[0]
    tm, tn = _tile(T, 1024), _tile(N, 768)

    def body(dz_ref, w_ref, h_ref, u_ref, dep_ref, dh_ref, du_ref):
        dz = dz_ref[...]
        chunks = [slice(c, c + MXU_COLS) for c in range(0, tn, MXU_COLS)]
        das = [lax.dot_general(dz, w_ref[c, :], NT_DIMS, preferred_element_type=F32) for c in chunks]
        for c, da in zip(chunks, das):
            da = scale * da
            hv = h_ref[:, c].astype(F32)
            s = jax.nn.sigmoid(hv)
            silu = hv * s
            dh_ref[:, c] = (da * u_ref[:, c].astype(F32) * (s + silu * (1.0 - s))).astype(BF16)
            du_ref[:, c] = (da * silu).astype(BF16)

    tile = pl.BlockSpec((tm, tn), lambda j, i: (i, j))
    return pl.pallas_call(
        body, grid=(N // tn, T // tm),
        in_specs=[pl.BlockSpec((tm, K), lambda j, i: (i, 0)), pl.BlockSpec((tn, K), lambda j, i: (j, 0)), tile, tile, ANY],
        out_specs=[tile, tile],
        out_shape=[jax.ShapeDtypeStruct((T, N), BF16)] * 2,
        compiler_params=_params("parallel", "parallel"), name=name)(dzb, wd, h, u, dep)


def mm_nt_res(pairs, dz, alpha, ln_before, name):
    T, N = pairs[0][0].shape
    D = pairs[0][1].shape[0]
    n = len(pairs)
    tm = _tile(T, 512 if n == 1 else 256)

    def body(*refs):
        a_refs, w_refs, dz_ref = refs[:n], refs[n:2 * n], refs[2 * n]
        d = lax.dot_general(a_refs[0][...], w_refs[0][...], NT_DIMS, preferred_element_type=F32)
        for p in range(1, n):
            d += lax.dot_general(a_refs[p][...], w_refs[p][...], NT_DIMS, preferred_element_type=F32)
        d = d + alpha * dz_ref[...]
        if ln_before is None:
            refs[2 * n + 1][...] = d
        else:
            z_ref, g_ref = refs[2 * n + 1:2 * n + 3]
            _ln_bwd_rows(z_ref[...], g_ref[...], d, pl.program_id(0) == 0, *refs[2 * n + 3:])

    row = pl.BlockSpec((tm, D), lambda i: (i, 0))
    vec = pl.BlockSpec((1, D), lambda i: (0, 0))
    in_specs = ([pl.BlockSpec((tm, N), lambda i: (i, 0))] * n
                + [pl.BlockSpec((D, N), functools.partial(lambda i, col: (0, col), col=p[2])) for p in pairs] + [row])
    operands = [p[0] for p in pairs] + [p[1] for p in pairs] + [dz]
    if ln_before is None:
        return pl.pallas_call(
            body, grid=(T // tm,), in_specs=in_specs, out_specs=row, out_shape=jax.ShapeDtypeStruct((T, D), F32),
            compiler_params=_params("parallel"), name=name)(*operands)
    return pl.pallas_call(
        body, grid=(T // tm,), in_specs=in_specs + [row, vec], out_specs=[row, row, vec, vec],
        out_shape=_ln_bwd_outs(T, D), compiler_params=_params("arbitrary"), name=name)(*operands, *ln_before)


def mm_tn(a, b, scale, dep, name, into=None, col=0):
    T, M = a.shape
    N = b.shape[1]
    tm, tn, tk = _tile(M, 512), _tile(N, 1024), _tile(T, 4096)
    nk = T // tk
    if into is None:
        into = lax.empty((M, N), BF16)

    def body(a_ref, b_ref, dep_ref, into_ref, o_ref, acc_ref):
        k = pl.program_id(2)

        @pl.when(k == 0)
        def _():
            acc_ref[...] = jnp.zeros_like(acc_ref)

        acc_ref[...] += lax.dot_general(a_ref[...], b_ref[...], TN_DIMS, preferred_element_type=F32)

        @pl.when(k == nk - 1)
        def _():
            o_ref[...] = (scale * acc_ref[...]).astype(BF16)

    return pl.pallas_call(
        body, grid=(M // tm, N // tn, nk),
        in_specs=[pl.BlockSpec((tk, tm), lambda i, j, k: (k, i)), pl.BlockSpec((tk, tn), lambda i, j, k: (k, j)), ANY, ANY],
        out_specs=pl.BlockSpec((tm, tn), lambda i, j, k: (i, col * (N // tn) + j)),
        out_shape=jax.ShapeDtypeStruct(into.shape, BF16), input_output_aliases={3: 0},
        scratch_shapes=[pltpu.VMEM((tm, tn), F32)],
        compiler_params=_params("parallel", "parallel", "arbitrary"), name=name)(a, b, dep, into)


def loss_head(y, target, name):
    T, D = y.shape
    tm = _tile(T, 512)

    def body(y_ref, t_ref, l_ref, dy_ref):
        @pl.when(pl.program_id(0) == 0)
        def _():
            l_ref[...] = jnp.zeros_like(l_ref)

        e = y_ref[...] - t_ref[...]
        dy_ref[...] = e * (1.0 / D)
        rows = jnp.sum(e * e, axis=1, keepdims=True) * (0.5 / D)
        l_ref[...] += jnp.sum(rows, axis=0, keepdims=True)

    row = pl.BlockSpec((tm, D), lambda i: (i, 0))
    return pl.pallas_call(
        body, grid=(T // tm,), in_specs=[row, row],
        out_specs=[pl.BlockSpec((1, 1), lambda i: (0, 0)), row],
        out_shape=[jax.ShapeDtypeStruct((1, 1), F32), jax.ShapeDtypeStruct((T, D), F32)],
        compiler_params=_params("arbitrary"), name=name)(y, target)


def _rel_index(i, j):
    return jnp.clip(i - j + LEFT_CHUNKS * CHUNK, -REL_CLIP, REL_CLIP) + REL_CLIP


def bias_band(table_t, name):
    H = table_t.shape[0]
    rows = SUBLANE

    def body(t_ref, o_ref):
        i0 = pl.program_id(0) * rows
        parts = _split3(t_ref[...])
        r = lax.broadcasted_iota(jnp.int32, (REL_PAD, A_W), 0)
        j = lax.broadcasted_iota(jnp.int32, (REL_PAD, A_W), 1)
        for ii in range(rows):
            onehot = jnp.where(r == _rel_index(i0 + ii, j), 1.0, 0.0).astype(BF16)
            o_ref[ii] = sum(jnp.dot(p, onehot, preferred_element_type=F32) for p in parts)

    return pl.pallas_call(
        body, grid=(A_TQ // rows,),
        in_specs=[pl.BlockSpec((H, REL_PAD), lambda i: (0, 0))],
        out_specs=pl.BlockSpec((rows, H, A_W), lambda i: (i, 0, 0)),
        out_shape=jax.ShapeDtypeStruct((A_TQ, H, A_W), F32),
        compiler_params=_params("parallel"), name=name)(table_t)


def bias_band_bwd(dbands, name):
    H = dbands[0].shape[1]
    rows = SUBLANE
    n = len(dbands)

    def body(*refs):
        g_refs, o_ref = refs[:n], refs[n]

        @pl.when(pl.program_id(0) == 0)
        def _():
            o_ref[...] = jnp.zeros_like(o_ref)

        i0 = pl.program_id(0) * rows
        j = lax.broadcasted_iota(jnp.int32, (A_W, REL_PAD), 0)
        r = lax.broadcasted_iota(jnp.int32, (A_W, REL_PAD), 1)
        acc = jnp.zeros((H, REL_PAD), F32)
        for ii in range(rows):
            onehot = jnp.where(r == _rel_index(i0 + ii, j), 1.0, 0.0).astype(BF16)
            g = g_refs[0][ii]
            for q in range(1, n):
                g = g + g_refs[q][ii]
            acc += _dot3(g, onehot)
        o_ref[...] += acc

    spec = pl.BlockSpec((rows, H, A_W), lambda i: (i, 0, 0))
    return pl.pallas_call(
        body, grid=(A_TQ // rows,), in_specs=[spec] * n,
        out_specs=pl.BlockSpec((H, REL_PAD), lambda i: (0, 0)),
        out_shape=jax.ShapeDtypeStruct((H, REL_PAD), F32),
        compiler_params=_params("arbitrary"), name=name)(*dbands)


def _a_window(ref, qi):
    parts = []
    for d in range(A_NWB):
        kb = jnp.maximum(qi - (A_NWB - 1) + d, 0)
        parts.append(ref[pl.ds(pl.multiple_of(kb * A_TQ, A_TQ), A_TQ), :])
    return jnp.concatenate(parts, axis=0)


def _a_valid(qi):
    i = lax.broadcasted_iota(jnp.int32, (A_TQ, A_W), 0)
    j = lax.broadcasted_iota(jnp.int32, (A_TQ, A_W), 1)
    ic, jc = i // CHUNK, j // CHUNK
    return (jc >= ic) & (jc <= ic + LEFT_CHUNKS) & (j >= LEFT_CHUNKS * CHUNK - qi * A_TQ)


def _head_masks():
    lane = lax.broadcasted_iota(jnp.int32, (1, LANE), 1)
    return [lane < HEAD_DIM, lane >= HEAD_DIM]


def _a_probs(qms, kws, valids, b_ref):
    scores = [lax.dot_general(qm, kws[i // 2], NT_DIMS, preferred_element_type=F32) for i, qm in enumerate(qms)]
    probs = []
    for i, s in enumerate(scores):
        s = jnp.where(valids[i // 2], s * (HEAD_DIM ** -0.5) + b_ref[i % 2], NEG)
        p = jnp.exp(s - jnp.max(s, axis=1, keepdims=True))
        probs.append(p / jnp.sum(p, axis=1, keepdims=True))
    return probs


def _a_subtiles(ref, masks):
    out = []
    for u in range(ref.shape[0] // A_TQ):
        x = ref[u * A_TQ:(u + 1) * A_TQ, :]
        out += [jnp.where(hm, x, jnp.zeros_like(x)) for hm in masks]
    return out


def _attn_specs(S, D, tq):
    hp_n = D // LANE
    nq = S // tq
    q = pl.BlockSpec((tq, LANE), lambda hp, b, qi: (b * nq + qi, hp))
    k = pl.BlockSpec((S, LANE), lambda hp, b, qi: (b, hp_n + hp))
    v = pl.BlockSpec((S, LANE), lambda hp, b, qi: (b, 2 * hp_n + hp))
    tile = pl.BlockSpec((tq, LANE), lambda hp, b, qi: (b * nq + qi, hp))
    seq = pl.BlockSpec((S, LANE), lambda hp, b, qi: (b, hp))
    return q, k, v, tile, seq


def attn_a_fwd(qkv, band, S, name):
    T, D3 = qkv.shape
    D = D3 // 3
    sub = min(A_SUB, S // A_TQ)
    nb, ng = T // S, S // (A_TQ * sub)

    def body(q_ref, k_ref, v_ref, b_ref, o_ref):
        qis = [pl.program_id(2) * sub + u for u in range(sub)]
        masks = _head_masks()
        kws, vws = [_a_window(k_ref, qi) for qi in qis], [_a_window(v_ref, qi) for qi in qis]
        probs = _a_probs(_a_subtiles(q_ref, masks), kws, [_a_valid(qi) for qi in qis], b_ref)
        outs = [jnp.dot(p.astype(BF16), vws[i // 2], preferred_element_type=F32) for i, p in enumerate(probs)]
        for u in range(sub):
            o_ref[u * A_TQ:(u + 1) * A_TQ, :] = jnp.where(masks[0], outs[2 * u], outs[2 * u + 1]).astype(BF16)

    q, k, v, tile, _ = _attn_specs(S, D, A_TQ * sub)
    return pl.pallas_call(
        body, grid=(D // LANE, nb, ng),
        in_specs=[q, k, v, pl.BlockSpec((2, A_TQ, A_W), lambda hp, b, qi: (hp, 0, 0))],
        out_specs=tile, out_shape=jax.ShapeDtypeStruct((T, D), BF16),
        compiler_params=_params("parallel", "parallel", "parallel"), name=name)(qkv, qkv, qkv, band)


def attn_a_bwd(qkv, band, do, S, name):
    T, D3 = qkv.shape
    D = D3 // 3
    sub = min(A_SUB, S // A_TQ)
    nb, ng = T // S, S // (A_TQ * sub)
    scale = HEAD_DIM ** -0.5

    def body(q_ref, k_ref, v_ref, b_ref, do_ref, dq_ref, dk_ref, dv_ref, db_ref, dk_acc, dv_acc):
        b, qg = pl.program_id(1), pl.program_id(2)

        @pl.when((b == 0) & (qg == 0))
        def _():
            db_ref[...] = jnp.zeros_like(db_ref)

        @pl.when(qg == 0)
        def _():
            dk_acc[...] = jnp.zeros_like(dk_acc)
            dv_acc[...] = jnp.zeros_like(dv_acc)

        qis = [qg * sub + u for u in range(sub)]
        masks = _head_masks()
        kws, vws = [_a_window(k_ref, qi) for qi in qis], [_a_window(v_ref, qi) for qi in qis]
        qms, doms = _a_subtiles(q_ref, masks), _a_subtiles(do_ref, masks)
        dps = [lax.dot_general(dom, vws[i // 2], NT_DIMS, preferred_element_type=F32) for i, dom in enumerate(doms)]
        probs = _a_probs(qms, kws, [_a_valid(qi) for qi in qis], b_ref)
        dss = [p * (dp - jnp.sum(p * dp, axis=1, keepdims=True)) for p, dp in zip(probs, dps)]
        for hh in range(2):
            db_ref[hh] += sum(dss[hh::2])
        dsbs = [ds.astype(BF16) for ds in dss]
        dqs = [jnp.dot(dsb, kws[i // 2], preferred_element_type=F32) for i, dsb in enumerate(dsbs)]
        dks = [lax.dot_general(dsb, qm, TN_DIMS, preferred_element_type=F32) for dsb, qm in zip(dsbs, qms)]
        dvs = [lax.dot_general(p.astype(BF16), dom, TN_DIMS, preferred_element_type=F32) for p, dom in zip(probs, doms)]
        for u, qi in enumerate(qis):
            dq_ref[u * A_TQ:(u + 1) * A_TQ, :] = (jnp.where(masks[0], dqs[2 * u], dqs[2 * u + 1]) * scale).astype(BF16)
            dkw = (dks[2 * u] + dks[2 * u + 1]) * scale
            dvw = dvs[2 * u] + dvs[2 * u + 1]
            for d in range(A_NWB):
                kb = jnp.maximum(qi - (A_NWB - 1) + d, 0)
                rows = pl.ds(pl.multiple_of(kb * A_TQ, A_TQ), A_TQ)
                dk_acc[rows, :] += dkw[d * A_TQ:(d + 1) * A_TQ]
                dv_acc[rows, :] += dvw[d * A_TQ:(d + 1) * A_TQ]

        @pl.when(qg == ng - 1)
        def _():
            dk_ref[...] = dk_acc[...].astype(BF16)
            dv_ref[...] = dv_acc[...].astype(BF16)

    q, k, v, tile, seq = _attn_specs(S, D, A_TQ * sub)
    bspec = pl.BlockSpec((2, A_TQ, A_W), lambda hp, b, qi: (hp, 0, 0))
    act = jax.ShapeDtypeStruct((T, D), BF16)
    return pl.pallas_call(
        body, grid=(D // LANE, nb, ng),
        in_specs=[q, k, v, bspec, tile], out_specs=[tile, seq, seq, bspec],
        out_shape=[act, act, act, jax.ShapeDtypeStruct(band.shape, F32)],
        scratch_shapes=[pltpu.VMEM((S, LANE), F32), pltpu.VMEM((S, LANE), F32)],
        compiler_params=_params("arbitrary", "arbitrary", "arbitrary"), name=name)(qkv, qkv, qkv, band, do)


def _dot2(v, w):
    h = v.astype(BF16)
    lo = (v - h.astype(F32)).astype(BF16)
    return jnp.dot(h, w, preferred_element_type=F32) + jnp.dot(lo, w, preferred_element_type=F32)


def _b_weights(qms, kts, rights, after, diagonal):
    inst = [(t, hh) for t in range(len(kts)) for hh in range(2)]
    zs = [lax.dot_general(qms[hh], kts[t], NT_DIMS, preferred_element_type=F32) for t, hh in inst]
    lbs, l1ms, his, los = [], [], [], []
    causal = _strictly_causal() if any(diagonal) else None
    for (t, hh), z in zip(inst, zs):
        lb = jnp.minimum(z, 0.0) - jnp.log(1.0 + jnp.exp(jnp.minimum(z, -z)))
        lbs.append(lb)
        l1m = lb - z
        if diagonal[t]:
            l1m = jnp.where(causal, l1m, 0.0)
        hi = l1m.astype(BF16)
        l1ms.append(l1m)
        his.append(hi)
        los.append((l1m - hi.astype(F32)).astype(BF16))
    sums = [jnp.dot(hi, after, preferred_element_type=F32) + jnp.dot(lo, after, preferred_element_type=F32)
            for hi, lo in zip(his, los)]
    rights = list(rights)
    weights = []
    for (t, hh), lb, l1m, c in zip(inst, lbs, l1ms, sums):
        a = jnp.exp(lb + (rights[hh] + c))
        weights.append(jnp.where(causal, a, 0.0) if diagonal[t] else a)
        rights[hh] = rights[hh] + (c[:, :1] + l1m[:, :1])
    return zs, weights, rights


def _strictly_causal():
    row = lax.broadcasted_iota(jnp.int32, (B_T, B_T), 0)
    col = lax.broadcasted_iota(jnp.int32, (B_T, B_T), 1)
    return col < row


def _tri(strict_lower):
    r = lax.broadcasted_iota(jnp.int32, (B_T, B_T), 0)
    c = lax.broadcasted_iota(jnp.int32, (B_T, B_T), 1)
    return jnp.where((r > c) if strict_lower else (r < c), 1.0, 0.0).astype(BF16)


def _scaled_heads(q2):
    qs = q2 * (HEAD_DIM ** -0.5)
    return [jnp.where(hm, qs, jnp.zeros_like(qs)) for hm in _head_masks()]


def attn_b_fwd(qkv, S, name):
    T, D3 = qkv.shape
    D = D3 // 3
    nb, nq = T // S, S // B_T

    def body(q_ref, k_ref, v_ref, o_ref):
        qi = pl.program_id(2)
        after = _tri(True)
        qms = _scaled_heads(q_ref[...])

        def tiles(kbs, carry, diagonal):
            rows = [pl.ds(pl.multiple_of(kb * B_T, B_T), B_T) for kb in kbs]
            kts, vts = [k_ref[r, :] for r in rows], [v_ref[r, :] for r in rows]
            _, weights, rights = _b_weights(qms, kts, (carry[0], carry[2]), after, diagonal)
            accs = [carry[1], carry[3]]
            for i, a in enumerate(weights):
                accs[i % 2] = accs[i % 2] + jnp.dot(a.astype(BF16), vts[i // 2], preferred_element_type=F32)
            return rights[0], accs[0], rights[1], accs[1]

        init = (jnp.zeros((B_T, 1), F32), jnp.zeros((B_T, LANE), F32)) * 2
        res = lax.cond(qi % 2 == 1, lambda c: tiles([qi, qi - 1], c, [True, False]), lambda c: tiles([qi], c, [True]), init)
        top = qi - 1 - qi % 2
        res = lax.fori_loop(0, qi // 2, lambda p, c: tiles([top - 2 * p, top - 1 - 2 * p], c, [False, False]), res)
        o_ref[...] = jnp.where(_head_masks()[0], res[1], res[3]).astype(BF16)

    q, k, v, tile, _ = _attn_specs(S, D, B_T)
    return pl.pallas_call(
        body, grid=(D // LANE, nb, nq), in_specs=[q, k, v], out_specs=tile,
        out_shape=jax.ShapeDtypeStruct((T, D), BF16),
        compiler_params=_params("parallel", "parallel", "parallel"), name=name)(qkv, qkv, qkv)


def attn_b_bwd(qkv, do, S, name):
    T, D3 = qkv.shape
    D = D3 // 3
    nb, nq = T // S, S // B_T
    scale = HEAD_DIM ** -0.5

    def body(q_ref, k_ref, v_ref, do_ref, dq_ref, dk_ref, dv_ref, dk_acc, dv_acc, z_s, g_s):
        qi = pl.program_id(2)

        @pl.when(qi == 0)
        def _():
            dk_acc[...] = jnp.zeros_like(dk_acc)
            dv_acc[...] = jnp.zeros_like(dv_acc)

        do2 = do_ref[...]
        after, before = _tri(True), _tri(False)
        masks = _head_masks()
        qms = _scaled_heads(q_ref[...])
        doms = [jnp.where(hm, do2, jnp.zeros_like(do2)) for hm in masks]

        def sweep_left(kbs, rights, diagonal):
            rows = [pl.ds(pl.multiple_of(kb * B_T, B_T), B_T) for kb in kbs]
            kts, vts = [k_ref[r, :] for r in rows], [v_ref[r, :] for r in rows]
            das = [lax.dot_general(doms[hh], vt, NT_DIMS, preferred_element_type=F32) for vt in vts for hh in range(2)]
            zs, weights, rights = _b_weights(qms, kts, rights, after, diagonal)
            for i, (z, a, da) in enumerate(zip(zs, weights, das)):
                z_s[i % 2, kbs[i // 2]] = z
                g_s[i % 2, kbs[i // 2]] = da * a
            dvs = [lax.dot_general(a.astype(BF16), doms[i % 2], TN_DIMS, preferred_element_type=F32)
                   for i, a in enumerate(weights)]
            for t, r in enumerate(rows):
                dv_acc[r, :] += dvs[2 * t] + dvs[2 * t + 1]
            return tuple(rights)

        zero_col = jnp.zeros((B_T, 1), F32)
        odd = qi % 2 == 1
        top = qi - 1 - qi % 2
        rights = lax.cond(odd, lambda c: sweep_left([qi, qi - 1], c, [True, False]), lambda c: sweep_left([qi], c, [True]),
                          (zero_col, zero_col))
        lax.fori_loop(0, qi // 2, lambda p, c: sweep_left([top - 2 * p, top - 1 - 2 * p], c, [False, False]), rights)

        def sweep_right(kbs, carry, diagonal):
            rows = [pl.ds(pl.multiple_of(kb * B_T, B_T), B_T) for kb in kbs]
            inst = [(t, hh) for t in range(len(kbs)) for hh in range(2)]
            gs = [g_s[hh, kbs[t]] for t, hh in inst]
            sums = [_dot2(g, before) for g in gs]
            lefts, dqs = [carry[0], carry[2]], [carry[1], carry[3]]
            dzbs = []
            for (t, hh), g, c in zip(inst, gs, sums):
                beta = jax.nn.sigmoid(z_s[hh, kbs[t]])
                dz = g * (1.0 - beta) - beta * (lefts[hh] + c)
                if diagonal[t]:
                    dz = jnp.where(_strictly_causal(), dz, 0.0)
                dzbs.append(dz.astype(BF16))
                lefts[hh] = lefts[hh] + jnp.sum(g, axis=1, keepdims=True)
            dks = [lax.dot_general(dzb, qms[hh], TN_DIMS, preferred_element_type=F32) for (t, hh), dzb in zip(inst, dzbs)]
            for (t, hh), dzb in zip(inst, dzbs):
                dqs[hh] = dqs[hh] + jnp.dot(dzb, k_ref[rows[t], :], preferred_element_type=F32)
            for t, r in enumerate(rows):
                dk_acc[r, :] += dks[2 * t] + dks[2 * t + 1]
            return lefts[0], dqs[0], lefts[1], dqs[1]

        init = (zero_col, jnp.zeros((B_T, LANE), F32)) * 2
        res = lax.fori_loop(0, qi // 2, lambda p, c: sweep_right([2 * p, 2 * p + 1], c, [False, False]), init)
        res = lax.cond(odd, lambda c: sweep_right([qi - 1, qi], c, [False, True]), lambda c: sweep_right([qi], c, [True]), res)
        dq_ref[...] = (jnp.where(masks[0], res[1], res[3]) * scale).astype(BF16)

        @pl.when(qi == nq - 1)
        def _():
            dk_ref[...] = dk_acc[...].astype(BF16)
            dv_ref[...] = dv_acc[...].astype(BF16)

    q, k, v, tile, seq = _attn_specs(S, D, B_T)
    act = jax.ShapeDtypeStruct((T, D), BF16)
    return pl.pallas_call(
        body, grid=(D // LANE, nb, nq),
        in_specs=[q, k, v, tile], out_specs=[tile, seq, seq], out_shape=[act, act, act],
        scratch_shapes=[pltpu.VMEM((S, LANE), F32), pltpu.VMEM((S, LANE), F32),
                        pltpu.VMEM((2, nq, B_T, B_T), F32), pltpu.VMEM((2, nq, B_T, B_T), F32)],
        compiler_params=_params("arbitrary", "arbitrary", "arbitrary"), name=name)(qkv, qkv, qkv, do)


HBM = pl.BlockSpec(memory_space=pltpu.HBM)
SEM = pl.BlockSpec(memory_space=pltpu.SEMAPHORE)
N_PEERS = N_DEV - 1
GATHERS_AHEAD = 3


def _place():
    return lax.axis_index("x"), lax.axis_index("y"), lax.axis_index("c")


def _peers(x, y, c):
    return [(1 - x if r & 4 else x, 1 - y if r & 2 else y, 1 - c if r & 1 else c) for r in range(1, N_DEV)]


def _block(ref, shape, by_cols, j):
    r, w = shape
    if by_cols:
        return ref.at[:, pl.ds(pl.multiple_of(j * w, LANE), w)]
    return ref.at[pl.ds(pl.multiple_of(j * r, SUBLANE), r), :]


def _exchange_copies(gather, src_refs, land_refs, send_sems, recv_sems, by_cols):
    x, y, c = _place()
    me = 4 * x + 2 * y + c
    out = []
    for t, (src, land) in enumerate(zip(src_refs, land_refs)):
        for r, peer in enumerate(_peers(x, y, c)):
            pj = 4 * peer[0] + 2 * peer[1] + peer[2]
            if gather:
                mine, to_me, theirs = src, _block(land, src.shape, by_cols[t], me), _block(land, src.shape, by_cols[t], pj)
            else:
                mine, to_me, theirs = _block(src, land.shape[1:], by_cols[t], pj), land.at[me], land.at[pj]
            sems = dict(send_sem=send_sems.at[N_PEERS * t + r], recv_sem=recv_sems.at[N_PEERS * t + r],
                        device_id=peer, device_id_type=MESH_ID)
            out.append((pltpu.make_async_remote_copy(src_ref=mine, dst_ref=to_me, **sems),
                        pltpu.make_async_remote_copy(src_ref=mine, dst_ref=theirs, **sems)))
    return out


def _own_copies(gather, src_refs, land_refs, own_sems, by_cols):
    x, y, c = _place()
    me = 4 * x + 2 * y + c
    out = []
    for t, (src, land) in enumerate(zip(src_refs, land_refs)):
        if gather:
            out.append(pltpu.make_async_copy(src, _block(land, src.shape, by_cols[t], me), own_sems.at[t]))
        else:
            out.append(pltpu.make_async_copy(_block(src, land.shape[1:], by_cols[t], me), land.at[me], own_sems.at[t]))
    return out


def exchange_start(gather, srcs, land_shapes, by_cols, after, name):
    n = len(srcs)

    def body(*refs):
        src_refs, land_refs = refs[:n], refs[n:2 * n]
        send_sems, recv_sems, own_sems = refs[2 * n + 1:2 * n + 4]
        token = refs[-1]
        for cp in _own_copies(gather, src_refs, land_refs, own_sems, by_cols):
            cp.start()
        for mine, _ in _exchange_copies(gather, src_refs, land_refs, send_sems, recv_sems, by_cols):
            mine.start()
        token[...] = jnp.zeros_like(token)

    lands = [pltpu.with_memory_space_constraint(lax.empty(s.shape, s.dtype), pltpu.HBM) for s in land_shapes]
    srcs = [pltpu.with_memory_space_constraint(s, pltpu.HBM) for s in srcs]
    res = pl.pallas_call(
        body, name=name,
        out_shape=(pltpu.SemaphoreType.DMA((N_PEERS * n,)), pltpu.SemaphoreType.DMA((N_PEERS * n,)),
                   pltpu.SemaphoreType.DMA((n,)),
                   *[pltpu.HBM(s.shape, s.dtype) for s in srcs], *[pltpu.HBM(s.shape, s.dtype) for s in land_shapes],
                   jax.ShapeDtypeStruct((SUBLANE, LANE), F32)),
        in_specs=[HBM] * (2 * n) + [ANY],
        out_specs=(SEM, SEM, SEM, *[HBM] * (2 * n), pl.BlockSpec(memory_space=pltpu.VMEM)),
        input_output_aliases={i: 3 + i for i in range(2 * n)},
        compiler_params=pltpu.CompilerParams(has_side_effects=pltpu.SideEffectType.DATAFLOW_SIDE_EFFECTING),
    )(*srcs, *lands, after)
    return dict(gather=gather, n=n, by_cols=by_cols, sems=res[:3], srcs=res[3:3 + n],
                lands=res[3 + n:3 + 2 * n], token=res[-1])


def exchange_wait(started, after, name):
    n, gather, by_cols = started["n"], started["gather"], started["by_cols"]

    def body(*refs):
        src_refs, land_refs = refs[:n], refs[n:2 * n]
        send_sems, recv_sems, own_sems = refs[2 * n:2 * n + 3]
        for mine, theirs in _exchange_copies(gather, src_refs, land_refs, send_sems, recv_sems, by_cols):
            mine.wait_send()
            theirs.wait_recv()
        for cp in _own_copies(gather, src_refs, land_refs, own_sems, by_cols):
            cp.wait()

    res = pl.pallas_call(
        body, name=name,
        out_shape=tuple(pltpu.HBM(s.shape, s.dtype) for s in (*started["srcs"], *started["lands"])),
        in_specs=[HBM] * (2 * n) + [SEM, SEM, SEM, ANY], out_specs=tuple([HBM] * (2 * n)),
        input_output_aliases={i: i for i in range(2 * n)},
        compiler_params=pltpu.CompilerParams(has_side_effects=pltpu.SideEffectType.DATAFLOW_SIDE_EFFECTING),
    )(*started["srcs"], *started["lands"], *started["sems"], after)
    return res[n:]


def gather_small(v, dep, name):
    R, C = v.shape

    def body(v_ref, dep_ref, o_ref, send_sems, recv_sems):
        x, y, c = _place()
        o_ref[4 * x + 2 * y + c] = v_ref[...]
        peers = _peers(x, y, c)

        def copy(r, owner, to):
            slot = o_ref.at[4 * owner[0] + 2 * owner[1] + owner[2]]
            return pltpu.make_async_remote_copy(
                src_ref=slot, dst_ref=slot, send_sem=send_sems.at[r], recv_sem=recv_sems.at[r],
                device_id=to, device_id_type=MESH_ID)

        sends = [copy(r, (x, y, c), peer) for r, peer in enumerate(peers)]
        for cp in sends:
            cp.start()
        for r, peer in enumerate(peers):
            copy(r, peer, (x, y, c)).wait_recv()
        for cp in sends:
            cp.wait_send()

    vm = pl.BlockSpec(memory_space=pltpu.VMEM)
    return pl.pallas_call(
        body, in_specs=[vm, ANY], out_specs=vm, out_shape=jax.ShapeDtypeStruct((N_DEV, R, C), F32),
        scratch_shapes=[pltpu.SemaphoreType.DMA((N_PEERS,)), pltpu.SemaphoreType.DMA((N_PEERS,))],
        name=name)(v, dep)


def sum_devices(g, name):
    _, R, C = g.shape

    def body(g_ref, o_ref):
        acc = g_ref[0]
        for j in range(1, N_DEV):
            acc = acc + g_ref[j]
        o_ref[...] = acc

    vm = pl.BlockSpec(memory_space=pltpu.VMEM)
    return pl.pallas_call(body, in_specs=[vm], out_specs=vm, out_shape=jax.ShapeDtypeStruct((R, C), F32), name=name)(g)


def adamw(w, m, v, index, parts, outs, name):
    L, R, C = w.shape
    if outs is None:
        outs = [lax.empty((L, R, C), F32) for _ in range(4)]
    P, Rp, Cp = parts.shape
    tr = _tile(R, 512, SUBLANE) if Rp == R else R

    def body(w_ref, m_ref, v_ref, p_ref, g_in, d_in, m_in, v_in, g_out, d_out, m_out, v_out):
        g = p_ref[0, :tr, :C].astype(F32)
        for q in range(1, P):
            g = g + p_ref[q, :tr, :C].astype(F32)
        mn = ADAM_B1 * m_ref[...] + (1.0 - ADAM_B1) * g
        vn = ADAM_B2 * v_ref[...] + (1.0 - ADAM_B2) * (g * g)
        m_hat = mn / (1.0 - ADAM_B1 ** ADAM_STEP)
        v_hat = vn / (1.0 - ADAM_B2 ** ADAM_STEP)
        g_out[...] = g
        d_out[...] = -ADAM_LR * (m_hat / (jnp.sqrt(v_hat) + ADAM_EPS) + ADAM_WD * w_ref[...])
        m_out[...] = mn
        v_out[...] = vn

    slab = pl.BlockSpec((None, tr, C), lambda r: (index, r, 0))
    pspec = pl.BlockSpec((P, tr if Rp == R else Rp, Cp), lambda r: (0, r, 0))
    return pl.pallas_call(
        body, grid=(R // tr,), in_specs=[slab] * 3 + [pspec] + [ANY] * 4, out_specs=[slab] * 4,
        out_shape=[jax.ShapeDtypeStruct((L, R, C), F32)] * 4,
        input_output_aliases={4 + q: q for q in range(4)},
        compiler_params=_params("parallel"), name=name)(w, m, v, parts, *outs)


def _pad_to(a, axis, size):
    pad = [(0, 0)] * a.ndim
    pad[axis] = (0, size - a.shape[axis])
    return jnp.pad(a, pad)


def kernel(x, w_qkv_a, w_o_a, rel_bias, w_qkv_b, w_o_b, ffn_w_gate, ffn_w_up, ffn_w_down, ln_g, ln_b, loss_target, m_w_qkv_a, m_w_o_a, m_rel_bias, m_w_qkv_b, m_w_o_b, m_ffn_w_gate, m_ffn_w_up, m_ffn_w_down, m_ln_g, m_ln_b, v_w_qkv_a, v_w_o_a, v_rel_bias, v_w_qkv_b, v_w_o_b, v_ffn_w_gate, v_ffn_w_up, v_ffn_w_down, v_ln_g, v_ln_b):
    nb, S, D = x.shape
    T = nb * S
    depth = ffn_w_gate.shape[0]
    H = D // HEAD_DIM
    fs = ffn_w_gate.shape[-1]
    fp = -(-fs // LANE) * LANE
    alpha = (2.0 * depth) ** 0.25
    cx, cy, cc = _place()
    me = 4 * cx + 2 * cy + cc

    ln_local = jnp.concatenate([ln_g.reshape(depth * 3, -1), ln_b.reshape(depth * 3, -1)], axis=0)
    ln_all = gather_small(ln_local, ln_local, "gather_ln")
    ln_full = jnp.transpose(ln_all, (1, 0, 2)).reshape(2 * depth * 3, D)
    ln_gain = lambda i, s: ln_full[3 * i + s][None, :]
    ln_bias = lambda i, s: ln_full[3 * depth + 3 * i + s][None, :]

    table_t = _pad_to(rel_bias.T, 1, REL_PAD)
    band = jnp.transpose(bias_band(table_t, "bias_band"), (1, 0, 2))

    subs = []
    for i in range(depth):
        for s in (0, 1, 2):
            if s == 1:
                wq, wo = (w_qkv_a, w_o_a) if i % 2 == 0 else (w_qkv_b, w_o_b)
                subs.append(([wq[i // 2].astype(BF16), wo[i // 2].astype(BF16)], [True, False]))
            else:
                f = 0 if s == 0 else 1
                subs.append(([_pad_to(ffn_w_gate[i, f].astype(BF16), 1, fp), _pad_to(ffn_w_up[i, f].astype(BF16), 1, fp),
                              _pad_to(ffn_w_down[i, f].astype(BF16), 0, fp)], [True, True, False]))

    def start_gather(k, after):
        shards, by_cols = subs[k]
        groups = [(0, 2), (2, 3)] if k == 0 else [(0, len(shards))]
        out = []
        for part, (lo, hi) in enumerate(groups):
            shapes = [jax.ShapeDtypeStruct((s.shape[0], N_DEV * s.shape[1]) if col else (N_DEV * s.shape[0], s.shape[1]), BF16)
                      for s, col in zip(shards[lo:hi], by_cols[lo:hi])]
            out.append(exchange_start(True, shards[lo:hi], shapes, by_cols[lo:hi], after, f"gather_start_{k}_{part}"))
            after = out[-1]["token"]
        return out

    xf = x.reshape(T, D)
    act, act_b = xf, xf.astype(BF16)
    gathers = {}
    for k in range(min(GATHERS_AHEAD, len(subs))):
        gathers[k] = start_gather(k, xf if k == 0 else gathers[k - 1][-1]["token"])
    newest = gathers[k][-1]["token"]
    saved = []
    for i in range(depth):
        layer = {}
        for s in (0, 1, 2):
            k = 3 * i + s
            tag = f"L{i}S{s}"
            if 0 < k and k + GATHERS_AHEAD - 1 < len(subs):
                gathers[k + GATHERS_AHEAD - 1] = start_gather(k + GATHERS_AHEAD - 1, act_b)
                newest = gathers[k + GATHERS_AHEAD - 1][-1]["token"]
            parts = gathers.pop(k)
            full = list(exchange_wait(parts[0], newest, f"gather_wait_{k}_0"))
            if s == 1:
                wqkv_f, wo_f = full
                qkv = mm_nn(act_b, wqkv_f, "qkv_" + tag)
                if i % 2 == 0:
                    att = attn_a_fwd(qkv, band, S, "attn_a_fwd_" + tag)
                else:
                    att = attn_b_fwd(qkv, S, "attn_b_fwd_" + tag)
                z, o, ob = mm_res_ln(att, wo_f, act, ln_gain(i, s), ln_bias(i, s), alpha, 1.0, "out_ln_" + tag)
                layer[s] = dict(x_b=act_b, qkv=qkv, att=att, z=z, wqkv=wqkv_f, wo=wo_f)
            else:
                wg_f, wu_f = full[:2]
                h, u, a = ffn_up(act_b, wg_f, wu_f, "ffn_up_" + tag)
                wd_f = full[2] if len(parts) == 1 else exchange_wait(parts[1], a, f"gather_wait_{k}_1")[0]
                z, o, ob = mm_res_ln(a, wd_f, act, ln_gain(i, s), ln_bias(i, s), alpha, 0.5, "down_ln_" + tag)
                layer[s] = dict(x_b=act_b, h=h, u=u, a=a, z=z, wg=wg_f, wu=wu_f, wd=wd_f)
            act, act_b = o, ob
        saved.append(layer)

    loss_local, d_act = loss_head(act, loss_target.reshape(T, D), "loss_head")
    loss = lax.psum(loss_local[0, 0], ("x", "y", "c"))

    results = {}

    def update(name, w, m, v, index, parts):
        L = w.shape[0] if w.ndim == 3 else w.shape[0] * w.shape[1]
        flat = lambda t: t.reshape((L,) + t.shape[-2:])
        results[name] = adamw(flat(w), flat(m), flat(v), index, parts, results.get(name), f"adamw_{name}_{index}")

    def finish(entry, after):
        exchange, targets, tag = entry
        lands = exchange_wait(exchange, after, "scatter_wait_" + tag)
        for (name, w, m, v, index), parts in zip(targets, lands):
            update(name, w, m, v, index, parts)
        return results[targets[-1][0]][0]

    pending = []
    started = d_act
    dbands = []
    dln_g = [None] * (3 * depth)
    dln_b = [None] * (3 * depth)
    ln_grads = ln_bwd(saved[depth - 1][2]["z"], ln_gain(depth - 1, 2), d_act, "ln_bwd_last")
    for i in reversed(range(depth)):
        for s in (2, 1, 0):
            tag = f"L{i}S{s}"
            sv = saved[i][s]
            dz, dzb, dln_g[3 * i + s], dln_b[3 * i + s] = ln_grads
            before = (i, s - 1) if s > 0 else (i - 1, 2) if i > 0 else None
            ln_before = None if before is None else (saved[before[0]][before[1]]["z"], ln_gain(*before))
            if s == 1:
                j = i // 2
                d_att = mm_nt(dzb, sv["wo"], started, "att_bwd_" + tag)
                g_wo = mm_tn(sv["att"], dzb, 1.0, started, "dwo_" + tag)
                if i % 2 == 0:
                    dq, dk, dv, dband = attn_a_bwd(sv["qkv"], band, d_att, S, "attn_a_bwd_" + tag)
                    dbands.append(jnp.transpose(dband, (1, 0, 2)))
                else:
                    dq, dk, dv = attn_b_bwd(sv["qkv"], d_att, S, "attn_b_bwd_" + tag)
                g_wqkv = lax.empty((D, 3 * D), BF16)
                for col, (piece, d_piece) in enumerate(zip("qkv", (dq, dk, dv))):
                    g_wqkv = mm_tn(sv["x_b"], d_piece, 1.0, started, f"dw{piece}_" + tag, g_wqkv, col)
                ln_grads = mm_nt_res([(dq, sv["wqkv"], 0), (dk, sv["wqkv"], 1), (dv, sv["wqkv"], 2)], dz, alpha, ln_before,
                                     "dx_mix_" + tag)
                grads, by_cols = [g_wqkv, g_wo], [True, False]
                slabs = [(D, 3 * D // N_DEV), (D // N_DEV, D)]
                if i % 2 == 0:
                    targets = [("w_qkv_a", w_qkv_a, m_w_qkv_a, v_w_qkv_a, j), ("w_o_a", w_o_a, m_w_o_a, v_w_o_a, j)]
                else:
                    targets = [("w_qkv_b", w_qkv_b, m_w_qkv_b, v_w_qkv_b, j), ("w_o_b", w_o_b, m_w_o_b, v_w_o_b, j)]
            else:
                f = 0 if s == 0 else 1
                dh, du = ffn_bwd_mid(dzb, sv["wd"], sv["h"], sv["u"], 0.5, started, "ffn_mid_" + tag)
                g_wd = mm_tn(sv["a"], dzb, 0.5, started, "dwd_" + tag)
                g_wg = mm_tn(sv["x_b"], dh, 1.0, started, "dwg_" + tag)
                g_wu = mm_tn(sv["x_b"], du, 1.0, started, "dwu_" + tag)
                ln_grads = mm_nt_res([(dh, sv["wg"], 0), (du, sv["wu"], 0)], dz, alpha, ln_before, "dx_ffn_" + tag)
                grads, by_cols = [g_wg, g_wu, g_wd], [True, True, False]
                slabs = [(D, fp), (D, fp), (fp, D)]
                idx = 2 * i + f
                targets = [("ffn_w_gate", ffn_w_gate, m_ffn_w_gate, v_ffn_w_gate, idx),
                           ("ffn_w_up", ffn_w_up, m_ffn_w_up, v_ffn_w_up, idx),
                           ("ffn_w_down", ffn_w_down, m_ffn_w_down, v_ffn_w_down, idx)]
            after = ln_grads[0] if before is not None else ln_grads
            if before is None:
                grad_x = ln_grads.reshape(nb, S, D)
                dtable_t = bias_band_bwd(dbands, "bias_band_bwd")
                small = jnp.concatenate(dln_g + dln_b + [_pad_to(dtable_t, 1, D)], axis=0)
                small = _pad_to(small, 0, -(-small.shape[0] // SUBLANE) * SUBLANE)
                total = after = sum_devices(gather_small(small, grads[0], "gather_small_grads"), "sum_small_grads")
            shapes = [jax.ShapeDtypeStruct((N_DEV,) + slab, BF16) for slab in slabs]
            pending.append((exchange_start(False, grads, shapes, by_cols, after, "scatter_start_" + tag), targets, tag))
            started = pending[-1][0]["token"]

    last = pending.pop()
    done = started
    for entry in pending:
        done = finish(entry, done)
    finish(last, done)
    n_ln = 3 * depth
    g_ln_g = lax.dynamic_slice_in_dim(total[:n_ln], me * (D // N_DEV), D // N_DEV, axis=1)
    g_ln_b = lax.dynamic_slice_in_dim(total[n_ln:2 * n_ln], me * (D // N_DEV), D // N_DEV, axis=1)
    g_rel = total[2 * n_ln:2 * n_ln + H, :N_REL].T
    as3 = lambda t: t.reshape((1, -1, t.shape[-1]))
    results["ln_g"] = adamw(as3(ln_g), as3(m_ln_g), as3(v_ln_g), 0, g_ln_g[None], None, "adamw_ln_g")
    results["ln_b"] = adamw(as3(ln_b), as3(m_ln_b), as3(v_ln_b), 0, g_ln_b[None], None, "adamw_ln_b")
    results["rel_bias"] = adamw(as3(rel_bias), as3(m_rel_bias), as3(v_rel_bias), 0, g_rel[None], None, "adamw_rel_bias")

    order = [("w_qkv_a", w_qkv_a), ("w_o_a", w_o_a), ("rel_bias", rel_bias), ("w_qkv_b", w_qkv_b), ("w_o_b", w_o_b),
             ("ffn_w_gate", ffn_w_gate), ("ffn_w_up", ffn_w_up), ("ffn_w_down", ffn_w_down), ("ln_g", ln_g), ("ln_b", ln_b)]
    outs = [loss, grad_x]
    for q in range(4):
        for name, like in order:
            outs.append(results[name][q].reshape(like.shape))
    return tuple(outs)
```

```python
import functools

import jax
import jax.numpy as jnp
from jax import lax
from jax.experimental import pallas as pl
from jax.experimental.pallas import tpu as pltpu

BF16 = jnp.bfloat16
F32 = jnp.float32
MESH_ID = pl.DeviceIdType.MESH
ANY = pl.BlockSpec(memory_space=pl.ANY)

N_DEV = 8
LANE = 128
MXU_COLS = 256
SUBLANE = 8
VMEM_LIMIT = 56 * 1024 * 1024

HEAD_DIM = 64
CHUNK = 64
LEFT_CHUNKS = 8
REL_CLIP = 128
N_REL = 2 * REL_CLIP + 1
REL_PAD = 384
LN_EPS = 1e-5
ADAM_LR, ADAM_B1, ADAM_B2, ADAM_EPS, ADAM_WD, ADAM_STEP = 0.001, 0.9, 0.999, 1e-08, 0.01, 10
NEG = -1e30

A_TQ = 128
A_NWB = LEFT_CHUNKS * CHUNK // A_TQ + 1
A_W = A_NWB * A_TQ
A_SUB = 8
B_T = 256

NT_DIMS = (((1,), (1,)), ((), ()))
TN_DIMS = (((0,), (0,)), ((), ()))


def _tile(n, pref, unit=LANE):
    if n <= pref:
        return n
    t = pref - pref % unit
    while t > unit and n % t:
        t -= unit
    assert n % t == 0, (n, pref)
    return t


def _params(*sem):
    return pltpu.CompilerParams(dimension_semantics=sem, vmem_limit_bytes=VMEM_LIMIT)


def _split3(v):
    h = v.astype(BF16)
    r = v - h.astype(F32)
    m = r.astype(BF16)
    lo = (r - m.astype(F32)).astype(BF16)
    return h, m, lo


def _dot3(v, w):
    h, m, lo = _split3(v)
    return (jnp.dot(h, w, preferred_element_type=F32) + jnp.dot(m, w, preferred_element_type=F32)
            + jnp.dot(lo, w, preferred_element_type=F32))


def mm_nn(a, w, name):
    T, K = a.shape
    N = w.shape[1]
    tm, tn = _tile(T, 1024), _tile(N, 768)

    def body(a_ref, w_ref, o_ref):
        o_ref[...] = jnp.dot(a_ref[...], w_ref[...], preferred_element_type=F32).astype(o_ref.dtype)

    return pl.pallas_call(
        body, grid=(T // tm, N // tn),
        in_specs=[pl.BlockSpec((tm, K), lambda i, j: (i, 0)), pl.BlockSpec((K, tn), lambda i, j: (0, j))],
        out_specs=pl.BlockSpec((tm, tn), lambda i, j: (i, j)),
        out_shape=jax.ShapeDtypeStruct((T, N), BF16),
        compiler_params=_params("parallel", "parallel"), name=name)(a, w)


def ffn_up(xb, wg, wu, name):
    T, K = xb.shape
    N = wg.shape[1]
    tm, tn = _tile(T, 1024), _tile(N, 768)

    def body(x_ref, wg_ref, wu_ref, h_ref, u_ref, a_ref):
        x = x_ref[...]
        chunks = [slice(c, c + MXU_COLS) for c in range(0, tn, MXU_COLS)]
        hs = [jnp.dot(x, wg_ref[:, c], preferred_element_type=F32) for c in chunks]
        us = [jnp.dot(x, wu_ref[:, c], preferred_element_type=F32) for c in chunks]
        for c, h, u in zip(chunks, hs, us):
            h_ref[:, c] = h.astype(BF16)
            u_ref[:, c] = u.astype(BF16)
            a_ref[:, c] = (h * jax.nn.sigmoid(h) * u).astype(BF16)

    wspec = pl.BlockSpec((K, tn), lambda j, i: (0, j))
    ospec = pl.BlockSpec((tm, tn), lambda j, i: (i, j))
    return pl.pallas_call(
        body, grid=(N // tn, T // tm),
        in_specs=[pl.BlockSpec((tm, K), lambda j, i: (i, 0)), wspec, wspec],
        out_specs=[ospec, ospec, ospec],
        out_shape=[jax.ShapeDtypeStruct((T, N), BF16)] * 3,
        compiler_params=_params("parallel", "parallel"), name=name)(xb, wg, wu)


def mm_res_ln(a, w, x, g, b, alpha, scale, name):
    T, K = a.shape
    D = w.shape[1]
    tm = _tile(T, 512)
    parts = [slice(r, r + MXU_COLS) for r in range(0, tm, MXU_COLS)] if tm % MXU_COLS == 0 else [slice(0, tm)]

    def body(a_ref, w_ref, x_ref, g_ref, b_ref, z_ref, o_ref, ob_ref):
        ys = [jnp.dot(a_ref[r, :], w_ref[...], preferred_element_type=F32) for r in parts]
        for r, y in zip(parts, ys):
            z = alpha * x_ref[r, :] + scale * y
            mu = jnp.mean(z, axis=1, keepdims=True)
            zc = z - mu
            var = jnp.mean(zc * zc, axis=1, keepdims=True)
            o = zc * lax.rsqrt(var + LN_EPS) * g_ref[...] + b_ref[...]
            z_ref[r, :] = z
            o_ref[r, :] = o
            ob_ref[r, :] = o.astype(BF16)

    row = pl.BlockSpec((tm, D), lambda i: (i, 0))
    vec = pl.BlockSpec((1, D), lambda i: (0, 0))
    return pl.pallas_call(
        body, grid=(T // tm,),
        in_specs=[pl.BlockSpec((tm, K), lambda i: (i, 0)), pl.BlockSpec((K, D), lambda i: (0, 0)), row, vec, vec],
        out_specs=[row, row, row],
        out_shape=[jax.ShapeDtypeStruct((T, D), F32), jax.ShapeDtypeStruct((T, D), F32),
                   jax.ShapeDtypeStruct((T, D), BF16)],
        compiler_params=_params("parallel"), name=name)(a, w, x, g, b)


def _ln_bwd_rows(zv, gain, dov, first, dz_ref, dzb_ref, dg_ref, db_ref):
    @pl.when(first)
    def _():
        dg_ref[...] = jnp.zeros_like(dg_ref)
        db_ref[...] = jnp.zeros_like(db_ref)

    mu = jnp.mean(zv, axis=1, keepdims=True)
    zc = zv - mu
    var = jnp.mean(zc * zc, axis=1, keepdims=True)
    rstd = lax.rsqrt(var + LN_EPS)
    xhat = zc * rstd
    dxhat = dov * gain
    m1 = jnp.mean(dxhat, axis=1, keepdims=True)
    m2 = jnp.mean(dxhat * xhat, axis=1, keepdims=True)
    dz = rstd * (dxhat - m1 - xhat * m2)
    dz_ref[...] = dz
    dzb_ref[...] = dz.astype(BF16)
    dg_ref[...] += jnp.sum(dov * xhat, axis=0, keepdims=True)
    db_ref[...] += jnp.sum(dov, axis=0, keepdims=True)


def _ln_bwd_outs(T, D):
    return [jax.ShapeDtypeStruct((T, D), F32), jax.ShapeDtypeStruct((T, D), BF16),
            jax.ShapeDtypeStruct((1, D), F32), jax.ShapeDtypeStruct((1, D), F32)]


def ln_bwd(z, g, do, name):
    T, D = z.shape
    tm = _tile(T, 512)

    def body(z_ref, g_ref, do_ref, dz_ref, dzb_ref, dg_ref, db_ref):
        _ln_bwd_rows(z_ref[...], g_ref[...], do_ref[...], pl.program_id(0) == 0, dz_ref, dzb_ref, dg_ref, db_ref)

    row = pl.BlockSpec((tm, D), lambda i: (i, 0))
    vec = pl.BlockSpec((1, D), lambda i: (0, 0))
    return pl.pallas_call(
        body, grid=(T // tm,), in_specs=[row, vec, row], out_specs=[row, row, vec, vec],
        out_shape=_ln_bwd_outs(T, D), compiler_params=_params("arbitrary"), name=name)(z, g, do)


def mm_nt(a, w, dep, name):
    T, K = a.shape
    N = w.shape[0]
    tm, tn = _tile(T, 1024), _tile(N, 512)

    def body(a_ref, w_ref, dep_ref, o_ref):
        o_ref[...] = lax.dot_general(a_ref[...], w_ref[...], NT_DIMS, preferred_element_type=F32).astype(o_ref.dtype)

    return pl.pallas_call(
        body, grid=(T // tm, N // tn),
        in_specs=[pl.BlockSpec((tm, K), lambda i, j: (i, 0)), pl.BlockSpec((tn, K), lambda i, j: (j, 0)), ANY],
        out_specs=pl.BlockSpec((tm, tn), lambda i, j: (i, j)),
        out_shape=jax.ShapeDtypeStruct((T, N), BF16),
        compiler_params=_params("parallel", "parallel"), name=name)(a, w, dep)


def ffn_bwd_mid(dzb, wd, h, u, scale, dep, name):
    T, K = dzb.shape
    N = wd.shape[0]
    tm, tn = _tile(T, 1024), _tile(N, 768)

    def body(dz_ref, w_ref, h_ref, u_ref, dep_ref, dh_ref, du_ref):
        dz = dz_ref[...]
        chunks = [slice(c, c + MXU_COLS) for c in range(0, tn, MXU_COLS)]
        das = [lax.dot_general(dz, w_ref[c, :], NT_DIMS, preferred_element_type=F32) for c in chunks]
        for c, da in zip(chunks, das):
            da = scale * da
            hv = h_ref[:, c].astype(F32)
            s = jax.nn.sigmoid(hv)
            silu = hv * s
            dh_ref[:, c] = (da * u_ref[:, c].astype(F32) * (s + silu * (1.0 - s))).astype(BF16)
            du_ref[:, c] = (da * silu).astype(BF16)

    tile = pl.BlockSpec((tm, tn), lambda j, i: (i, j))
    return pl.pallas_call(
        body, grid=(N // tn, T // tm),
        in_specs=[pl.BlockSpec((tm, K), lambda j, i: (i, 0)), pl.BlockSpec((tn, K), lambda j, i: (j, 0)), tile, tile, ANY],
        out_specs=[tile, tile],
        out_shape=[jax.ShapeDtypeStruct((T, N), BF16)] * 2,
        compiler_params=_params("parallel", "parallel"), name=name)(dzb, wd, h, u, dep)


def mm_nt_res(pairs, dz, alpha, ln_before, name):
    T, N = pairs[0][0].shape
    D = pairs[0][1].shape[0]
    n = len(pairs)
    tm = _tile(T, 512 if n == 1 else 256)

    def body(*refs):
        a_refs, w_refs, dz_ref = refs[:n], refs[n:2 * n], refs[2 * n]
        d = lax.dot_general(a_refs[0][...], w_refs[0][...], NT_DIMS, preferred_element_type=F32)
        for p in range(1, n):
            d += lax.dot_general(a_refs[p][...], w_refs[p][...], NT_DIMS, preferred_element_type=F32)
        d = d + alpha * dz_ref[...]
        if ln_before is None:
            refs[2 * n + 1][...] = d
        else:
            z_ref, g_ref = refs[2 * n + 1:2 * n + 3]
            _ln_bwd_rows(z_ref[...], g_ref[...], d, pl.program_id(0) == 0, *refs[2 * n + 3:])

    row = pl.BlockSpec((tm, D), lambda i: (i, 0))
    vec = pl.BlockSpec((1, D), lambda i: (0, 0))
    in_specs = ([pl.BlockSpec((tm, N), lambda i: (i, 0))] * n
                + [pl.BlockSpec((D, N), functools.partial(lambda i, col: (0, col), col=p[2])) for p in pairs] + [row])
    operands = [p[0] for p in pairs] + [p[1] for p in pairs] + [dz]
    if ln_before is None:
        return pl.pallas_call(
            body, grid=(T // tm,), in_specs=in_specs, out_specs=row, out_shape=jax.ShapeDtypeStruct((T, D), F32),
            compiler_params=_params("parallel"), name=name)(*operands)
    return pl.pallas_call(
        body, grid=(T // tm,), in_specs=in_specs + [row, vec], out_specs=[row, row, vec, vec],
        out_shape=_ln_bwd_outs(T, D), compiler_params=_params("arbitrary"), name=name)(*operands, *ln_before)


def mm_tn(a, b, scale, dep, name, into=None, col=0):
    T, M = a.shape
    N = b.shape[1]
    tm, tn, tk = _tile(M, 512), _tile(N, 1024), _tile(T, 4096)
    nk = T // tk
    if into is None:
        into = lax.empty((M, N), BF16)

    def body(a_ref, b_ref, dep_ref, into_ref, o_ref, acc_ref):
        k = pl.program_id(2)

        @pl.when(k == 0)
        def _():
            acc_ref[...] = jnp.zeros_like(acc_ref)

        acc_ref[...] += lax.dot_general(a_ref[...], b_ref[...], TN_DIMS, preferred_element_type=F32)

        @pl.when(k == nk - 1)
        def _():
            o_ref[...] = (scale * acc_ref[...]).astype(BF16)

    return pl.pallas_call(
        body, grid=(M // tm, N // tn, nk),
        in_specs=[pl.BlockSpec((tk, tm), lambda i, j, k: (k, i)), pl.BlockSpec((tk, tn), lambda i, j, k: (k, j)), ANY, ANY],
        out_specs=pl.BlockSpec((tm, tn), lambda i, j, k: (i, col * (N // tn) + j)),
        out_shape=jax.ShapeDtypeStruct(into.shape, BF16), input_output_aliases={3: 0},
        scratch_shapes=[pltpu.VMEM((tm, tn), F32)],
        compiler_params=_params("parallel", "parallel", "arbitrary"), name=name)(a, b, dep, into)


def loss_head(y, target, name):
    T, D = y.shape
    tm = _tile(T, 512)

    def body(y_ref, t_ref, l_ref, dy_ref):
        @pl.when(pl.program_id(0) == 0)
        def _():
            l_ref[...] = jnp.zeros_like(l_ref)

        e = y_ref[...] - t_ref[...]
        dy_ref[...] = e * (1.0 / D)
        rows = jnp.sum(e * e, axis=1, keepdims=True) * (0.5 / D)
        l_ref[...] += jnp.sum(rows, axis=0, keepdims=True)

    row = pl.BlockSpec((tm, D), lambda i: (i, 0))
    return pl.pallas_call(
        body, grid=(T // tm,), in_specs=[row, row],
        out_specs=[pl.BlockSpec((1, 1), lambda i: (0, 0)), row],
        out_shape=[jax.ShapeDtypeStruct((1, 1), F32), jax.ShapeDtypeStruct((T, D), F32)],
        compiler_params=_params("arbitrary"), name=name)(y, target)


def _rel_index(i, j):
    return jnp.clip(i - j + LEFT_CHUNKS * CHUNK, -REL_CLIP, REL_CLIP) + REL_CLIP


def bias_band(table_t, name):
    H = table_t.shape[0]
    rows = SUBLANE

    def body(t_ref, o_ref):
        i0 = pl.program_id(0) * rows
        parts = _split3(t_ref[...])
        r = lax.broadcasted_iota(jnp.int32, (REL_PAD, A_W), 0)
        j = lax.broadcasted_iota(jnp.int32, (REL_PAD, A_W), 1)
        for ii in range(rows):
            onehot = jnp.where(r == _rel_index(i0 + ii, j), 1.0, 0.0).astype(BF16)
            o_ref[ii] = sum(jnp.dot(p, onehot, preferred_element_type=F32) for p in parts)

    return pl.pallas_call(
        body, grid=(A_TQ // rows,),
        in_specs=[pl.BlockSpec((H, REL_PAD), lambda i: (0, 0))],
        out_specs=pl.BlockSpec((rows, H, A_W), lambda i: (i, 0, 0)),
        out_shape=jax.ShapeDtypeStruct((A_TQ, H, A_W), F32),
        compiler_params=_params("parallel"), name=name)(table_t)


def bias_band_bwd(dbands, name):
    H = dbands[0].shape[1]
    rows = SUBLANE
    n = len(dbands)

    def body(*refs):
        g_refs, o_ref = refs[:n], refs[n]

        @pl.when(pl.program_id(0) == 0)
        def _():
            o_ref[...] = jnp.zeros_like(o_ref)

        i0 = pl.program_id(0) * rows
        j = lax.broadcasted_iota(jnp.int32, (A_W, REL_PAD), 0)
        r = lax.broadcasted_iota(jnp.int32, (A_W, REL_PAD), 1)
        acc = jnp.zeros((H, REL_PAD), F32)
        for ii in range(rows):
            onehot = jnp.where(r == _rel_index(i0 + ii, j), 1.0, 0.0).astype(BF16)
            g = g_refs[0][ii]
            for q in range(1, n):
                g = g + g_refs[q][ii]
            acc += _dot3(g, onehot)
        o_ref[...] += acc

    spec = pl.BlockSpec((rows, H, A_W), lambda i: (i, 0, 0))
    return pl.pallas_call(
        body, grid=(A_TQ // rows,), in_specs=[spec] * n,
        out_specs=pl.BlockSpec((H, REL_PAD), lambda i: (0, 0)),
        out_shape=jax.ShapeDtypeStruct((H, REL_PAD), F32),
        compiler_params=_params("arbitrary"), name=name)(*dbands)


def _a_window(ref, qi):
    parts = []
    for d in range(A_NWB):
        kb = jnp.maximum(qi - (A_NWB - 1) + d, 0)
        parts.append(ref[pl.ds(pl.multiple_of(kb * A_TQ, A_TQ), A_TQ), :])
    return jnp.concatenate(parts, axis=0)


def _a_valid(qi):
    i = lax.broadcasted_iota(jnp.int32, (A_TQ, A_W), 0)
    j = lax.broadcasted_iota(jnp.int32, (A_TQ, A_W), 1)
    ic, jc = i // CHUNK, j // CHUNK
    return (jc >= ic) & (jc <= ic + LEFT_CHUNKS) & (j >= LEFT_CHUNKS * CHUNK - qi * A_TQ)


def _head_masks():
    lane = lax.broadcasted_iota(jnp.int32, (1, LANE), 1)
    return [lane < HEAD_DIM, lane >= HEAD_DIM]


def _a_probs(qms, kws, valids, b_ref):
    scores = [lax.dot_general(qm, kws[i // 2], NT_DIMS, preferred_element_type=F32) for i, qm in enumerate(qms)]
    probs = []
    for i, s in enumerate(scores):
        s = jnp.where(valids[i // 2], s * (HEAD_DIM ** -0.5) + b_ref[i % 2], NEG)
        p = jnp.exp(s - jnp.max(s, axis=1, keepdims=True))
        probs.append(p / jnp.sum(p, axis=1, keepdims=True))
    return probs


def _a_subtiles(ref, masks):
    out = []
    for u in range(ref.shape[0] // A_TQ):
        x = ref[u * A_TQ:(u + 1) * A_TQ, :]
        out += [jnp.where(hm, x, jnp.zeros_like(x)) for hm in masks]
    return out


def _attn_specs(S, D, tq):
    hp_n = D // LANE
    nq = S // tq
    q = pl.BlockSpec((tq, LANE), lambda hp, b, qi: (b * nq + qi, hp))
    k = pl.BlockSpec((S, LANE), lambda hp, b, qi: (b, hp_n + hp))
    v = pl.BlockSpec((S, LANE), lambda hp, b, qi: (b, 2 * hp_n + hp))
    tile = pl.BlockSpec((tq, LANE), lambda hp, b, qi: (b * nq + qi, hp))
    seq = pl.BlockSpec((S, LANE), lambda hp, b, qi: (b, hp))
    return q, k, v, tile, seq


def attn_a_fwd(qkv, band, S, name):
    T, D3 = qkv.shape
    D = D3 // 3
    sub = min(A_SUB, S // A_TQ)
    nb, ng = T // S, S // (A_TQ * sub)

    def body(q_ref, k_ref, v_ref, b_ref, o_ref):
        qis = [pl.program_id(2) * sub + u for u in range(sub)]
        masks = _head_masks()
        kws, vws = [_a_window(k_ref, qi) for qi in qis], [_a_window(v_ref, qi) for qi in qis]
        probs = _a_probs(_a_subtiles(q_ref, masks), kws, [_a_valid(qi) for qi in qis], b_ref)
        outs = [jnp.dot(p.astype(BF16), vws[i // 2], preferred_element_type=F32) for i, p in enumerate(probs)]
        for u in range(sub):
            o_ref[u * A_TQ:(u + 1) * A_TQ, :] = jnp.where(masks[0], outs[2 * u], outs[2 * u + 1]).astype(BF16)

    q, k, v, tile, _ = _attn_specs(S, D, A_TQ * sub)
    return pl.pallas_call(
        body, grid=(D // LANE, nb, ng),
        in_specs=[q, k, v, pl.BlockSpec((2, A_TQ, A_W), lambda hp, b, qi: (hp, 0, 0))],
        out_specs=tile, out_shape=jax.ShapeDtypeStruct((T, D), BF16),
        compiler_params=_params("parallel", "parallel", "parallel"), name=name)(qkv, qkv, qkv, band)


def attn_a_bwd(qkv, band, do, S, name):
    T, D3 = qkv.shape
    D = D3 // 3
    sub = min(A_SUB, S // A_TQ)
    nb, ng = T // S, S // (A_TQ * sub)
    scale = HEAD_DIM ** -0.5

    def body(q_ref, k_ref, v_ref, b_ref, do_ref, dq_ref, dk_ref, dv_ref, db_ref, dk_acc, dv_acc):
        b, qg = pl.program_id(1), pl.program_id(2)

        @pl.when((b == 0) & (qg == 0))
        def _():
            db_ref[...] = jnp.zeros_like(db_ref)

        @pl.when(qg == 0)
        def _():
            dk_acc[...] = jnp.zeros_like(dk_acc)
            dv_acc[...] = jnp.zeros_like(dv_acc)

        qis = [qg * sub + u for u in range(sub)]
        masks = _head_masks()
        kws, vws = [_a_window(k_ref, qi) for qi in qis], [_a_window(v_ref, qi) for qi in qis]
        qms, doms = _a_subtiles(q_ref, masks), _a_subtiles(do_ref, masks)
        dps = [lax.dot_general(dom, vws[i // 2], NT_DIMS, preferred_element_type=F32) for i, dom in enumerate(doms)]
        probs = _a_probs(qms, kws, [_a_valid(qi) for qi in qis], b_ref)
        dss = [p * (dp - jnp.sum(p * dp, axis=1, keepdims=True)) for p, dp in zip(probs, dps)]
        for hh in range(2):
            db_ref[hh] += sum(dss[hh::2])
        dsbs = [ds.astype(BF16) for ds in dss]
        dqs = [jnp.dot(dsb, kws[i // 2], preferred_element_type=F32) for i, dsb in enumerate(dsbs)]
        dks = [lax.dot_general(dsb, qm, TN_DIMS, preferred_element_type=F32) for dsb, qm in zip(dsbs, qms)]
        dvs = [lax.dot_general(p.astype(BF16), dom, TN_DIMS, preferred_element_type=F32) for p, dom in zip(probs, doms)]
        for u, qi in enumerate(qis):
            dq_ref[u * A_TQ:(u + 1) * A_TQ, :] = (jnp.where(masks[0], dqs[2 * u], dqs[2 * u + 1]) * scale).astype(BF16)
            dkw = (dks[2 * u] + dks[2 * u + 1]) * scale
            dvw = dvs[2 * u] + dvs[2 * u + 1]
            for d in range(A_NWB):
                kb = jnp.maximum(qi - (A_NWB - 1) + d, 0)
                rows = pl.ds(pl.multiple_of(kb * A_TQ, A_TQ), A_TQ)
                dk_acc[rows, :] += dkw[d * A_TQ:(d + 1) * A_TQ]
                dv_acc[rows, :] += dvw[d * A_TQ:(d + 1) * A_TQ]

        @pl.when(qg == ng - 1)
        def _():
            dk_ref[...] = dk_acc[...].astype(BF16)
            dv_ref[...] = dv_acc[...].astype(BF16)

    q, k, v, tile, seq = _attn_specs(S, D, A_TQ * sub)
    bspec = pl.BlockSpec((2, A_TQ, A_W), lambda hp, b, qi: (hp, 0, 0))
    act = jax.ShapeDtypeStruct((T, D), BF16)
    return pl.pallas_call(
        body, grid=(D // LANE, nb, ng),
        in_specs=[q, k, v, bspec, tile], out_specs=[tile, seq, seq, bspec],
        out_shape=[act, act, act, jax.ShapeDtypeStruct(band.shape, F32)],
        scratch_shapes=[pltpu.VMEM((S, LANE), F32), pltpu.VMEM((S, LANE), F32)],
        compiler_params=_params("arbitrary", "arbitrary", "arbitrary"), name=name)(qkv, qkv, qkv, band, do)


def _dot2(v, w):
    h = v.astype(BF16)
    lo = (v - h.astype(F32)).astype(BF16)
    return jnp.dot(h, w, preferred_element_type=F32) + jnp.dot(lo, w, preferred_element_type=F32)


def _b_weights(qms, kts, rights, after, diagonal):
    inst = [(t, hh) for t in range(len(kts)) for hh in range(2)]
    zs = [lax.dot_general(qms[hh], kts[t], NT_DIMS, preferred_element_type=F32) for t, hh in inst]
    lbs, l1ms, his, los = [], [], [], []
    causal = _strictly_causal() if any(diagonal) else None
    for (t, hh), z in zip(inst, zs):
        lb = jnp.minimum(z, 0.0) - jnp.log(1.0 + jnp.exp(jnp.minimum(z, -z)))
        lbs.append(lb)
        l1m = lb - z
        if diagonal[t]:
            l1m = jnp.where(causal, l1m, 0.0)
        hi = l1m.astype(BF16)
        l1ms.append(l1m)
        his.append(hi)
        los.append((l1m - hi.astype(F32)).astype(BF16))
    sums = [jnp.dot(hi, after, preferred_element_type=F32) + jnp.dot(lo, after, preferred_element_type=F32)
            for hi, lo in zip(his, los)]
    rights = list(rights)
    weights = []
    for (t, hh), lb, l1m, c in zip(inst, lbs, l1ms, sums):
        a = jnp.exp(lb + (rights[hh] + c))
        weights.append(jnp.where(causal, a, 0.0) if diagonal[t] else a)
        rights[hh] = rights[hh] + (c[:, :1] + l1m[:, :1])
    return zs, weights, rights


def _strictly_causal():
    row = lax.broadcasted_iota(jnp.int32, (B_T, B_T), 0)
    col = lax.broadcasted_iota(jnp.int32, (B_T, B_T), 1)
    return col < row


def _tri(strict_lower):
    r = lax.broadcasted_iota(jnp.int32, (B_T, B_T), 0)
    c = lax.broadcasted_iota(jnp.int32, (B_T, B_T), 1)
    return jnp.where((r > c) if strict_lower else (r < c), 1.0, 0.0).astype(BF16)


def _scaled_heads(q2):
    qs = q2 * (HEAD_DIM ** -0.5)
    return [jnp.where(hm, qs, jnp.zeros_like(qs)) for hm in _head_masks()]


def attn_b_fwd(qkv, S, name):
    T, D3 = qkv.shape
    D = D3 // 3
    nb, nq = T // S, S // B_T

    def body(q_ref, k_ref, v_ref, o_ref):
        qi = pl.program_id(2)
        after = _tri(True)
        qms = _scaled_heads(q_ref[...])

        def tiles(kbs, carry, diagonal):
            rows = [pl.ds(pl.multiple_of(kb * B_T, B_T), B_T) for kb in kbs]
            kts, vts = [k_ref[r, :] for r in rows], [v_ref[r, :] for r in rows]
            _, weights, rights = _b_weights(qms, kts, (carry[0], carry[2]), after, diagonal)
            accs = [carry[1], carry[3]]
            for i, a in enumerate(weights):
                accs[i % 2] = accs[i % 2] + jnp.dot(a.astype(BF16), vts[i // 2], preferred_element_type=F32)
            return rights[0], accs[0], rights[1], accs[1]

        init = (jnp.zeros((B_T, 1), F32), jnp.zeros((B_T, LANE), F32)) * 2
        res = lax.cond(qi % 2 == 1, lambda c: tiles([qi, qi - 1], c, [True, False]), lambda c: tiles([qi], c, [True]), init)
        top = qi - 1 - qi % 2
        res = lax.fori_loop(0, qi // 2, lambda p, c: tiles([top - 2 * p, top - 1 - 2 * p], c, [False, False]), res)
        o_ref[...] = jnp.where(_head_masks()[0], res[1], res[3]).astype(BF16)

    q, k, v, tile, _ = _attn_specs(S, D, B_T)
    return pl.pallas_call(
        body, grid=(D // LANE, nb, nq), in_specs=[q, k, v], out_specs=tile,
        out_shape=jax.ShapeDtypeStruct((T, D), BF16),
        compiler_params=_params("parallel", "parallel", "parallel"), name=name)(qkv, qkv, qkv)


def attn_b_bwd(qkv, do, S, name):
    T, D3 = qkv.shape
    D = D3 // 3
    nb, nq = T // S, S // B_T
    scale = HEAD_DIM ** -0.5

    def body(q_ref, k_ref, v_ref, do_ref, dq_ref, dk_ref, dv_ref, dk_acc, dv_acc, z_s, g_s):
        qi = pl.program_id(2)

        @pl.when(qi == 0)
        def _():
            dk_acc[...] = jnp.zeros_like(dk_acc)
            dv_acc[...] = jnp.zeros_like(dv_acc)

        do2 = do_ref[...]
        after, before = _tri(True), _tri(False)
        masks = _head_masks()
        qms = _scaled_heads(q_ref[...])
        doms = [jnp.where(hm, do2, jnp.zeros_like(do2)) for hm in masks]

        def sweep_left(kbs, rights, diagonal):
            rows = [pl.ds(pl.multiple_of(kb * B_T, B_T), B_T) for kb in kbs]
            kts, vts = [k_ref[r, :] for r in rows], [v_ref[r, :] for r in rows]
            das = [lax.dot_general(doms[hh], vt, NT_DIMS, preferred_element_type=F32) for vt in vts for hh in range(2)]
            zs, weights, rights = _b_weights(qms, kts, rights, after, diagonal)
            for i, (z, a, da) in enumerate(zip(zs, weights, das)):
                z_s[i % 2, kbs[i // 2]] = z
                g_s[i % 2, kbs[i // 2]] = da * a
            dvs = [lax.dot_general(a.astype(BF16), doms[i % 2], TN_DIMS, preferred_element_type=F32)
                   for i, a in enumerate(weights)]
            for t, r in enumerate(rows):
                dv_acc[r, :] += dvs[2 * t] + dvs[2 * t + 1]
            return tuple(rights)

        zero_col = jnp.zeros((B_T, 1), F32)
        odd = qi % 2 == 1
        top = qi - 1 - qi % 2
        rights = lax.cond(odd, lambda c: sweep_left([qi, qi - 1], c, [True, False]), lambda c: sweep_left([qi], c, [True]),
                          (zero_col, zero_col))
        lax.fori_loop(0, qi // 2, lambda p, c: sweep_left([top - 2 * p, top - 1 - 2 * p], c, [False, False]), rights)

        def sweep_right(kbs, carry, diagonal):
            rows = [pl.ds(pl.multiple_of(kb * B_T, B_T), B_T) for kb in kbs]
            inst = [(t, hh) for t in range(len(kbs)) for hh in range(2)]
            gs = [g_s[hh, kbs[t]] for t, hh in inst]
            sums = [_dot2(g, before) for g in gs]
            lefts, dqs = [carry[0], carry[2]], [carry[1], carry[3]]
            dzbs = []
            for (t, hh), g, c in zip(inst, gs, sums):
                beta = jax.nn.sigmoid(z_s[hh, kbs[t]])
                dz = g * (1.0 - beta) - beta * (lefts[hh] + c)
                if diagonal[t]:
                    dz = jnp.where(_strictly_causal(), dz, 0.0)
                dzbs.append(dz.astype(BF16))
                lefts[hh] = lefts[hh] + jnp.sum(g, axis=1, keepdims=True)
            dks = [lax.dot_general(dzb, qms[hh], TN_DIMS, preferred_element_type=F32) for (t, hh), dzb in zip(inst, dzbs)]
            for (t, hh), dzb in zip(inst, dzbs):
                dqs[hh] = dqs[hh] + jnp.dot(dzb, k_ref[rows[t], :], preferred_element_type=F32)
            for t, r in enumerate(rows):
                dk_acc[r, :] += dks[2 * t] + dks[2 * t + 1]
            return lefts[0], dqs[0], lefts[1], dqs[1]

        init = (zero_col, jnp.zeros((B_T, LANE), F32)) * 2
        res = lax.fori_loop(0, qi // 2, lambda p, c: sweep_right([2 * p, 2 * p + 1], c, [False, False]), init)
        res = lax.cond(odd, lambda c: sweep_right([qi - 1, qi], c, [False, True]), lambda c: sweep_right([qi], c, [True]), res)
        dq_ref[...] = (jnp.where(masks[0], res[1], res[3]) * scale).astype(BF16)

        @pl.when(qi == nq - 1)
        def _():
            dk_ref[...] = dk_acc[...].astype(BF16)
            dv_ref[...] = dv_acc[...].astype(BF16)

    q, k, v, tile, seq = _attn_specs(S, D, B_T)
    act = jax.ShapeDtypeStruct((T, D), BF16)
    return pl.pallas_call(
        body, grid=(D // LANE, nb, nq),
        in_specs=[q, k, v, tile], out_specs=[tile, seq, seq], out_shape=[act, act, act],
        scratch_shapes=[pltpu.VMEM((S, LANE), F32), pltpu.VMEM((S, LANE), F32),
                        pltpu.VMEM((2, nq, B_T, B_T), F32), pltpu.VMEM((2, nq, B_T, B_T), F32)],
        compiler_params=_params("arbitrary", "arbitrary", "arbitrary"), name=name)(qkv, qkv, qkv, do)


HBM = pl.BlockSpec(memory_space=pltpu.HBM)
SEM = pl.BlockSpec(memory_space=pltpu.SEMAPHORE)
N_PEERS = N_DEV - 1
GATHERS_AHEAD = 3


def _place():
    return lax.axis_index("x"), lax.axis_index("y"), lax.axis_index("c")


def _peers(x, y, c):
    return [(1 - x if r & 4 else x, 1 - y if r & 2 else y, 1 - c if r & 1 else c) for r in range(1, N_DEV)]


def _block(ref, shape, by_cols, j):
    r, w = shape
    if by_cols:
        return ref.at[:, pl.ds(pl.multiple_of(j * w, LANE), w)]
    return ref.at[pl.ds(pl.multiple_of(j * r, SUBLANE), r), :]


def _exchange_copies(gather, src_refs, land_refs, send_sems, recv_sems, by_cols):
    x, y, c = _place()
    me = 4 * x + 2 * y + c
    out = []
    for t, (src, land) in enumerate(zip(src_refs, land_refs)):
        for r, peer in enumerate(_peers(x, y, c)):
            pj = 4 * peer[0] + 2 * peer[1] + peer[2]
            if gather:
                mine, to_me, theirs = src, _block(land, src.shape, by_cols[t], me), _block(land, src.shape, by_cols[t], pj)
            else:
                mine, to_me, theirs = _block(src, land.shape[1:], by_cols[t], pj), land.at[me], land.at[pj]
            sems = dict(send_sem=send_sems.at[N_PEERS * t + r], recv_sem=recv_sems.at[N_PEERS * t + r],
                        device_id=peer, device_id_type=MESH_ID)
            out.append((pltpu.make_async_remote_copy(src_ref=mine, dst_ref=to_me, **sems),
                        pltpu.make_async_remote_copy(src_ref=mine, dst_ref=theirs, **sems)))
    return out


def _own_copies(gather, src_refs, land_refs, own_sems, by_cols):
    x, y, c = _place()
    me = 4 * x + 2 * y + c
    out = []
    for t, (src, land) in enumerate(zip(src_refs, land_refs)):
        if gather:
            out.append(pltpu.make_async_copy(src, _block(land, src.shape, by_cols[t], me), own_sems.at[t]))
        else:
            out.append(pltpu.make_async_copy(_block(src, land.shape[1:], by_cols[t], me), land.at[me], own_sems.at[t]))
    return out


def exchange_start(gather, srcs, land_shapes, by_cols, after, name):
    n = len(srcs)

    def body(*refs):
        src_refs, land_refs = refs[:n], refs[n:2 * n]
        send_sems, recv_sems, own_sems = refs[2 * n + 1:2 * n + 4]
        token = refs[-1]
        for cp in _own_copies(gather, src_refs, land_refs, own_sems, by_cols):
            cp.start()
        for mine, _ in _exchange_copies(gather, src_refs, land_refs, send_sems, recv_sems, by_cols):
            mine.start()
        token[...] = jnp.zeros_like(token)

    lands = [pltpu.with_memory_space_constraint(lax.empty(s.shape, s.dtype), pltpu.HBM) for s in land_shapes]
    srcs = [pltpu.with_memory_space_constraint(s, pltpu.HBM) for s in srcs]
    res = pl.pallas_call(
        body, name=name,
        out_shape=(pltpu.SemaphoreType.DMA((N_PEERS * n,)), pltpu.SemaphoreType.DMA((N_PEERS * n,)),
                   pltpu.SemaphoreType.DMA((n,)),
                   *[pltpu.HBM(s.shape, s.dtype) for s in srcs], *[pltpu.HBM(s.shape, s.dtype) for s in land_shapes],
                   jax.ShapeDtypeStruct((SUBLANE, LANE), F32)),
        in_specs=[HBM] * (2 * n) + [ANY],
        out_specs=(SEM, SEM, SEM, *[HBM] * (2 * n), pl.BlockSpec(memory_space=pltpu.VMEM)),
        input_output_aliases={i: 3 + i for i in range(2 * n)},
        compiler_params=pltpu.CompilerParams(has_side_effects=pltpu.SideEffectType.DATAFLOW_SIDE_EFFECTING),
    )(*srcs, *lands, after)
    return dict(gather=gather, n=n, by_cols=by_cols, sems=res[:3], srcs=res[3:3 + n],
                lands=res[3 + n:3 + 2 * n], token=res[-1])


def exchange_wait(started, after, name):
    n, gather, by_cols = started["n"], started["gather"], started["by_cols"]

    def body(*refs):
        src_refs, land_refs = refs[:n], refs[n:2 * n]
        send_sems, recv_sems, own_sems = refs[2 * n:2 * n + 3]
        for mine, theirs in _exchange_copies(gather, src_refs, land_refs, send_sems, recv_sems, by_cols):
            mine.wait_send()
            theirs.wait_recv()
        for cp in _own_copies(gather, src_refs, land_refs, own_sems, by_cols):
            cp.wait()

    res = pl.pallas_call(
        body, name=name,
        out_shape=tuple(pltpu.HBM(s.shape, s.dtype) for s in (*started["srcs"], *started["lands"])),
        in_specs=[HBM] * (2 * n) + [SEM, SEM, SEM, ANY], out_specs=tuple([HBM] * (2 * n)),
        input_output_aliases={i: i for i in range(2 * n)},
        compiler_params=pltpu.CompilerParams(has_side_effects=pltpu.SideEffectType.DATAFLOW_SIDE_EFFECTING),
    )(*started["srcs"], *started["lands"], *started["sems"], after)
    return res[n:]


def gather_small(v, dep, name):
    R, C = v.shape

    def body(v_ref, dep_ref, o_ref, send_sems, recv_sems):
        x, y, c = _place()
        o_ref[4 * x + 2 * y + c] = v_ref[...]
        peers = _peers(x, y, c)

        def copy(r, owner, to):
            slot = o_ref.at[4 * owner[0] + 2 * owner[1] + owner[2]]
            return pltpu.make_async_remote_copy(
                src_ref=slot, dst_ref=slot, send_sem=send_sems.at[r], recv_sem=recv_sems.at[r],
                device_id=to, device_id_type=MESH_ID)

        sends = [copy(r, (x, y, c), peer) for r, peer in enumerate(peers)]
        for cp in sends:
            cp.start()
        for r, peer in enumerate(peers):
            copy(r, peer, (x, y, c)).wait_recv()
        for cp in sends:
            cp.wait_send()

    vm = pl.BlockSpec(memory_space=pltpu.VMEM)
    return pl.pallas_call(
        body, in_specs=[vm, ANY], out_specs=vm, out_shape=jax.ShapeDtypeStruct((N_DEV, R, C), F32),
        scratch_shapes=[pltpu.SemaphoreType.DMA((N_PEERS,)), pltpu.SemaphoreType.DMA((N_PEERS,))],
        name=name)(v, dep)


def sum_devices(g, name):
    _, R, C = g.shape

    def body(g_ref, o_ref):
        acc = g_ref[0]
        for j in range(1, N_DEV):
            acc = acc + g_ref[j]
        o_ref[...] = acc

    vm = pl.BlockSpec(memory_space=pltpu.VMEM)
    return pl.pallas_call(body, in_specs=[vm], out_specs=vm, out_shape=jax.ShapeDtypeStruct((R, C), F32), name=name)(g)


def adamw(w, m, v, index, parts, outs, name):
    L, R, C = w.shape
    if outs is None:
        outs = [lax.empty((L, R, C), F32) for _ in range(4)]
    P, Rp, Cp = parts.shape
    tr = _tile(R, 128, SUBLANE) if Rp == R and R % 128 == 0 else R

    def body(w_ref, m_ref, v_ref, p_ref, g_in, d_in, m_in, v_in, g_out, d_out, m_out, v_out):
        g = p_ref[0, :tr, :C].astype(F32)
        for q in range(1, P):
            g = g + p_ref[q, :tr, :C].astype(F32)
        mn = ADAM_B1 * m_ref[...] + (1.0 - ADAM_B1) * g
        vn = ADAM_B2 * v_ref[...] + (1.0 - ADAM_B2) * (g * g)
        m_hat = mn / (1.0 - ADAM_B1 ** ADAM_STEP)
        v_hat = vn / (1.0 - ADAM_B2 ** ADAM_STEP)
        g_out[...] = g
        d_out[...] = -ADAM_LR * (m_hat / (jnp.sqrt(v_hat) + ADAM_EPS) + ADAM_WD * w_ref[...])
        m_out[...] = mn
        v_out[...] = vn

    slab = pl.BlockSpec((None, tr, C), lambda r: (index, r, 0))
    pspec = pl.BlockSpec((P, tr if Rp == R else Rp, Cp), lambda r: (0, r, 0))
    return pl.pallas_call(
        body, grid=(R // tr,), in_specs=[slab] * 3 + [pspec] + [ANY] * 4, out_specs=[slab] * 4,
        out_shape=[jax.ShapeDtypeStruct((L, R, C), F32)] * 4,
        input_output_aliases={4 + q: q for q in range(4)},
        compiler_params=_params("parallel"), name=name)(w, m, v, parts, *outs)


def _pad_to(a, axis, size):
    pad = [(0, 0)] * a.ndim
    pad[axis] = (0, size - a.shape[axis])
    return jnp.pad(a, pad)


def kernel(x, w_qkv_a, w_o_a, rel_bias, w_qkv_b, w_o_b, ffn_w_gate, ffn_w_up, ffn_w_down, ln_g, ln_b, loss_target, m_w_qkv_a, m_w_o_a, m_rel_bias, m_w_qkv_b, m_w_o_b, m_ffn_w_gate, m_ffn_w_up, m_ffn_w_down, m_ln_g, m_ln_b, v_w_qkv_a, v_w_o_a, v_rel_bias, v_w_qkv_b, v_w_o_b, v_ffn_w_gate, v_ffn_w_up, v_ffn_w_down, v_ln_g, v_ln_b):
    nb, S, D = x.shape
    T = nb * S
    depth = ffn_w_gate.shape[0]
    H = D // HEAD_DIM
    fs = ffn_w_gate.shape[-1]
    fp = -(-fs // LANE) * LANE
    alpha = (2.0 * depth) ** 0.25
    cx, cy, cc = _place()
    me = 4 * cx + 2 * cy + cc

    ln_local = jnp.concatenate([ln_g.reshape(depth * 3, -1), ln_b.reshape(depth * 3, -1)], axis=0)
    ln_all = gather_small(ln_local, ln_local, "gather_ln")
    ln_full = jnp.transpose(ln_all, (1, 0, 2)).reshape(2 * depth * 3, D)
    ln_gain = lambda i, s: ln_full[3 * i + s][None, :]
    ln_bias = lambda i, s: ln_full[3 * depth + 3 * i + s][None, :]

    table_t = _pad_to(rel_bias.T, 1, REL_PAD)
    band = jnp.transpose(bias_band(table_t, "bias_band"), (1, 0, 2))

    subs = []
    for i in range(depth):
        for s in (0, 1, 2):
            if s == 1:
                wq, wo = (w_qkv_a, w_o_a) if i % 2 == 0 else (w_qkv_b, w_o_b)
                subs.append(([wq[i // 2].astype(BF16), wo[i // 2].astype(BF16)], [True, False]))
            else:
                f = 0 if s == 0 else 1
                subs.append(([_pad_to(ffn_w_gate[i, f].astype(BF16), 1, fp), _pad_to(ffn_w_up[i, f].astype(BF16), 1, fp),
                              _pad_to(ffn_w_down[i, f].astype(BF16), 0, fp)], [True, True, False]))

    def start_gather(k, after):
        shards, by_cols = subs[k]
        groups = [(0, 2), (2, 3)] if k == 0 else [(0, len(shards))]
        out = []
        for part, (lo, hi) in enumerate(groups):
            shapes = [jax.ShapeDtypeStruct((s.shape[0], N_DEV * s.shape[1]) if col else (N_DEV * s.shape[0], s.shape[1]), BF16)
                      for s, col in zip(shards[lo:hi], by_cols[lo:hi])]
            out.append(exchange_start(True, shards[lo:hi], shapes, by_cols[lo:hi], after, f"gather_start_{k}_{part}"))
            after = out[-1]["token"]
        return out

    xf = x.reshape(T, D)
    act, act_b = xf, xf.astype(BF16)
    gathers = {}
    for k in range(min(GATHERS_AHEAD, len(subs))):
        gathers[k] = start_gather(k, xf if k == 0 else gathers[k - 1][-1]["token"])
    newest = gathers[k][-1]["token"]
    saved = []
    for i in range(depth):
        layer = {}
        for s in (0, 1, 2):
            k = 3 * i + s
            tag = f"L{i}S{s}"
            if 0 < k and k + GATHERS_AHEAD - 1 < len(subs):
                gathers[k + GATHERS_AHEAD - 1] = start_gather(k + GATHERS_AHEAD - 1, act_b)
                newest = gathers[k + GATHERS_AHEAD - 1][-1]["token"]
            parts = gathers.pop(k)
            full = list(exchange_wait(parts[0], newest, f"gather_wait_{k}_0"))
            if s == 1:
                wqkv_f, wo_f = full
                qkv = mm_nn(act_b, wqkv_f, "qkv_" + tag)
                if i % 2 == 0:
                    att = attn_a_fwd(qkv, band, S, "attn_a_fwd_" + tag)
                else:
                    att = attn_b_fwd(qkv, S, "attn_b_fwd_" + tag)
                z, o, ob = mm_res_ln(att, wo_f, act, ln_gain(i, s), ln_bias(i, s), alpha, 1.0, "out_ln_" + tag)
                layer[s] = dict(x_b=act_b, qkv=qkv, att=att, z=z, wqkv=wqkv_f, wo=wo_f)
            else:
                wg_f, wu_f = full[:2]
                h, u, a = ffn_up(act_b, wg_f, wu_f, "ffn_up_" + tag)
                wd_f = full[2] if len(parts) == 1 else exchange_wait(parts[1], a, f"gather_wait_{k}_1")[0]
                z, o, ob = mm_res_ln(a, wd_f, act, ln_gain(i, s), ln_bias(i, s), alpha, 0.5, "down_ln_" + tag)
                layer[s] = dict(x_b=act_b, h=h, u=u, a=a, z=z, wg=wg_f, wu=wu_f, wd=wd_f)
            act, act_b = o, ob
        saved.append(layer)

    loss_local, d_act = loss_head(act, loss_target.reshape(T, D), "loss_head")
    loss = lax.psum(loss_local[0, 0], ("x", "y", "c"))

    results = {}

    def update(name, w, m, v, index, parts):
        L = w.shape[0] if w.ndim == 3 else w.shape[0] * w.shape[1]
        flat = lambda t: t.reshape((L,) + t.shape[-2:])
        results[name] = adamw(flat(w), flat(m), flat(v), index, parts, results.get(name), f"adamw_{name}_{index}")

    def finish(entry, after):
        exchange, targets, tag = entry
        lands = exchange_wait(exchange, after, "scatter_wait_" + tag)
        for (name, w, m, v, index), parts in zip(targets, lands):
            update(name, w, m, v, index, parts)
        return results[targets[-1][0]][0]

    pending = []
    started = d_act
    dbands = []
    dln_g = [None] * (3 * depth)
    dln_b = [None] * (3 * depth)
    ln_grads = ln_bwd(saved[depth - 1][2]["z"], ln_gain(depth - 1, 2), d_act, "ln_bwd_last")
    for i in reversed(range(depth)):
        for s in (2, 1, 0):
            tag = f"L{i}S{s}"
            sv = saved[i][s]
            dz, dzb, dln_g[3 * i + s], dln_b[3 * i + s] = ln_grads
            before = (i, s - 1) if s > 0 else (i - 1, 2) if i > 0 else None
            ln_before = None if before is None else (saved[before[0]][before[1]]["z"], ln_gain(*before))
            if s == 1:
                j = i // 2
                d_att = mm_nt(dzb, sv["wo"], started, "att_bwd_" + tag)
                g_wo = mm_tn(sv["att"], dzb, 1.0, started, "dwo_" + tag)
                if i % 2 == 0:
                    dq, dk, dv, dband = attn_a_bwd(sv["qkv"], band, d_att, S, "attn_a_bwd_" + tag)
                    dbands.append(jnp.transpose(dband, (1, 0, 2)))
                else:
                    dq, dk, dv = attn_b_bwd(sv["qkv"], d_att, S, "attn_b_bwd_" + tag)
                g_wqkv = lax.empty((D, 3 * D), BF16)
                for col, (piece, d_piece) in enumerate(zip("qkv", (dq, dk, dv))):
                    g_wqkv = mm_tn(sv["x_b"], d_piece, 1.0, started, f"dw{piece}_" + tag, g_wqkv, col)
                ln_grads = mm_nt_res([(dq, sv["wqkv"], 0), (dk, sv["wqkv"], 1), (dv, sv["wqkv"], 2)], dz, alpha, ln_before,
                                     "dx_mix_" + tag)
                grads, by_cols = [g_wqkv, g_wo], [True, False]
                slabs = [(D, 3 * D // N_DEV), (D // N_DEV, D)]
                if i % 2 == 0:
                    targets = [("w_qkv_a", w_qkv_a, m_w_qkv_a, v_w_qkv_a, j), ("w_o_a", w_o_a, m_w_o_a, v_w_o_a, j)]
                else:
                    targets = [("w_qkv_b", w_qkv_b, m_w_qkv_b, v_w_qkv_b, j), ("w_o_b", w_o_b, m_w_o_b, v_w_o_b, j)]
            else:
                f = 0 if s == 0 else 1
                dh, du = ffn_bwd_mid(dzb, sv["wd"], sv["h"], sv["u"], 0.5, started, "ffn_mid_" + tag)
                g_wd = mm_tn(sv["a"], dzb, 0.5, started, "dwd_" + tag)
                g_wg = mm_tn(sv["x_b"], dh, 1.0, started, "dwg_" + tag)
                g_wu = mm_tn(sv["x_b"], du, 1.0, started, "dwu_" + tag)
                ln_grads = mm_nt_res([(dh, sv["wg"], 0), (du, sv["wu"], 0)], dz, alpha, ln_before, "dx_ffn_" + tag)
                grads, by_cols = [g_wg, g_wu, g_wd], [True, True, False]
                slabs = [(D, fp), (D, fp), (fp, D)]
                idx = 2 * i + f
                targets = [("ffn_w_gate", ffn_w_gate, m_ffn_w_gate, v_ffn_w_gate, idx),
                           ("ffn_w_up", ffn_w_up, m_ffn_w_up, v_ffn_w_up, idx),
                           ("ffn_w_down", ffn_w_down, m_ffn_w_down, v_ffn_w_down, idx)]
            after = ln_grads[0] if before is not None else ln_grads
            if before is None:
                grad_x = ln_grads.reshape(nb, S, D)
                dtable_t = bias_band_bwd(dbands, "bias_band_bwd")
                small = jnp.concatenate(dln_g + dln_b + [_pad_to(dtable_t, 1, D)], axis=0)
                small = _pad_to(small, 0, -(-small.shape[0] // SUBLANE) * SUBLANE)
                total = after = sum_devices(gather_small(small, grads[0], "gather_small_grads"), "sum_small_grads")
            shapes = [jax.ShapeDtypeStruct((N_DEV,) + slab, BF16) for slab in slabs]
            pending.append((exchange_start(False, grads, shapes, by_cols, after, "scatter_start_" + tag), targets, tag))
            started = pending[-1][0]["token"]

    last = pending.pop()
    done = started
    for entry in pending:
        done = finish(entry, done)
    finish(last, done)
    n_ln = 3 * depth
    g_ln_g = lax.dynamic_slice_in_dim(total[:n_ln], me * (D // N_DEV), D // N_DEV, axis=1)
    g_ln_b = lax.dynamic_slice_in_dim(total[n_ln:2 * n_ln], me * (D // N_DEV), D // N_DEV, axis=1)
    g_rel = total[2 * n_ln:2 * n_ln + H, :N_REL].T
    as3 = lambda t: t.reshape((1, -1, t.shape[-1]))
    results["ln_g"] = adamw(as3(ln_g), as3(m_ln_g), as3(v_ln_g), 0, g_ln_g[None], None, "adamw_ln_g")
    results["ln_b"] = adamw(as3(ln_b), as3(m_ln_b), as3(v_ln_b), 0, g_ln_b[None], None, "adamw_ln_b")
    results["rel_bias"] = adamw(as3(rel_bias), as3(m_rel_bias), as3(v_rel_bias), 0, g_rel[None], None, "adamw_rel_bias")

    order = [("w_qkv_a", w_qkv_a), ("w_o_a", w_o_a), ("rel_bias", rel_bias), ("w_qkv_b", w_qkv_b), ("w_o_b", w_o_b),
             ("ffn_w_gate", ffn_w_gate), ("ffn_w_up", ffn_w_up), ("ffn_w_down", ffn_w_down), ("ln_g", ln_g), ("ln_b", ln_b)]
    outs = [loss, grad_x]
    for q in range(4):
        for name, like in order:
            outs.append(results[name][q].reshape(like.shape))
    return tuple(outs)
```
